```python
import jax, jax.numpy as jnp
from jax import lax
import numpy as np

D_MODEL = 2048
BATCH = 2
SEQ = 4096
DEPTH = 4

CHUNK = 64
PLE_DIM = 256
EPS = 1e-6
MLA_HEADS = 16
Q_LORA = 512
KV_LORA = 512
QK_NOPE = 128
QK_ROPE = 64
V_DIM = 128
QK_DIM = QK_NOPE + QK_ROPE
ROPE_THETA = 10000.0
Q_BLOCK = 128
SSM_EXPAND = 2
D_INNER = SSM_EXPAND * D_MODEL
SSM_HEADDIM = 64
SSM_HEADS = D_INNER // SSM_HEADDIM
SSM_GROUPS = 8
SSM_STATE = 128
HEADS_PER_GROUP = SSM_HEADS // SSM_GROUPS
CONV_WIDTH = 4
CONV_DIM = D_INNER + 2 * SSM_GROUPS * SSM_STATE
SSD_CHUNK = CHUNK
D_FF = 4 * D_MODEL
N_BRANCH = 2
SPLIT_POINTS = (
    Q_LORA,
    Q_LORA + KV_LORA,
    Q_LORA + KV_LORA + QK_ROPE,
    Q_LORA + KV_LORA + QK_ROPE + D_INNER,
    Q_LORA + KV_LORA + QK_ROPE + D_INNER + CONV_DIM,
    Q_LORA + KV_LORA + QK_ROPE + D_INNER + CONV_DIM + SSM_HEADS,
)
D_IN_PROJ = Q_LORA + KV_LORA + QK_ROPE + D_INNER + CONV_DIM + SSM_HEADS + N_BRANCH * D_MODEL

kernel_name = 'hybrid_mla_ssd_parallel_gated_trunk'


def rms_norm(x, w):
    xf = x.astype(jnp.float32)
    y = xf * lax.rsqrt(jnp.mean(xf * xf, axis=-1, keepdims=True) + EPS)
    return (y * w.astype(jnp.float32)).astype(x.dtype)


def rope_tables(positions):
    inv_freq = 1.0 / (ROPE_THETA ** (jnp.arange(0, QK_ROPE, 2, dtype=jnp.float32) / QK_ROPE))
    ang = positions.astype(jnp.float32)[..., None] * inv_freq
    return jnp.cos(ang), jnp.sin(ang)


def apply_rope(x, cos, sin):
    xf = x.astype(jnp.float32)
    x1, x2 = jnp.split(xf, 2, axis=-1)
    c = cos[:, :, None, :]
    s = sin[:, :, None, :]
    return jnp.concatenate([x1 * c - x2 * s, x1 * s + x2 * c], axis=-1).astype(x.dtype)


def chunk_causal_attention(q, k, v):
    B, S, H, Dh = q.shape
    n_blk = S // Q_BLOCK
    scale = Dh ** -0.5
    qb = q.reshape(B, n_blk, Q_BLOCK, H, Dh).transpose(1, 0, 3, 2, 4)
    kh = k.transpose(0, 2, 1, 3)
    vh = v.transpose(0, 2, 1, 3)
    key_chunk = jnp.arange(S) // CHUNK

    def one_block(args):
        q_blk, blk = args
        s = jnp.einsum('bhqd,bhkd->bhqk', q_blk, kh, preferred_element_type=jnp.float32) * scale
        q_chunk = (blk * Q_BLOCK + jnp.arange(Q_BLOCK)) // CHUNK
        mask = key_chunk[None, :] <= q_chunk[:, None]
        s = jnp.where(mask, s, -jnp.inf)
        pr = jax.nn.softmax(s, axis=-1)
        return jnp.einsum('bhqk,bhkd->bhqd', pr.astype(vh.dtype), vh)

    o = lax.map(one_block, (qb, jnp.arange(n_blk)))
    return o.transpose(1, 0, 3, 2, 4).reshape(B, S, H, V_DIM)


def mla_branch(c_q, c_kv, k_r, cos, sin, q_a_norm_w, w_uq, kv_a_norm_w, w_ukv, q_norm_w, k_norm_w):
    B, S, _ = c_q.shape
    q = (rms_norm(c_q, q_a_norm_w) @ w_uq).reshape(B, S, MLA_HEADS, QK_DIM)
    kv = (rms_norm(c_kv, kv_a_norm_w) @ w_ukv).reshape(B, S, MLA_HEADS, QK_NOPE + V_DIM)
    k_nope, v = kv[..., :QK_NOPE], kv[..., QK_NOPE:]
    k_rope = jnp.broadcast_to(k_r[:, :, None, :], (B, S, MLA_HEADS, QK_ROPE))
    k = jnp.concatenate([k_nope, k_rope], axis=-1)
    q = rms_norm(q, q_norm_w)
    k = rms_norm(k, k_norm_w)
    q = jnp.concatenate([q[..., :QK_NOPE], apply_rope(q[..., QK_NOPE:], cos, sin)], axis=-1)
    k = jnp.concatenate([k[..., :QK_NOPE], apply_rope(k[..., QK_NOPE:], cos, sin)], axis=-1)
    o = chunk_causal_attention(q, k, v)
    return o.reshape(B, S, MLA_HEADS * V_DIM)


def causal_depthwise_conv(x, w, b):
    S = x.shape[1]
    xp = jnp.pad(x, ((0, 0), (CONV_WIDTH - 1, 0), (0, 0)))
    y = xp[:, 0:S] * w[0]
    for tap in range(1, CONV_WIDTH):
        y = y + xp[:, tap:tap + S] * w[tap]
    return y + b


def ssd_branch(xbc_raw, z, dt_raw, conv_w, conv_b, dt_bias, a_log, d_skip, ssm_norm_w):
    B, S, _ = z.shape
    G, R, P, N, T = SSM_GROUPS, HEADS_PER_GROUP, SSM_HEADDIM, SSM_STATE, SSD_CHUNK
    nc = S // T
    xbc = jax.nn.silu(causal_depthwise_conv(xbc_raw, conv_w, conv_b)).astype(jnp.float32)
    x5 = xbc[..., :D_INNER].reshape(B, nc, T, G, R, P)
    Bm = xbc[..., D_INNER:D_INNER + G * N].reshape(B, nc, T, G, N)
    Cm = xbc[..., D_INNER + G * N:].reshape(B, nc, T, G, N)
    dt = jax.nn.softplus(dt_raw.astype(jnp.float32) + dt_bias.astype(jnp.float32))
    A = -jnp.exp(a_log.astype(jnp.float32))
    dt5 = dt.reshape(B, nc, T, G, R)
    a_cum = jnp.cumsum((dt * A).reshape(B, nc, T, G, R), axis=2)
    seg = a_cum[:, :, :, None] - a_cum[:, :, None]
    tri = jnp.tril(jnp.ones((T, T), dtype=bool))
    decay = jnp.exp(jnp.where(tri[:, :, None, None], seg, -jnp.inf))
    cb = jnp.einsum('bctgn,bcsgn->bctsg', Cm, Bm)
    m = cb[..., None] * decay * dt5[:, :, None]
    y_diag = jnp.einsum('bctsgr,bcsgrp->bctgrp', m, x5)
    decay_to_end = jnp.exp(a_cum[:, :, -1:] - a_cum)
    states = jnp.einsum('bcsgn,bcsgr,bcsgrp->bcgrpn', Bm, decay_to_end * dt5, x5)
    chunk_decay = jnp.exp(a_cum[:, :, -1])

    def step(h, inp):
        st, dec = inp
        return h * dec[..., None, None] + st, h

    h0 = jnp.zeros((B, G, R, P, N), jnp.float32)
    _, prev = lax.scan(step, h0, (jnp.moveaxis(states, 1, 0), jnp.moveaxis(chunk_decay, 1, 0)))
    prev = jnp.moveaxis(prev, 0, 1)
    y_off = jnp.einsum('bctgn,bcgrpn,bctgr->bctgrp', Cm, prev, jnp.exp(a_cum))
    y = y_diag + y_off + d_skip.astype(jnp.float32).reshape(G, R)[..., None] * x5
    y = y.reshape(B, S, D_INNER) * jax.nn.silu(z.astype(jnp.float32))
    yg = y.reshape(B, S, G, D_INNER // G)
    yg = yg * lax.rsqrt(jnp.mean(yg * yg, axis=-1, keepdims=True) + EPS)
    y = yg.reshape(B, S, D_INNER) * ssm_norm_w.astype(jnp.float32)
    return y.astype(z.dtype)


def setup_inputs(seed: int = 0) -> dict:
    key = jax.random.key(seed)
    ks = jax.random.split(key, 32)
    f32 = jnp.float32

    def nrm(k, shape, fan_in):
        return jax.random.normal(k, shape, f32) * (fan_in ** -0.5)

    def gain(k, dim):
        return 1.0 + 0.02 * jax.random.normal(k, (DEPTH, dim), f32)

    x = jax.random.normal(ks[0], (BATCH, SEQ, D_MODEL), f32)
    p = jax.random.normal(ks[1], (DEPTH, BATCH, SEQ, PLE_DIM), f32)
    start = jax.random.randint(ks[2], (BATCH, 1), 0, 16384, dtype=jnp.int32)
    positions = (start + jnp.arange(SEQ, dtype=jnp.int32)[None, :]).astype(jnp.int32)
    dt_init = jnp.exp(jax.random.uniform(ks[15], (DEPTH, SSM_HEADS), f32, np.log(1e-3), np.log(1e-1)))
    dt_bias = dt_init + jnp.log(-jnp.expm1(-dt_init))
    a_log = jnp.log(jax.random.uniform(ks[16], (DEPTH, SSM_HEADS), f32, 1.0, 16.0))
    return {
        'x': x,
        'p': p,
        'positions': positions,
        'norm_mix_w': gain(ks[3], D_MODEL),
        'w_in': nrm(ks[4], (DEPTH, D_MODEL, D_IN_PROJ), D_MODEL),
        'q_a_norm_w': gain(ks[5], Q_LORA),
        'w_uq': nrm(ks[6], (DEPTH, Q_LORA, MLA_HEADS * QK_DIM), Q_LORA),
        'kv_a_norm_w': gain(ks[7], KV_LORA),
        'w_ukv': nrm(ks[8], (DEPTH, KV_LORA, MLA_HEADS * (QK_NOPE + V_DIM)), KV_LORA),
        'q_norm_w': gain(ks[9], QK_DIM),
        'k_norm_w': gain(ks[10], QK_DIM),
        'w_o_mla': nrm(ks[11], (DEPTH, MLA_HEADS * V_DIM, D_MODEL), MLA_HEADS * V_DIM),
        'conv_w': nrm(ks[12], (DEPTH, CONV_WIDTH, CONV_DIM), CONV_WIDTH),
        'conv_b': 0.01 * jax.random.normal(ks[13], (DEPTH, CONV_DIM), f32),
        'dt_bias': dt_bias,
        'a_log': a_log,
        'd_skip': 1.0 + 0.1 * jax.random.normal(ks[17], (DEPTH, SSM_HEADS), f32),
        'ssm_norm_w': gain(ks[18], D_INNER),
        'w_o_ssm': nrm(ks[19], (DEPTH, D_INNER, D_MODEL), D_INNER),
        'w_out': nrm(ks[20], (DEPTH, D_MODEL, D_MODEL), D_MODEL),
        'norm_mlp_w': gain(ks[21], D_MODEL),
        'w_up': nrm(ks[22], (DEPTH, D_MODEL, D_FF), D_MODEL),
        'w_down': nrm(ks[23], (DEPTH, D_FF, D_MODEL), D_FF),
        'ple_norm_w': gain(ks[24], D_MODEL),
        'w_ple_gate': nrm(ks[25], (DEPTH, D_MODEL, D_MODEL), D_MODEL),
        'w_ple': nrm(ks[26], (DEPTH, PLE_DIM, D_MODEL), PLE_DIM),
    }


def reference(x, p, positions, norm_mix_w, w_in, q_a_norm_w, w_uq, kv_a_norm_w, w_ukv,
              q_norm_w, k_norm_w, w_o_mla, conv_w, conv_b, dt_bias, a_log, d_skip,
              ssm_norm_w, w_o_ssm, w_out, norm_mlp_w, w_up, w_down, ple_norm_w,
              w_ple_gate, w_ple):
    B, S, _ = x.shape
    cos, sin = rope_tables(positions)
    for i in range(DEPTH):
        h = rms_norm(x, norm_mix_w[i])
        proj = h @ w_in[i]
        c_q, c_kv, k_r, z, xbc, dt_raw, gate_logits = jnp.split(proj, SPLIT_POINTS, axis=-1)
        y_a = mla_branch(c_q, c_kv, k_r, cos, sin, q_a_norm_w[i], w_uq[i], kv_a_norm_w[i],
                         w_ukv[i], q_norm_w[i], k_norm_w[i]) @ w_o_mla[i]
        y_b = ssd_branch(xbc, z, dt_raw, conv_w[i], conv_b[i], dt_bias[i], a_log[i],
                         d_skip[i], ssm_norm_w[i]) @ w_o_ssm[i]
        g = jax.nn.sigmoid(gate_logits.astype(jnp.float32)).reshape(B, S, N_BRANCH, D_MODEL)
        merged = (g[:, :, 0] * y_a + g[:, :, 1] * y_b).astype(x.dtype)
        x = x + (merged @ w_out[i]).astype(x.dtype)
        h2 = rms_norm(x, norm_mlp_w[i])
        x = x + (jnp.square(jax.nn.relu(h2 @ w_up[i])) @ w_down[i]).astype(x.dtype)
        ple_gate = jax.nn.sigmoid((rms_norm(x, ple_norm_w[i]) @ w_ple_gate[i]).astype(jnp.float32))
        x = x + ((p[i] @ w_ple[i]) * ple_gate).astype(x.dtype)
    return x
```

```python
import functools

import jax
import jax.numpy as jnp
from jax import lax
from jax.experimental import pallas as pl
from jax.experimental.pallas import tpu as pltpu

F32 = jnp.float32
BF16 = jnp.bfloat16

D_MODEL = 2048
DEPTH = 4
CHUNK = 64
PLE_DIM = 256
EPS = 1e-6
MLA_HEADS = 16
Q_LORA = 512
KV_LORA = 512
QK_NOPE = 128
QK_ROPE = 64
V_DIM = 128
QK_DIM = QK_NOPE + QK_ROPE
ROPE_THETA = 10000.0
D_INNER = 2 * D_MODEL
SSM_HEADDIM = 64
SSM_HEADS = D_INNER // SSM_HEADDIM
SSM_GROUPS = 8
SSM_STATE = 128
GROUP_W = D_INNER // SSM_GROUPS
CONV_WIDTH = 4
D_FF = 4 * D_MODEL

LANES = 128
HEAD_W = 2 * LANES
SMALL_N = Q_LORA + KV_LORA + 2 * LANES
BIG_N = D_INNER + (D_INNER + 2 * SSM_GROUPS * SSM_STATE) + 2 * D_MODEL
VMEM_LIMIT = 56 * 1024 * 1024


def _cparams(sem):
    return pltpu.CompilerParams(dimension_semantics=sem, vmem_limit_bytes=VMEM_LIMIT)


def _rms(xf, w):
    ms = jnp.mean(xf * xf, axis=-1, keepdims=True)
    return xf * lax.rsqrt(ms + EPS) * w


def _dot(a, b):
    return jnp.dot(a, b, preferred_element_type=F32)


def _dot_nt(a, b):
    return lax.dot_general(a, b, (((1,), (1,)), ((), ())), preferred_element_type=F32)


def _rope_body(pos_ref, freq_ref, sign_ref, ct_ref, st_ref):
    ang = pos_ref[...].astype(F32) * freq_ref[...]
    ct_ref[...] = jnp.cos(ang)
    st_ref[...] = jnp.sin(ang) * sign_ref[...]


def _rope_tables(pos_col, freq, sign, tm):
    m = pos_col.shape[0]
    return pl.pallas_call(
        _rope_body,
        grid=(m // tm,),
        in_specs=[pl.BlockSpec((tm, 1), lambda i: (i, 0)),
                  pl.BlockSpec((1, LANES), lambda i: (0, 0)),
                  pl.BlockSpec((1, LANES), lambda i: (0, 0))],
        out_specs=[pl.BlockSpec((tm, LANES), lambda i: (i, 0)),
                   pl.BlockSpec((tm, LANES), lambda i: (i, 0))],
        out_shape=[jax.ShapeDtypeStruct((m, LANES), F32)] * 2,
        compiler_params=_cparams(("parallel",)),
        name="rope_tables",
    )(pos_col, freq, sign)


def _norm_mm_body(x_ref, nw_ref, w_ref, o_ref, h_ref):
    @pl.when(pl.program_id(1) == 0)
    def _():
        h_ref[...] = _rms(x_ref[...], nw_ref[...]).astype(BF16)

    o_ref[...] = _dot(h_ref[...], w_ref[...]).astype(o_ref.dtype)


def _norm_matmul(x, nw, w, out_dtype, tm, tn, name):
    m, k = x.shape
    n = w.shape[1]
    return pl.pallas_call(
        _norm_mm_body,
        grid=(m // tm, n // tn),
        in_specs=[pl.BlockSpec((tm, k), lambda i, j: (i, 0)),
                  pl.BlockSpec((1, k), lambda i, j: (0, 0)),
                  pl.BlockSpec((k, tn), lambda i, j: (0, j))],
        out_specs=pl.BlockSpec((tm, tn), lambda i, j: (i, j)),
        out_shape=jax.ShapeDtypeStruct((m, n), out_dtype),
        scratch_shapes=[pltpu.VMEM((tm, k), BF16)],
        compiler_params=_cparams(("parallel", "arbitrary")),
        name=name,
    )(x, nw, w)


def _rope_apply(v, c, s):
    return v * c + pltpu.roll(v, 64, 1) * s


def _q_body(c_ref, nw_ref, w_ref, qnw_ref, ct_ref, st_ref, o_ref):
    cn = _rms(c_ref[...], nw_ref[...]).astype(BF16)
    acc = _dot(cn, w_ref[...])
    c = ct_ref[...]
    s = st_ref[...]
    wn = qnw_ref[:, :LANES]
    wr = qnw_ref[:, LANES:]
    scale = QK_DIM ** -0.5
    for h in range(MLA_HEADS):
        qn = acc[:, HEAD_W * h:HEAD_W * h + LANES]
        qr = acc[:, HEAD_W * h + LANES:HEAD_W * (h + 1)]
        ssq = jnp.sum(qn * qn, axis=-1, keepdims=True) + jnp.sum(qr * qr, axis=-1, keepdims=True)
        rs = lax.rsqrt(ssq / QK_DIM + EPS)
        o_ref[:, HEAD_W * h:HEAD_W * h + LANES] = (qn * rs * wn * scale).astype(BF16)
        o_ref[:, HEAD_W * h + LANES:HEAD_W * (h + 1)] = (_rope_apply(qr * rs * wr, c, s) * scale).astype(BF16)


def _q_proj(small, nw, wq, qnw, ct, st, tm):
    m = small.shape[0]
    n = MLA_HEADS * HEAD_W
    return pl.pallas_call(
        _q_body,
        grid=(m // tm,),
        in_specs=[pl.BlockSpec((tm, Q_LORA), lambda i: (i, 0)),
                  pl.BlockSpec((1, Q_LORA), lambda i: (0, 0)),
                  pl.BlockSpec((Q_LORA, n), lambda i: (0, 0)),
                  pl.BlockSpec((1, HEAD_W), lambda i: (0, 0)),
                  pl.BlockSpec((tm, LANES), lambda i: (i, 0)),
                  pl.BlockSpec((tm, LANES), lambda i: (i, 0))],
        out_specs=pl.BlockSpec((tm, n), lambda i: (i, 0)),
        out_shape=jax.ShapeDtypeStruct((m, n), BF16),
        compiler_params=_cparams(("parallel",)),
        name="q_proj",
    )(small, nw, wq, qnw, ct, st)


def _kv_body(c_ref, kr_ref, nw_ref, w_ref, kwn_ref, kwr_ref, ct_ref, st_ref, k_ref, v_ref):
    cn = _rms(c_ref[...], nw_ref[...]).astype(BF16)
    acc = _dot(cn, w_ref[...])
    c = ct_ref[...]
    s = st_ref[...]
    kr = kr_ref[...]
    wn = kwn_ref[...]
    wr = kwr_ref[...]
    ssr = jnp.sum(kr * kr, axis=-1, keepdims=True)
    for h in range(MLA_HEADS):
        kn = acc[:, LANES * h:LANES * (h + 1)]
        ssq = jnp.sum(kn * kn, axis=-1, keepdims=True) + ssr
        rs = lax.rsqrt(ssq / QK_DIM + EPS)
        k_ref[:, HEAD_W * h:HEAD_W * h + LANES] = (kn * rs * wn).astype(BF16)
        k_ref[:, HEAD_W * h + LANES:HEAD_W * (h + 1)] = _rope_apply(kr * rs * wr, c, s).astype(BF16)
    v_ref[...] = acc[:, MLA_HEADS * LANES:].astype(BF16)


def _kv_proj(small, nw, wkv, kwn, kwr, ct, st, tm):
    m = small.shape[0]
    nk = MLA_HEADS * HEAD_W
    nv = MLA_HEADS * V_DIM
    kr_blk = (Q_LORA + KV_LORA) // LANES
    return pl.pallas_call(
        _kv_body,
        grid=(m // tm,),
        in_specs=[pl.BlockSpec((tm, KV_LORA), lambda i: (i, 1)),
                  pl.BlockSpec((tm, LANES), lambda i: (i, kr_blk)),
                  pl.BlockSpec((1, KV_LORA), lambda i: (0, 0)),
                  pl.BlockSpec((KV_LORA, MLA_HEADS * (QK_NOPE + V_DIM)), lambda i: (0, 0)),
                  pl.BlockSpec((1, LANES), lambda i: (0, 0)),
                  pl.BlockSpec((1, LANES), lambda i: (0, 0)),
                  pl.BlockSpec((tm, LANES), lambda i: (i, 0)),
                  pl.BlockSpec((tm, LANES), lambda i: (i, 0))],
        out_specs=[pl.BlockSpec((tm, nk), lambda i: (i, 0)),
                   pl.BlockSpec((tm, nv), lambda i: (i, 0))],
        out_shape=[jax.ShapeDtypeStruct((m, nk), BF16), jax.ShapeDtypeStruct((m, nv), BF16)],
        compiler_params=_cparams(("parallel",)),
        name="kv_proj",
    )(small, small, nw, wkv, kwn, kwr, ct, st)


def _attn_body(q_ref, k_ref, v_ref, o_ref, *, tq):
    qi = pl.program_id(2)
    q = q_ref[...]

    def step(k, v, carry, mask):
        m, l, acc = carry
        s = _dot_nt(q, k)
        if mask is not None:
            s = jnp.where(mask, s, -jnp.inf)
        m_new = jnp.maximum(m, jnp.max(s, axis=-1, keepdims=True))
        p = jnp.exp(s - m_new)
        alpha = jnp.exp(m - m_new)
        l = alpha * l + jnp.sum(p, axis=-1, keepdims=True)
        acc = alpha * acc + _dot(p.astype(BF16), v)
        return m_new, l, acc

    def body(kb, carry):
        off = pl.multiple_of(kb * tq, tq)
        return step(k_ref[pl.ds(off, tq), :], v_ref[pl.ds(off, tq), :], carry, None)

    init = (jnp.full((tq, 1), -jnp.inf, F32), jnp.zeros((tq, 1), F32), jnp.zeros((tq, V_DIM), F32))
    carry = lax.fori_loop(0, qi, body, init)
    off = pl.multiple_of(qi * tq, tq)
    row = lax.broadcasted_iota(jnp.int32, (tq, tq), 0) // CHUNK
    col = lax.broadcasted_iota(jnp.int32, (tq, tq), 1) // CHUNK
    _, l, acc = step(k_ref[pl.ds(off, tq), :], v_ref[pl.ds(off, tq), :], carry, col <= row)
    o_ref[...] = (acc / l).astype(BF16)


def _attention(q, k, v, batch, seq, tq):
    m = q.shape[0]
    nq = seq // tq
    return pl.pallas_call(
        functools.partial(_attn_body, tq=tq),
        grid=(batch, MLA_HEADS, nq),
        in_specs=[pl.BlockSpec((tq, HEAD_W), lambda b, h, i: (b * nq + i, h)),
                  pl.BlockSpec((seq, HEAD_W), lambda b, h, i: (b, h)),
                  pl.BlockSpec((seq, V_DIM), lambda b, h, i: (b, h))],
        out_specs=pl.BlockSpec((tq, V_DIM), lambda b, h, i: (b * nq + i, h)),
        out_shape=jax.ShapeDtypeStruct((m, MLA_HEADS * V_DIM), BF16),
        compiler_params=_cparams(("parallel", "parallel", "arbitrary")),
        name="attention",
    )(q, k, v)


def _split3(v):
    hi = v.astype(BF16)
    r = v - hi.astype(F32)
    mid = r.astype(BF16)
    lo = (r - mid.astype(F32)).astype(BF16)
    return hi, mid, lo


def _ssd_body(z_ref, x_ref, bc_ref, dt_ref, cwx_ref, cwb_ref, cbx_ref, cbb_ref, dtb_ref, alog_ref,
              dsk_ref, nw_ref, sel_ref, eye_ref, tri_ref, tri64_ref, y_ref, state_ref, xbuf, bcbuf):
    t = CHUNK

    @pl.when(pl.program_id(1) == 0)
    def _():
        state_ref[...] = jnp.zeros_like(state_ref)
        xbuf[0:8, :] = jnp.zeros((8, xbuf.shape[1]), F32)
        bcbuf[0:8, :] = jnp.zeros((8, bcbuf.shape[1]), F32)

    def conv_silu(raw_ref, buf, w_ref, b_ref):
        buf[8:8 + t, :] = raw_ref[...].astype(F32)
        base = 8 - (CONV_WIDTH - 1)
        y = buf[base:base + t, :] * w_ref[0:1, :]
        for tap in range(1, CONV_WIDTH):
            y = y + buf[base + tap:base + tap + t, :] * w_ref[tap:tap + 1, :]
        y = y + b_ref[...]
        buf[0:8, :] = buf[t:t + 8, :]
        return y * jax.nn.sigmoid(y)

    xs = conv_silu(x_ref, xbuf, cwx_ref, cbx_ref)
    bc = conv_silu(bc_ref, bcbuf, cwb_ref, cbb_ref)
    gn = SSM_GROUPS * SSM_STATE

    dt = jax.nn.softplus(dt_ref[...] + dtb_ref[...])
    da = dt * (-jnp.exp(alog_ref[...]))
    tri64 = tri64_ref[...]
    h3 = _split3(da)
    a_cum = _dot(tri64, h3[0]) + _dot(tri64, h3[1]) + _dot(tri64, h3[2])
    sel = sel_ref[...]
    s3 = _split3(jnp.concatenate([a_cum, dt], axis=0))
    ex = _dot(s3[0], sel) + _dot(s3[1], sel) + _dot(s3[2], sel)
    a_col = ex[:t]
    dt_col = ex[t:]
    a_row = jnp.sum(a_col * eye_ref[...], axis=0, keepdims=True)
    decay = jnp.exp(jnp.where(tri_ref[...] > 0, a_col - a_row, -jnp.inf))
    a_last = a_col[t - 1:t, :]
    ea = jnp.exp(a_col)
    chunk_decay = jnp.exp(a_last)
    xdt = xs * dt_col
    xdt_b = xdt.astype(BF16)
    xw_b = (xdt * jnp.exp(a_last - a_col)).astype(BF16)
    left = lax.broadcasted_iota(jnp.int32, (t, LANES), 1) < SSM_HEADDIM
    zero_b = jnp.zeros((t, LANES), BF16)

    for g in range(SSM_GROUPS):
        lo, hi = GROUP_W * g, GROUP_W * (g + 1)
        bg = bc[:, SSM_STATE * g:SSM_STATE * (g + 1)]
        cg_b = bc[:, gn + SSM_STATE * g:gn + SSM_STATE * (g + 1)].astype(BF16)
        bg_b = bg.astype(BF16)
        cb2 = _dot_nt(cg_b, jnp.concatenate([bg_b, bg_b], axis=0))
        yd = []
        for j in range(GROUP_W // LANES):
            l0 = lo + LANES * j
            mm = (cb2 * decay[:, l0:l0 + LANES]).astype(BF16)
            xp = xdt_b[:, l0:l0 + LANES]
            rhs = jnp.concatenate([jnp.where(left, xp, zero_b), jnp.where(left, zero_b, xp)], axis=0)
            yd.append(_dot(mm, rhs))
        y_diag = jnp.concatenate(yd, axis=1)
        st = state_ref[g]
        y_off = _dot(cg_b, st.astype(BF16)) * ea[:, lo:hi]
        inc = _dot(bg.T.astype(BF16), xw_b[:, lo:hi])
        state_ref[g] = st * chunk_decay[:, lo:hi] + inc
        yg = y_diag + y_off + dsk_ref[:, lo:hi] * xs[:, lo:hi]
        zg = z_ref[:, lo:hi].astype(F32)
        yg = yg * (zg * jax.nn.sigmoid(zg))
        ms = jnp.mean(yg * yg, axis=-1, keepdims=True)
        y_ref[:, lo:hi] = (yg * lax.rsqrt(ms + EPS) * nw_ref[:, lo:hi]).astype(BF16)


def _ssd(big, small, cwx, cwb, cbx, cbb, dtb, alog, dsk, nw, sel, eye_t, tri_t, tri64, batch, seq):
    m = big.shape[0]
    t = CHUNK
    nc = seq // t
    bcw = 2 * SSM_GROUPS * SSM_STATE
    bc_blk = (2 * D_INNER) // bcw
    dt_blk = (SMALL_N - LANES) // LANES
    row = lambda b, c: b * nc + c
    const = lambda shape: pl.BlockSpec(shape, lambda b, c: (0,) * len(shape))
    return pl.pallas_call(
        _ssd_body,
        grid=(batch, nc),
        in_specs=[pl.BlockSpec((t, D_INNER), lambda b, c: (row(b, c), 0)),
                  pl.BlockSpec((t, D_INNER), lambda b, c: (row(b, c), 1)),
                  pl.BlockSpec((t, bcw), lambda b, c: (row(b, c), bc_blk)),
                  pl.BlockSpec((t, LANES), lambda b, c: (row(b, c), dt_blk)),
                  const((CONV_WIDTH, D_INNER)), const((CONV_WIDTH, bcw)),
                  const((1, D_INNER)), const((1, bcw)),
                  const((1, LANES)), const((1, LANES)),
                  const((1, D_INNER)), const((1, D_INNER)),
                  const((LANES, D_INNER)), const((t, D_INNER)), const((t, D_INNER)), const((t, t))],
        out_specs=pl.BlockSpec((t, D_INNER), lambda b, c: (row(b, c), 0)),
        out_shape=jax.ShapeDtypeStruct((m, D_INNER), BF16),
        scratch_shapes=[pltpu.VMEM((SSM_GROUPS, SSM_STATE, GROUP_W), F32),
                        pltpu.VMEM((t + 8, D_INNER), F32),
                        pltpu.VMEM((t + 8, bcw), F32)],
        compiler_params=_cparams(("parallel", "arbitrary")),
        name="ssd",
    )(big, big, big, small, cwx, cwb, cbx, cbb, dtb, alog, dsk, nw, sel, eye_t, tri_t, tri64)


def _merge_body(o_ref, y_ref, wa_ref, wb_ref, g0_ref, g1_ref, out_ref):
    ya = _dot(o_ref[...], wa_ref[...])
    yb = _dot(y_ref[...], wb_ref[...])
    g0 = jax.nn.sigmoid(g0_ref[...].astype(F32))
    g1 = jax.nn.sigmoid(g1_ref[...].astype(F32))
    out_ref[...] = (g0 * ya + g1 * yb).astype(out_ref.dtype)


def _merge(o, y, wa, wb, big, tm, tn):
    m = o.shape[0]
    g0_blk = (2 * D_INNER + 2 * SSM_GROUPS * SSM_STATE) // tn
    g1_blk = g0_blk + D_MODEL // tn
    return pl.pallas_call(
        _merge_body,
        grid=(m // tm, D_MODEL // tn),
        in_specs=[pl.BlockSpec((tm, o.shape[1]), lambda i, j: (i, 0)),
                  pl.BlockSpec((tm, y.shape[1]), lambda i, j: (i, 0)),
                  pl.BlockSpec((wa.shape[0], tn), lambda i, j: (0, j)),
                  pl.BlockSpec((wb.shape[0], tn), lambda i, j: (0, j)),
                  pl.BlockSpec((tm, tn), lambda i, j: (i, g0_blk + j)),
                  pl.BlockSpec((tm, tn), lambda i, j: (i, g1_blk + j))],
        out_specs=pl.BlockSpec((tm, tn), lambda i, j: (i, j)),
        out_shape=jax.ShapeDtypeStruct((m, D_MODEL), BF16),
        compiler_params=_cparams(("parallel", "arbitrary")),
        name="merge",
    )(o, y, wa, wb, big, big)


def _res_mm_body(x_ref, a_ref, w_ref, o_ref):
    o_ref[...] = x_ref[...] + _dot(a_ref[...], w_ref[...])


def _res_matmul(x, a, w, tm):
    m, n = x.shape
    k = a.shape[1]
    return pl.pallas_call(
        _res_mm_body,
        grid=(m // tm,),
        in_specs=[pl.BlockSpec((tm, n), lambda i: (i, 0)),
                  pl.BlockSpec((tm, k), lambda i: (i, 0)),
                  pl.BlockSpec((k, n), lambda i: (0, 0))],
        out_specs=pl.BlockSpec((tm, n), lambda i: (i, 0)),
        out_shape=jax.ShapeDtypeStruct((m, n), F32),
        compiler_params=_cparams(("parallel",)),
        name="out_proj",
    )(x, a, w)


def _mlp_body(x_ref, nw_ref, wu_ref, wd_ref, o_ref, h_ref, acc_ref):
    j = pl.program_id(1)

    @pl.when(j == 0)
    def _():
        h_ref[...] = _rms(x_ref[...], nw_ref[...]).astype(BF16)
        acc_ref[...] = jnp.zeros_like(acc_ref)

    u = _dot(h_ref[...], wu_ref[...])
    hid = jnp.square(jnp.maximum(u, 0.0)).astype(BF16)
    acc_ref[...] += _dot(hid, wd_ref[...])

    @pl.when(j == pl.num_programs(1) - 1)
    def _():
        o_ref[...] = x_ref[...] + acc_ref[...]


def _mlp(x, nw, wu, wd, tm, tf):
    m, d = x.shape
    f = wu.shape[1]
    return pl.pallas_call(
        _mlp_body,
        grid=(m // tm, f // tf),
        in_specs=[pl.BlockSpec((tm, d), lambda i, j: (i, 0)),
                  pl.BlockSpec((1, d), lambda i, j: (0, 0)),
                  pl.BlockSpec((d, tf), lambda i, j: (0, j)),
                  pl.BlockSpec((tf, d), lambda i, j: (j, 0))],
        out_specs=pl.BlockSpec((tm, d), lambda i, j: (i, 0)),
        out_shape=jax.ShapeDtypeStruct((m, d), F32),
        scratch_shapes=[pltpu.VMEM((tm, d), BF16), pltpu.VMEM((tm, d), F32)],
        compiler_params=_cparams(("parallel", "arbitrary")),
        name="mlp",
    )(x, nw, wu, wd)


def _ple_body(x_ref, nw_ref, wg_ref, p_ref, wp_ref, o_ref):
    x = x_ref[...]
    h = _rms(x, nw_ref[...]).astype(BF16)
    gate = jax.nn.sigmoid(_dot(h, wg_ref[...]))
    pe = _dot(p_ref[...].astype(BF16), wp_ref[...])
    o_ref[...] = x + pe * gate


def _ple(x, nw, wg, p, wp, tm):
    m, d = x.shape
    return pl.pallas_call(
        _ple_body,
        grid=(m // tm,),
        in_specs=[pl.BlockSpec((tm, d), lambda i: (i, 0)),
                  pl.BlockSpec((1, d), lambda i: (0, 0)),
                  pl.BlockSpec((d, d), lambda i: (0, 0)),
                  pl.BlockSpec((tm, PLE_DIM), lambda i: (i, 0)),
                  pl.BlockSpec((PLE_DIM, d), lambda i: (0, 0))],
        out_specs=pl.BlockSpec((tm, d), lambda i: (i, 0)),
        out_shape=jax.ShapeDtypeStruct((m, d), F32),
        compiler_params=_cparams(("parallel",)),
        name="ple",
    )(x, nw, wg, p, wp)


def _rope_cols(w):
    half = QK_ROPE // 2
    z = jnp.zeros(w.shape[:-1] + (LANES // 2 - half,), w.dtype)
    return jnp.concatenate([w[..., :half], z, w[..., half:], z], axis=-1)


def _row(v):
    return v.reshape(1, -1).astype(F32)


def kernel(x, p, positions, norm_mix_w, w_in, q_a_norm_w, w_uq, kv_a_norm_w, w_ukv, q_norm_w, k_norm_w,
           w_o_mla, conv_w, conv_b, dt_bias, a_log, d_skip, ssm_norm_w, w_o_ssm, w_out, norm_mlp_w, w_up,
           w_down, ple_norm_w, w_ple_gate, w_ple):
    batch, seq, d = x.shape
    m = batch * seq
    depth = w_in.shape[0]
    tm_big = min(1024, m)
    tm_mid = min(512, m)
    tm_small = min(256, m)
    tq = min(512, seq)

    inv_freq = 1.0 / (ROPE_THETA ** (jnp.arange(0, QK_ROPE, 2, dtype=F32) / QK_ROPE))
    freq = _rope_cols(jnp.concatenate([inv_freq, inv_freq]))[None, :]
    sign = _rope_cols(jnp.concatenate([-jnp.ones_like(inv_freq), jnp.ones_like(inv_freq)]))[None, :]
    ct, st = _rope_tables(positions.reshape(m, 1), freq, sign, tm_big)

    lane = jnp.arange(D_INNER)
    sel = (jnp.arange(LANES)[:, None] == (lane // SSM_HEADDIM)[None, :]).astype(BF16)
    rows = jnp.arange(CHUNK)[:, None]
    eye_t = (rows == (lane % SSM_HEADDIM)[None, :]).astype(F32)
    tri_t = (rows >= (lane % SSM_HEADDIM)[None, :]).astype(F32)
    tri64 = (rows >= jnp.arange(CHUNK)[None, :]).astype(BF16)

    s0 = Q_LORA + KV_LORA + QK_ROPE
    s1 = s0 + D_INNER + (D_INNER + 2 * SSM_GROUPS * SSM_STATE)
    s2 = s1 + SSM_HEADS

    xf = x.reshape(m, d)
    for i in range(depth):
        wi = w_in[i]
        w_small = jnp.concatenate(
            [wi[:, :Q_LORA + KV_LORA], _rope_cols(wi[:, Q_LORA + KV_LORA:s0]), wi[:, s1:s2],
             jnp.zeros((d, LANES - SSM_HEADS), F32)], axis=1).astype(BF16)
        w_big = jnp.concatenate([wi[:, s0:s1], wi[:, s2:]], axis=1).astype(BF16)
        wq3 = w_uq[i].reshape(Q_LORA, MLA_HEADS, QK_DIM)
        wq = jnp.concatenate([wq3[..., :QK_NOPE], _rope_cols(wq3[..., QK_NOPE:])], axis=-1)
        wq = wq.reshape(Q_LORA, MLA_HEADS * HEAD_W).astype(BF16)
        qnw = jnp.concatenate([q_norm_w[i, :QK_NOPE], _rope_cols(q_norm_w[i, QK_NOPE:])])[None, :]
        wkv3 = w_ukv[i].reshape(KV_LORA, MLA_HEADS, QK_NOPE + V_DIM)
        wkv = jnp.concatenate([wkv3[..., :QK_NOPE].reshape(KV_LORA, -1),
                               wkv3[..., QK_NOPE:].reshape(KV_LORA, -1)], axis=1).astype(BF16)
        kwn = k_norm_w[i, :QK_NOPE][None, :]
        kwr = _rope_cols(k_norm_w[i, QK_NOPE:])[None, :]
        pad_h = jnp.zeros((LANES - SSM_HEADS,), F32)
        dtb = jnp.concatenate([dt_bias[i], pad_h])[None, :]
        alog = jnp.concatenate([a_log[i], pad_h])[None, :]
        dsk = jnp.repeat(d_skip[i], SSM_HEADDIM)[None, :]

        small = _norm_matmul(xf, _row(norm_mix_w[i]), w_small, F32, tm_mid, SMALL_N, "in_proj_small")
        big = _norm_matmul(xf, _row(norm_mix_w[i]), w_big, BF16, tm_big, 1024, "in_proj_big")
        q = _q_proj(small, _row(q_a_norm_w[i]), wq, qnw, ct, st, tm_small)
        k, v = _kv_proj(small, _row(kv_a_norm_w[i]), wkv, kwn, kwr, ct, st, tm_small)
        o = _attention(q, k, v, batch, seq, tq)
        y = _ssd(big, small, conv_w[i][:, :D_INNER], conv_w[i][:, D_INNER:], conv_b[i][None, :D_INNER],
                 conv_b[i][None, D_INNER:], dtb, alog, dsk, _row(ssm_norm_w[i]), sel, eye_t, tri_t, tri64,
                 batch, seq)
        merged = _merge(o, y, w_o_mla[i].astype(BF16), w_o_ssm[i].astype(BF16), big, tm_mid, 512)
        xf = _res_matmul(xf, merged, w_out[i].astype(BF16), tm_mid)
        xf = _mlp(xf, _row(norm_mlp_w[i]), w_up[i].astype(BF16), w_down[i].astype(BF16), tm_mid, 512)
        xf = _ple(xf, _row(ple_norm_w[i]), w_ple_gate[i].astype(BF16), p[i].reshape(m, PLE_DIM),
                  w_ple[i].astype(BF16), tm_mid)
    return xf.reshape(batch, seq, d)
```

```python
import functools

import jax
import jax.numpy as jnp
from jax import lax
from jax.experimental import pallas as pl
from jax.experimental.pallas import tpu as pltpu

F32 = jnp.float32
BF16 = jnp.bfloat16

D_MODEL = 2048
DEPTH = 4
CHUNK = 64
PLE_DIM = 256
EPS = 1e-6
MLA_HEADS = 16
Q_LORA = 512
KV_LORA = 512
QK_NOPE = 128
QK_ROPE = 64
V_DIM = 128
QK_DIM = QK_NOPE + QK_ROPE
ROPE_THETA = 10000.0
D_INNER = 2 * D_MODEL
SSM_HEADDIM = 64
SSM_HEADS = D_INNER // SSM_HEADDIM
SSM_GROUPS = 8
SSM_STATE = 128
GROUP_W = D_INNER // SSM_GROUPS
CONV_WIDTH = 4
D_FF = 4 * D_MODEL

LOG2_E = 1.4426950408889634
LANES = 128
HEAD_W = 2 * LANES
SMALL_N = Q_LORA + KV_LORA + 2 * LANES
BIG_N = D_INNER + (D_INNER + 2 * SSM_GROUPS * SSM_STATE) + 2 * D_MODEL
VMEM_LIMIT = 56 * 1024 * 1024


def _cparams(sem):
    return pltpu.CompilerParams(dimension_semantics=sem, vmem_limit_bytes=VMEM_LIMIT)


def _rms(xf, w):
    ms = jnp.mean(xf * xf, axis=-1, keepdims=True)
    return xf * lax.rsqrt(ms + EPS) * w


def _dot(a, b):
    return jnp.dot(a, b, preferred_element_type=F32)


def _dot_nt(a, b):
    return lax.dot_general(a, b, (((1,), (1,)), ((), ())), preferred_element_type=F32)


def _rope_body(pos_ref, freq_ref, sign_ref, ct_ref, st_ref):
    ang = pos_ref[...].astype(F32) * freq_ref[...]
    ct_ref[...] = jnp.cos(ang)
    st_ref[...] = jnp.sin(ang) * sign_ref[...]


def _rope_tables(pos_col, freq, sign, tm):
    m = pos_col.shape[0]
    return pl.pallas_call(
        _rope_body,
        grid=(m // tm,),
        in_specs=[pl.BlockSpec((tm, 1), lambda i: (i, 0)),
                  pl.BlockSpec((1, LANES), lambda i: (0, 0)),
                  pl.BlockSpec((1, LANES), lambda i: (0, 0))],
        out_specs=[pl.BlockSpec((tm, LANES), lambda i: (i, 0)),
                   pl.BlockSpec((tm, LANES), lambda i: (i, 0))],
        out_shape=[jax.ShapeDtypeStruct((m, LANES), F32)] * 2,
        compiler_params=_cparams(("parallel",)),
        name="rope_tables",
    )(pos_col, freq, sign)


def _norm_mm_body(x_ref, nw_ref, w_ref, o_ref, h_ref):
    @pl.when(pl.program_id(1) == 0)
    def _():
        h_ref[...] = _rms(x_ref[...], nw_ref[...]).astype(BF16)

    o_ref[...] = _dot(h_ref[...], w_ref[...]).astype(o_ref.dtype)


def _norm_matmul(x, nw, w, out_dtype, tm, tn, name):
    m, k = x.shape
    n = w.shape[1]
    return pl.pallas_call(
        _norm_mm_body,
        grid=(m // tm, n // tn),
        in_specs=[pl.BlockSpec((tm, k), lambda i, j: (i, 0)),
                  pl.BlockSpec((1, k), lambda i, j: (0, 0)),
                  pl.BlockSpec((k, tn), lambda i, j: (0, j))],
        out_specs=pl.BlockSpec((tm, tn), lambda i, j: (i, j)),
        out_shape=jax.ShapeDtypeStruct((m, n), out_dtype),
        scratch_shapes=[pltpu.VMEM((tm, k), BF16)],
        compiler_params=_cparams(("parallel", "arbitrary")),
        name=name,
    )(x, nw, w)


def _rope_apply(v, c, s):
    return v * c + pltpu.roll(v, 64, 1) * s


def _q_body(c_ref, nw_ref, w_ref, qnw_ref, ct_ref, st_ref, o_ref):
    cn = _rms(c_ref[...], nw_ref[...]).astype(BF16)
    acc = _dot(cn, w_ref[...])
    c = ct_ref[...]
    s = st_ref[...]
    wn = qnw_ref[:, :LANES]
    wr = qnw_ref[:, LANES:]
    scale = QK_DIM ** -0.5 * LOG2_E
    for h in range(MLA_HEADS):
        qn = acc[:, HEAD_W * h:HEAD_W * h + LANES]
        qr = acc[:, HEAD_W * h + LANES:HEAD_W * (h + 1)]
        ssq = jnp.sum(qn * qn, axis=-1, keepdims=True) + jnp.sum(qr * qr, axis=-1, keepdims=True)
        rs = lax.rsqrt(ssq / QK_DIM + EPS)
        o_ref[:, HEAD_W * h:HEAD_W * h + LANES] = (qn * rs * wn * scale).astype(BF16)
        o_ref[:, HEAD_W * h + LANES:HEAD_W * (h + 1)] = (_rope_apply(qr * rs * wr, c, s) * scale).astype(BF16)


def _q_proj(small, nw, wq, qnw, ct, st, tm):
    m = small.shape[0]
    n = MLA_HEADS * HEAD_W
    return pl.pallas_call(
        _q_body,
        grid=(m // tm,),
        in_specs=[pl.BlockSpec((tm, Q_LORA), lambda i: (i, 0)),
                  pl.BlockSpec((1, Q_LORA), lambda i: (0, 0)),
                  pl.BlockSpec((Q_LORA, n), lambda i: (0, 0)),
                  pl.BlockSpec((1, HEAD_W), lambda i: (0, 0)),
                  pl.BlockSpec((tm, LANES), lambda i: (i, 0)),
                  pl.BlockSpec((tm, LANES), lambda i: (i, 0))],
        out_specs=pl.BlockSpec((tm, n), lambda i: (i, 0)),
        out_shape=jax.ShapeDtypeStruct((m, n), BF16),
        compiler_params=_cparams(("parallel",)),
        name="q_proj",
    )(small, nw, wq, qnw, ct, st)


def _kv_body(c_ref, kr_ref, nw_ref, wk_ref, wvt_ref, kwn_ref, kwr_ref, ct_ref, st_ref, k_ref, vt_ref):
    cn = _rms(c_ref[...], nw_ref[...]).astype(BF16)
    acc = _dot(cn, wk_ref[...])
    vt_ref[0] = _dot_nt(wvt_ref[...], cn).astype(BF16)
    c = ct_ref[...]
    s = st_ref[...]
    kr = kr_ref[...]
    wn = kwn_ref[...]
    wr = kwr_ref[...]
    ssr = jnp.sum(kr * kr, axis=-1, keepdims=True)
    for h in range(MLA_HEADS):
        kn = acc[:, LANES * h:LANES * (h + 1)]
        ssq = jnp.sum(kn * kn, axis=-1, keepdims=True) + ssr
        rs = lax.rsqrt(ssq / QK_DIM + EPS)
        k_ref[:, HEAD_W * h:HEAD_W * h + LANES] = (kn * rs * wn).astype(BF16)
        k_ref[:, HEAD_W * h + LANES:HEAD_W * (h + 1)] = _rope_apply(kr * rs * wr, c, s).astype(BF16)


def _kv_proj(small, nw, wk, wvt, kwn, kwr, ct, st, tm):
    m = small.shape[0]
    nk = MLA_HEADS * HEAD_W
    nv = MLA_HEADS * V_DIM
    kr_blk = (Q_LORA + KV_LORA) // LANES
    return pl.pallas_call(
        _kv_body,
        grid=(m // tm,),
        in_specs=[pl.BlockSpec((tm, KV_LORA), lambda i: (i, 1)),
                  pl.BlockSpec((tm, LANES), lambda i: (i, kr_blk)),
                  pl.BlockSpec((1, KV_LORA), lambda i: (0, 0)),
                  pl.BlockSpec((KV_LORA, MLA_HEADS * QK_NOPE), lambda i: (0, 0)),
                  pl.BlockSpec((nv, KV_LORA), lambda i: (0, 0)),
                  pl.BlockSpec((1, LANES), lambda i: (0, 0)),
                  pl.BlockSpec((1, LANES), lambda i: (0, 0)),
                  pl.BlockSpec((tm, LANES), lambda i: (i, 0)),
                  pl.BlockSpec((tm, LANES), lambda i: (i, 0))],
        out_specs=[pl.BlockSpec((tm, nk), lambda i: (i, 0)),
                   pl.BlockSpec((1, nv, tm), lambda i: (i, 0, 0))],
        out_shape=[jax.ShapeDtypeStruct((m, nk), BF16), jax.ShapeDtypeStruct((m // tm, nv, tm), BF16)],
        compiler_params=_cparams(("parallel",)),
        name="kv_proj",
    )(small, small, nw, wk, wvt, kwn, kwr, ct, st)


def _attn_body(q_ref, k_ref, vt_ref, o_ref, p_scr, *, tq, nq, hb):
    key_chunk = lax.broadcasted_iota(jnp.int32, (tq, tq), 0) // CHUNK
    qry_chunk = lax.broadcasted_iota(jnp.int32, (tq, tq), 1) // CHUNK
    diag_mask = key_chunk <= qry_chunk

    def scores(qs, kb, hh):
        koff = pl.multiple_of(kb * tq, tq)
        return _dot_nt(k_ref[pl.ds(koff, tq), HEAD_W * hh:HEAD_W * (hh + 1)], qs[hh])

    def pv(kb, hh):
        return _dot(vt_ref[kb, V_DIM * hh:V_DIM * (hh + 1), :], p_scr[hh])

    def softmax_update(s, m, l, acc, hh):
        m_new = jnp.maximum(m, jnp.max(s, axis=0, keepdims=True))
        p = jnp.exp2(s - m_new)
        alpha = jnp.exp2(m - m_new)
        p_scr[hh] = p.astype(BF16)
        return m_new, alpha * l + jnp.sum(p, axis=0, keepdims=True), alpha * acc

    def q_block(qi, c):
        qoff = pl.multiple_of(qi * tq, tq)
        qs = [q_ref[pl.ds(qoff, tq), HEAD_W * hh:HEAD_W * (hh + 1)] for hh in range(hb)]
        carries = []
        for hh in range(hb):
            s = jnp.where(diag_mask, scores(qs, qi, hh), -jnp.inf)
            carries.append(softmax_update(s, jnp.full((1, tq), -jnp.inf, F32), jnp.zeros((1, tq), F32),
                                          jnp.zeros((V_DIM, tq), F32), hh))

        def body(j, carries):
            prev = jnp.where(j == 0, qi, j - 1)
            ss = [scores(qs, j, hh) for hh in range(hb)]
            pvs = [pv(prev, hh) for hh in range(hb)]
            return tuple(softmax_update(ss[hh], carries[hh][0], carries[hh][1], carries[hh][2] + pvs[hh], hh)
                         for hh in range(hb))

        carries = lax.fori_loop(0, qi, body, tuple(carries))
        last = jnp.maximum(qi - 1, 0)
        for hh in range(hb):
            _, l, acc = carries[hh]
            o_ref[pl.ds(qoff, tq), V_DIM * hh:V_DIM * (hh + 1)] = ((acc + pv(last, hh)) / l).T.astype(BF16)
        return c

    lax.fori_loop(0, nq, q_block, 0)


def _attention(q, k, vt, batch, seq, tq, hb):
    m = q.shape[0]
    nq = seq // tq
    return pl.pallas_call(
        functools.partial(_attn_body, tq=tq, nq=nq, hb=hb),
        grid=(batch, MLA_HEADS // hb),
        in_specs=[pl.BlockSpec((seq, hb * HEAD_W), lambda b, h: (b, h)),
                  pl.BlockSpec((seq, hb * HEAD_W), lambda b, h: (b, h)),
                  pl.BlockSpec((nq, hb * V_DIM, tq), lambda b, h: (b, h, 0))],
        out_specs=pl.BlockSpec((seq, hb * V_DIM), lambda b, h: (b, h)),
        out_shape=jax.ShapeDtypeStruct((m, MLA_HEADS * V_DIM), BF16),
        scratch_shapes=[pltpu.VMEM((hb, tq, tq), BF16)],
        compiler_params=_cparams(("parallel", "arbitrary")),
        name="attention",
    )(q, k, vt)


def _split3(v):
    hi = v.astype(BF16)
    r = v - hi.astype(F32)
    mid = r.astype(BF16)
    lo = (r - mid.astype(F32)).astype(BF16)
    return hi, mid, lo


def _ssd_body(z_ref, x_ref, bc_ref, dt_ref, cwx_ref, cwb_ref, cbx_ref, cbb_ref, dtb_ref, alog_ref,
              dsk_ref, nw_ref, sel_ref, eye_ref, tri_ref, tri64_ref, y_ref, state_ref, xhalo, bchalo):
    t = CHUNK

    @pl.when(pl.program_id(1) == 0)
    def _():
        state_ref[...] = jnp.zeros_like(state_ref)
        xhalo[...] = jnp.zeros_like(xhalo)
        bchalo[...] = jnp.zeros_like(bchalo)

    def conv_silu(raw_ref, halo, w_ref, b_ref):
        cur = raw_ref[...].astype(F32)
        width = cur.shape[1]
        tiles = jnp.concatenate([halo[...], cur], axis=0).reshape(t // 8 + 1, 8, width)
        halo[...] = cur[t - 8:, :]
        sub = lax.broadcasted_iota(jnp.int32, (1, 8, width), 1)
        y = None
        for tap in range(CONV_WIDTH):
            shift = CONV_WIDTH - 1 - tap
            if shift == 0:
                xt = cur
            else:
                r = pltpu.roll(tiles, shift, 1)
                xt = jnp.where(sub < shift, r[:-1], r[1:]).reshape(t, width)
            term = xt * w_ref[tap:tap + 1, :]
            y = term if y is None else y + term
        y = y + b_ref[...]
        return y * jax.nn.sigmoid(y)

    xs = conv_silu(x_ref, xhalo, cwx_ref, cbx_ref)
    bc = conv_silu(bc_ref, bchalo, cwb_ref, cbb_ref)
    gn = SSM_GROUPS * SSM_STATE

    head_lane = lax.broadcasted_iota(jnp.int32, (t, LANES), 1) < SSM_HEADS
    dt = jnp.where(head_lane, jax.nn.softplus(dt_ref[...] + dtb_ref[...]), 0.0)
    da = dt * (-jnp.exp(alog_ref[...]))
    tri64 = tri64_ref[...]
    h3 = _split3(da)
    a_cum = (_dot(tri64, h3[0]) + _dot(tri64, h3[1]) + _dot(tri64, h3[2])) * LOG2_E
    ea_c = jnp.where(head_lane, jnp.exp2(a_cum), 0.0)
    dte_c = jnp.where(head_lane, jnp.exp2(a_cum[t - 1:t, :] - a_cum), 0.0)
    s3 = [part.astype(F32) for part in _split3(jnp.concatenate([a_cum, dt, ea_c, dte_c], axis=0))]
    packed = jnp.concatenate([(s3[0] + pltpu.roll(s3[1], SSM_HEADS, 1)).astype(BF16), s3[2].astype(BF16)], axis=1)
    ex = _dot(packed, sel_ref[...])
    a_col = ex[:t]
    dt_col = ex[t:2 * t]
    ea = ex[2 * t:3 * t]
    a_row = jnp.sum(jnp.where(eye_ref[...] > 0, a_col, 0.0), axis=0, keepdims=True)
    decay = jnp.exp2(jnp.where(tri_ref[...] > 0, a_col - a_row, -jnp.inf))
    chunk_decay = ea[t - 1:t, :]
    xdt = xs * dt_col
    xdt_b = xdt.astype(BF16)
    xw_b = (xdt * ex[3 * t:]).astype(BF16)
    left = lax.broadcasted_iota(jnp.int32, (t, LANES), 1) < SSM_HEADDIM
    zero_b = jnp.zeros((t, LANES), BF16)

    for g in range(SSM_GROUPS):
        lo, hi = GROUP_W * g, GROUP_W * (g + 1)
        bg = bc[:, SSM_STATE * g:SSM_STATE * (g + 1)]
        cg_b = bc[:, gn + SSM_STATE * g:gn + SSM_STATE * (g + 1)].astype(BF16)
        bg_b = bg.astype(BF16)
        cb2 = _dot_nt(cg_b, jnp.concatenate([bg_b, bg_b], axis=0))
        yd = []
        for j in range(GROUP_W // LANES):
            l0 = lo + LANES * j
            mm = (cb2 * decay[:, l0:l0 + LANES]).astype(BF16)
            xp = xdt_b[:, l0:l0 + LANES]
            rhs = jnp.concatenate([jnp.where(left, xp, zero_b), jnp.where(left, zero_b, xp)], axis=0)
            yd.append(_dot(mm, rhs))
        y_diag = jnp.concatenate(yd, axis=1)
        st = state_ref[g]
        y_off = _dot(cg_b, st.astype(BF16)) * ea[:, lo:hi]
        inc = _dot(bg.T.astype(BF16), xw_b[:, lo:hi])
        state_ref[g] = st * chunk_decay[:, lo:hi] + inc
        yg = y_diag + y_off + dsk_ref[:, lo:hi] * xs[:, lo:hi]
        zg = z_ref[:, lo:hi].astype(F32)
        yg = yg * (zg * jax.nn.sigmoid(zg))
        ms = jnp.mean(yg * yg, axis=-1, keepdims=True)
        y_ref[:, lo:hi] = (yg * lax.rsqrt(ms + EPS) * nw_ref[:, lo:hi]).astype(BF16)


def _ssd(big, small, cwx, cwb, cbx, cbb, dtb, alog, dsk, nw, sel, eye_t, tri_t, tri64, batch, seq):
    m = big.shape[0]
    t = CHUNK
    nc = seq // t
    bcw = 2 * SSM_GROUPS * SSM_STATE
    bc_blk = (2 * D_INNER) // bcw
    dt_blk = (SMALL_N - LANES) // LANES
    row = lambda b, c: b * nc + c
    const = lambda shape: pl.BlockSpec(shape, lambda b, c: (0,) * len(shape))
    return pl.pallas_call(
        _ssd_body,
        grid=(batch, nc),
        in_specs=[pl.BlockSpec((t, D_INNER), lambda b, c: (row(b, c), 0)),
                  pl.BlockSpec((t, D_INNER), lambda b, c: (row(b, c), 1)),
                  pl.BlockSpec((t, bcw), lambda b, c: (row(b, c), bc_blk)),
                  pl.BlockSpec((t, LANES), lambda b, c: (row(b, c), dt_blk)),
                  const((CONV_WIDTH, D_INNER)), const((CONV_WIDTH, bcw)),
                  const((1, D_INNER)), const((1, bcw)),
                  const((1, LANES)), const((1, LANES)),
                  const((1, D_INNER)), const((1, D_INNER)),
                  const((2 * LANES, D_INNER)), const((t, D_INNER)), const((t, D_INNER)), const((t, t))],
        out_specs=pl.BlockSpec((t, D_INNER), lambda b, c: (row(b, c), 0)),
        out_shape=jax.ShapeDtypeStruct((m, D_INNER), BF16),
        scratch_shapes=[pltpu.VMEM((SSM_GROUPS, SSM_STATE, GROUP_W), F32),
                        pltpu.VMEM((8, D_INNER), F32),
                        pltpu.VMEM((8, bcw), F32)],
        compiler_params=_cparams(("parallel", "arbitrary")),
        name="ssd",
    )(big, big, big, small, cwx, cwb, cbx, cbb, dtb, alog, dsk, nw, sel, eye_t, tri_t, tri64)


def _merge_body(o_ref, y_ref, wa_ref, wb_ref, g0_ref, g1_ref, out_ref):
    ya = _dot(o_ref[...], wa_ref[...])
    yb = _dot(y_ref[...], wb_ref[...])
    g0 = jax.nn.sigmoid(g0_ref[...].astype(F32))
    g1 = jax.nn.sigmoid(g1_ref[...].astype(F32))
    out_ref[...] = (g0 * ya + g1 * yb).astype(out_ref.dtype)


def _merge(o, y, wa, wb, big, tm, tn):
    m = o.shape[0]
    g0_blk = (2 * D_INNER + 2 * SSM_GROUPS * SSM_STATE) // tn
    g1_blk = g0_blk + D_MODEL // tn
    return pl.pallas_call(
        _merge_body,
        grid=(m // tm, D_MODEL // tn),
        in_specs=[pl.BlockSpec((tm, o.shape[1]), lambda i, j: (i, 0)),
                  pl.BlockSpec((tm, y.shape[1]), lambda i, j: (i, 0)),
                  pl.BlockSpec((wa.shape[0], tn), lambda i, j: (0, j)),
                  pl.BlockSpec((wb.shape[0], tn), lambda i, j: (0, j)),
                  pl.BlockSpec((tm, tn), lambda i, j: (i, g0_blk + j)),
                  pl.BlockSpec((tm, tn), lambda i, j: (i, g1_blk + j))],
        out_specs=pl.BlockSpec((tm, tn), lambda i, j: (i, j)),
        out_shape=jax.ShapeDtypeStruct((m, D_MODEL), BF16),
        compiler_params=_cparams(("parallel", "arbitrary")),
        name="merge",
    )(o, y, wa, wb, big, big)


def _res_mm_body(x_ref, a_ref, w_ref, o_ref):
    o_ref[...] = x_ref[...] + _dot(a_ref[...], w_ref[...])


def _res_matmul(x, a, w, tm):
    m, n = x.shape
    k = a.shape[1]
    return pl.pallas_call(
        _res_mm_body,
        grid=(m // tm,),
        in_specs=[pl.BlockSpec((tm, n), lambda i: (i, 0)),
                  pl.BlockSpec((tm, k), lambda i: (i, 0)),
                  pl.BlockSpec((k, n), lambda i: (0, 0))],
        out_specs=pl.BlockSpec((tm, n), lambda i: (i, 0)),
        out_shape=jax.ShapeDtypeStruct((m, n), F32),
        compiler_params=_cparams(("parallel",)),
        name="out_proj",
    )(x, a, w)


def _mlp_body(x_ref, nw_ref, wu_ref, wd_ref, o_ref, h_ref, acc_ref):
    j = pl.program_id(1)

    @pl.when(j == 0)
    def _():
        h_ref[...] = _rms(x_ref[...], nw_ref[...]).astype(BF16)
        acc_ref[...] = jnp.zeros_like(acc_ref)

    u = _dot(h_ref[...], wu_ref[...])
    hid = jnp.square(jnp.maximum(u, 0.0)).astype(BF16)
    acc_ref[...] += _dot(hid, wd_ref[...])

    @pl.when(j == pl.num_programs(1) - 1)
    def _():
        o_ref[...] = x_ref[...] + acc_ref[...]


def _mlp(x, nw, wu, wd, tm, tf):
    m, d = x.shape
    f = wu.shape[1]
    return pl.pallas_call(
        _mlp_body,
        grid=(m // tm, f // tf),
        in_specs=[pl.BlockSpec((tm, d), lambda i, j: (i, 0)),
                  pl.BlockSpec((1, d), lambda i, j: (0, 0)),
                  pl.BlockSpec((d, tf), lambda i, j: (0, j)),
                  pl.BlockSpec((tf, d), lambda i, j: (j, 0))],
        out_specs=pl.BlockSpec((tm, d), lambda i, j: (i, 0)),
        out_shape=jax.ShapeDtypeStruct((m, d), F32),
        scratch_shapes=[pltpu.VMEM((tm, d), BF16), pltpu.VMEM((tm, d), F32)],
        compiler_params=_cparams(("parallel", "arbitrary")),
        name="mlp",
    )(x, nw, wu, wd)


def _ple_body(x_ref, nw_ref, wg_ref, p_ref, wp_ref, o_ref):
    x = x_ref[...]
    h = _rms(x, nw_ref[...]).astype(BF16)
    gate = jax.nn.sigmoid(_dot(h, wg_ref[...]))
    pe = _dot(p_ref[...].astype(BF16), wp_ref[...])
    o_ref[...] = x + pe * gate


def _ple(x, nw, wg, p, wp, tm):
    m, d = x.shape
    return pl.pallas_call(
        _ple_body,
        grid=(m // tm,),
        in_specs=[pl.BlockSpec((tm, d), lambda i: (i, 0)),
                  pl.BlockSpec((1, d), lambda i: (0, 0)),
                  pl.BlockSpec((d, d), lambda i: (0, 0)),
                  pl.BlockSpec((tm, PLE_DIM), lambda i: (i, 0)),
                  pl.BlockSpec((PLE_DIM, d), lambda i: (0, 0))],
        out_specs=pl.BlockSpec((tm, d), lambda i: (i, 0)),
        out_shape=jax.ShapeDtypeStruct((m, d), F32),
        compiler_params=_cparams(("parallel",)),
        name="ple",
    )(x, nw, wg, p, wp)


def _rope_cols(w):
    half = QK_ROPE // 2
    z = jnp.zeros(w.shape[:-1] + (LANES // 2 - half,), w.dtype)
    return jnp.concatenate([w[..., :half], z, w[..., half:], z], axis=-1)


def _row(v):
    return v.reshape(1, -1).astype(F32)


def kernel(x, p, positions, norm_mix_w, w_in, q_a_norm_w, w_uq, kv_a_norm_w, w_ukv, q_norm_w, k_norm_w,
           w_o_mla, conv_w, conv_b, dt_bias, a_log, d_skip, ssm_norm_w, w_o_ssm, w_out, norm_mlp_w, w_up,
           w_down, ple_norm_w, w_ple_gate, w_ple):
    batch, seq, d = x.shape
    m = batch * seq
    depth = w_in.shape[0]
    tm_big = min(1024, m)
    tm_mid = min(512, m)
    tm_small = min(256, m)
    tq = min(512, seq)

    inv_freq = 1.0 / (ROPE_THETA ** (jnp.arange(0, QK_ROPE, 2, dtype=F32) / QK_ROPE))
    freq = _rope_cols(jnp.concatenate([inv_freq, inv_freq]))[None, :]
    sign = _rope_cols(jnp.concatenate([-jnp.ones_like(inv_freq), jnp.ones_like(inv_freq)]))[None, :]
    ct, st = _rope_tables(positions.reshape(m, 1), freq, sign, tm_big)

    lane = jnp.arange(D_INNER)
    sel_row = jnp.arange(2 * LANES)[:, None]
    sel = ((sel_row % SSM_HEADS == (lane // SSM_HEADDIM)[None, :]) & (sel_row < 3 * SSM_HEADS)).astype(BF16)
    rows = jnp.arange(CHUNK)[:, None]
    eye_t = (rows == (lane % SSM_HEADDIM)[None, :]).astype(F32)
    tri_t = (rows >= (lane % SSM_HEADDIM)[None, :]).astype(F32)
    tri64 = (rows >= jnp.arange(CHUNK)[None, :]).astype(BF16)

    s0 = Q_LORA + KV_LORA + QK_ROPE
    s1 = s0 + D_INNER + (D_INNER + 2 * SSM_GROUPS * SSM_STATE)
    s2 = s1 + SSM_HEADS

    xf = x.reshape(m, d)
    for i in range(depth):
        wi = w_in[i]
        w_small = jnp.concatenate(
            [wi[:, :Q_LORA + KV_LORA], _rope_cols(wi[:, Q_LORA + KV_LORA:s0]), wi[:, s1:s2],
             jnp.zeros((d, LANES - SSM_HEADS), F32)], axis=1).astype(BF16)
        w_big = jnp.concatenate([wi[:, s0:s1], wi[:, s2:]], axis=1).astype(BF16)
        wq3 = w_uq[i].reshape(Q_LORA, MLA_HEADS, QK_DIM)
        wq = jnp.concatenate([wq3[..., :QK_NOPE], _rope_cols(wq3[..., QK_NOPE:])], axis=-1)
        wq = wq.reshape(Q_LORA, MLA_HEADS * HEAD_W).astype(BF16)
        qnw = jnp.concatenate([q_norm_w[i, :QK_NOPE], _rope_cols(q_norm_w[i, QK_NOPE:])])[None, :]
        wkv3 = w_ukv[i].reshape(KV_LORA, MLA_HEADS, QK_NOPE + V_DIM)
        wk = wkv3[..., :QK_NOPE].reshape(KV_LORA, -1).astype(BF16)
        wvt = wkv3[..., QK_NOPE:].reshape(KV_LORA, -1).T.astype(BF16)
        kwn = k_norm_w[i, :QK_NOPE][None, :]
        kwr = _rope_cols(k_norm_w[i, QK_NOPE:])[None, :]
        pad_h = jnp.zeros((LANES - SSM_HEADS,), F32)
        dtb = jnp.concatenate([dt_bias[i], pad_h])[None, :]
        alog = jnp.concatenate([a_log[i], pad_h])[None, :]
        dsk = jnp.repeat(d_skip[i], SSM_HEADDIM)[None, :]

        small = _norm_matmul(xf, _row(norm_mix_w[i]), w_small, F32, tm_mid, SMALL_N, "in_proj_small")
        big = _norm_matmul(xf, _row(norm_mix_w[i]), w_big, BF16, tm_big, 1024, "in_proj_big")
        q = _q_proj(small, _row(q_a_norm_w[i]), wq, qnw, ct, st, tm_small)
        k, vt = _kv_proj(small, _row(kv_a_norm_w[i]), wk, wvt, kwn, kwr, ct, st, tq)
        o = _attention(q, k, vt, batch, seq, tq, 2)
        y = _ssd(big, small, conv_w[i][:, :D_INNER], conv_w[i][:, D_INNER:], conv_b[i][None, :D_INNER],
                 conv_b[i][None, D_INNER:], dtb, alog, dsk, _row(ssm_norm_w[i]), sel, eye_t, tri_t, tri64,
                 batch, seq)
        merged = _merge(o, y, w_o_mla[i].astype(BF16), w_o_ssm[i].astype(BF16), big, tm_mid, 512)
        xf = _res_matmul(xf, merged, w_out[i].astype(BF16), tm_mid)
        xf = _mlp(xf, _row(norm_mlp_w[i]), w_up[i].astype(BF16), w_down[i].astype(BF16), tm_mid, 512)
        xf = _ple(xf, _row(ple_norm_w[i]), w_ple_gate[i].astype(BF16), p[i].reshape(m, PLE_DIM),
                  w_ple[i].astype(BF16), tm_mid)
    return xf.reshape(batch, seq, d)
```

```python
import functools

import jax
import jax.numpy as jnp
from jax import lax
from jax.experimental import pallas as pl
from jax.experimental.pallas import tpu as pltpu

F32 = jnp.float32
BF16 = jnp.bfloat16

D_MODEL = 2048
DEPTH = 4
CHUNK = 64
PLE_DIM = 256
EPS = 1e-6
MLA_HEADS = 16
Q_LORA = 512
KV_LORA = 512
QK_NOPE = 128
QK_ROPE = 64
V_DIM = 128
QK_DIM = QK_NOPE + QK_ROPE
ROPE_THETA = 10000.0
D_INNER = 2 * D_MODEL
SSM_HEADDIM = 64
SSM_HEADS = D_INNER // SSM_HEADDIM
SSM_GROUPS = 8
SSM_STATE = 128
GROUP_W = D_INNER // SSM_GROUPS
CONV_WIDTH = 4
D_FF = 4 * D_MODEL

LOG2_E = 1.4426950408889634
LANES = 128
HEAD_W = 2 * LANES
SMALL_N = Q_LORA + KV_LORA + 2 * LANES
BIG_N = D_INNER + (D_INNER + 2 * SSM_GROUPS * SSM_STATE) + 2 * D_MODEL
VMEM_LIMIT = 56 * 1024 * 1024


def _cparams(sem):
    return pltpu.CompilerParams(dimension_semantics=sem, vmem_limit_bytes=VMEM_LIMIT)


def _rms(xf, w):
    ms = jnp.mean(xf * xf, axis=-1, keepdims=True)
    return xf * lax.rsqrt(ms + EPS) * w


def _lspec(layer, block, imap=None):
    if imap is None:
        imap = lambda *g: (0,) * len(block)
    return pl.BlockSpec((None,) + tuple(block), lambda *g: (layer,) + tuple(imap(*g)))


def _dot(a, b):
    return jnp.dot(a, b, preferred_element_type=F32)


def _dot_nt(a, b):
    return lax.dot_general(a, b, (((1,), (1,)), ((), ())), preferred_element_type=F32)


def _rope_body(pos_ref, freq_ref, sign_ref, ct_ref, st_ref):
    ang = pos_ref[...].astype(F32) * freq_ref[...]
    ct_ref[...] = jnp.cos(ang)
    st_ref[...] = jnp.sin(ang) * sign_ref[...]


def _rope_tables(pos_col, freq, sign, tm):
    m = pos_col.shape[0]
    return pl.pallas_call(
        _rope_body,
        grid=(m // tm,),
        in_specs=[pl.BlockSpec((tm, 1), lambda i: (i, 0)),
                  pl.BlockSpec((1, LANES), lambda i: (0, 0)),
                  pl.BlockSpec((1, LANES), lambda i: (0, 0))],
        out_specs=[pl.BlockSpec((tm, LANES), lambda i: (i, 0)),
                   pl.BlockSpec((tm, LANES), lambda i: (i, 0))],
        out_shape=[jax.ShapeDtypeStruct((m, LANES), F32)] * 2,
        compiler_params=_cparams(("parallel",)),
        name="rope_tables",
    )(pos_col, freq, sign)


def _norm_mm_body(x_ref, nw_ref, w_ref, o_ref, h_ref):
    @pl.when(pl.program_id(1) == 0)
    def _():
        h_ref[...] = _rms(x_ref[...], nw_ref[...]).astype(BF16)

    o_ref[...] = _dot(h_ref[...], w_ref[...]).astype(o_ref.dtype)


def _norm_matmul(x, nw, w, layer, out_dtype, tm, tn, name):
    m, k = x.shape
    n = w.shape[2]
    return pl.pallas_call(
        _norm_mm_body,
        grid=(m // tm, n // tn),
        in_specs=[pl.BlockSpec((tm, k), lambda i, j: (i, 0)),
                  _lspec(layer, (1, k)),
                  _lspec(layer, (k, tn), lambda i, j: (0, j))],
        out_specs=pl.BlockSpec((tm, tn), lambda i, j: (i, j)),
        out_shape=jax.ShapeDtypeStruct((m, n), out_dtype),
        scratch_shapes=[pltpu.VMEM((tm, k), BF16)],
        compiler_params=_cparams(("parallel", "arbitrary")),
        name=name,
    )(x, nw, w)


def _rope_apply(v, c, s):
    return v * c + pltpu.roll(v, 64, 1) * s


def _q_body(c_ref, nw_ref, w_ref, qnw_ref, ct_ref, st_ref, o_ref):
    cn = _rms(c_ref[...], nw_ref[...]).astype(BF16)
    acc = _dot(cn, w_ref[...])
    c = ct_ref[...]
    s = st_ref[...]
    wn = qnw_ref[:, :LANES]
    wr = qnw_ref[:, LANES:]
    scale = QK_DIM ** -0.5 * LOG2_E
    for h in range(MLA_HEADS):
        qn = acc[:, HEAD_W * h:HEAD_W * h + LANES]
        qr = acc[:, HEAD_W * h + LANES:HEAD_W * (h + 1)]
        ssq = jnp.sum(qn * qn + qr * qr, axis=-1, keepdims=True)
        rs = lax.rsqrt(ssq / QK_DIM + EPS)
        o_ref[:, HEAD_W * h:HEAD_W * h + LANES] = (qn * rs * wn * scale).astype(BF16)
        o_ref[:, HEAD_W * h + LANES:HEAD_W * (h + 1)] = (_rope_apply(qr * rs * wr, c, s) * scale).astype(BF16)


def _q_proj(small, nw, wq, qnw, ct, st, layer, tm):
    m = small.shape[0]
    n = MLA_HEADS * HEAD_W
    return pl.pallas_call(
        _q_body,
        grid=(m // tm,),
        in_specs=[pl.BlockSpec((tm, Q_LORA), lambda i: (i, 0)),
                  _lspec(layer, (1, Q_LORA)),
                  _lspec(layer, (Q_LORA, n)),
                  _lspec(layer, (1, HEAD_W)),
                  pl.BlockSpec((tm, LANES), lambda i: (i, 0)),
                  pl.BlockSpec((tm, LANES), lambda i: (i, 0))],
        out_specs=pl.BlockSpec((tm, n), lambda i: (i, 0)),
        out_shape=jax.ShapeDtypeStruct((m, n), BF16),
        compiler_params=_cparams(("parallel",)),
        name="q_proj",
    )(small, nw, wq, qnw, ct, st)


def _kv_body(c_ref, kr_ref, nw_ref, wk_ref, wvt_ref, kwn_ref, kwr_ref, ct_ref, st_ref, k_ref, vt_ref):
    cn = _rms(c_ref[...], nw_ref[...]).astype(BF16)
    acc = _dot(cn, wk_ref[...])
    vt_ref[0] = _dot_nt(wvt_ref[...], cn).astype(BF16)
    c = ct_ref[...]
    s = st_ref[...]
    kr = kr_ref[...]
    wn = kwn_ref[...]
    wr = kwr_ref[...]
    ssr = jnp.sum(kr * kr, axis=-1, keepdims=True)
    for h in range(MLA_HEADS):
        kn = acc[:, LANES * h:LANES * (h + 1)]
        ssq = jnp.sum(kn * kn, axis=-1, keepdims=True) + ssr
        rs = lax.rsqrt(ssq / QK_DIM + EPS)
        k_ref[:, HEAD_W * h:HEAD_W * h + LANES] = (kn * rs * wn).astype(BF16)
        k_ref[:, HEAD_W * h + LANES:HEAD_W * (h + 1)] = _rope_apply(kr * rs * wr, c, s).astype(BF16)


def _kv_proj(small, nw, wk, wvt, kwn, kwr, ct, st, layer, tm):
    m = small.shape[0]
    nk = MLA_HEADS * HEAD_W
    nv = MLA_HEADS * V_DIM
    kr_blk = (Q_LORA + KV_LORA) // LANES
    return pl.pallas_call(
        _kv_body,
        grid=(m // tm,),
        in_specs=[pl.BlockSpec((tm, KV_LORA), lambda i: (i, 1)),
                  pl.BlockSpec((tm, LANES), lambda i: (i, kr_blk)),
                  _lspec(layer, (1, KV_LORA)),
                  _lspec(layer, (KV_LORA, MLA_HEADS * QK_NOPE)),
                  _lspec(layer, (nv, KV_LORA)),
                  _lspec(layer, (1, LANES)),
                  _lspec(layer, (1, LANES)),
                  pl.BlockSpec((tm, LANES), lambda i: (i, 0)),
                  pl.BlockSpec((tm, LANES), lambda i: (i, 0))],
        out_specs=[pl.BlockSpec((tm, nk), lambda i: (i, 0)),
                   pl.BlockSpec((1, nv, tm), lambda i: (i, 0, 0))],
        out_shape=[jax.ShapeDtypeStruct((m, nk), BF16), jax.ShapeDtypeStruct((m // tm, nv, tm), BF16)],
        compiler_params=_cparams(("parallel",)),
        name="kv_proj",
    )(small, small, nw, wk, wvt, kwn, kwr, ct, st)


def _attn_body(q_ref, k_ref, vt_ref, o_ref, p_scr, *, tq, nq, hb):
    key_chunk = lax.broadcasted_iota(jnp.int32, (tq, tq), 0) // CHUNK
    qry_chunk = lax.broadcasted_iota(jnp.int32, (tq, tq), 1) // CHUNK
    diag_mask = key_chunk <= qry_chunk

    def scores(qs, kb, hh):
        koff = pl.multiple_of(kb * tq, tq)
        return _dot_nt(k_ref[pl.ds(koff, tq), HEAD_W * hh:HEAD_W * (hh + 1)], qs[hh])

    def pv(kb, hh):
        return _dot(vt_ref[kb, V_DIM * hh:V_DIM * (hh + 1), :], p_scr[hh])

    def softmax_update(s, m, l, acc, hh):
        m_new = jnp.maximum(m, jnp.max(s, axis=0, keepdims=True))
        p = jnp.exp2(s - m_new)
        alpha = jnp.exp2(m - m_new)
        p_scr[hh] = p.astype(BF16)
        return m_new, alpha * l + jnp.sum(p, axis=0, keepdims=True), alpha * acc

    def q_block(qi, c):
        qoff = pl.multiple_of(qi * tq, tq)
        qs = [q_ref[pl.ds(qoff, tq), HEAD_W * hh:HEAD_W * (hh + 1)] for hh in range(hb)]
        carries = []
        for hh in range(hb):
            s = jnp.where(diag_mask, scores(qs, qi, hh), -jnp.inf)
            carries.append(softmax_update(s, jnp.full((1, tq), -jnp.inf, F32), jnp.zeros((1, tq), F32),
                                          jnp.zeros((V_DIM, tq), F32), hh))

        def body(j, carries):
            prev = jnp.where(j == 0, qi, j - 1)
            ss = [scores(qs, j, hh) for hh in range(hb)]
            pvs = [pv(prev, hh) for hh in range(hb)]
            return tuple(softmax_update(ss[hh], carries[hh][0], carries[hh][1], carries[hh][2] + pvs[hh], hh)
                         for hh in range(hb))

        carries = lax.fori_loop(0, qi, body, tuple(carries))
        last = jnp.maximum(qi - 1, 0)
        for hh in range(hb):
            _, l, acc = carries[hh]
            o_ref[pl.ds(qoff, tq), V_DIM * hh:V_DIM * (hh + 1)] = ((acc + pv(last, hh)) / l).T.astype(BF16)
        return c

    lax.fori_loop(0, nq, q_block, 0)


def _attention(q, k, vt, batch, seq, tq, hb):
    m = q.shape[0]
    nq = seq // tq
    return pl.pallas_call(
        functools.partial(_attn_body, tq=tq, nq=nq, hb=hb),
        grid=(batch, MLA_HEADS // hb),
        in_specs=[pl.BlockSpec((seq, hb * HEAD_W), lambda b, h: (b, h)),
                  pl.BlockSpec((seq, hb * HEAD_W), lambda b, h: (b, h)),
                  pl.BlockSpec((nq, hb * V_DIM, tq), lambda b, h: (b, h, 0))],
        out_specs=pl.BlockSpec((seq, hb * V_DIM), lambda b, h: (b, h)),
        out_shape=jax.ShapeDtypeStruct((m, MLA_HEADS * V_DIM), BF16),
        scratch_shapes=[pltpu.VMEM((hb, tq, tq), BF16)],
        compiler_params=_cparams(("parallel", "arbitrary")),
        name="attention",
    )(q, k, vt)


def _split3(v):
    hi = v.astype(BF16)
    r = v - hi.astype(F32)
    mid = r.astype(BF16)
    lo = (r - mid.astype(F32)).astype(BF16)
    return hi, mid, lo


def _ssd_body(z_ref, x_ref, bc_ref, dt_ref, cwx_ref, cwb_ref, cbx_ref, cbb_ref, dtb_ref, alog_ref,
              dsk_ref, nw_ref, sel_ref, eye_ref, tri_ref, tri64_ref, y_ref, state_ref, xhalo, bchalo):
    t = CHUNK

    @pl.when(pl.program_id(1) == 0)
    def _():
        state_ref[...] = jnp.zeros_like(state_ref)
        xhalo[...] = jnp.zeros_like(xhalo)
        bchalo[...] = jnp.zeros_like(bchalo)

    def conv_silu(raw_ref, halo, w_ref, b_ref):
        cur = raw_ref[...].astype(F32)
        width = cur.shape[1]
        tiles = jnp.concatenate([halo[...], cur], axis=0).reshape(t // 8 + 1, 8, width)
        halo[...] = cur[t - 8:, :]
        sub = lax.broadcasted_iota(jnp.int32, (1, 8, width), 1)
        y = None
        for tap in range(CONV_WIDTH):
            shift = CONV_WIDTH - 1 - tap
            if shift == 0:
                xt = cur
            else:
                r = pltpu.roll(tiles, shift, 1)
                xt = jnp.where(sub < shift, r[:-1], r[1:]).reshape(t, width)
            term = xt * w_ref[tap:tap + 1, :]
            y = term if y is None else y + term
        y = y + b_ref[...]
        return y * jax.nn.sigmoid(y)

    xs = conv_silu(x_ref, xhalo, cwx_ref, cbx_ref)
    bc = conv_silu(bc_ref, bchalo, cwb_ref, cbb_ref)
    gn = SSM_GROUPS * SSM_STATE

    head_lane = lax.broadcasted_iota(jnp.int32, (t, LANES), 1) < SSM_HEADS
    dt = jnp.where(head_lane, jax.nn.softplus(dt_ref[...] + dtb_ref[...]), 0.0)
    da = dt * (-jnp.exp(alog_ref[...]))
    tri64 = tri64_ref[...]
    h3 = _split3(da)
    a_cum = (_dot(tri64, h3[0]) + _dot(tri64, h3[1]) + _dot(tri64, h3[2])) * LOG2_E
    ea_c = jnp.where(head_lane, jnp.exp2(a_cum), 0.0)
    dte_c = jnp.where(head_lane, jnp.exp2(a_cum[t - 1:t, :] - a_cum), 0.0)
    s3 = [part.astype(F32) for part in _split3(jnp.concatenate([a_cum, dt, ea_c, dte_c], axis=0))]
    packed = jnp.concatenate([(s3[0] + pltpu.roll(s3[1], SSM_HEADS, 1)).astype(BF16), s3[2].astype(BF16)], axis=1)
    ex = _dot(packed, sel_ref[...])
    a_col = ex[:t]
    dt_col = ex[t:2 * t]
    ea = ex[2 * t:3 * t]
    a_row = jnp.sum(jnp.where(eye_ref[...] > 0, a_col, 0.0), axis=0, keepdims=True)
    decay = jnp.exp2(jnp.where(tri_ref[...] > 0, a_col - a_row, -jnp.inf))
    chunk_decay = ea[t - 1:t, :]
    xdt = xs * dt_col
    xdt_b = xdt.astype(BF16)
    xw_b = (xdt * ex[3 * t:]).astype(BF16)
    left = lax.broadcasted_iota(jnp.int32, (t, LANES), 1) < SSM_HEADDIM
    zero_b = jnp.zeros((t, LANES), BF16)

    for g in range(SSM_GROUPS):
        lo, hi = GROUP_W * g, GROUP_W * (g + 1)
        bg = bc[:, SSM_STATE * g:SSM_STATE * (g + 1)]
        cg_b = bc[:, gn + SSM_STATE * g:gn + SSM_STATE * (g + 1)].astype(BF16)
        bg_b = bg.astype(BF16)
        cb2 = _dot_nt(cg_b, jnp.concatenate([bg_b, bg_b], axis=0))
        yd = []
        for j in range(GROUP_W // LANES):
            l0 = lo + LANES * j
            mm = (cb2 * decay[:, l0:l0 + LANES]).astype(BF16)
            xp = xdt_b[:, l0:l0 + LANES]
            rhs = jnp.concatenate([jnp.where(left, xp, zero_b), jnp.where(left, zero_b, xp)], axis=0)
            yd.append(_dot(mm, rhs))
        y_diag = jnp.concatenate(yd, axis=1)
        st = state_ref[g]
        y_off = _dot(cg_b, st.astype(BF16)) * ea[:, lo:hi]
        inc = _dot(bg.T.astype(BF16), xw_b[:, lo:hi])
        state_ref[g] = st * chunk_decay[:, lo:hi] + inc
        yg = y_diag + y_off + dsk_ref[:, lo:hi] * xs[:, lo:hi]
        zg = z_ref[:, lo:hi].astype(F32)
        yg = yg * (zg * jax.nn.sigmoid(zg))
        ms = jnp.mean(yg * yg, axis=-1, keepdims=True)
        y_ref[:, lo:hi] = (yg * lax.rsqrt(ms + EPS) * nw_ref[:, lo:hi]).astype(BF16)


def _ssd(big, small, cwx, cwb, cbx, cbb, dtb, alog, dsk, nw, sel, eye_t, tri_t, tri64, layer, batch, seq):
    m = big.shape[0]
    t = CHUNK
    nc = seq // t
    bcw = 2 * SSM_GROUPS * SSM_STATE
    bc_blk = (2 * D_INNER) // bcw
    dt_blk = (SMALL_N - LANES) // LANES
    row = lambda b, c: b * nc + c
    const = lambda shape: pl.BlockSpec(shape, lambda b, c: (0,) * len(shape))
    return pl.pallas_call(
        _ssd_body,
        grid=(batch, nc),
        in_specs=[pl.BlockSpec((t, D_INNER), lambda b, c: (row(b, c), 0)),
                  pl.BlockSpec((t, D_INNER), lambda b, c: (row(b, c), 1)),
                  pl.BlockSpec((t, bcw), lambda b, c: (row(b, c), bc_blk)),
                  pl.BlockSpec((t, LANES), lambda b, c: (row(b, c), dt_blk)),
                  _lspec(layer, (CONV_WIDTH, D_INNER)), _lspec(layer, (CONV_WIDTH, bcw)),
                  _lspec(layer, (1, D_INNER)), _lspec(layer, (1, bcw)),
                  _lspec(layer, (1, LANES)), _lspec(layer, (1, LANES)),
                  _lspec(layer, (1, D_INNER)), _lspec(layer, (1, D_INNER)),
                  const((2 * LANES, D_INNER)), const((t, D_INNER)), const((t, D_INNER)), const((t, t))],
        out_specs=pl.BlockSpec((t, D_INNER), lambda b, c: (row(b, c), 0)),
        out_shape=jax.ShapeDtypeStruct((m, D_INNER), BF16),
        scratch_shapes=[pltpu.VMEM((SSM_GROUPS, SSM_STATE, GROUP_W), F32),
                        pltpu.VMEM((8, D_INNER), F32),
                        pltpu.VMEM((8, bcw), F32)],
        compiler_params=_cparams(("parallel", "arbitrary")),
        name="ssd",
    )(big, big, big, small, cwx, cwb, cbx, cbb, dtb, alog, dsk, nw, sel, eye_t, tri_t, tri64)


def _merge_body(o_ref, y_ref, wa_ref, wb_ref, g0_ref, g1_ref, out_ref):
    ya = _dot(o_ref[...], wa_ref[...])
    yb = _dot(y_ref[...], wb_ref[...])
    g0 = jax.nn.sigmoid(g0_ref[...].astype(F32))
    g1 = jax.nn.sigmoid(g1_ref[...].astype(F32))
    out_ref[...] = (g0 * ya + g1 * yb).astype(out_ref.dtype)


def _merge(o, y, wa, wb, big, layer, tm, tn):
    m = o.shape[0]
    g0_blk = (2 * D_INNER + 2 * SSM_GROUPS * SSM_STATE) // tn
    g1_blk = g0_blk + D_MODEL // tn
    return pl.pallas_call(
        _merge_body,
        grid=(m // tm, D_MODEL // tn),
        in_specs=[pl.BlockSpec((tm, o.shape[1]), lambda i, j: (i, 0)),
                  pl.BlockSpec((tm, y.shape[1]), lambda i, j: (i, 0)),
                  _lspec(layer, (wa.shape[1], tn), lambda i, j: (0, j)),
                  _lspec(layer, (wb.shape[1], tn), lambda i, j: (0, j)),
                  pl.BlockSpec((tm, tn), lambda i, j: (i, g0_blk + j)),
                  pl.BlockSpec((tm, tn), lambda i, j: (i, g1_blk + j))],
        out_specs=pl.BlockSpec((tm, tn), lambda i, j: (i, j)),
        out_shape=jax.ShapeDtypeStruct((m, D_MODEL), BF16),
        compiler_params=_cparams(("parallel", "arbitrary")),
        name="merge",
    )(o, y, wa, wb, big, big)


def _res_mm_body(x_ref, a_ref, w_ref, o_ref):
    o_ref[...] = x_ref[...] + _dot(a_ref[...], w_ref[...])


def _res_matmul(x, a, w, layer, tm):
    m, n = x.shape
    k = a.shape[1]
    return pl.pallas_call(
        _res_mm_body,
        grid=(m // tm,),
        in_specs=[pl.BlockSpec((tm, n), lambda i: (i, 0)),
                  pl.BlockSpec((tm, k), lambda i: (i, 0)),
                  _lspec(layer, (k, n))],
        out_specs=pl.BlockSpec((tm, n), lambda i: (i, 0)),
        out_shape=jax.ShapeDtypeStruct((m, n), F32),
        compiler_params=_cparams(("parallel",)),
        name="out_proj",
    )(x, a, w)


def _mlp_body(x_ref, nw_ref, wu_ref, wd_ref, o_ref, h_ref, acc_ref):
    j = pl.program_id(1)

    @pl.when(j == 0)
    def _():
        h_ref[...] = _rms(x_ref[...], nw_ref[...]).astype(BF16)
        acc_ref[...] = jnp.zeros_like(acc_ref)

    u = _dot(h_ref[...], wu_ref[...])
    hid = jnp.square(jnp.maximum(u, 0.0)).astype(BF16)
    acc_ref[...] += _dot(hid, wd_ref[...])

    @pl.when(j == pl.num_programs(1) - 1)
    def _():
        o_ref[...] = x_ref[...] + acc_ref[...]


def _mlp(x, nw, wu, wd, layer, tm, tf):
    m, d = x.shape
    f = wu.shape[2]
    return pl.pallas_call(
        _mlp_body,
        grid=(m // tm, f // tf),
        in_specs=[pl.BlockSpec((tm, d), lambda i, j: (i, 0), pipeline_mode=pl.Buffered(1)),
                  _lspec(layer, (1, d)),
                  _lspec(layer, (d, tf), lambda i, j: (0, j)),
                  _lspec(layer, (tf, d), lambda i, j: (j, 0))],
        out_specs=pl.BlockSpec((tm, d), lambda i, j: (i, 0)),
        out_shape=jax.ShapeDtypeStruct((m, d), F32),
        scratch_shapes=[pltpu.VMEM((tm, d), BF16), pltpu.VMEM((tm, d), F32)],
        compiler_params=_cparams(("parallel", "arbitrary")),
        name="mlp",
    )(x, nw, wu, wd)


def _ple_body(x_ref, nw_ref, wg_ref, p_ref, wp_ref, o_ref):
    x = x_ref[...]
    h = _rms(x, nw_ref[...]).astype(BF16)
    gate = jax.nn.sigmoid(_dot(h, wg_ref[...]))
    pe = _dot(p_ref[...].astype(BF16), wp_ref[...])
    o_ref[...] = x + pe * gate


def _ple(x, nw, wg, p, wp, layer, tm):
    m, d = x.shape
    return pl.pallas_call(
        _ple_body,
        grid=(m // tm,),
        in_specs=[pl.BlockSpec((tm, d), lambda i: (i, 0)),
                  _lspec(layer, (1, d)),
                  _lspec(layer, (d, d)),
                  _lspec(layer, (tm, PLE_DIM), lambda i: (i, 0)),
                  _lspec(layer, (PLE_DIM, d))],
        out_specs=pl.BlockSpec((tm, d), lambda i: (i, 0)),
        out_shape=jax.ShapeDtypeStruct((m, d), F32),
        compiler_params=_cparams(("parallel",)),
        name="ple",
    )(x, nw, wg, p, wp)


def _rope_cols(w):
    half = QK_ROPE // 2
    z = jnp.zeros(w.shape[:-1] + (LANES // 2 - half,), w.dtype)
    return jnp.concatenate([w[..., :half], z, w[..., half:], z], axis=-1)


def kernel(x, p, positions, norm_mix_w, w_in, q_a_norm_w, w_uq, kv_a_norm_w, w_ukv, q_norm_w, k_norm_w,
           w_o_mla, conv_w, conv_b, dt_bias, a_log, d_skip, ssm_norm_w, w_o_ssm, w_out, norm_mlp_w, w_up,
           w_down, ple_norm_w, w_ple_gate, w_ple):
    batch, seq, d = x.shape
    m = batch * seq
    depth = w_in.shape[0]
    tm_big = min(1024, m)
    tm_mid = min(512, m)
    tm_small = min(256, m)
    tq = min(512, seq)

    inv_freq = 1.0 / (ROPE_THETA ** (jnp.arange(0, QK_ROPE, 2, dtype=F32) / QK_ROPE))
    freq = _rope_cols(jnp.concatenate([inv_freq, inv_freq]))[None, :]
    sign = _rope_cols(jnp.concatenate([-jnp.ones_like(inv_freq), jnp.ones_like(inv_freq)]))[None, :]
    ct, st = _rope_tables(positions.reshape(m, 1), freq, sign, tm_big)

    lane = jnp.arange(D_INNER)
    sel_row = jnp.arange(2 * LANES)[:, None]
    sel = ((sel_row % SSM_HEADS == (lane // SSM_HEADDIM)[None, :]) & (sel_row < 3 * SSM_HEADS)).astype(BF16)
    rows = jnp.arange(CHUNK)[:, None]
    eye_t = (rows == (lane % SSM_HEADDIM)[None, :]).astype(F32)
    tri_t = (rows >= (lane % SSM_HEADDIM)[None, :]).astype(F32)
    tri64 = (rows >= jnp.arange(CHUNK)[None, :]).astype(BF16)

    s0 = Q_LORA + KV_LORA + QK_ROPE
    s1 = s0 + D_INNER + (D_INNER + 2 * SSM_GROUPS * SSM_STATE)
    s2 = s1 + SSM_HEADS

    w_small = jnp.concatenate(
        [w_in[..., :Q_LORA + KV_LORA], _rope_cols(w_in[..., Q_LORA + KV_LORA:s0]), w_in[..., s1:s2],
         jnp.zeros((depth, d, LANES - SSM_HEADS), F32)], axis=-1).astype(BF16)
    w_big = jnp.concatenate([w_in[..., s0:s1], w_in[..., s2:]], axis=-1).astype(BF16)
    wq4 = w_uq.reshape(depth, Q_LORA, MLA_HEADS, QK_DIM)
    wq = jnp.concatenate([wq4[..., :QK_NOPE], _rope_cols(wq4[..., QK_NOPE:])], axis=-1)
    wq = wq.reshape(depth, Q_LORA, MLA_HEADS * HEAD_W).astype(BF16)
    wkv4 = w_ukv.reshape(depth, KV_LORA, MLA_HEADS, QK_NOPE + V_DIM)
    wk = wkv4[..., :QK_NOPE].reshape(depth, KV_LORA, -1).astype(BF16)
    wvt = jnp.swapaxes(wkv4[..., QK_NOPE:].reshape(depth, KV_LORA, -1), 1, 2).astype(BF16)
    w_o_mla_b = w_o_mla.astype(BF16)
    w_o_ssm_b = w_o_ssm.astype(BF16)
    w_out_b = w_out.astype(BF16)
    w_up_b = w_up.astype(BF16)
    w_down_b = w_down.astype(BF16)
    w_gate_b = w_ple_gate.astype(BF16)
    w_ple_b = w_ple.astype(BF16)

    rows3 = lambda v: v.reshape(depth, 1, -1).astype(F32)
    pad_h = jnp.zeros((depth, LANES - SSM_HEADS), F32)
    qnw = rows3(jnp.concatenate([q_norm_w[:, :QK_NOPE], _rope_cols(q_norm_w[:, QK_NOPE:])], axis=-1))
    kwn = rows3(k_norm_w[:, :QK_NOPE])
    kwr = rows3(_rope_cols(k_norm_w[:, QK_NOPE:]))
    dtb = rows3(jnp.concatenate([dt_bias, pad_h], axis=-1))
    alog = rows3(jnp.concatenate([a_log, pad_h], axis=-1))
    dsk = rows3(jnp.repeat(d_skip, SSM_HEADDIM, axis=-1))
    mix_nw, qa_nw, kva_nw = rows3(norm_mix_w), rows3(q_a_norm_w), rows3(kv_a_norm_w)
    ssm_nw, mlp_nw, ple_nw = rows3(ssm_norm_w), rows3(norm_mlp_w), rows3(ple_norm_w)
    cwx, cwb = conv_w[..., :D_INNER], conv_w[..., D_INNER:]
    cbx, cbb = rows3(conv_b[:, :D_INNER]), rows3(conv_b[:, D_INNER:])
    p3 = p.reshape(depth, m, PLE_DIM)

    xf = x.reshape(m, d)
    for i in range(depth):
        small = _norm_matmul(xf, mix_nw, w_small, i, F32, tm_mid, SMALL_N, "in_proj_small")
        big = _norm_matmul(xf, mix_nw, w_big, i, BF16, tm_big, 1024, "in_proj_big")
        q = _q_proj(small, qa_nw, wq, qnw, ct, st, i, tm_small)
        k, vt = _kv_proj(small, kva_nw, wk, wvt, kwn, kwr, ct, st, i, tq)
        o = _attention(q, k, vt, batch, seq, tq, 2)
        y = _ssd(big, small, cwx, cwb, cbx, cbb, dtb, alog, dsk, ssm_nw, sel, eye_t, tri_t, tri64, i, batch, seq)
        merged = _merge(o, y, w_o_mla_b, w_o_ssm_b, big, i, tm_mid, 512)
        xf = _res_matmul(xf, merged, w_out_b, i, tm_mid)
        xf = _mlp(xf, mlp_nw, w_up_b, w_down_b, i, tm_big, 512)
        xf = _ple(xf, ple_nw, w_gate_b, p3, w_ple_b, i, tm_mid)
    return xf.reshape(batch, seq, d)
```

```python
import functools

import jax
import jax.numpy as jnp
from jax import lax
from jax.experimental import pallas as pl
from jax.experimental.pallas import tpu as pltpu

F32 = jnp.float32
BF16 = jnp.bfloat16

D_MODEL = 2048
DEPTH = 4
CHUNK = 64
PLE_DIM = 256
EPS = 1e-6
MLA_HEADS = 16
Q_LORA = 512
KV_LORA = 512
QK_NOPE = 128
QK_ROPE = 64
V_DIM = 128
QK_DIM = QK_NOPE + QK_ROPE
ROPE_THETA = 10000.0
D_INNER = 2 * D_MODEL
SSM_HEADDIM = 64
SSM_HEADS = D_INNER // SSM_HEADDIM
SSM_GROUPS = 8
SSM_STATE = 128
GROUP_W = D_INNER // SSM_GROUPS
CONV_WIDTH = 4
D_FF = 4 * D_MODEL

LOG2_E = 1.4426950408889634
LANES = 128
HEAD_W = 2 * LANES
SMALL_N = Q_LORA + KV_LORA + 2 * LANES
BIG_N = D_INNER + (D_INNER + 2 * SSM_GROUPS * SSM_STATE) + 2 * D_MODEL
VMEM_LIMIT = 56 * 1024 * 1024


def _cparams(sem):
    return pltpu.CompilerParams(dimension_semantics=sem, vmem_limit_bytes=VMEM_LIMIT)


def _rms(xf, w):
    ms = jnp.mean(xf * xf, axis=-1, keepdims=True)
    return xf * lax.rsqrt(ms + EPS) * w


def _lspec(layer, block, imap=None):
    if imap is None:
        imap = lambda *g: (0,) * len(block)
    return pl.BlockSpec((None,) + tuple(block), lambda *g: (layer,) + tuple(imap(*g)))


def _dot(a, b):
    return jnp.dot(a, b, preferred_element_type=F32)


def _dot_nt(a, b):
    return lax.dot_general(a, b, (((1,), (1,)), ((), ())), preferred_element_type=F32)


def _rope_body(pos_ref, freq_ref, sign_ref, ct_ref, st_ref):
    ang = pos_ref[...].astype(F32) * freq_ref[...]
    ct_ref[...] = jnp.cos(ang)
    st_ref[...] = jnp.sin(ang) * sign_ref[...]


def _rope_tables(pos_col, freq, sign, tm):
    m = pos_col.shape[0]
    return pl.pallas_call(
        _rope_body,
        grid=(m // tm,),
        in_specs=[pl.BlockSpec((tm, 1), lambda i: (i, 0)),
                  pl.BlockSpec((1, LANES), lambda i: (0, 0)),
                  pl.BlockSpec((1, LANES), lambda i: (0, 0))],
        out_specs=[pl.BlockSpec((tm, LANES), lambda i: (i, 0)),
                   pl.BlockSpec((tm, LANES), lambda i: (i, 0))],
        out_shape=[jax.ShapeDtypeStruct((m, LANES), F32)] * 2,
        compiler_params=_cparams(("parallel",)),
        name="rope_tables",
    )(pos_col, freq, sign)


def _norm_mm_body(x_ref, nw_ref, w_ref, o_ref, h_ref):
    @pl.when(pl.program_id(1) == 0)
    def _():
        h_ref[...] = _rms(x_ref[...], nw_ref[...]).astype(BF16)

    o_ref[...] = _dot(h_ref[...], w_ref[...]).astype(o_ref.dtype)


def _norm_matmul(x, nw, w, layer, out_dtype, tm, tn, name):
    m, k = x.shape
    n = w.shape[2]
    return pl.pallas_call(
        _norm_mm_body,
        grid=(m // tm, n // tn),
        in_specs=[pl.BlockSpec((tm, k), lambda i, j: (i, 0)),
                  _lspec(layer, (1, k)),
                  _lspec(layer, (k, tn), lambda i, j: (0, j))],
        out_specs=pl.BlockSpec((tm, tn), lambda i, j: (i, j)),
        out_shape=jax.ShapeDtypeStruct((m, n), out_dtype),
        scratch_shapes=[pltpu.VMEM((tm, k), BF16)],
        compiler_params=_cparams(("parallel", "arbitrary")),
        name=name,
    )(x, nw, w)


def _rope_apply(v, c, s):
    return v * c + pltpu.roll(v, 64, 1) * s


def _q_body(c_ref, nw_ref, w_ref, qnw_ref, ct_ref, st_ref, o_ref):
    cn = _rms(c_ref[...], nw_ref[...]).astype(BF16)
    acc = _dot(cn, w_ref[...])
    c = ct_ref[...]
    s = st_ref[...]
    wn = qnw_ref[:, :LANES]
    wr = qnw_ref[:, LANES:]
    scale = QK_DIM ** -0.5 * LOG2_E
    for h in range(MLA_HEADS):
        qn = acc[:, HEAD_W * h:HEAD_W * h + LANES]
        qr = acc[:, HEAD_W * h + LANES:HEAD_W * (h + 1)]
        ssq = jnp.sum(qn * qn + qr * qr, axis=-1, keepdims=True)
        rs = lax.rsqrt(ssq / QK_DIM + EPS)
        o_ref[:, HEAD_W * h:HEAD_W * h + LANES] = (qn * rs * wn * scale).astype(BF16)
        o_ref[:, HEAD_W * h + LANES:HEAD_W * (h + 1)] = (_rope_apply(qr * rs * wr, c, s) * scale).astype(BF16)


def _q_proj(small, nw, wq, qnw, ct, st, layer, tm):
    m = small.shape[0]
    n = MLA_HEADS * HEAD_W
    return pl.pallas_call(
        _q_body,
        grid=(m // tm,),
        in_specs=[pl.BlockSpec((tm, Q_LORA), lambda i: (i, 0)),
                  _lspec(layer, (1, Q_LORA)),
                  _lspec(layer, (Q_LORA, n)),
                  _lspec(layer, (1, HEAD_W)),
                  pl.BlockSpec((tm, LANES), lambda i: (i, 0)),
                  pl.BlockSpec((tm, LANES), lambda i: (i, 0))],
        out_specs=pl.BlockSpec((tm, n), lambda i: (i, 0)),
        out_shape=jax.ShapeDtypeStruct((m, n), BF16),
        compiler_params=_cparams(("parallel",)),
        name="q_proj",
    )(small, nw, wq, qnw, ct, st)


def _kv_body(c_ref, kr_ref, nw_ref, wk_ref, wvt_ref, kwn_ref, kwr_ref, ct_ref, st_ref, k_ref, vt_ref):
    cn = _rms(c_ref[...], nw_ref[...]).astype(BF16)
    acc = _dot(cn, wk_ref[...])
    vt_ref[0] = _dot_nt(wvt_ref[...], cn).astype(BF16)
    c = ct_ref[...]
    s = st_ref[...]
    kr = kr_ref[...]
    wn = kwn_ref[...]
    wr = kwr_ref[...]
    ssr = jnp.sum(kr * kr, axis=-1, keepdims=True)
    for h in range(MLA_HEADS):
        kn = acc[:, LANES * h:LANES * (h + 1)]
        ssq = jnp.sum(kn * kn, axis=-1, keepdims=True) + ssr
        rs = lax.rsqrt(ssq / QK_DIM + EPS)
        k_ref[:, HEAD_W * h:HEAD_W * h + LANES] = (kn * rs * wn).astype(BF16)
        k_ref[:, HEAD_W * h + LANES:HEAD_W * (h + 1)] = _rope_apply(kr * rs * wr, c, s).astype(BF16)


def _kv_proj(small, nw, wk, wvt, kwn, kwr, ct, st, layer, tm):
    m = small.shape[0]
    nk = MLA_HEADS * HEAD_W
    nv = MLA_HEADS * V_DIM
    kr_blk = (Q_LORA + KV_LORA) // LANES
    return pl.pallas_call(
        _kv_body,
        grid=(m // tm,),
        in_specs=[pl.BlockSpec((tm, KV_LORA), lambda i: (i, 1)),
                  pl.BlockSpec((tm, LANES), lambda i: (i, kr_blk)),
                  _lspec(layer, (1, KV_LORA)),
                  _lspec(layer, (KV_LORA, MLA_HEADS * QK_NOPE)),
                  _lspec(layer, (nv, KV_LORA)),
                  _lspec(layer, (1, LANES)),
                  _lspec(layer, (1, LANES)),
                  pl.BlockSpec((tm, LANES), lambda i: (i, 0)),
                  pl.BlockSpec((tm, LANES), lambda i: (i, 0))],
        out_specs=[pl.BlockSpec((tm, nk), lambda i: (i, 0)),
                   pl.BlockSpec((1, nv, tm), lambda i: (i, 0, 0))],
        out_shape=[jax.ShapeDtypeStruct((m, nk), BF16), jax.ShapeDtypeStruct((m // tm, nv, tm), BF16)],
        compiler_params=_cparams(("parallel",)),
        name="kv_proj",
    )(small, small, nw, wk, wvt, kwn, kwr, ct, st)


def _attn_body(q_ref, k_ref, vt_ref, o_ref, s0_scr, s1_scr, p0_scr, p1_scr, m_scr, l_scr, acc_scr, *, tq, nq, hb):
    s_scr = (s0_scr, s1_scr)
    p_scr = (p0_scr, p1_scr)
    tk = tq // 2
    nk = tq // tk
    key_chunk = lax.broadcasted_iota(jnp.int32, (tk, tq), 0) // CHUNK
    qry_chunk = lax.broadcasted_iota(jnp.int32, (tk, tq), 1) // CHUNK
    diag_masks = [key_chunk + c * (tk // CHUNK) <= qry_chunk for c in range(nk)]

    def scores(qs, kb, slot, mask=None):
        koff = pl.multiple_of(kb * tk, tk)
        for hh in range(hb):
            s = _dot(k_ref[pl.ds(koff, tk), HEAD_W * hh:HEAD_W * (hh + 1)], qs[hh])
            s_scr[slot][hh] = s if mask is None else jnp.where(mask, s, -jnp.inf)

    def pv(kb, slot):
        return [_dot(vt_ref[kb, V_DIM * hh:V_DIM * (hh + 1), :], p_scr[slot][hh]) for hh in range(hb)]

    def softmax(slot, pvs):
        for hh in range(hb):
            m = m_scr[hh]
            s = s_scr[slot][hh]
            m_new = jnp.maximum(m, jnp.max(s, axis=0, keepdims=True))
            p = jnp.exp2(s - m_new)
            alpha = jnp.exp2(m - m_new)
            p_scr[slot][hh] = p.astype(BF16)
            m_scr[hh] = m_new
            l_scr[hh] = alpha * l_scr[hh] + jnp.sum(p, axis=0, keepdims=True)
            acc = acc_scr[hh] if pvs is None else acc_scr[hh] + pvs[hh]
            acc_scr[hh] = alpha * acc

    def q_block(qi, c):
        qoff = pl.multiple_of(qi * tq, tq)
        qs = [q_ref[pl.ds(qoff, tq), HEAD_W * hh:HEAD_W * (hh + 1)].astype(F32).T.astype(BF16) for hh in range(hb)]
        m_scr[...] = jnp.full(m_scr.shape, -jnp.inf, F32)
        l_scr[...] = jnp.zeros(l_scr.shape, F32)
        acc_scr[...] = jnp.zeros(acc_scr.shape, F32)
        d0 = nk * qi
        scores(qs, d0, 0, diag_masks[0])
        scores(qs, d0 + 1, 1, diag_masks[1])
        softmax(0, None)

        def body(r, c2):
            b0 = nk * (r - 1)
            a0 = jnp.where(r == 1, d0, b0 - nk)
            scores(qs, b0, 0)
            softmax(1, pv(a0, 0))
            scores(qs, b0 + 1, 1)
            softmax(0, pv(a0 + 1, 1))
            return c2

        lax.fori_loop(1, qi + 1, body, 0)
        a0 = jnp.where(qi == 0, d0, nk * (qi - 1))
        softmax(1, pv(a0, 0))
        pvs = pv(a0 + 1, 1)
        for hh in range(hb):
            o_ref[pl.ds(qoff, tq), V_DIM * hh:V_DIM * (hh + 1)] = ((acc_scr[hh] + pvs[hh]) / l_scr[hh]).T.astype(BF16)
        return c

    lax.fori_loop(0, nq, q_block, 0)


def _attention(q, k, vt, batch, seq, tq, hb):
    m = q.shape[0]
    nq = seq // tq
    tk = vt.shape[2]
    assert 2 * tk == tq
    return pl.pallas_call(
        functools.partial(_attn_body, tq=tq, nq=nq, hb=hb),
        grid=(batch, MLA_HEADS // hb),
        in_specs=[pl.BlockSpec((seq, hb * HEAD_W), lambda b, h: (b, h)),
                  pl.BlockSpec((seq, hb * HEAD_W), lambda b, h: (b, h)),
                  pl.BlockSpec((seq // tk, hb * V_DIM, tk), lambda b, h: (b, h, 0))],
        out_specs=pl.BlockSpec((seq, hb * V_DIM), lambda b, h: (b, h)),
        out_shape=jax.ShapeDtypeStruct((m, MLA_HEADS * V_DIM), BF16),
        scratch_shapes=[pltpu.VMEM((hb, tk, tq), F32), pltpu.VMEM((hb, tk, tq), F32),
                        pltpu.VMEM((hb, tk, tq), BF16), pltpu.VMEM((hb, tk, tq), BF16),
                        pltpu.VMEM((hb, 1, tq), F32), pltpu.VMEM((hb, 1, tq), F32), pltpu.VMEM((hb, V_DIM, tq), F32)],
        compiler_params=_cparams(("parallel", "arbitrary")),
        name="attention",
    )(q, k, vt)


def _split3(v):
    hi = v.astype(BF16)
    r = v - hi.astype(F32)
    mid = r.astype(BF16)
    lo = (r - mid.astype(F32)).astype(BF16)
    return hi, mid, lo


def _ssd_body(z_ref, x_ref, bc_ref, dt_ref, cwx_ref, cwb_ref, cbx_ref, cbb_ref, dtb_ref, alog_ref,
              dsk_ref, nw_ref, sel_ref, eye_ref, tri_ref, tri64_ref, y_ref, state_ref, xhalo, bchalo):
    t = CHUNK

    @pl.when(pl.program_id(1) == 0)
    def _():
        state_ref[...] = jnp.zeros_like(state_ref)
        xhalo[...] = jnp.zeros_like(xhalo)
        bchalo[...] = jnp.zeros_like(bchalo)

    def conv_silu(raw_ref, halo, w_ref, b_ref):
        cur = raw_ref[...].astype(F32)
        width = cur.shape[1]
        tiles = jnp.concatenate([halo[...], cur], axis=0).reshape(t // 8 + 1, 8, width)
        halo[...] = cur[t - 8:, :]
        sub = lax.broadcasted_iota(jnp.int32, (1, 8, width), 1)
        y = None
        for tap in range(CONV_WIDTH):
            shift = CONV_WIDTH - 1 - tap
            if shift == 0:
                xt = cur
            else:
                r = pltpu.roll(tiles, shift, 1)
                xt = jnp.where(sub < shift, r[:-1], r[1:]).reshape(t, width)
            term = xt * w_ref[tap:tap + 1, :]
            y = term if y is None else y + term
        y = y + b_ref[...]
        return y * jax.nn.sigmoid(y)

    xs = conv_silu(x_ref, xhalo, cwx_ref, cbx_ref)
    bc = conv_silu(bc_ref, bchalo, cwb_ref, cbb_ref)
    gn = SSM_GROUPS * SSM_STATE

    head_lane = lax.broadcasted_iota(jnp.int32, (t, LANES), 1) < SSM_HEADS
    dt = jnp.where(head_lane, jax.nn.softplus(dt_ref[...] + dtb_ref[...]), 0.0)
    da = dt * (-jnp.exp(alog_ref[...]))
    tri64 = tri64_ref[...]
    h3 = _split3(da)
    a_cum = (_dot(tri64, h3[0]) + _dot(tri64, h3[1]) + _dot(tri64, h3[2])) * LOG2_E
    ea_c = jnp.where(head_lane, jnp.exp2(a_cum), 0.0)
    dte_c = jnp.where(head_lane, jnp.exp2(a_cum[t - 1:t, :] - a_cum), 0.0)
    s3 = [part.astype(F32) for part in _split3(jnp.concatenate([a_cum, dt, ea_c, dte_c], axis=0))]
    packed = jnp.concatenate([(s3[0] + pltpu.roll(s3[1], SSM_HEADS, 1)).astype(BF16), s3[2].astype(BF16)], axis=1)
    ex = _dot(packed, sel_ref[...])
    a_col = ex[:t]
    dt_col = ex[t:2 * t]
    ea = ex[2 * t:3 * t]
    a_row = jnp.sum(jnp.where(eye_ref[...] > 0, a_col, 0.0), axis=0, keepdims=True)
    decay = jnp.exp2(jnp.where(tri_ref[...] > 0, a_col - a_row, -jnp.inf))
    chunk_decay = ea[t - 1:t, :]
    xdt = xs * dt_col
    xdt_b = xdt.astype(BF16)
    xw_b = (xdt * ex[3 * t:]).astype(BF16)
    left = lax.broadcasted_iota(jnp.int32, (t, LANES), 1) < SSM_HEADDIM
    zero_b = jnp.zeros((t, LANES), BF16)

    for g in range(SSM_GROUPS):
        lo, hi = GROUP_W * g, GROUP_W * (g + 1)
        bg = bc[:, SSM_STATE * g:SSM_STATE * (g + 1)]
        cg_b = bc[:, gn + SSM_STATE * g:gn + SSM_STATE * (g + 1)].astype(BF16)
        bg_b = bg.astype(BF16)
        cb2 = _dot_nt(cg_b, jnp.concatenate([bg_b, bg_b], axis=0))
        yd = []
        for j in range(GROUP_W // LANES):
            l0 = lo + LANES * j
            mm = (cb2 * decay[:, l0:l0 + LANES]).astype(BF16)
            xp = xdt_b[:, l0:l0 + LANES]
            rhs = jnp.concatenate([jnp.where(left, xp, zero_b), jnp.where(left, zero_b, xp)], axis=0)
            yd.append(_dot(mm, rhs))
        y_diag = jnp.concatenate(yd, axis=1)
        st = state_ref[g]
        y_off = _dot(cg_b, st.astype(BF16)) * ea[:, lo:hi]
        inc = _dot(bg.T.astype(BF16), xw_b[:, lo:hi])
        state_ref[g] = st * chunk_decay[:, lo:hi] + inc
        yg = y_diag + y_off + dsk_ref[:, lo:hi] * xs[:, lo:hi]
        zg = z_ref[:, lo:hi].astype(F32)
        yg = yg * (zg * jax.nn.sigmoid(zg))
        ms = jnp.mean(yg * yg, axis=-1, keepdims=True)
        y_ref[:, lo:hi] = (yg * lax.rsqrt(ms + EPS) * nw_ref[:, lo:hi]).astype(BF16)


def _ssd(big, small, cwx, cwb, cbx, cbb, dtb, alog, dsk, nw, sel, eye_t, tri_t, tri64, layer, batch, seq):
    m = big.shape[0]
    t = CHUNK
    nc = seq // t
    bcw = 2 * SSM_GROUPS * SSM_STATE
    bc_blk = (2 * D_INNER) // bcw
    dt_blk = (SMALL_N - LANES) // LANES
    row = lambda b, c: b * nc + c
    const = lambda shape: pl.BlockSpec(shape, lambda b, c: (0,) * len(shape))
    return pl.pallas_call(
        _ssd_body,
        grid=(batch, nc),
        in_specs=[pl.BlockSpec((t, D_INNER), lambda b, c: (row(b, c), 0)),
                  pl.BlockSpec((t, D_INNER), lambda b, c: (row(b, c), 1)),
                  pl.BlockSpec((t, bcw), lambda b, c: (row(b, c), bc_blk)),
                  pl.BlockSpec((t, LANES), lambda b, c: (row(b, c), dt_blk)),
                  _lspec(layer, (CONV_WIDTH, D_INNER)), _lspec(layer, (CONV_WIDTH, bcw)),
                  _lspec(layer, (1, D_INNER)), _lspec(layer, (1, bcw)),
                  _lspec(layer, (1, LANES)), _lspec(layer, (1, LANES)),
                  _lspec(layer, (1, D_INNER)), _lspec(layer, (1, D_INNER)),
                  const((2 * LANES, D_INNER)), const((t, D_INNER)), const((t, D_INNER)), const((t, t))],
        out_specs=pl.BlockSpec((t, D_INNER), lambda b, c: (row(b, c), 0)),
        out_shape=jax.ShapeDtypeStruct((m, D_INNER), BF16),
        scratch_shapes=[pltpu.VMEM((SSM_GROUPS, SSM_STATE, GROUP_W), F32),
                        pltpu.VMEM((8, D_INNER), F32),
                        pltpu.VMEM((8, bcw), F32)],
        compiler_params=_cparams(("parallel", "arbitrary")),
        name="ssd",
    )(big, big, big, small, cwx, cwb, cbx, cbb, dtb, alog, dsk, nw, sel, eye_t, tri_t, tri64)


def _merge_body(o_ref, y_ref, wa_ref, wb_ref, g0_ref, g1_ref, out_ref):
    ya = _dot(o_ref[...], wa_ref[...])
    yb = _dot(y_ref[...], wb_ref[...])
    g0 = jax.nn.sigmoid(g0_ref[...].astype(F32))
    g1 = jax.nn.sigmoid(g1_ref[...].astype(F32))
    out_ref[...] = (g0 * ya + g1 * yb).astype(out_ref.dtype)


def _merge(o, y, wa, wb, big, layer, tm, tn):
    m = o.shape[0]
    g0_blk = (2 * D_INNER + 2 * SSM_GROUPS * SSM_STATE) // tn
    g1_blk = g0_blk + D_MODEL // tn
    return pl.pallas_call(
        _merge_body,
        grid=(m // tm, D_MODEL // tn),
        in_specs=[pl.BlockSpec((tm, o.shape[1]), lambda i, j: (i, 0)),
                  pl.BlockSpec((tm, y.shape[1]), lambda i, j: (i, 0)),
                  _lspec(layer, (wa.shape[1], tn), lambda i, j: (0, j)),
                  _lspec(layer, (wb.shape[1], tn), lambda i, j: (0, j)),
                  pl.BlockSpec((tm, tn), lambda i, j: (i, g0_blk + j)),
                  pl.BlockSpec((tm, tn), lambda i, j: (i, g1_blk + j))],
        out_specs=pl.BlockSpec((tm, tn), lambda i, j: (i, j)),
        out_shape=jax.ShapeDtypeStruct((m, D_MODEL), BF16),
        compiler_params=_cparams(("parallel", "arbitrary")),
        name="merge",
    )(o, y, wa, wb, big, big)


def _res_mm_body(x_ref, a_ref, w_ref, o_ref):
    o_ref[...] = x_ref[...] + _dot(a_ref[...], w_ref[...])


def _res_matmul(x, a, w, layer, tm):
    m, n = x.shape
    k = a.shape[1]
    return pl.pallas_call(
        _res_mm_body,
        grid=(m // tm,),
        in_specs=[pl.BlockSpec((tm, n), lambda i: (i, 0)),
                  pl.BlockSpec((tm, k), lambda i: (i, 0)),
                  _lspec(layer, (k, n))],
        out_specs=pl.BlockSpec((tm, n), lambda i: (i, 0)),
        out_shape=jax.ShapeDtypeStruct((m, n), F32),
        compiler_params=_cparams(("parallel",)),
        name="out_proj",
    )(x, a, w)


def _mlp_body(x_ref, nw_ref, wu_ref, wd_ref, o_ref, h_ref):
    @pl.when(pl.program_id(1) == 0)
    def _():
        h_ref[...] = _rms(x_ref[...], nw_ref[...]).astype(BF16)
        o_ref[...] = x_ref[...]

    u = _dot(h_ref[...], wu_ref[...])
    hid = jnp.square(jnp.maximum(u, 0.0)).astype(BF16)
    o_ref[...] += _dot(hid, wd_ref[...])


def _mlp(x, nw, wu, wd, layer, tm, tf):
    m, d = x.shape
    f = wu.shape[2]
    return pl.pallas_call(
        _mlp_body,
        grid=(m // tm, f // tf),
        in_specs=[pl.BlockSpec((tm, d), lambda i, j: (i, 0), pipeline_mode=pl.Buffered(1)),
                  _lspec(layer, (1, d)),
                  _lspec(layer, (d, tf), lambda i, j: (0, j)),
                  _lspec(layer, (tf, d), lambda i, j: (j, 0))],
        out_specs=pl.BlockSpec((tm, d), lambda i, j: (i, 0)),
        out_shape=jax.ShapeDtypeStruct((m, d), F32),
        scratch_shapes=[pltpu.VMEM((tm, d), BF16)],
        compiler_params=_cparams(("parallel", "arbitrary")),
        name="mlp",
    )(x, nw, wu, wd)


def _ple_body(x_ref, nw_ref, wg_ref, p_ref, wp_ref, o_ref):
    x = x_ref[...]
    h = _rms(x, nw_ref[...]).astype(BF16)
    gate = jax.nn.sigmoid(_dot(h, wg_ref[...]))
    pe = _dot(p_ref[...].astype(BF16), wp_ref[...])
    o_ref[...] = x + pe * gate


def _ple(x, nw, wg, p, wp, layer, tm):
    m, d = x.shape
    return pl.pallas_call(
        _ple_body,
        grid=(m // tm,),
        in_specs=[pl.BlockSpec((tm, d), lambda i: (i, 0)),
                  _lspec(layer, (1, d)),
                  _lspec(layer, (d, d)),
                  _lspec(layer, (tm, PLE_DIM), lambda i: (i, 0)),
                  _lspec(layer, (PLE_DIM, d))],
        out_specs=pl.BlockSpec((tm, d), lambda i: (i, 0)),
        out_shape=jax.ShapeDtypeStruct((m, d), F32),
        compiler_params=_cparams(("parallel",)),
        name="ple",
    )(x, nw, wg, p, wp)


def _rope_cols(w):
    half = QK_ROPE // 2
    z = jnp.zeros(w.shape[:-1] + (LANES // 2 - half,), w.dtype)
    return jnp.concatenate([w[..., :half], z, w[..., half:], z], axis=-1)


def kernel(x, p, positions, norm_mix_w, w_in, q_a_norm_w, w_uq, kv_a_norm_w, w_ukv, q_norm_w, k_norm_w,
           w_o_mla, conv_w, conv_b, dt_bias, a_log, d_skip, ssm_norm_w, w_o_ssm, w_out, norm_mlp_w, w_up,
           w_down, ple_norm_w, w_ple_gate, w_ple):
    batch, seq, d = x.shape
    m = batch * seq
    depth = w_in.shape[0]
    tm_big = min(1024, m)
    tm_mid = min(512, m)
    tm_small = min(256, m)
    tq = min(512, seq)

    inv_freq = 1.0 / (ROPE_THETA ** (jnp.arange(0, QK_ROPE, 2, dtype=F32) / QK_ROPE))
    freq = _rope_cols(jnp.concatenate([inv_freq, inv_freq]))[None, :]
    sign = _rope_cols(jnp.concatenate([-jnp.ones_like(inv_freq), jnp.ones_like(inv_freq)]))[None, :]
    ct, st = _rope_tables(positions.reshape(m, 1), freq, sign, tm_big)

    lane = jnp.arange(D_INNER)
    sel_row = jnp.arange(2 * LANES)[:, None]
    sel = ((sel_row % SSM_HEADS == (lane // SSM_HEADDIM)[None, :]) & (sel_row < 3 * SSM_HEADS)).astype(BF16)
    rows = jnp.arange(CHUNK)[:, None]
    eye_t = (rows == (lane % SSM_HEADDIM)[None, :]).astype(F32)
    tri_t = (rows >= (lane % SSM_HEADDIM)[None, :]).astype(F32)
    tri64 = (rows >= jnp.arange(CHUNK)[None, :]).astype(BF16)

    s0 = Q_LORA + KV_LORA + QK_ROPE
    s1 = s0 + D_INNER + (D_INNER + 2 * SSM_GROUPS * SSM_STATE)
    s2 = s1 + SSM_HEADS

    w_in_b = w_in.astype(BF16)
    w_small = jnp.concatenate(
        [w_in_b[..., :Q_LORA + KV_LORA], _rope_cols(w_in_b[..., Q_LORA + KV_LORA:s0]), w_in_b[..., s1:s2],
         jnp.zeros((depth, d, LANES - SSM_HEADS), BF16)], axis=-1)
    w_big = jnp.concatenate([w_in_b[..., s0:s1], w_in_b[..., s2:]], axis=-1)
    wq4 = w_uq.reshape(depth, Q_LORA, MLA_HEADS, QK_DIM)
    wq = jnp.concatenate([wq4[..., :QK_NOPE], _rope_cols(wq4[..., QK_NOPE:])], axis=-1)
    wq = wq.reshape(depth, Q_LORA, MLA_HEADS * HEAD_W).astype(BF16)
    wkv4 = w_ukv.reshape(depth, KV_LORA, MLA_HEADS, QK_NOPE + V_DIM)
    wk = wkv4[..., :QK_NOPE].reshape(depth, KV_LORA, -1).astype(BF16)
    wvt = jnp.swapaxes(wkv4[..., QK_NOPE:].reshape(depth, KV_LORA, -1), 1, 2).astype(BF16)
    w_o_mla_b = w_o_mla.astype(BF16)
    w_o_ssm_b = w_o_ssm.astype(BF16)
    w_out_b = w_out.astype(BF16)
    w_up_b = w_up.astype(BF16)
    w_down_b = w_down.astype(BF16)
    w_gate_b = w_ple_gate.astype(BF16)
    w_ple_b = w_ple.astype(BF16)

    rows3 = lambda v: v.reshape(depth, 1, -1).astype(F32)
    pad_h = jnp.zeros((depth, LANES - SSM_HEADS), F32)
    qnw = rows3(jnp.concatenate([q_norm_w[:, :QK_NOPE], _rope_cols(q_norm_w[:, QK_NOPE:])], axis=-1))
    kwn = rows3(k_norm_w[:, :QK_NOPE])
    kwr = rows3(_rope_cols(k_norm_w[:, QK_NOPE:]))
    dtb = rows3(jnp.concatenate([dt_bias, pad_h], axis=-1))
    alog = rows3(jnp.concatenate([a_log, pad_h], axis=-1))
    dsk = rows3(jnp.repeat(d_skip, SSM_HEADDIM, axis=-1))
    mix_nw, qa_nw, kva_nw = rows3(norm_mix_w), rows3(q_a_norm_w), rows3(kv_a_norm_w)
    ssm_nw, mlp_nw, ple_nw = rows3(ssm_norm_w), rows3(norm_mlp_w), rows3(ple_norm_w)
    cwx, cwb = conv_w[..., :D_INNER], conv_w[..., D_INNER:]
    cbx, cbb = rows3(conv_b[:, :D_INNER]), rows3(conv_b[:, D_INNER:])
    p3 = p.reshape(depth, m, PLE_DIM)

    xf = x.reshape(m, d)
    for i in range(depth):
        small = _norm_matmul(xf, mix_nw, w_small, i, F32, tm_mid, SMALL_N, "in_proj_small")
        big = _norm_matmul(xf, mix_nw, w_big, i, BF16, tm_big, 1024, "in_proj_big")
        q = _q_proj(small, qa_nw, wq, qnw, ct, st, i, tm_small)
        k, vt = _kv_proj(small, kva_nw, wk, wvt, kwn, kwr, ct, st, i, tq // 2)
        o = _attention(q, k, vt, batch, seq, tq, 2)
        y = _ssd(big, small, cwx, cwb, cbx, cbb, dtb, alog, dsk, ssm_nw, sel, eye_t, tri_t, tri64, i, batch, seq)
        merged = _merge(o, y, w_o_mla_b, w_o_ssm_b, big, i, tm_big, 512)
        xf = _res_matmul(xf, merged, w_out_b, i, tm_mid)
        xf = _mlp(xf, mlp_nw, w_up_b, w_down_b, i, tm_mid, 1024)
        xf = _ple(xf, ple_nw, w_gate_b, p3, w_ple_b, i, tm_mid)
    return xf.reshape(batch, seq, d)
```

```python
import functools

import jax
import jax.numpy as jnp
from jax import lax
from jax.experimental import pallas as pl
from jax.experimental.pallas import tpu as pltpu

F32 = jnp.float32
BF16 = jnp.bfloat16

D_MODEL = 2048
DEPTH = 4
CHUNK = 64
PLE_DIM = 256
EPS = 1e-6
MLA_HEADS = 16
Q_LORA = 512
KV_LORA = 512
QK_NOPE = 128
QK_ROPE = 64
V_DIM = 128
QK_DIM = QK_NOPE + QK_ROPE
ROPE_THETA = 10000.0
D_INNER = 2 * D_MODEL
SSM_HEADDIM = 64
SSM_HEADS = D_INNER // SSM_HEADDIM
SSM_GROUPS = 8
SSM_STATE = 128
GROUP_W = D_INNER // SSM_GROUPS
CONV_WIDTH = 4
D_FF = 4 * D_MODEL

LOG2_E = 1.4426950408889634
LANES = 128
HEAD_W = 2 * LANES
SMALL_N = Q_LORA + KV_LORA + 2 * LANES
BIG_N = D_INNER + (D_INNER + 2 * SSM_GROUPS * SSM_STATE) + 2 * D_MODEL
VMEM_LIMIT = 56 * 1024 * 1024


def _cparams(sem):
    return pltpu.CompilerParams(dimension_semantics=sem, vmem_limit_bytes=VMEM_LIMIT)


def _rms(xf, w):
    ms = jnp.mean(xf * xf, axis=-1, keepdims=True)
    return xf * lax.rsqrt(ms + EPS) * w


def _lspec(layer, block, imap=None):
    if imap is None:
        imap = lambda *g: (0,) * len(block)
    return pl.BlockSpec((None,) + tuple(block), lambda *g: (layer,) + tuple(imap(*g)))


def _dot(a, b):
    return jnp.dot(a, b, preferred_element_type=F32)


def _dot_nt(a, b):
    return lax.dot_general(a, b, (((1,), (1,)), ((), ())), preferred_element_type=F32)


def _rope_body(pos_ref, freq_ref, sign_ref, ct_ref, st_ref):
    ang = pos_ref[...].astype(F32) * freq_ref[...]
    ct_ref[...] = jnp.cos(ang)
    st_ref[...] = jnp.sin(ang) * sign_ref[...]


def _rope_tables(pos_col, freq, sign, tm):
    m = pos_col.shape[0]
    return pl.pallas_call(
        _rope_body,
        grid=(m // tm,),
        in_specs=[pl.BlockSpec((tm, 1), lambda i: (i, 0)),
                  pl.BlockSpec((1, LANES), lambda i: (0, 0)),
                  pl.BlockSpec((1, LANES), lambda i: (0, 0))],
        out_specs=[pl.BlockSpec((tm, LANES), lambda i: (i, 0)),
                   pl.BlockSpec((tm, LANES), lambda i: (i, 0))],
        out_shape=[jax.ShapeDtypeStruct((m, LANES), F32)] * 2,
        compiler_params=_cparams(("parallel",)),
        name="rope_tables",
    )(pos_col, freq, sign)


def _norm_mm_body(x_ref, nw_ref, w_ref, o_ref, h_ref):
    @pl.when(pl.program_id(1) == 0)
    def _():
        h_ref[...] = _rms(x_ref[...], nw_ref[...]).astype(BF16)

    o_ref[...] = _dot(h_ref[...], w_ref[...]).astype(o_ref.dtype)


def _norm_matmul(x, nw, w, layer, out_dtype, tm, tn, name):
    m, k = x.shape
    n = w.shape[2]
    return pl.pallas_call(
        _norm_mm_body,
        grid=(m // tm, n // tn),
        in_specs=[pl.BlockSpec((tm, k), lambda i, j: (i, 0)),
                  _lspec(layer, (1, k)),
                  _lspec(layer, (k, tn), lambda i, j: (0, j))],
        out_specs=pl.BlockSpec((tm, tn), lambda i, j: (i, j)),
        out_shape=jax.ShapeDtypeStruct((m, n), out_dtype),
        scratch_shapes=[pltpu.VMEM((tm, k), BF16)],
        compiler_params=_cparams(("parallel", "arbitrary")),
        name=name,
    )(x, nw, w)


def _rope_apply(v, c, s):
    return v * c + pltpu.roll(v, 64, 1) * s


def _q_body(c_ref, nw_ref, w_ref, qnw_ref, ct_ref, st_ref, o_ref):
    cn = _rms(c_ref[...], nw_ref[...]).astype(BF16)
    acc = _dot(cn, w_ref[...])
    c = ct_ref[...]
    s = st_ref[...]
    wn = qnw_ref[:, :LANES]
    wr = qnw_ref[:, LANES:]
    scale = QK_DIM ** -0.5 * LOG2_E
    for h in range(MLA_HEADS):
        qn = acc[:, HEAD_W * h:HEAD_W * h + LANES]
        qr = acc[:, HEAD_W * h + LANES:HEAD_W * (h + 1)]
        ssq = jnp.sum(qn * qn + qr * qr, axis=-1, keepdims=True)
        rs = lax.rsqrt(ssq / QK_DIM + EPS)
        o_ref[:, HEAD_W * h:HEAD_W * h + LANES] = (qn * rs * wn * scale).astype(BF16)
        o_ref[:, HEAD_W * h + LANES:HEAD_W * (h + 1)] = (_rope_apply(qr * rs * wr, c, s) * scale).astype(BF16)


def _q_proj(small, nw, wq, qnw, ct, st, layer, tm):
    m = small.shape[0]
    n = MLA_HEADS * HEAD_W
    return pl.pallas_call(
        _q_body,
        grid=(m // tm,),
        in_specs=[pl.BlockSpec((tm, Q_LORA), lambda i: (i, 0)),
                  _lspec(layer, (1, Q_LORA)),
                  _lspec(layer, (Q_LORA, n)),
                  _lspec(layer, (1, HEAD_W)),
                  pl.BlockSpec((tm, LANES), lambda i: (i, 0)),
                  pl.BlockSpec((tm, LANES), lambda i: (i, 0))],
        out_specs=pl.BlockSpec((tm, n), lambda i: (i, 0)),
        out_shape=jax.ShapeDtypeStruct((m, n), BF16),
        compiler_params=_cparams(("parallel",)),
        name="q_proj",
    )(small, nw, wq, qnw, ct, st)


def _kv_body(c_ref, kr_ref, nw_ref, wk_ref, wvt_ref, kwn_ref, kwr_ref, ct_ref, st_ref, k_ref, vt_ref):
    cn = _rms(c_ref[...], nw_ref[...]).astype(BF16)
    acc = _dot(cn, wk_ref[...])
    vt_ref[0] = _dot_nt(wvt_ref[...], cn).astype(BF16)
    c = ct_ref[...]
    s = st_ref[...]
    kr = kr_ref[...]
    wn = kwn_ref[...]
    wr = kwr_ref[...]
    ssr = jnp.sum(kr * kr, axis=-1, keepdims=True)
    kr_rot = _rope_apply(kr * wr, c, s)
    for h in range(MLA_HEADS):
        kn = acc[:, LANES * h:LANES * (h + 1)]
        ssq = jnp.sum(kn * kn, axis=-1, keepdims=True) + ssr
        rs = lax.rsqrt(ssq / QK_DIM + EPS)
        k_ref[:, HEAD_W * h:HEAD_W * h + LANES] = (kn * rs * wn).astype(BF16)
        k_ref[:, HEAD_W * h + LANES:HEAD_W * (h + 1)] = (kr_rot * rs).astype(BF16)


def _kv_proj(small, nw, wk, wvt, kwn, kwr, ct, st, layer, tm):
    m = small.shape[0]
    nk = MLA_HEADS * HEAD_W
    nv = MLA_HEADS * V_DIM
    kr_blk = (Q_LORA + KV_LORA) // LANES
    return pl.pallas_call(
        _kv_body,
        grid=(m // tm,),
        in_specs=[pl.BlockSpec((tm, KV_LORA), lambda i: (i, 1)),
                  pl.BlockSpec((tm, LANES), lambda i: (i, kr_blk)),
                  _lspec(layer, (1, KV_LORA)),
                  _lspec(layer, (KV_LORA, MLA_HEADS * QK_NOPE)),
                  _lspec(layer, (nv, KV_LORA)),
                  _lspec(layer, (1, LANES)),
                  _lspec(layer, (1, LANES)),
                  pl.BlockSpec((tm, LANES), lambda i: (i, 0)),
                  pl.BlockSpec((tm, LANES), lambda i: (i, 0))],
        out_specs=[pl.BlockSpec((tm, nk), lambda i: (i, 0)),
                   pl.BlockSpec((1, nv, tm), lambda i: (i, 0, 0))],
        out_shape=[jax.ShapeDtypeStruct((m, nk), BF16), jax.ShapeDtypeStruct((m // tm, nv, tm), BF16)],
        compiler_params=_cparams(("parallel",)),
        name="kv_proj",
    )(small, small, nw, wk, wvt, kwn, kwr, ct, st)


def _attn_body(q_ref, k_ref, vt_ref, o_ref, s0_scr, s1_scr, p0_scr, p1_scr, m_scr, l_scr, acc_scr, *, tq, nq, hb):
    s_scr = (s0_scr, s1_scr)
    p_scr = (p0_scr, p1_scr)
    tk = tq // 2
    nk = tq // tk
    key_chunk = lax.broadcasted_iota(jnp.int32, (tk, tq), 0) // CHUNK
    qry_chunk = lax.broadcasted_iota(jnp.int32, (tk, tq), 1) // CHUNK
    diag_masks = [key_chunk + c * (tk // CHUNK) <= qry_chunk for c in range(nk)]

    def scores(qs, kb, slot, mask=None):
        koff = pl.multiple_of(kb * tk, tk)
        for hh in range(hb):
            s = _dot(k_ref[pl.ds(koff, tk), HEAD_W * hh:HEAD_W * (hh + 1)], qs[hh])
            s_scr[slot][hh] = s if mask is None else jnp.where(mask, s, -jnp.inf)

    def pv(pair, half):
        return [_dot(vt_ref[pair, V_DIM * hh:V_DIM * (hh + 1), half * tk:(half + 1) * tk], p_scr[half][hh])
                for hh in range(hb)]

    def softmax(slot, pvs):
        for hh in range(hb):
            m = m_scr[hh]
            s = s_scr[slot][hh]
            m_new = jnp.maximum(m, jnp.max(s, axis=0, keepdims=True))
            p = jnp.exp2(s - m_new)
            alpha = jnp.exp2(m - m_new)
            p_scr[slot][hh] = p.astype(BF16)
            m_scr[hh] = m_new
            l_scr[hh] = alpha * l_scr[hh] + jnp.sum(p, axis=0, keepdims=True)
            acc = acc_scr[hh] if pvs is None else acc_scr[hh] + pvs[hh]
            acc_scr[hh] = alpha * acc

    def q_block(qi, c):
        qoff = pl.multiple_of(qi * tq, tq)
        qs = [q_ref[pl.ds(qoff, tq), HEAD_W * hh:HEAD_W * (hh + 1)].astype(F32).T.astype(BF16) for hh in range(hb)]
        m_scr[...] = jnp.full(m_scr.shape, -jnp.inf, F32)
        l_scr[...] = jnp.zeros(l_scr.shape, F32)
        acc_scr[...] = jnp.zeros(acc_scr.shape, F32)
        d0 = nk * qi
        scores(qs, d0, 0, diag_masks[0])
        scores(qs, d0 + 1, 1, diag_masks[1])
        softmax(0, None)

        def body(r, c2):
            b0 = nk * (r - 1)
            a0 = jnp.where(r == 1, qi, r - 2)
            scores(qs, b0, 0)
            softmax(1, pv(a0, 0))
            scores(qs, b0 + 1, 1)
            softmax(0, pv(a0, 1))
            return c2

        lax.fori_loop(1, qi + 1, body, 0)
        a0 = jnp.where(qi == 0, qi, qi - 1)
        softmax(1, pv(a0, 0))
        pvs = pv(a0, 1)
        for hh in range(hb):
            o_ref[pl.ds(qoff, tq), V_DIM * hh:V_DIM * (hh + 1)] = ((acc_scr[hh] + pvs[hh]) / l_scr[hh]).T.astype(BF16)
        return c

    lax.fori_loop(0, nq, q_block, 0)


def _attention(q, k, vt, batch, seq, tq, hb):
    m = q.shape[0]
    nq = seq // tq
    tk = tq // 2
    assert vt.shape[2] == tq
    return pl.pallas_call(
        functools.partial(_attn_body, tq=tq, nq=nq, hb=hb),
        grid=(batch, MLA_HEADS // hb),
        in_specs=[pl.BlockSpec((seq, hb * HEAD_W), lambda b, h: (b, h)),
                  pl.BlockSpec((seq, hb * HEAD_W), lambda b, h: (b, h)),
                  pl.BlockSpec((nq, hb * V_DIM, tq), lambda b, h: (b, h, 0))],
        out_specs=pl.BlockSpec((seq, hb * V_DIM), lambda b, h: (b, h)),
        out_shape=jax.ShapeDtypeStruct((m, MLA_HEADS * V_DIM), BF16),
        scratch_shapes=[pltpu.VMEM((hb, tk, tq), F32), pltpu.VMEM((hb, tk, tq), F32),
                        pltpu.VMEM((hb, tk, tq), BF16), pltpu.VMEM((hb, tk, tq), BF16),
                        pltpu.VMEM((hb, 1, tq), F32), pltpu.VMEM((hb, 1, tq), F32), pltpu.VMEM((hb, V_DIM, tq), F32)],
        compiler_params=_cparams(("parallel", "arbitrary")),
        name="attention",
    )(q, k, vt)


def _silu(y):
    h = 0.5 * y
    return h + h * jnp.tanh(h)


def _split3(v):
    hi = v.astype(BF16)
    r = v - hi.astype(F32)
    mid = r.astype(BF16)
    lo = (r - mid.astype(F32)).astype(BF16)
    return hi, mid, lo


def _ssd_body(z_ref, x_ref, bc_ref, dt_ref, cwx_ref, cwb_ref, cbx_ref, cbb_ref, dtb_ref, alog_ref,
              dsk_ref, nw_ref, sel_ref, eye_ref, tri_ref, tri64_ref, shift_ref, y_ref, state_ref, xhalo, bchalo,
              *, cpb):
    t = CHUNK
    gn = SSM_GROUPS * SSM_STATE

    @pl.when(pl.program_id(1) == 0)
    def _():
        state_ref[...] = jnp.zeros_like(state_ref)
        xhalo[...] = jnp.zeros_like(xhalo)
        bchalo[...] = jnp.zeros_like(bchalo)

    def conv_silu(raw_ref, halo, w_ref, b_ref):
        raws = [raw_ref[c * t:(c + 1) * t, :] for c in range(cpb)]
        prevs = [halo[...]] + raws[:-1]
        halo[...] = raws[-1]
        shifted = [_dot(shift_ref[...], jnp.concatenate([prevs[c], raws[c]], axis=0)) for c in range(cpb)]
        outs = []
        for c in range(cpb):
            y = None
            for tap in range(CONV_WIDTH):
                shift = CONV_WIDTH - 1 - tap
                xt = raws[c].astype(F32) if shift == 0 else shifted[c][(shift - 1) * t:shift * t, :]
                term = xt * w_ref[tap:tap + 1, :]
                y = term if y is None else y + term
            outs.append(_silu(y + b_ref[...]))
        return outs

    xs_all = conv_silu(x_ref, xhalo, cwx_ref, cbx_ref)
    bc_all = conv_silu(bc_ref, bchalo, cwb_ref, cbb_ref)

    head_lane = lax.broadcasted_iota(jnp.int32, (t, LANES), 1) < SSM_HEADS
    neg_a = -jnp.exp(alog_ref[...])
    tri64 = tri64_ref[...]
    stacks = []
    for c in range(cpb):
        dt = jnp.where(head_lane, jax.nn.softplus(dt_ref[c * t:(c + 1) * t, :] + dtb_ref[...]), 0.0)
        h3 = _split3(dt * neg_a)
        a_cum = (_dot(tri64, h3[0]) + _dot(tri64, h3[1]) + _dot(tri64, h3[2])) * LOG2_E
        ea_c = jnp.where(head_lane, jnp.exp2(a_cum), 0.0)
        dte_c = jnp.where(head_lane, jnp.exp2(a_cum[t - 1:t, :] - a_cum), 0.0)
        stacks += [a_cum, dt, ea_c, dte_c]
    s3 = [part.astype(F32) for part in _split3(jnp.concatenate(stacks, axis=0))]
    packed = jnp.concatenate([(s3[0] + pltpu.roll(s3[1], SSM_HEADS, 1)).astype(BF16), s3[2].astype(BF16)], axis=1)
    ex_all = _dot(packed, sel_ref[...])
    left = lax.broadcasted_iota(jnp.int32, (t, LANES), 1) < SSM_HEADDIM
    zero_b = jnp.zeros((t, LANES), BF16)

    pre = []
    for c in range(cpb):
        ex = ex_all[4 * t * c:4 * t * (c + 1)]
        a_col = ex[:t]
        ea = ex[2 * t:3 * t]
        a_row = jnp.sum(jnp.where(eye_ref[...] > 0, a_col, 0.0), axis=0, keepdims=True)
        decay = jnp.exp2(jnp.where(tri_ref[...] > 0, a_col - a_row, -jnp.inf))
        xdt = xs_all[c] * ex[t:2 * t]
        pre.append((decay, ea, xdt.astype(BF16), (xdt * ex[3 * t:]).astype(BF16)))

    for c in range(cpb):
        decay, ea, xdt_b, xw_b = pre[c]
        xs, bc = xs_all[c], bc_all[c]
        chunk_decay = ea[t - 1:t, :]
        r0 = c * t
        for g in range(SSM_GROUPS):
            lo, hi = GROUP_W * g, GROUP_W * (g + 1)
            bg = bc[:, SSM_STATE * g:SSM_STATE * (g + 1)]
            cg_b = bc[:, gn + SSM_STATE * g:gn + SSM_STATE * (g + 1)].astype(BF16)
            bg_b = bg.astype(BF16)
            cb2 = _dot_nt(cg_b, jnp.concatenate([bg_b, bg_b], axis=0))
            yd = []
            for j in range(GROUP_W // LANES):
                l0 = lo + LANES * j
                mm = (cb2 * decay[:, l0:l0 + LANES]).astype(BF16)
                xp = xdt_b[:, l0:l0 + LANES]
                rhs = jnp.concatenate([jnp.where(left, xp, zero_b), jnp.where(left, zero_b, xp)], axis=0)
                yd.append(_dot(mm, rhs))
            y_diag = jnp.concatenate(yd, axis=1)
            st = state_ref[g]
            y_off = _dot(cg_b, st.astype(BF16)) * ea[:, lo:hi]
            inc = _dot(bg.T.astype(BF16), xw_b[:, lo:hi])
            state_ref[g] = st * chunk_decay[:, lo:hi] + inc
            yg = y_diag + y_off + dsk_ref[:, lo:hi] * xs[:, lo:hi]
            yg = yg * _silu(z_ref[r0:r0 + t, lo:hi].astype(F32))
            ms = jnp.mean(yg * yg, axis=-1, keepdims=True)
            y_ref[r0:r0 + t, lo:hi] = (yg * lax.rsqrt(ms + EPS) * nw_ref[:, lo:hi]).astype(BF16)


def _ssd(big, small, cwx, cwb, cbx, cbb, dtb, alog, dsk, nw, sel, eye_t, tri_t, tri64, shift_m, layer, batch, seq,
         cpb):
    m = big.shape[0]
    t = CHUNK
    tb = cpb * t
    nb = seq // tb
    bcw = 2 * SSM_GROUPS * SSM_STATE
    bc_blk = (2 * D_INNER) // bcw
    dt_blk = (SMALL_N - LANES) // LANES
    row = lambda b, c: b * nb + c
    const = lambda shape: pl.BlockSpec(shape, lambda b, c: (0,) * len(shape))
    return pl.pallas_call(
        functools.partial(_ssd_body, cpb=cpb),
        grid=(batch, nb),
        in_specs=[pl.BlockSpec((tb, D_INNER), lambda b, c: (row(b, c), 0)),
                  pl.BlockSpec((tb, D_INNER), lambda b, c: (row(b, c), 1)),
                  pl.BlockSpec((tb, bcw), lambda b, c: (row(b, c), bc_blk)),
                  pl.BlockSpec((tb, LANES), lambda b, c: (row(b, c), dt_blk)),
                  _lspec(layer, (CONV_WIDTH, D_INNER)), _lspec(layer, (CONV_WIDTH, bcw)),
                  _lspec(layer, (1, D_INNER)), _lspec(layer, (1, bcw)),
                  _lspec(layer, (1, LANES)), _lspec(layer, (1, LANES)),
                  _lspec(layer, (1, D_INNER)), _lspec(layer, (1, D_INNER)),
                  const((2 * LANES, D_INNER)), const((t, D_INNER)), const((t, D_INNER)), const((t, t)),
                  const(((CONV_WIDTH - 1) * t, 2 * t))],
        out_specs=pl.BlockSpec((tb, D_INNER), lambda b, c: (row(b, c), 0)),
        out_shape=jax.ShapeDtypeStruct((m, D_INNER), BF16),
        scratch_shapes=[pltpu.VMEM((SSM_GROUPS, SSM_STATE, GROUP_W), F32),
                        pltpu.VMEM((t, D_INNER), BF16),
                        pltpu.VMEM((t, bcw), BF16)],
        compiler_params=_cparams(("parallel", "arbitrary")),
        name="ssd",
    )(big, big, big, small, cwx, cwb, cbx, cbb, dtb, alog, dsk, nw, sel, eye_t, tri_t, tri64, shift_m)


def _merge_body(o_ref, y_ref, wa_ref, wb_ref, g0_ref, g1_ref, out_ref):
    ya = _dot(o_ref[...], wa_ref[...])
    yb = _dot(y_ref[...], wb_ref[...])
    g0 = jax.nn.sigmoid(g0_ref[...].astype(F32))
    g1 = jax.nn.sigmoid(g1_ref[...].astype(F32))
    out_ref[...] = (g0 * ya + g1 * yb).astype(out_ref.dtype)


def _merge(o, y, wa, wb, big, layer, tm, tn):
    m = o.shape[0]
    g0_blk = (2 * D_INNER + 2 * SSM_GROUPS * SSM_STATE) // tn
    g1_blk = g0_blk + D_MODEL // tn
    return pl.pallas_call(
        _merge_body,
        grid=(m // tm, D_MODEL // tn),
        in_specs=[pl.BlockSpec((tm, o.shape[1]), lambda i, j: (i, 0)),
                  pl.BlockSpec((tm, y.shape[1]), lambda i, j: (i, 0)),
                  _lspec(layer, (wa.shape[1], tn), lambda i, j: (0, j)),
                  _lspec(layer, (wb.shape[1], tn), lambda i, j: (0, j)),
                  pl.BlockSpec((tm, tn), lambda i, j: (i, g0_blk + j)),
                  pl.BlockSpec((tm, tn), lambda i, j: (i, g1_blk + j))],
        out_specs=pl.BlockSpec((tm, tn), lambda i, j: (i, j)),
        out_shape=jax.ShapeDtypeStruct((m, D_MODEL), BF16),
        compiler_params=_cparams(("parallel", "arbitrary")),
        name="merge",
    )(o, y, wa, wb, big, big)


def _res_mm_body(x_ref, a_ref, w_ref, o_ref):
    o_ref[...] = x_ref[...] + _dot(a_ref[...], w_ref[...])


def _res_matmul(x, a, w, layer, tm):
    m, n = x.shape
    k = a.shape[1]
    return pl.pallas_call(
        _res_mm_body,
        grid=(m // tm,),
        in_specs=[pl.BlockSpec((tm, n), lambda i: (i, 0)),
                  pl.BlockSpec((tm, k), lambda i: (i, 0)),
                  _lspec(layer, (k, n))],
        out_specs=pl.BlockSpec((tm, n), lambda i: (i, 0)),
        out_shape=jax.ShapeDtypeStruct((m, n), F32),
        compiler_params=_cparams(("parallel",)),
        name="out_proj",
    )(x, a, w)


def _mlp_body(x_ref, nw_ref, wu_ref, wd_ref, o_ref, h_ref):
    @pl.when(pl.program_id(1) == 0)
    def _():
        h_ref[...] = _rms(x_ref[...], nw_ref[...]).astype(BF16)
        o_ref[...] = x_ref[...]

    u = _dot(h_ref[...], wu_ref[...])
    hid = jnp.square(jnp.maximum(u, 0.0)).astype(BF16)
    o_ref[...] += _dot(hid, wd_ref[...])


def _mlp(x, nw, wu, wd, layer, tm, tf):
    m, d = x.shape
    f = wu.shape[2]
    return pl.pallas_call(
        _mlp_body,
        grid=(m // tm, f // tf),
        in_specs=[pl.BlockSpec((tm, d), lambda i, j: (i, 0), pipeline_mode=pl.Buffered(1)),
                  _lspec(layer, (1, d)),
                  _lspec(layer, (d, tf), lambda i, j: (0, j)),
                  _lspec(layer, (tf, d), lambda i, j: (j, 0))],
        out_specs=pl.BlockSpec((tm, d), lambda i, j: (i, 0)),
        out_shape=jax.ShapeDtypeStruct((m, d), F32),
        scratch_shapes=[pltpu.VMEM((tm, d), BF16)],
        compiler_params=_cparams(("parallel", "arbitrary")),
        name="mlp",
    )(x, nw, wu, wd)


def _ple_body(x_ref, nw_ref, wg_ref, p_ref, wp_ref, o_ref):
    x = x_ref[...]
    h = _rms(x, nw_ref[...]).astype(BF16)
    gate = jax.nn.sigmoid(_dot(h, wg_ref[...]))
    pe = _dot(p_ref[...].astype(BF16), wp_ref[...])
    o_ref[...] = x + pe * gate


def _ple(x, nw, wg, p, wp, layer, tm):
    m, d = x.shape
    return pl.pallas_call(
        _ple_body,
        grid=(m // tm,),
        in_specs=[pl.BlockSpec((tm, d), lambda i: (i, 0)),
                  _lspec(layer, (1, d)),
                  _lspec(layer, (d, d)),
                  _lspec(layer, (tm, PLE_DIM), lambda i: (i, 0)),
                  _lspec(layer, (PLE_DIM, d))],
        out_specs=pl.BlockSpec((tm, d), lambda i: (i, 0)),
        out_shape=jax.ShapeDtypeStruct((m, d), F32),
        compiler_params=_cparams(("parallel",)),
        name="ple",
    )(x, nw, wg, p, wp)


def _rope_cols(w):
    half = QK_ROPE // 2
    z = jnp.zeros(w.shape[:-1] + (LANES // 2 - half,), w.dtype)
    return jnp.concatenate([w[..., :half], z, w[..., half:], z], axis=-1)


def kernel(x, p, positions, norm_mix_w, w_in, q_a_norm_w, w_uq, kv_a_norm_w, w_ukv, q_norm_w, k_norm_w,
           w_o_mla, conv_w, conv_b, dt_bias, a_log, d_skip, ssm_norm_w, w_o_ssm, w_out, norm_mlp_w, w_up,
           w_down, ple_norm_w, w_ple_gate, w_ple):
    batch, seq, d = x.shape
    m = batch * seq
    depth = w_in.shape[0]
    tm_big = min(1024, m)
    tm_mid = min(512, m)
    tm_small = min(256, m)
    tq = min(512, seq)

    inv_freq = 1.0 / (ROPE_THETA ** (jnp.arange(0, QK_ROPE, 2, dtype=F32) / QK_ROPE))
    freq = _rope_cols(jnp.concatenate([inv_freq, inv_freq]))[None, :]
    sign = _rope_cols(jnp.concatenate([-jnp.ones_like(inv_freq), jnp.ones_like(inv_freq)]))[None, :]
    ct, st = _rope_tables(positions.reshape(m, 1), freq, sign, tm_big)

    lane = jnp.arange(D_INNER)
    sel_row = jnp.arange(2 * LANES)[:, None]
    sel = ((sel_row % SSM_HEADS == (lane // SSM_HEADDIM)[None, :]) & (sel_row < 3 * SSM_HEADS)).astype(BF16)
    rows = jnp.arange(CHUNK)[:, None]
    eye_t = (rows == (lane % SSM_HEADDIM)[None, :]).astype(F32)
    tri_t = (rows >= (lane % SSM_HEADDIM)[None, :]).astype(F32)
    tri64 = (rows >= jnp.arange(CHUNK)[None, :]).astype(BF16)
    sh_rows = jnp.arange((CONV_WIDTH - 1) * CHUNK)[:, None]
    shift_m = (jnp.arange(2 * CHUNK)[None, :] == CHUNK + sh_rows % CHUNK - (sh_rows // CHUNK + 1)).astype(BF16)

    s0 = Q_LORA + KV_LORA + QK_ROPE
    s1 = s0 + D_INNER + (D_INNER + 2 * SSM_GROUPS * SSM_STATE)
    s2 = s1 + SSM_HEADS

    w_in_b = w_in.astype(BF16)
    w_small = jnp.concatenate(
        [w_in_b[..., :Q_LORA + KV_LORA], _rope_cols(w_in_b[..., Q_LORA + KV_LORA:s0]), w_in_b[..., s1:s2],
         jnp.zeros((depth, d, LANES - SSM_HEADS), BF16)], axis=-1)
    w_big = jnp.concatenate([w_in_b[..., s0:s1], w_in_b[..., s2:]], axis=-1)
    wq4 = w_uq.reshape(depth, Q_LORA, MLA_HEADS, QK_DIM)
    wq = jnp.concatenate([wq4[..., :QK_NOPE], _rope_cols(wq4[..., QK_NOPE:])], axis=-1)
    wq = wq.reshape(depth, Q_LORA, MLA_HEADS * HEAD_W).astype(BF16)
    wkv4 = w_ukv.reshape(depth, KV_LORA, MLA_HEADS, QK_NOPE + V_DIM)
    wk = wkv4[..., :QK_NOPE].reshape(depth, KV_LORA, -1).astype(BF16)
    wvt = jnp.swapaxes(wkv4[..., QK_NOPE:].reshape(depth, KV_LORA, -1), 1, 2).astype(BF16)
    w_o_mla_b = w_o_mla.astype(BF16)
    w_o_ssm_b = w_o_ssm.astype(BF16)
    w_out_b = w_out.astype(BF16)
    w_up_b = w_up.astype(BF16)
    w_down_b = w_down.astype(BF16)
    w_gate_b = w_ple_gate.astype(BF16)
    w_ple_b = w_ple.astype(BF16)

    rows3 = lambda v: v.reshape(depth, 1, -1).astype(F32)
    pad_h = jnp.zeros((depth, LANES - SSM_HEADS), F32)
    qnw = rows3(jnp.concatenate([q_norm_w[:, :QK_NOPE], _rope_cols(q_norm_w[:, QK_NOPE:])], axis=-1))
    kwn = rows3(k_norm_w[:, :QK_NOPE])
    kwr = rows3(_rope_cols(k_norm_w[:, QK_NOPE:]))
    dtb = rows3(jnp.concatenate([dt_bias, pad_h], axis=-1))
    alog = rows3(jnp.concatenate([a_log, pad_h], axis=-1))
    dsk = rows3(jnp.repeat(d_skip, SSM_HEADDIM, axis=-1))
    mix_nw, qa_nw, kva_nw = rows3(norm_mix_w), rows3(q_a_norm_w), rows3(kv_a_norm_w)
    ssm_nw, mlp_nw, ple_nw = rows3(ssm_norm_w), rows3(norm_mlp_w), rows3(ple_norm_w)
    cwx, cwb = conv_w[..., :D_INNER], conv_w[..., D_INNER:]
    cbx, cbb = rows3(conv_b[:, :D_INNER]), rows3(conv_b[:, D_INNER:])
    p3 = p.reshape(depth, m, PLE_DIM)

    xf = x.reshape(m, d)
    for i in range(depth):
        small = _norm_matmul(xf, mix_nw, w_small, i, F32, tm_mid, SMALL_N, "in_proj_small")
        big = _norm_matmul(xf, mix_nw, w_big, i, BF16, tm_big, 1024, "in_proj_big")
        q = _q_proj(small, qa_nw, wq, qnw, ct, st, i, tm_small)
        k, vt = _kv_proj(small, kva_nw, wk, wvt, kwn, kwr, ct, st, i, tq)
        o = _attention(q, k, vt, batch, seq, tq, 2)
        y = _ssd(big, small, cwx, cwb, cbx, cbb, dtb, alog, dsk, ssm_nw, sel, eye_t, tri_t, tri64, shift_m, i,
                 batch, seq, 2)
        merged = _merge(o, y, w_o_mla_b, w_o_ssm_b, big, i, tm_big, 512)
        xf = _res_matmul(xf, merged, w_out_b, i, tm_mid)
        xf = _mlp(xf, mlp_nw, w_up_b, w_down_b, i, tm_mid, 1024)
        xf = _ple(xf, ple_nw, w_gate_b, p3, w_ple_b, i, tm_mid)
    return xf.reshape(batch, seq, d)
```

```python
import functools

import jax
import jax.numpy as jnp
from jax import lax
from jax.experimental import pallas as pl
from jax.experimental.pallas import tpu as pltpu

F32 = jnp.float32
BF16 = jnp.bfloat16

D_MODEL = 2048
DEPTH = 4
CHUNK = 64
PLE_DIM = 256
EPS = 1e-6
MLA_HEADS = 16
Q_LORA = 512
KV_LORA = 512
QK_NOPE = 128
QK_ROPE = 64
V_DIM = 128
QK_DIM = QK_NOPE + QK_ROPE
ROPE_THETA = 10000.0
D_INNER = 2 * D_MODEL
SSM_HEADDIM = 64
SSM_HEADS = D_INNER // SSM_HEADDIM
SSM_GROUPS = 8
SSM_STATE = 128
GROUP_W = D_INNER // SSM_GROUPS
CONV_WIDTH = 4
D_FF = 4 * D_MODEL

LOG2_E = 1.4426950408889634
LANES = 128
ACC_ROWS = V_DIM + 16
HEAD_W = 2 * LANES
SMALL_N = Q_LORA + KV_LORA + 2 * LANES
BIG_N = D_INNER + (D_INNER + 2 * SSM_GROUPS * SSM_STATE) + 2 * D_MODEL
VMEM_LIMIT = 56 * 1024 * 1024


def _cparams(sem):
    return pltpu.CompilerParams(dimension_semantics=sem, vmem_limit_bytes=VMEM_LIMIT)


def _rms(xf, w):
    ms = jnp.mean(xf * xf, axis=-1, keepdims=True)
    return xf * lax.rsqrt(ms + EPS) * w


def _lspec(layer, block, imap=None):
    if imap is None:
        imap = lambda *g: (0,) * len(block)
    return pl.BlockSpec((None,) + tuple(block), lambda *g: (layer,) + tuple(imap(*g)))


def _dot(a, b):
    return jnp.dot(a, b, preferred_element_type=F32)


def _dot_nt(a, b):
    return lax.dot_general(a, b, (((1,), (1,)), ((), ())), preferred_element_type=F32)


def _rope_body(pos_ref, freq_ref, sign_ref, ct_ref, st_ref):
    ang = pos_ref[...].astype(F32) * freq_ref[...]
    ct_ref[...] = jnp.cos(ang)
    st_ref[...] = jnp.sin(ang) * sign_ref[...]


def _rope_tables(pos_col, freq, sign, tm):
    m = pos_col.shape[0]
    return pl.pallas_call(
        _rope_body,
        grid=(m // tm,),
        in_specs=[pl.BlockSpec((tm, 1), lambda i: (i, 0)),
                  pl.BlockSpec((1, LANES), lambda i: (0, 0)),
                  pl.BlockSpec((1, LANES), lambda i: (0, 0))],
        out_specs=[pl.BlockSpec((tm, LANES), lambda i: (i, 0)),
                   pl.BlockSpec((tm, LANES), lambda i: (i, 0))],
        out_shape=[jax.ShapeDtypeStruct((m, LANES), F32)] * 2,
        compiler_params=_cparams(("parallel",)),
        name="rope_tables",
    )(pos_col, freq, sign)


def _norm_mm_body(x_ref, nw_ref, w_ref, o_ref, h_ref):
    @pl.when(pl.program_id(1) == 0)
    def _():
        h_ref[...] = _rms(x_ref[...], nw_ref[...]).astype(BF16)

    o_ref[...] = _dot(h_ref[...], w_ref[...]).astype(o_ref.dtype)


def _norm_matmul(x, nw, w, layer, out_dtype, tm, tn, name):
    m, k = x.shape
    n = w.shape[2]
    return pl.pallas_call(
        _norm_mm_body,
        grid=(m // tm, n // tn),
        in_specs=[pl.BlockSpec((tm, k), lambda i, j: (i, 0)),
                  _lspec(layer, (1, k)),
                  _lspec(layer, (k, tn), lambda i, j: (0, j))],
        out_specs=pl.BlockSpec((tm, tn), lambda i, j: (i, j)),
        out_shape=jax.ShapeDtypeStruct((m, n), out_dtype),
        scratch_shapes=[pltpu.VMEM((tm, k), BF16)],
        compiler_params=_cparams(("parallel", "arbitrary")),
        name=name,
    )(x, nw, w)


def _rope_apply(v, c, s):
    return v * c + pltpu.roll(v, 64, 1) * s


def _q_body(c_ref, nw_ref, w_ref, qnw_ref, ct_ref, st_ref, o_ref):
    cn = _rms(c_ref[...], nw_ref[...]).astype(BF16)
    acc = _dot(cn, w_ref[...])
    c = ct_ref[...]
    s = st_ref[...]
    wn = qnw_ref[:, :LANES]
    wr = qnw_ref[:, LANES:]
    scale = QK_DIM ** -0.5 * LOG2_E
    for h in range(MLA_HEADS):
        qn = acc[:, HEAD_W * h:HEAD_W * h + LANES]
        qr = acc[:, HEAD_W * h + LANES:HEAD_W * (h + 1)]
        ssq = jnp.sum(qn * qn + qr * qr, axis=-1, keepdims=True)
        rs = lax.rsqrt(ssq / QK_DIM + EPS)
        o_ref[:, HEAD_W * h:HEAD_W * h + LANES] = (qn * rs * wn * scale).astype(BF16)
        o_ref[:, HEAD_W * h + LANES:HEAD_W * (h + 1)] = (_rope_apply(qr * rs * wr, c, s) * scale).astype(BF16)


def _q_proj(small, nw, wq, qnw, ct, st, layer, tm):
    m = small.shape[0]
    n = MLA_HEADS * HEAD_W
    return pl.pallas_call(
        _q_body,
        grid=(m // tm,),
        in_specs=[pl.BlockSpec((tm, Q_LORA), lambda i: (i, 0)),
                  _lspec(layer, (1, Q_LORA)),
                  _lspec(layer, (Q_LORA, n)),
                  _lspec(layer, (1, HEAD_W)),
                  pl.BlockSpec((tm, LANES), lambda i: (i, 0)),
                  pl.BlockSpec((tm, LANES), lambda i: (i, 0))],
        out_specs=pl.BlockSpec((tm, n), lambda i: (i, 0)),
        out_shape=jax.ShapeDtypeStruct((m, n), BF16),
        compiler_params=_cparams(("parallel",)),
        name="q_proj",
    )(small, nw, wq, qnw, ct, st)


def _kv_body(c_ref, kr_ref, nw_ref, wk_ref, wvt_ref, kwn_ref, kwr_ref, ct_ref, st_ref, k_ref, vt_ref):
    cn = _rms(c_ref[...], nw_ref[...]).astype(BF16)
    acc = _dot(cn, wk_ref[...])
    vt_ref[0] = _dot_nt(wvt_ref[...], cn).astype(BF16)
    c = ct_ref[...]
    s = st_ref[...]
    kr = kr_ref[...]
    wn = kwn_ref[...]
    wr = kwr_ref[...]
    ssr = jnp.sum(kr * kr, axis=-1, keepdims=True)
    kr_rot = _rope_apply(kr * wr, c, s)
    for h in range(MLA_HEADS):
        kn = acc[:, LANES * h:LANES * (h + 1)]
        ssq = jnp.sum(kn * kn, axis=-1, keepdims=True) + ssr
        rs = lax.rsqrt(ssq / QK_DIM + EPS)
        k_ref[:, HEAD_W * h:HEAD_W * h + LANES] = (kn * rs * wn).astype(BF16)
        k_ref[:, HEAD_W * h + LANES:HEAD_W * (h + 1)] = (kr_rot * rs).astype(BF16)


def _kv_proj(small, nw, wk, wvt, kwn, kwr, ct, st, layer, tm):
    m = small.shape[0]
    nk = MLA_HEADS * HEAD_W
    nv = MLA_HEADS * V_DIM
    kr_blk = (Q_LORA + KV_LORA) // LANES
    return pl.pallas_call(
        _kv_body,
        grid=(m // tm,),
        in_specs=[pl.BlockSpec((tm, KV_LORA), lambda i: (i, 1)),
                  pl.BlockSpec((tm, LANES), lambda i: (i, kr_blk)),
                  _lspec(layer, (1, KV_LORA)),
                  _lspec(layer, (KV_LORA, MLA_HEADS * QK_NOPE)),
                  _lspec(layer, (nv, KV_LORA)),
                  _lspec(layer, (1, LANES)),
                  _lspec(layer, (1, LANES)),
                  pl.BlockSpec((tm, LANES), lambda i: (i, 0)),
                  pl.BlockSpec((tm, LANES), lambda i: (i, 0))],
        out_specs=[pl.BlockSpec((tm, nk), lambda i: (i, 0)),
                   pl.BlockSpec((1, nv, tm), lambda i: (i, 0, 0))],
        out_shape=[jax.ShapeDtypeStruct((m, nk), BF16), jax.ShapeDtypeStruct((m // tm, nv, tm), BF16)],
        compiler_params=_cparams(("parallel",)),
        name="kv_proj",
    )(small, small, nw, wk, wvt, kwn, kwr, ct, st)


def _attn_body(q_ref, k_ref, vt_ref, o_ref, s0_scr, s1_scr, p0_scr, p1_scr, m_scr, a_scr, acc_scr, *, tq, nq, hb):
    s_scr = (s0_scr, s1_scr)
    p_scr = (p0_scr, p1_scr)
    tk = tq // 2
    key_chunk = lax.broadcasted_iota(jnp.int32, (tk, tq), 0) // CHUNK
    qry_chunk = lax.broadcasted_iota(jnp.int32, (tk, tq), 1) // CHUNK
    diag_masks = [key_chunk + h * (tk // CHUNK) <= qry_chunk for h in range(2)]
    ones_rows = jnp.ones((ACC_ROWS - V_DIM, tk), BF16)

    def scores(qs, kpair, h, mask=None):
        koff = pl.multiple_of(kpair * tq + h * tk, tk)
        for hh in range(hb):
            s = _dot(k_ref[pl.ds(koff, tk), HEAD_W * hh:HEAD_W * (hh + 1)], qs[hh])
            s_scr[h][hh] = s if mask is None else jnp.where(mask, s, -jnp.inf)

    def pv(kpair, h):
        return [_dot(jnp.concatenate([vt_ref[kpair, V_DIM * hh:V_DIM * (hh + 1), h * tk:(h + 1) * tk],
                                      ones_rows], axis=0), p_scr[h][hh])
                for hh in range(hb)]

    def softmax(h, pvs):
        for hh in range(hb):
            m = m_scr[hh]
            s = s_scr[h][hh]
            m_new = jnp.maximum(m, jnp.max(s, axis=0, keepdims=True))
            p_scr[h][hh] = jnp.exp2((s - m_new).astype(BF16))
            alpha = jnp.exp2(m - m_new)
            acc_scr[hh] = alpha * (acc_scr[hh] + a_scr[hh] * pvs[hh])
            a_scr[hh] = alpha
            m_scr[hh] = m_new

    def q_block(qi, c):
        qoff = pl.multiple_of(qi * tq, tq)
        qs = [q_ref[pl.ds(qoff, tq), HEAD_W * hh:HEAD_W * (hh + 1)].astype(F32).T.astype(BF16) for hh in range(hb)]
        m_scr[...] = jnp.full(m_scr.shape, -jnp.inf, F32)
        a_scr[...] = jnp.zeros(a_scr.shape, F32)
        acc_scr[...] = jnp.zeros(acc_scr.shape, F32)
        for h in range(2):
            p_scr[h][...] = jnp.zeros(p_scr[h].shape, BF16)
            scores(qs, qi, h, diag_masks[h])

        def kpair_of(r):
            return jnp.where(r == 0, qi, r - 1)

        def body(r, c2):
            prev = kpair_of(jnp.maximum(r - 1, 0))
            for h in range(2):
                pvs = pv(prev, h)
                softmax(h, pvs)
                scores(qs, r, h)
            return c2

        lax.fori_loop(0, qi, body, 0)
        prev = kpair_of(jnp.maximum(qi - 1, 0))
        for h in range(2):
            softmax(h, pv(prev, h))
        last = kpair_of(qi)
        pv0, pv1 = pv(last, 0), pv(last, 1)
        for hh in range(hb):
            acc = acc_scr[hh] + a_scr[hh] * pv0[hh] + pv1[hh]
            o_ref[pl.ds(qoff, tq), V_DIM * hh:V_DIM * (hh + 1)] = (
                acc[:V_DIM] / acc[V_DIM:V_DIM + 1]).T.astype(BF16)
        return c

    lax.fori_loop(0, nq, q_block, 0)


def _attention(q, k, vt, batch, seq, tq, hb):
    m = q.shape[0]
    nq = seq // tq
    tk = tq // 2
    assert vt.shape[2] == tq
    return pl.pallas_call(
        functools.partial(_attn_body, tq=tq, nq=nq, hb=hb),
        grid=(batch, MLA_HEADS // hb),
        in_specs=[pl.BlockSpec((seq, hb * HEAD_W), lambda b, h: (b, h)),
                  pl.BlockSpec((seq, hb * HEAD_W), lambda b, h: (b, h)),
                  pl.BlockSpec((nq, hb * V_DIM, tq), lambda b, h: (b, h, 0))],
        out_specs=pl.BlockSpec((seq, hb * V_DIM), lambda b, h: (b, h)),
        out_shape=jax.ShapeDtypeStruct((m, MLA_HEADS * V_DIM), BF16),
        scratch_shapes=[pltpu.VMEM((hb, tk, tq), F32), pltpu.VMEM((hb, tk, tq), F32),
                        pltpu.VMEM((hb, tk, tq), BF16), pltpu.VMEM((hb, tk, tq), BF16),
                        pltpu.VMEM((hb, 1, tq), F32), pltpu.VMEM((hb, 1, tq), F32),
                        pltpu.VMEM((hb, ACC_ROWS, tq), F32)],
        compiler_params=_cparams(("parallel", "arbitrary")),
        name="attention",
    )(q, k, vt)


def _silu(y):
    h = 0.5 * y
    return h + h * jnp.tanh(h)


def _split3(v):
    hi = v.astype(BF16)
    r = v - hi.astype(F32)
    mid = r.astype(BF16)
    lo = (r - mid.astype(F32)).astype(BF16)
    return hi, mid, lo


def _ssd_body(z_ref, x_ref, bc_ref, dt_ref, cwx_ref, cwb_ref, cbx_ref, cbb_ref, dtb_ref, alog_ref,
              dsk_ref, nw_ref, sel_ref, eye_ref, tri_ref, tri64_ref, shift_ref, y_ref, state_ref, xhalo, bchalo,
              *, cpb):
    t = CHUNK
    gn = SSM_GROUPS * SSM_STATE

    @pl.when(pl.program_id(1) == 0)
    def _():
        state_ref[...] = jnp.zeros_like(state_ref)
        xhalo[...] = jnp.zeros_like(xhalo)
        bchalo[...] = jnp.zeros_like(bchalo)

    def conv_silu(raw_ref, halo, w_ref, b_ref):
        raws = [raw_ref[c * t:(c + 1) * t, :] for c in range(cpb)]
        prevs = [halo[...]] + raws[:-1]
        halo[...] = raws[-1]
        shifted = [_dot(shift_ref[...], jnp.concatenate([prevs[c], raws[c]], axis=0)) for c in range(cpb)]
        outs = []
        for c in range(cpb):
            y = None
            for tap in range(CONV_WIDTH):
                shift = CONV_WIDTH - 1 - tap
                xt = raws[c].astype(F32) if shift == 0 else shifted[c][(shift - 1) * t:shift * t, :]
                term = xt * w_ref[tap:tap + 1, :]
                y = term if y is None else y + term
            outs.append(_silu(y + b_ref[...]))
        return outs

    xs_all = conv_silu(x_ref, xhalo, cwx_ref, cbx_ref)
    bc_all = conv_silu(bc_ref, bchalo, cwb_ref, cbb_ref)

    head_lane = lax.broadcasted_iota(jnp.int32, (t, LANES), 1) < SSM_HEADS
    neg_a = -jnp.exp(alog_ref[...])
    tri64 = tri64_ref[...]
    stacks = []
    for c in range(cpb):
        dt = jnp.where(head_lane, jax.nn.softplus(dt_ref[c * t:(c + 1) * t, :] + dtb_ref[...]), 0.0)
        h3 = _split3(dt * neg_a)
        a_cum = (_dot(tri64, h3[0]) + _dot(tri64, h3[1]) + _dot(tri64, h3[2])) * LOG2_E
        ea_c = jnp.where(head_lane, jnp.exp2(a_cum), 0.0)
        dte_c = jnp.where(head_lane, jnp.exp2(a_cum[t - 1:t, :] - a_cum), 0.0)
        stacks += [a_cum, dt, ea_c, dte_c]
    s3 = [part.astype(F32) for part in _split3(jnp.concatenate(stacks, axis=0))]
    packed = jnp.concatenate([(s3[0] + pltpu.roll(s3[1], SSM_HEADS, 1)).astype(BF16), s3[2].astype(BF16)], axis=1)
    ex_all = _dot(packed, sel_ref[...])
    left = lax.broadcasted_iota(jnp.int32, (t, LANES), 1) < SSM_HEADDIM
    zero_b = jnp.zeros((t, LANES), BF16)

    pre = []
    for c in range(cpb):
        ex = ex_all[4 * t * c:4 * t * (c + 1)]
        a_col = ex[:t]
        ea = ex[2 * t:3 * t]
        a_row = jnp.sum(jnp.where(eye_ref[...] > 0, a_col, 0.0), axis=0, keepdims=True)
        decay = jnp.exp2(jnp.where(tri_ref[...] > 0, a_col - a_row, -jnp.inf))
        xdt = xs_all[c] * ex[t:2 * t]
        pre.append((decay, ea, xdt.astype(BF16), (xdt * ex[3 * t:]).astype(BF16)))

    for c in range(cpb):
        decay, ea, xdt_b, xw_b = pre[c]
        xs, bc = xs_all[c], bc_all[c]
        chunk_decay = ea[t - 1:t, :]
        r0 = c * t
        for g in range(SSM_GROUPS):
            lo, hi = GROUP_W * g, GROUP_W * (g + 1)
            bg = bc[:, SSM_STATE * g:SSM_STATE * (g + 1)]
            cg_b = bc[:, gn + SSM_STATE * g:gn + SSM_STATE * (g + 1)].astype(BF16)
            bg_b = bg.astype(BF16)
            cb2 = _dot_nt(cg_b, jnp.concatenate([bg_b, bg_b], axis=0))
            yd = []
            for j in range(GROUP_W // LANES):
                l0 = lo + LANES * j
                mm = (cb2 * decay[:, l0:l0 + LANES]).astype(BF16)
                xp = xdt_b[:, l0:l0 + LANES]
                rhs = jnp.concatenate([jnp.where(left, xp, zero_b), jnp.where(left, zero_b, xp)], axis=0)
                yd.append(_dot(mm, rhs))
            y_diag = jnp.concatenate(yd, axis=1)
            st = state_ref[g]
            y_off = _dot(cg_b, st.astype(BF16)) * ea[:, lo:hi]
            inc = _dot(bg.T.astype(BF16), xw_b[:, lo:hi])
            state_ref[g] = st * chunk_decay[:, lo:hi] + inc
            yg = y_diag + y_off + dsk_ref[:, lo:hi] * xs[:, lo:hi]
            yg = yg * _silu(z_ref[r0:r0 + t, lo:hi].astype(F32))
            ms = jnp.mean(yg * yg, axis=-1, keepdims=True)
            y_ref[r0:r0 + t, lo:hi] = (yg * lax.rsqrt(ms + EPS) * nw_ref[:, lo:hi]).astype(BF16)


def _ssd(big, small, cwx, cwb, cbx, cbb, dtb, alog, dsk, nw, sel, eye_t, tri_t, tri64, shift_m, layer, batch, seq,
         cpb):
    m = big.shape[0]
    t = CHUNK
    tb = cpb * t
    nb = seq // tb
    bcw = 2 * SSM_GROUPS * SSM_STATE
    bc_blk = (2 * D_INNER) // bcw
    dt_blk = (SMALL_N - LANES) // LANES
    row = lambda b, c: b * nb + c
    const = lambda shape: pl.BlockSpec(shape, lambda b, c: (0,) * len(shape))
    return pl.pallas_call(
        functools.partial(_ssd_body, cpb=cpb),
        grid=(batch, nb),
        in_specs=[pl.BlockSpec((tb, D_INNER), lambda b, c: (row(b, c), 0)),
                  pl.BlockSpec((tb, D_INNER), lambda b, c: (row(b, c), 1)),
                  pl.BlockSpec((tb, bcw), lambda b, c: (row(b, c), bc_blk)),
                  pl.BlockSpec((tb, LANES), lambda b, c: (row(b, c), dt_blk)),
                  _lspec(layer, (CONV_WIDTH, D_INNER)), _lspec(layer, (CONV_WIDTH, bcw)),
                  _lspec(layer, (1, D_INNER)), _lspec(layer, (1, bcw)),
                  _lspec(layer, (1, LANES)), _lspec(layer, (1, LANES)),
                  _lspec(layer, (1, D_INNER)), _lspec(layer, (1, D_INNER)),
                  const((2 * LANES, D_INNER)), const((t, D_INNER)), const((t, D_INNER)), const((t, t)),
                  const(((CONV_WIDTH - 1) * t, 2 * t))],
        out_specs=pl.BlockSpec((tb, D_INNER), lambda b, c: (row(b, c), 0)),
        out_shape=jax.ShapeDtypeStruct((m, D_INNER), BF16),
        scratch_shapes=[pltpu.VMEM((SSM_GROUPS, SSM_STATE, GROUP_W), F32),
                        pltpu.VMEM((t, D_INNER), BF16),
                        pltpu.VMEM((t, bcw), BF16)],
        compiler_params=_cparams(("parallel", "arbitrary")),
        name="ssd",
    )(big, big, big, small, cwx, cwb, cbx, cbb, dtb, alog, dsk, nw, sel, eye_t, tri_t, tri64, shift_m)


def _merge_body(o_ref, y_ref, wa_ref, wb_ref, g0_ref, g1_ref, out_ref):
    ya = _dot(o_ref[...], wa_ref[...])
    yb = _dot(y_ref[...], wb_ref[...])
    g0 = jax.nn.sigmoid(g0_ref[...].astype(F32))
    g1 = jax.nn.sigmoid(g1_ref[...].astype(F32))
    out_ref[...] = (g0 * ya + g1 * yb).astype(out_ref.dtype)


def _merge(o, y, wa, wb, big, layer, tm, tn):
    m = o.shape[0]
    g0_blk = (2 * D_INNER + 2 * SSM_GROUPS * SSM_STATE) // tn
    g1_blk = g0_blk + D_MODEL // tn
    return pl.pallas_call(
        _merge_body,
        grid=(m // tm, D_MODEL // tn),
        in_specs=[pl.BlockSpec((tm, o.shape[1]), lambda i, j: (i, 0)),
                  pl.BlockSpec((tm, y.shape[1]), lambda i, j: (i, 0)),
                  _lspec(layer, (wa.shape[1], tn), lambda i, j: (0, j)),
                  _lspec(layer, (wb.shape[1], tn), lambda i, j: (0, j)),
                  pl.BlockSpec((tm, tn), lambda i, j: (i, g0_blk + j)),
                  pl.BlockSpec((tm, tn), lambda i, j: (i, g1_blk + j))],
        out_specs=pl.BlockSpec((tm, tn), lambda i, j: (i, j)),
        out_shape=jax.ShapeDtypeStruct((m, D_MODEL), BF16),
        compiler_params=_cparams(("parallel", "arbitrary")),
        name="merge",
    )(o, y, wa, wb, big, big)


def _res_mm_body(x_ref, a_ref, w_ref, o_ref):
    o_ref[...] = x_ref[...] + _dot(a_ref[...], w_ref[...])


def _res_matmul(x, a, w, layer, tm):
    m, n = x.shape
    k = a.shape[1]
    return pl.pallas_call(
        _res_mm_body,
        grid=(m // tm,),
        in_specs=[pl.BlockSpec((tm, n), lambda i: (i, 0)),
                  pl.BlockSpec((tm, k), lambda i: (i, 0)),
                  _lspec(layer, (k, n))],
        out_specs=pl.BlockSpec((tm, n), lambda i: (i, 0)),
        out_shape=jax.ShapeDtypeStruct((m, n), F32),
        compiler_params=_cparams(("parallel",)),
        name="out_proj",
    )(x, a, w)


def _mlp_body(x_ref, nw_ref, wu_ref, wd_ref, o_ref, h_ref):
    @pl.when(pl.program_id(1) == 0)
    def _():
        h_ref[...] = _rms(x_ref[...], nw_ref[...]).astype(BF16)
        o_ref[...] = x_ref[...]

    u = _dot(h_ref[...], wu_ref[...])
    hid = jnp.square(jnp.maximum(u, 0.0)).astype(BF16)
    o_ref[...] += _dot(hid, wd_ref[...])


def _mlp(x, nw, wu, wd, layer, tm, tf):
    m, d = x.shape
    f = wu.shape[2]
    return pl.pallas_call(
        _mlp_body,
        grid=(m // tm, f // tf),
        in_specs=[pl.BlockSpec((tm, d), lambda i, j: (i, 0), pipeline_mode=pl.Buffered(1)),
                  _lspec(layer, (1, d)),
                  _lspec(layer, (d, tf), lambda i, j: (0, j)),
                  _lspec(layer, (tf, d), lambda i, j: (j, 0))],
        out_specs=pl.BlockSpec((tm, d), lambda i, j: (i, 0)),
        out_shape=jax.ShapeDtypeStruct((m, d), F32),
        scratch_shapes=[pltpu.VMEM((tm, d), BF16)],
        compiler_params=_cparams(("parallel", "arbitrary")),
        name="mlp",
    )(x, nw, wu, wd)


def _ple_body(x_ref, nw_ref, wg_ref, p_ref, wp_ref, o_ref):
    x = x_ref[...]
    h = _rms(x, nw_ref[...]).astype(BF16)
    gate = jax.nn.sigmoid(_dot(h, wg_ref[...]))
    pe = _dot(p_ref[...].astype(BF16), wp_ref[...])
    o_ref[...] = x + pe * gate


def _ple(x, nw, wg, p, wp, layer, tm):
    m, d = x.shape
    return pl.pallas_call(
        _ple_body,
        grid=(m // tm,),
        in_specs=[pl.BlockSpec((tm, d), lambda i: (i, 0)),
                  _lspec(layer, (1, d)),
                  _lspec(layer, (d, d)),
                  _lspec(layer, (tm, PLE_DIM), lambda i: (i, 0)),
                  _lspec(layer, (PLE_DIM, d))],
        out_specs=pl.BlockSpec((tm, d), lambda i: (i, 0)),
        out_shape=jax.ShapeDtypeStruct((m, d), F32),
        compiler_params=_cparams(("parallel",)),
        name="ple",
    )(x, nw, wg, p, wp)


def _rope_cols(w):
    half = QK_ROPE // 2
    z = jnp.zeros(w.shape[:-1] + (LANES // 2 - half,), w.dtype)
    return jnp.concatenate([w[..., :half], z, w[..., half:], z], axis=-1)


def kernel(x, p, positions, norm_mix_w, w_in, q_a_norm_w, w_uq, kv_a_norm_w, w_ukv, q_norm_w, k_norm_w,
           w_o_mla, conv_w, conv_b, dt_bias, a_log, d_skip, ssm_norm_w, w_o_ssm, w_out, norm_mlp_w, w_up,
           w_down, ple_norm_w, w_ple_gate, w_ple):
    batch, seq, d = x.shape
    m = batch * seq
    depth = w_in.shape[0]
    tm_big = min(1024, m)
    tm_mid = min(512, m)
    tm_small = min(256, m)
    tq = min(512, seq)

    inv_freq = 1.0 / (ROPE_THETA ** (jnp.arange(0, QK_ROPE, 2, dtype=F32) / QK_ROPE))
    freq = _rope_cols(jnp.concatenate([inv_freq, inv_freq]))[None, :]
    sign = _rope_cols(jnp.concatenate([-jnp.ones_like(inv_freq), jnp.ones_like(inv_freq)]))[None, :]
    ct, st = _rope_tables(positions.reshape(m, 1), freq, sign, tm_big)

    lane = jnp.arange(D_INNER)
    sel_row = jnp.arange(2 * LANES)[:, None]
    sel = ((sel_row % SSM_HEADS == (lane // SSM_HEADDIM)[None, :]) & (sel_row < 3 * SSM_HEADS)).astype(BF16)
    rows = jnp.arange(CHUNK)[:, None]
    eye_t = (rows == (lane % SSM_HEADDIM)[None, :]).astype(F32)
    tri_t = (rows >= (lane % SSM_HEADDIM)[None, :]).astype(F32)
    tri64 = (rows >= jnp.arange(CHUNK)[None, :]).astype(BF16)
    sh_rows = jnp.arange((CONV_WIDTH - 1) * CHUNK)[:, None]
    shift_m = (jnp.arange(2 * CHUNK)[None, :] == CHUNK + sh_rows % CHUNK - (sh_rows // CHUNK + 1)).astype(BF16)

    s0 = Q_LORA + KV_LORA + QK_ROPE
    s1 = s0 + D_INNER + (D_INNER + 2 * SSM_GROUPS * SSM_STATE)
    s2 = s1 + SSM_HEADS

    w_in_b = w_in.astype(BF16)
    w_small = jnp.concatenate(
        [w_in_b[..., :Q_LORA + KV_LORA], _rope_cols(w_in_b[..., Q_LORA + KV_LORA:s0]), w_in_b[..., s1:s2],
         jnp.zeros((depth, d, LANES - SSM_HEADS), BF16)], axis=-1)
    w_big = jnp.concatenate([w_in_b[..., s0:s1], w_in_b[..., s2:]], axis=-1)
    wq4 = w_uq.reshape(depth, Q_LORA, MLA_HEADS, QK_DIM)
    wq = jnp.concatenate([wq4[..., :QK_NOPE], _rope_cols(wq4[..., QK_NOPE:])], axis=-1)
    wq = wq.reshape(depth, Q_LORA, MLA_HEADS * HEAD_W).astype(BF16)
    wkv4 = w_ukv.reshape(depth, KV_LORA, MLA_HEADS, QK_NOPE + V_DIM)
    wk = wkv4[..., :QK_NOPE].reshape(depth, KV_LORA, -1).astype(BF16)
    wvt = jnp.swapaxes(wkv4[..., QK_NOPE:].reshape(depth, KV_LORA, -1), 1, 2).astype(BF16)
    w_o_mla_b = w_o_mla.astype(BF16)
    w_o_ssm_b = w_o_ssm.astype(BF16)
    w_out_b = w_out.astype(BF16)
    w_up_b = w_up.astype(BF16)
    w_down_b = w_down.astype(BF16)
    w_gate_b = w_ple_gate.astype(BF16)
    w_ple_b = w_ple.astype(BF16)

    rows3 = lambda v: v.reshape(depth, 1, -1).astype(F32)
    pad_h = jnp.zeros((depth, LANES - SSM_HEADS), F32)
    qnw = rows3(jnp.concatenate([q_norm_w[:, :QK_NOPE], _rope_cols(q_norm_w[:, QK_NOPE:])], axis=-1))
    kwn = rows3(k_norm_w[:, :QK_NOPE])
    kwr = rows3(_rope_cols(k_norm_w[:, QK_NOPE:]))
    dtb = rows3(jnp.concatenate([dt_bias, pad_h], axis=-1))
    alog = rows3(jnp.concatenate([a_log, pad_h], axis=-1))
    dsk = rows3(jnp.repeat(d_skip, SSM_HEADDIM, axis=-1))
    mix_nw, qa_nw, kva_nw = rows3(norm_mix_w), rows3(q_a_norm_w), rows3(kv_a_norm_w)
    ssm_nw, mlp_nw, ple_nw = rows3(ssm_norm_w), rows3(norm_mlp_w), rows3(ple_norm_w)
    cwx, cwb = conv_w[..., :D_INNER], conv_w[..., D_INNER:]
    cbx, cbb = rows3(conv_b[:, :D_INNER]), rows3(conv_b[:, D_INNER:])
    p3 = p.reshape(depth, m, PLE_DIM)

    xf = x.reshape(m, d)
    for i in range(depth):
        small = _norm_matmul(xf, mix_nw, w_small, i, F32, tm_mid, SMALL_N, "in_proj_small")
        big = _norm_matmul(xf, mix_nw, w_big, i, BF16, tm_big, 1024, "in_proj_big")
        q = _q_proj(small, qa_nw, wq, qnw, ct, st, i, tm_small)
        k, vt = _kv_proj(small, kva_nw, wk, wvt, kwn, kwr, ct, st, i, tq)
        o = _attention(q, k, vt, batch, seq, tq, 2)
        y = _ssd(big, small, cwx, cwb, cbx, cbb, dtb, alog, dsk, ssm_nw, sel, eye_t, tri_t, tri64, shift_m, i,
                 batch, seq, 2)
        merged = _merge(o, y, w_o_mla_b, w_o_ssm_b, big, i, tm_big, 512)
        xf = _res_matmul(xf, merged, w_out_b, i, tm_mid)
        xf = _mlp(xf, mlp_nw, w_up_b, w_down_b, i, tm_mid, 1024)
        xf = _ple(xf, ple_nw, w_gate_b, p3, w_ple_b, i, tm_mid)
    return xf.reshape(batch, seq, d)
```

```python
import functools

import jax
import jax.numpy as jnp
from jax import lax
from jax.experimental import pallas as pl
from jax.experimental.pallas import tpu as pltpu

F32 = jnp.float32
BF16 = jnp.bfloat16

D_MODEL = 2048
DEPTH = 4
CHUNK = 64
PLE_DIM = 256
EPS = 1e-6
MLA_HEADS = 16
Q_LORA = 512
KV_LORA = 512
QK_NOPE = 128
QK_ROPE = 64
V_DIM = 128
QK_DIM = QK_NOPE + QK_ROPE
ROPE_THETA = 10000.0
D_INNER = 2 * D_MODEL
SSM_HEADDIM = 64
SSM_HEADS = D_INNER // SSM_HEADDIM
SSM_GROUPS = 8
SSM_STATE = 128
GROUP_W = D_INNER // SSM_GROUPS
CONV_WIDTH = 4
D_FF = 4 * D_MODEL

LOG2_E = 1.4426950408889634
LANES = 128
ACC_ROWS = V_DIM + 16
HEAD_W = 2 * LANES
SMALL_N = Q_LORA + KV_LORA + 2 * LANES
VMEM_LIMIT = 56 * 1024 * 1024


def _cparams(sem):
    return pltpu.CompilerParams(dimension_semantics=sem, vmem_limit_bytes=VMEM_LIMIT)


def _rms(xf, w):
    ms = jnp.mean(xf * xf, axis=-1, keepdims=True)
    return xf * lax.rsqrt(ms + EPS) * w


def _lspec(layer, block, imap=None):
    if imap is None:
        imap = lambda *g: (0,) * len(block)
    return pl.BlockSpec((None,) + tuple(block), lambda *g: (layer,) + tuple(imap(*g)))


def _dot(a, b):
    return jnp.dot(a, b, preferred_element_type=F32)


def _dot_nt(a, b):
    return lax.dot_general(a, b, (((1,), (1,)), ((), ())), preferred_element_type=F32)


def _rope_body(pos_ref, freq_ref, sign_ref, ct_ref, st_ref):
    ang = pos_ref[...].astype(F32) * freq_ref[...]
    ct_ref[...] = jnp.cos(ang)
    st_ref[...] = jnp.sin(ang) * sign_ref[...]


def _rope_tables(pos_col, freq, sign, tm):
    m = pos_col.shape[0]
    return pl.pallas_call(
        _rope_body,
        grid=(m // tm,),
        in_specs=[pl.BlockSpec((tm, 1), lambda i: (i, 0)),
                  pl.BlockSpec((1, LANES), lambda i: (0, 0)),
                  pl.BlockSpec((1, LANES), lambda i: (0, 0))],
        out_specs=[pl.BlockSpec((tm, LANES), lambda i: (i, 0)),
                   pl.BlockSpec((tm, LANES), lambda i: (i, 0))],
        out_shape=[jax.ShapeDtypeStruct((m, LANES), F32)] * 2,
        compiler_params=_cparams(("parallel",)),
        name="rope_tables",
    )(pos_col, freq, sign)


def _norm_mm_body(x_ref, nw_ref, w_ref, o_ref, h_ref):
    @pl.when(pl.program_id(1) == 0)
    def _():
        h_ref[...] = _rms(x_ref[...], nw_ref[...]).astype(BF16)

    o_ref[...] = _dot(h_ref[...], w_ref[...]).astype(o_ref.dtype)


def _norm_matmul(x, nw, w, layer, out_dtype, tm, tn, name):
    m, k = x.shape
    n = w.shape[2]
    return pl.pallas_call(
        _norm_mm_body,
        grid=(m // tm, n // tn),
        in_specs=[pl.BlockSpec((tm, k), lambda i, j: (i, 0)),
                  _lspec(layer, (1, k)),
                  _lspec(layer, (k, tn), lambda i, j: (0, j))],
        out_specs=pl.BlockSpec((tm, tn), lambda i, j: (i, j)),
        out_shape=jax.ShapeDtypeStruct((m, n), out_dtype),
        scratch_shapes=[pltpu.VMEM((tm, k), BF16)],
        compiler_params=_cparams(("parallel", "arbitrary")),
        name=name,
    )(x, nw, w)


def _shift_cast_body(a_ref, b_ref, o_ref, *, shift):
    v = jnp.concatenate([a_ref[...], b_ref[...]], axis=1)
    o_ref[...] = v[:, shift:shift + o_ref.shape[1]].astype(BF16)


def _shift_cast(w, base, shift, width, tn, name):
    depth, k, _ = w.shape
    assert base % tn == 0 and width % tn == 0 and 0 < shift <= LANES
    return pl.pallas_call(
        functools.partial(_shift_cast_body, shift=shift),
        grid=(depth, width // tn),
        in_specs=[pl.BlockSpec((None, k, tn), lambda l, j: (l, 0, base // tn + j)),
                  pl.BlockSpec((None, k, LANES), lambda l, j: (l, 0, (base + tn) // LANES + j * (tn // LANES)))],
        out_specs=pl.BlockSpec((None, k, tn), lambda l, j: (l, 0, j)),
        out_shape=jax.ShapeDtypeStruct((depth, k, width), BF16),
        compiler_params=_cparams(("parallel", "parallel")),
        name=name,
    )(w, w)


def _in_big_body(x_ref, nw_ref, wz_ref, wg_ref, oz_ref, og_ref, h_ref, *, nz):
    j = pl.program_id(1)

    @pl.when(j == 0)
    def _():
        h_ref[...] = _rms(x_ref[...], nw_ref[...]).astype(BF16)

    @pl.when(j < nz)
    def _():
        oz_ref[...] = _dot(h_ref[...], wz_ref[...]).astype(BF16)

    @pl.when(j >= nz)
    def _():
        og_ref[...] = _dot(h_ref[...], wg_ref[...]).astype(BF16)


def _in_proj_big(x, nw, wz, wg, layer, tm, tn):
    m, k = x.shape
    nz = wz.shape[2] // tn
    ng = wg.shape[2] // tn
    zcol = lambda i, j: jnp.minimum(j, nz - 1)
    gcol = lambda i, j: jnp.maximum(j - nz, 0)
    return pl.pallas_call(
        functools.partial(_in_big_body, nz=nz),
        grid=(m // tm, nz + ng),
        in_specs=[pl.BlockSpec((tm, k), lambda i, j: (i, 0)),
                  _lspec(layer, (1, k)),
                  _lspec(layer, (k, tn), lambda i, j: (0, zcol(i, j))),
                  _lspec(layer, (k, tn), lambda i, j: (0, gcol(i, j)))],
        out_specs=[pl.BlockSpec((tm, tn), lambda i, j: (i, zcol(i, j))),
                   pl.BlockSpec((tm, tn), lambda i, j: (i, gcol(i, j)))],
        out_shape=[jax.ShapeDtypeStruct((m, wz.shape[2]), BF16), jax.ShapeDtypeStruct((m, wg.shape[2]), BF16)],
        scratch_shapes=[pltpu.VMEM((tm, k), BF16)],
        compiler_params=_cparams(("parallel", "arbitrary")),
        name="in_proj_big",
    )(x, nw, wz, wg)


def _rope_apply(v, c, s):
    return v * c + pltpu.roll(v, 64, 1) * s


def _q_body(c_ref, nw_ref, w_ref, qnw_ref, ct_ref, st_ref, o_ref):
    cn = _rms(c_ref[...], nw_ref[...]).astype(BF16)
    acc = _dot(cn, w_ref[...])
    c = ct_ref[...]
    s = st_ref[...]
    wn = qnw_ref[:, :LANES]
    wr = qnw_ref[:, LANES:]
    scale = QK_DIM ** -0.5 * LOG2_E
    for h in range(MLA_HEADS):
        qn = acc[:, HEAD_W * h:HEAD_W * h + LANES]
        qr = acc[:, HEAD_W * h + LANES:HEAD_W * (h + 1)]
        ssq = jnp.sum(qn * qn + qr * qr, axis=-1, keepdims=True)
        rs = lax.rsqrt(ssq / QK_DIM + EPS)
        o_ref[:, HEAD_W * h:HEAD_W * h + LANES] = (qn * rs * wn * scale).astype(BF16)
        o_ref[:, HEAD_W * h + LANES:HEAD_W * (h + 1)] = (_rope_apply(qr * rs * wr, c, s) * scale).astype(BF16)


def _q_proj(small, nw, wq, qnw, ct, st, layer, tm):
    m = small.shape[0]
    n = MLA_HEADS * HEAD_W
    return pl.pallas_call(
        _q_body,
        grid=(m // tm,),
        in_specs=[pl.BlockSpec((tm, Q_LORA), lambda i: (i, 0)),
                  _lspec(layer, (1, Q_LORA)),
                  _lspec(layer, (Q_LORA, n)),
                  _lspec(layer, (1, HEAD_W)),
                  pl.BlockSpec((tm, LANES), lambda i: (i, 0)),
                  pl.BlockSpec((tm, LANES), lambda i: (i, 0))],
        out_specs=pl.BlockSpec((tm, n), lambda i: (i, 0)),
        out_shape=jax.ShapeDtypeStruct((m, n), BF16),
        compiler_params=_cparams(("parallel",)),
        name="q_proj",
    )(small, nw, wq, qnw, ct, st)


def _kv_body(c_ref, kr_ref, nw_ref, wk_ref, wvt_ref, kwn_ref, kwr_ref, ct_ref, st_ref, k_ref, vt_ref):
    cn = _rms(c_ref[...], nw_ref[...]).astype(BF16)
    acc = _dot(cn, wk_ref[...])
    vt_ref[0] = _dot_nt(wvt_ref[...], cn).astype(BF16)
    c = ct_ref[...]
    s = st_ref[...]
    kr = kr_ref[...]
    wn = kwn_ref[...]
    wr = kwr_ref[...]
    ssr = jnp.sum(kr * kr, axis=-1, keepdims=True)
    kr_rot = _rope_apply(kr * wr, c, s)
    for h in range(MLA_HEADS):
        kn = acc[:, LANES * h:LANES * (h + 1)]
        ssq = jnp.sum(kn * kn, axis=-1, keepdims=True) + ssr
        rs = lax.rsqrt(ssq / QK_DIM + EPS)
        k_ref[:, HEAD_W * h:HEAD_W * h + LANES] = (kn * rs * wn).astype(BF16)
        k_ref[:, HEAD_W * h + LANES:HEAD_W * (h + 1)] = (kr_rot * rs).astype(BF16)


def _kv_proj(small, nw, wk, wvt, kwn, kwr, ct, st, layer, tm):
    m = small.shape[0]
    nk = MLA_HEADS * HEAD_W
    nv = MLA_HEADS * V_DIM
    kr_blk = (Q_LORA + KV_LORA) // LANES
    return pl.pallas_call(
        _kv_body,
        grid=(m // tm,),
        in_specs=[pl.BlockSpec((tm, KV_LORA), lambda i: (i, 1)),
                  pl.BlockSpec((tm, LANES), lambda i: (i, kr_blk)),
                  _lspec(layer, (1, KV_LORA)),
                  _lspec(layer, (KV_LORA, MLA_HEADS * QK_NOPE)),
                  _lspec(layer, (nv, KV_LORA)),
                  _lspec(layer, (1, LANES)),
                  _lspec(layer, (1, LANES)),
                  pl.BlockSpec((tm, LANES), lambda i: (i, 0)),
                  pl.BlockSpec((tm, LANES), lambda i: (i, 0))],
        out_specs=[pl.BlockSpec((tm, nk), lambda i: (i, 0)),
                   pl.BlockSpec((1, nv, tm), lambda i: (i, 0, 0))],
        out_shape=[jax.ShapeDtypeStruct((m, nk), BF16), jax.ShapeDtypeStruct((m // tm, nv, tm), BF16)],
        compiler_params=_cparams(("parallel",)),
        name="kv_proj",
    )(small, small, nw, wk, wvt, kwn, kwr, ct, st)


def _attn_body(q_ref, k_ref, vt_ref, o_ref, s0_scr, s1_scr, p0_scr, p1_scr, m_scr, a_scr, acc_scr, *, tq, nq, hb):
    s_scr = (s0_scr, s1_scr)
    p_scr = (p0_scr, p1_scr)
    tk = tq // 2
    key_chunk = lax.broadcasted_iota(jnp.int32, (tk, tq), 0) // CHUNK
    qry_chunk = lax.broadcasted_iota(jnp.int32, (tk, tq), 1) // CHUNK
    diag_masks = [key_chunk + h * (tk // CHUNK) <= qry_chunk for h in range(2)]
    ones_rows = jnp.ones((ACC_ROWS - V_DIM, tk), BF16)

    def scores(qs, kpair, h, mask=None):
        koff = pl.multiple_of(kpair * tq + h * tk, tk)
        for hh in range(hb):
            s = _dot(k_ref[pl.ds(koff, tk), HEAD_W * hh:HEAD_W * (hh + 1)], qs[hh])
            s_scr[h][hh] = s if mask is None else jnp.where(mask, s, -jnp.inf)

    def pv(kpair, h):
        return [_dot(jnp.concatenate([vt_ref[kpair, V_DIM * hh:V_DIM * (hh + 1), h * tk:(h + 1) * tk],
                                      ones_rows], axis=0), p_scr[h][hh])
                for hh in range(hb)]

    def softmax(h, pvs):
        for hh in range(hb):
            m = m_scr[hh]
            s = s_scr[h][hh]
            m_new = jnp.maximum(m, jnp.max(s, axis=0, keepdims=True))
            p_scr[h][hh] = jnp.exp2((s - m_new).astype(BF16))
            alpha = jnp.exp2(m - m_new)
            acc_scr[hh] = alpha * (acc_scr[hh] + a_scr[hh] * pvs[hh])
            a_scr[hh] = alpha
            m_scr[hh] = m_new

    def q_block(qi, c):
        qoff = pl.multiple_of(qi * tq, tq)
        qs = [q_ref[pl.ds(qoff, tq), HEAD_W * hh:HEAD_W * (hh + 1)].astype(F32).T.astype(BF16) for hh in range(hb)]
        m_scr[...] = jnp.full(m_scr.shape, -jnp.inf, F32)
        a_scr[...] = jnp.zeros(a_scr.shape, F32)
        acc_scr[...] = jnp.zeros(acc_scr.shape, F32)
        for h in range(2):
            p_scr[h][...] = jnp.zeros(p_scr[h].shape, BF16)
            scores(qs, qi, h, diag_masks[h])

        def kpair_of(r):
            return jnp.where(r == 0, qi, r - 1)

        def body(r, c2):
            prev = kpair_of(jnp.maximum(r - 1, 0))
            for h in range(2):
                pvs = pv(prev, h)
                softmax(h, pvs)
                scores(qs, r, h)
            return c2

        lax.fori_loop(0, qi, body, 0)
        prev = kpair_of(jnp.maximum(qi - 1, 0))
        for h in range(2):
            softmax(h, pv(prev, h))
        last = kpair_of(qi)
        pv0, pv1 = pv(last, 0), pv(last, 1)
        for hh in range(hb):
            acc = acc_scr[hh] + a_scr[hh] * pv0[hh] + pv1[hh]
            o_ref[pl.ds(qoff, tq), V_DIM * hh:V_DIM * (hh + 1)] = (
                acc[:V_DIM] / acc[V_DIM:V_DIM + 1]).T.astype(BF16)
        return c

    lax.fori_loop(0, nq, q_block, 0)


def _attention(q, k, vt, batch, seq, tq, hb):
    m = q.shape[0]
    nq = seq // tq
    tk = tq // 2
    assert vt.shape[2] == tq
    return pl.pallas_call(
        functools.partial(_attn_body, tq=tq, nq=nq, hb=hb),
        grid=(batch, MLA_HEADS // hb),
        in_specs=[pl.BlockSpec((seq, hb * HEAD_W), lambda b, h: (b, h)),
                  pl.BlockSpec((seq, hb * HEAD_W), lambda b, h: (b, h)),
                  pl.BlockSpec((nq, hb * V_DIM, tq), lambda b, h: (b, h, 0))],
        out_specs=pl.BlockSpec((seq, hb * V_DIM), lambda b, h: (b, h)),
        out_shape=jax.ShapeDtypeStruct((m, MLA_HEADS * V_DIM), BF16),
        scratch_shapes=[pltpu.VMEM((hb, tk, tq), F32), pltpu.VMEM((hb, tk, tq), F32),
                        pltpu.VMEM((hb, tk, tq), BF16), pltpu.VMEM((hb, tk, tq), BF16),
                        pltpu.VMEM((hb, 1, tq), F32), pltpu.VMEM((hb, 1, tq), F32),
                        pltpu.VMEM((hb, ACC_ROWS, tq), F32)],
        compiler_params=_cparams(("parallel", "arbitrary")),
        name="attention",
    )(q, k, vt)


def _silu(y):
    h = 0.5 * y
    return h + h * jnp.tanh(h)


def _split3(v):
    hi = v.astype(BF16)
    r = v - hi.astype(F32)
    mid = r.astype(BF16)
    lo = (r - mid.astype(F32)).astype(BF16)
    return hi, mid, lo


def _ssd_body(z_ref, x_ref, bc_ref, dt_ref, cwx_ref, cwb_ref, cbx_ref, cbb_ref, dtb_ref, alog_ref,
              dsk_ref, nw_ref, sel_ref, eye_ref, tri_ref, tri64_ref, shift_ref, y_ref, state_ref, xhalo, bchalo,
              *, cpb):
    t = CHUNK
    gn = SSM_GROUPS * SSM_STATE

    @pl.when(pl.program_id(1) == 0)
    def _():
        state_ref[...] = jnp.zeros_like(state_ref)
        xhalo[...] = jnp.zeros_like(xhalo)
        bchalo[...] = jnp.zeros_like(bchalo)

    def conv_silu(raw_ref, halo, w_ref, b_ref):
        raws = [raw_ref[c * t:(c + 1) * t, :] for c in range(cpb)]
        prevs = [halo[...]] + raws[:-1]
        halo[...] = raws[-1]
        shifted = [_dot(shift_ref[...], jnp.concatenate([prevs[c], raws[c]], axis=0)) for c in range(cpb)]
        outs = []
        for c in range(cpb):
            y = None
            for tap in range(CONV_WIDTH):
                shift = CONV_WIDTH - 1 - tap
                xt = raws[c].astype(F32) if shift == 0 else shifted[c][(shift - 1) * t:shift * t, :]
                term = xt * w_ref[tap:tap + 1, :]
                y = term if y is None else y + term
            outs.append(_silu(y + b_ref[...]))
        return outs

    xs_all = conv_silu(x_ref, xhalo, cwx_ref, cbx_ref)
    bc_all = conv_silu(bc_ref, bchalo, cwb_ref, cbb_ref)

    head_lane = lax.broadcasted_iota(jnp.int32, (t, LANES), 1) < SSM_HEADS
    neg_a = -jnp.exp(alog_ref[...])
    tri64 = tri64_ref[...]
    stacks = []
    for c in range(cpb):
        dt = jnp.where(head_lane, jax.nn.softplus(dt_ref[c * t:(c + 1) * t, :] + dtb_ref[...]), 0.0)
        h3 = _split3(dt * neg_a)
        a_cum = (_dot(tri64, h3[0]) + _dot(tri64, h3[1]) + _dot(tri64, h3[2])) * LOG2_E
        ea_c = jnp.where(head_lane, jnp.exp2(a_cum), 0.0)
        dte_c = jnp.where(head_lane, jnp.exp2(a_cum[t - 1:t, :] - a_cum), 0.0)
        stacks += [a_cum, dt, ea_c, dte_c]
    s3 = [part.astype(F32) for part in _split3(jnp.concatenate(stacks, axis=0))]
    packed = jnp.concatenate([(s3[0] + pltpu.roll(s3[1], SSM_HEADS, 1)).astype(BF16), s3[2].astype(BF16)], axis=1)
    ex_all = _dot(packed, sel_ref[...])
    left = lax.broadcasted_iota(jnp.int32, (t, LANES), 1) < SSM_HEADDIM
    zero_b = jnp.zeros((t, LANES), BF16)

    pre = []
    for c in range(cpb):
        ex = ex_all[4 * t * c:4 * t * (c + 1)]
        a_col = ex[:t]
        ea = ex[2 * t:3 * t]
        a_row = jnp.sum(jnp.where(eye_ref[...] > 0, a_col, 0.0), axis=0, keepdims=True)
        decay = jnp.exp2(jnp.where(tri_ref[...] > 0, a_col - a_row, -jnp.inf))
        xdt = xs_all[c] * ex[t:2 * t]
        pre.append((decay, ea, xdt.astype(BF16), (xdt * ex[3 * t:]).astype(BF16)))

    for c in range(cpb):
        decay, ea, xdt_b, xw_b = pre[c]
        xs, bc = xs_all[c], bc_all[c]
        chunk_decay = ea[t - 1:t, :]
        r0 = c * t
        for g in range(SSM_GROUPS):
            lo, hi = GROUP_W * g, GROUP_W * (g + 1)
            bg = bc[:, SSM_STATE * g:SSM_STATE * (g + 1)]
            cg_b = bc[:, gn + SSM_STATE * g:gn + SSM_STATE * (g + 1)].astype(BF16)
            bg_b = bg.astype(BF16)
            cb2 = _dot_nt(cg_b, jnp.concatenate([bg_b, bg_b], axis=0))
            yd = []
            for j in range(GROUP_W // LANES):
                l0 = lo + LANES * j
                mm = (cb2 * decay[:, l0:l0 + LANES]).astype(BF16)
                xp = xdt_b[:, l0:l0 + LANES]
                rhs = jnp.concatenate([jnp.where(left, xp, zero_b), jnp.where(left, zero_b, xp)], axis=0)
                yd.append(_dot(mm, rhs))
            y_diag = jnp.concatenate(yd, axis=1)
            st = state_ref[g]
            y_off = _dot(cg_b, st.astype(BF16)) * ea[:, lo:hi]
            inc = _dot(bg.T.astype(BF16), xw_b[:, lo:hi])
            state_ref[g] = st * chunk_decay[:, lo:hi] + inc
            yg = y_diag + y_off + dsk_ref[:, lo:hi] * xs[:, lo:hi]
            yg = yg * _silu(z_ref[r0:r0 + t, lo:hi].astype(F32))
            ms = jnp.mean(yg * yg, axis=-1, keepdims=True)
            y_ref[r0:r0 + t, lo:hi] = (yg * lax.rsqrt(ms + EPS) * nw_ref[:, lo:hi]).astype(BF16)


def _ssd(big, small, cwx, cwb, cbx, cbb, dtb, alog, dsk, nw, sel, eye_t, tri_t, tri64, shift_m, layer, batch, seq,
         cpb):
    m = big.shape[0]
    t = CHUNK
    tb = cpb * t
    nb = seq // tb
    bcw = 2 * SSM_GROUPS * SSM_STATE
    bc_blk = (2 * D_INNER) // bcw
    dt_blk = (SMALL_N - LANES) // LANES
    row = lambda b, c: b * nb + c
    const = lambda shape: pl.BlockSpec(shape, lambda b, c: (0,) * len(shape))
    return pl.pallas_call(
        functools.partial(_ssd_body, cpb=cpb),
        grid=(batch, nb),
        in_specs=[pl.BlockSpec((tb, D_INNER), lambda b, c: (row(b, c), 0)),
                  pl.BlockSpec((tb, D_INNER), lambda b, c: (row(b, c), 1)),
                  pl.BlockSpec((tb, bcw), lambda b, c: (row(b, c), bc_blk)),
                  pl.BlockSpec((tb, LANES), lambda b, c: (row(b, c), dt_blk)),
                  _lspec(layer, (CONV_WIDTH, D_INNER)), _lspec(layer, (CONV_WIDTH, bcw)),
                  _lspec(layer, (1, D_INNER)), _lspec(layer, (1, bcw)),
                  _lspec(layer, (1, LANES)), _lspec(layer, (1, LANES)),
                  _lspec(layer, (1, D_INNER)), _lspec(layer, (1, D_INNER)),
                  const((2 * LANES, D_INNER)), const((t, D_INNER)), const((t, D_INNER)), const((t, t)),
                  const(((CONV_WIDTH - 1) * t, 2 * t))],
        out_specs=pl.BlockSpec((tb, D_INNER), lambda b, c: (row(b, c), 0)),
        out_shape=jax.ShapeDtypeStruct((m, D_INNER), BF16),
        scratch_shapes=[pltpu.VMEM((SSM_GROUPS, SSM_STATE, GROUP_W), F32),
                        pltpu.VMEM((t, D_INNER), BF16),
                        pltpu.VMEM((t, bcw), BF16)],
        compiler_params=_cparams(("parallel", "arbitrary")),
        name="ssd",
    )(big, big, big, small, cwx, cwb, cbx, cbb, dtb, alog, dsk, nw, sel, eye_t, tri_t, tri64, shift_m)


def _merge_body(o_ref, y_ref, wa_ref, wb_ref, g0_ref, g1_ref, out_ref):
    ya = _dot(o_ref[...], wa_ref[...])
    yb = _dot(y_ref[...], wb_ref[...])
    g0 = jax.nn.sigmoid(g0_ref[...].astype(F32))
    g1 = jax.nn.sigmoid(g1_ref[...].astype(F32))
    out_ref[...] = (g0 * ya + g1 * yb).astype(out_ref.dtype)


def _merge(o, y, wa, wb, gates, layer, tm, tn):
    m = o.shape[0]
    g0_blk = 0
    g1_blk = D_MODEL // tn
    return pl.pallas_call(
        _merge_body,
        grid=(m // tm, D_MODEL // tn),
        in_specs=[pl.BlockSpec((tm, o.shape[1]), lambda i, j: (i, 0)),
                  pl.BlockSpec((tm, y.shape[1]), lambda i, j: (i, 0)),
                  _lspec(layer, (wa.shape[1], tn), lambda i, j: (0, j)),
                  _lspec(layer, (wb.shape[1], tn), lambda i, j: (0, j)),
                  pl.BlockSpec((tm, tn), lambda i, j: (i, g0_blk + j)),
                  pl.BlockSpec((tm, tn), lambda i, j: (i, g1_blk + j))],
        out_specs=pl.BlockSpec((tm, tn), lambda i, j: (i, j)),
        out_shape=jax.ShapeDtypeStruct((m, D_MODEL), BF16),
        compiler_params=_cparams(("parallel", "arbitrary")),
        name="merge",
    )(o, y, wa, wb, gates, gates)


def _res_mm_body(x_ref, a_ref, w_ref, o_ref):
    o_ref[...] = x_ref[...] + _dot(a_ref[...], w_ref[...])


def _res_matmul(x, a, w, layer, tm):
    m, n = x.shape
    k = a.shape[1]
    return pl.pallas_call(
        _res_mm_body,
        grid=(m // tm,),
        in_specs=[pl.BlockSpec((tm, n), lambda i: (i, 0)),
                  pl.BlockSpec((tm, k), lambda i: (i, 0)),
                  _lspec(layer, (k, n))],
        out_specs=pl.BlockSpec((tm, n), lambda i: (i, 0)),
        out_shape=jax.ShapeDtypeStruct((m, n), F32),
        compiler_params=_cparams(("parallel",)),
        name="out_proj",
    )(x, a, w)


def _mlp_body(x_ref, nw_ref, wu_ref, wd_ref, o_ref, h_ref):
    @pl.when(pl.program_id(1) == 0)
    def _():
        h_ref[...] = _rms(x_ref[...], nw_ref[...]).astype(BF16)
        o_ref[...] = x_ref[...]

    u = _dot(h_ref[...], wu_ref[...])
    hid = jnp.square(jnp.maximum(u, 0.0)).astype(BF16)
    o_ref[...] += _dot(hid, wd_ref[...])


def _mlp(x, nw, wu, wd, layer, tm, tf):
    m, d = x.shape
    f = wu.shape[2]
    return pl.pallas_call(
        _mlp_body,
        grid=(m // tm, f // tf),
        in_specs=[pl.BlockSpec((tm, d), lambda i, j: (i, 0), pipeline_mode=pl.Buffered(1)),
                  _lspec(layer, (1, d)),
                  _lspec(layer, (d, tf), lambda i, j: (0, j)),
                  _lspec(layer, (tf, d), lambda i, j: (j, 0))],
        out_specs=pl.BlockSpec((tm, d), lambda i, j: (i, 0)),
        out_shape=jax.ShapeDtypeStruct((m, d), F32),
        scratch_shapes=[pltpu.VMEM((tm, d), BF16)],
        compiler_params=_cparams(("parallel", "arbitrary")),
        name="mlp",
    )(x, nw, wu, wd)


def _ple_body(x_ref, nw_ref, wg_ref, p_ref, wp_ref, o_ref):
    x = x_ref[...]
    h = _rms(x, nw_ref[...]).astype(BF16)
    gate = jax.nn.sigmoid(_dot(h, wg_ref[...]))
    pe = _dot(p_ref[...].astype(BF16), wp_ref[...])
    o_ref[...] = x + pe * gate


def _ple(x, nw, wg, p, wp, layer, tm):
    m, d = x.shape
    return pl.pallas_call(
        _ple_body,
        grid=(m // tm,),
        in_specs=[pl.BlockSpec((tm, d), lambda i: (i, 0)),
                  _lspec(layer, (1, d)),
                  _lspec(layer, (d, d)),
                  _lspec(layer, (tm, PLE_DIM), lambda i: (i, 0)),
                  _lspec(layer, (PLE_DIM, d))],
        out_specs=pl.BlockSpec((tm, d), lambda i: (i, 0)),
        out_shape=jax.ShapeDtypeStruct((m, d), F32),
        compiler_params=_cparams(("parallel",)),
        name="ple",
    )(x, nw, wg, p, wp)


def _rope_cols(w):
    half = QK_ROPE // 2
    z = jnp.zeros(w.shape[:-1] + (LANES // 2 - half,), w.dtype)
    return jnp.concatenate([w[..., :half], z, w[..., half:], z], axis=-1)


def kernel(x, p, positions, norm_mix_w, w_in, q_a_norm_w, w_uq, kv_a_norm_w, w_ukv, q_norm_w, k_norm_w,
           w_o_mla, conv_w, conv_b, dt_bias, a_log, d_skip, ssm_norm_w, w_o_ssm, w_out, norm_mlp_w, w_up,
           w_down, ple_norm_w, w_ple_gate, w_ple):
    batch, seq, d = x.shape
    m = batch * seq
    depth = w_in.shape[0]
    tm_big = min(1024, m)
    tm_mid = min(512, m)
    tm_small = min(256, m)
    tq = min(512, seq)

    inv_freq = 1.0 / (ROPE_THETA ** (jnp.arange(0, QK_ROPE, 2, dtype=F32) / QK_ROPE))
    freq = _rope_cols(jnp.concatenate([inv_freq, inv_freq]))[None, :]
    sign = _rope_cols(jnp.concatenate([-jnp.ones_like(inv_freq), jnp.ones_like(inv_freq)]))[None, :]
    ct, st = _rope_tables(positions.reshape(m, 1), freq, sign, tm_big)

    lane = jnp.arange(D_INNER)
    sel_row = jnp.arange(2 * LANES)[:, None]
    sel = ((sel_row % SSM_HEADS == (lane // SSM_HEADDIM)[None, :]) & (sel_row < 3 * SSM_HEADS)).astype(BF16)
    rows = jnp.arange(CHUNK)[:, None]
    eye_t = (rows == (lane % SSM_HEADDIM)[None, :]).astype(F32)
    tri_t = (rows >= (lane % SSM_HEADDIM)[None, :]).astype(F32)
    tri64 = (rows >= jnp.arange(CHUNK)[None, :]).astype(BF16)
    sh_rows = jnp.arange((CONV_WIDTH - 1) * CHUNK)[:, None]
    shift_m = (jnp.arange(2 * CHUNK)[None, :] == CHUNK + sh_rows % CHUNK - (sh_rows // CHUNK + 1)).astype(BF16)

    s0 = Q_LORA + KV_LORA + QK_ROPE
    s1 = s0 + D_INNER + (D_INNER + 2 * SSM_GROUPS * SSM_STATE)
    s2 = s1 + SSM_HEADS

    w_small = jnp.concatenate(
        [w_in[..., :Q_LORA + KV_LORA], _rope_cols(w_in[..., Q_LORA + KV_LORA:s0]), w_in[..., s1:s2],
         jnp.zeros((depth, d, LANES - SSM_HEADS), F32)], axis=-1).astype(BF16)
    w_zx = _shift_cast(w_in, s0 - QK_ROPE, QK_ROPE, s1 - s0, 512, "relayout_zx")
    w_g = _shift_cast(w_in, s2 - LANES, LANES, 2 * D_MODEL, 512, "relayout_gates")
    wq4 = w_uq.reshape(depth, Q_LORA, MLA_HEADS, QK_DIM)
    wq = jnp.concatenate([wq4[..., :QK_NOPE], _rope_cols(wq4[..., QK_NOPE:])], axis=-1)
    wq = wq.reshape(depth, Q_LORA, MLA_HEADS * HEAD_W).astype(BF16)
    wkv4 = w_ukv.reshape(depth, KV_LORA, MLA_HEADS, QK_NOPE + V_DIM)
    wk = wkv4[..., :QK_NOPE].reshape(depth, KV_LORA, -1).astype(BF16)
    wvt = jnp.swapaxes(wkv4[..., QK_NOPE:].reshape(depth, KV_LORA, -1), 1, 2).astype(BF16)
    w_o_mla_b = w_o_mla.astype(BF16)
    w_o_ssm_b = w_o_ssm.astype(BF16)
    w_out_b = w_out.astype(BF16)
    w_up_b = w_up.astype(BF16)
    w_down_b = w_down.astype(BF16)
    w_gate_b = w_ple_gate.astype(BF16)
    w_ple_b = w_ple.astype(BF16)

    rows3 = lambda v: v.reshape(depth, 1, -1).astype(F32)
    pad_h = jnp.zeros((depth, LANES - SSM_HEADS), F32)
    qnw = rows3(jnp.concatenate([q_norm_w[:, :QK_NOPE], _rope_cols(q_norm_w[:, QK_NOPE:])], axis=-1))
    kwn = rows3(k_norm_w[:, :QK_NOPE])
    kwr = rows3(_rope_cols(k_norm_w[:, QK_NOPE:]))
    dtb = rows3(jnp.concatenate([dt_bias, pad_h], axis=-1))
    alog = rows3(jnp.concatenate([a_log, pad_h], axis=-1))
    dsk = rows3(jnp.repeat(d_skip, SSM_HEADDIM, axis=-1))
    mix_nw, qa_nw, kva_nw = rows3(norm_mix_w), rows3(q_a_norm_w), rows3(kv_a_norm_w)
    ssm_nw, mlp_nw, ple_nw = rows3(ssm_norm_w), rows3(norm_mlp_w), rows3(ple_norm_w)
    cwx, cwb = conv_w[..., :D_INNER], conv_w[..., D_INNER:]
    cbx, cbb = rows3(conv_b[:, :D_INNER]), rows3(conv_b[:, D_INNER:])
    p3 = p.reshape(depth, m, PLE_DIM)

    xf = x.reshape(m, d)
    for i in range(depth):
        small = _norm_matmul(xf, mix_nw, w_small, i, F32, tm_mid, SMALL_N, "in_proj_small")
        zx, gates = _in_proj_big(xf, mix_nw, w_zx, w_g, i, tm_big, 1024)
        q = _q_proj(small, qa_nw, wq, qnw, ct, st, i, tm_small)
        k, vt = _kv_proj(small, kva_nw, wk, wvt, kwn, kwr, ct, st, i, tq)
        o = _attention(q, k, vt, batch, seq, tq, 2)
        y = _ssd(zx, small, cwx, cwb, cbx, cbb, dtb, alog, dsk, ssm_nw, sel, eye_t, tri_t, tri64, shift_m, i,
                 batch, seq, 2)
        merged = _merge(o, y, w_o_mla_b, w_o_ssm_b, gates, i, tm_big, 512)
        xf = _res_matmul(xf, merged, w_out_b, i, tm_mid)
        xf = _mlp(xf, mlp_nw, w_up_b, w_down_b, i, tm_mid, 1024)
        xf = _ple(xf, ple_nw, w_gate_b, p3, w_ple_b, i, tm_mid)
    return xf.reshape(batch, seq, d)
```

```python
import functools

import jax
import jax.numpy as jnp
from jax import lax
from jax.experimental import pallas as pl
from jax.experimental.pallas import tpu as pltpu

F32 = jnp.float32
BF16 = jnp.bfloat16

D_MODEL = 2048
DEPTH = 4
CHUNK = 64
PLE_DIM = 256
EPS = 1e-6
MLA_HEADS = 16
Q_LORA = 512
KV_LORA = 512
QK_NOPE = 128
QK_ROPE = 64
V_DIM = 128
QK_DIM = QK_NOPE + QK_ROPE
ROPE_THETA = 10000.0
D_INNER = 2 * D_MODEL
SSM_HEADDIM = 64
SSM_HEADS = D_INNER // SSM_HEADDIM
SSM_GROUPS = 8
SSM_STATE = 128
GROUP_W = D_INNER // SSM_GROUPS
CONV_WIDTH = 4
D_FF = 4 * D_MODEL

LOG2_E = 1.4426950408889634
LANES = 128
ACC_ROWS = V_DIM + 16
Q_ACC_W = 3 * LANES
HEAD_W = 2 * LANES
SMALL_N = Q_LORA + KV_LORA + 2 * LANES
VMEM_LIMIT = 56 * 1024 * 1024


def _cparams(sem):
    return pltpu.CompilerParams(dimension_semantics=sem, vmem_limit_bytes=VMEM_LIMIT)


def _rms(xf, w):
    ms = jnp.mean(xf * xf, axis=-1, keepdims=True)
    return xf * lax.rsqrt(ms + EPS) * w


def _lspec(layer, block, imap=None):
    if imap is None:
        imap = lambda *g: (0,) * len(block)
    return pl.BlockSpec((None,) + tuple(block), lambda *g: (layer,) + tuple(imap(*g)))


def _dot(a, b):
    return jnp.dot(a, b, preferred_element_type=F32)


def _dot_nt(a, b):
    return lax.dot_general(a, b, (((1,), (1,)), ((), ())), preferred_element_type=F32)


def _rope_body(pos_ref, freq_ref, sign_ref, ct_ref, st_ref):
    ang = pos_ref[...].astype(F32) * freq_ref[...]
    ct_ref[...] = jnp.cos(ang)
    st_ref[...] = jnp.sin(ang) * sign_ref[...]


def _rope_tables(pos_col, freq, sign, tm):
    m = pos_col.shape[0]
    return pl.pallas_call(
        _rope_body,
        grid=(m // tm,),
        in_specs=[pl.BlockSpec((tm, 1), lambda i: (i, 0)),
                  pl.BlockSpec((1, LANES), lambda i: (0, 0)),
                  pl.BlockSpec((1, LANES), lambda i: (0, 0))],
        out_specs=[pl.BlockSpec((tm, LANES), lambda i: (i, 0)),
                   pl.BlockSpec((tm, LANES), lambda i: (i, 0))],
        out_shape=[jax.ShapeDtypeStruct((m, LANES), F32)] * 2,
        compiler_params=_cparams(("parallel",)),
        name="rope_tables",
    )(pos_col, freq, sign)


def _norm_mm_body(x_ref, nw_ref, w_ref, o_ref, h_ref):
    @pl.when(pl.program_id(1) == 0)
    def _():
        h_ref[...] = _rms(x_ref[...], nw_ref[...]).astype(BF16)

    o_ref[...] = _dot(h_ref[...], w_ref[...]).astype(o_ref.dtype)


def _norm_matmul(x, nw, w, layer, out_dtype, tm, tn, name):
    m, k = x.shape
    n = w.shape[2]
    return pl.pallas_call(
        _norm_mm_body,
        grid=(m // tm, n // tn),
        in_specs=[pl.BlockSpec((tm, k), lambda i, j: (i, 0)),
                  _lspec(layer, (1, k)),
                  _lspec(layer, (k, tn), lambda i, j: (0, j))],
        out_specs=pl.BlockSpec((tm, tn), lambda i, j: (i, j)),
        out_shape=jax.ShapeDtypeStruct((m, n), out_dtype),
        scratch_shapes=[pltpu.VMEM((tm, k), BF16)],
        compiler_params=_cparams(("parallel", "arbitrary")),
        name=name,
    )(x, nw, w)


def _shift_cast_body(a_ref, b_ref, *rest, shift):
    o_ref = rest[-1]
    v = jnp.concatenate([a_ref[...], b_ref[...]], axis=1)
    o_ref[...] = v[:, shift:shift + o_ref.shape[1]].astype(BF16)


def _shift_cast(w, base, shift, width, tn, out_width, col0, into, name):
    depth, k, _ = w.shape
    assert base % tn == 0 and width % tn == 0 and col0 % tn == 0 and 0 < shift <= LANES
    in_specs = [pl.BlockSpec((None, k, tn), lambda l, j: (l, 0, base // tn + j)),
                pl.BlockSpec((None, k, LANES), lambda l, j: (l, 0, (base + tn) // LANES + j * (tn // LANES)))]
    args = [w, w]
    if into is not None:
        in_specs.append(pl.BlockSpec(memory_space=pl.ANY))
        args.append(into)
    return pl.pallas_call(
        functools.partial(_shift_cast_body, shift=shift),
        grid=(depth, width // tn),
        in_specs=in_specs,
        out_specs=pl.BlockSpec((None, k, tn), lambda l, j: (l, 0, col0 // tn + j)),
        out_shape=jax.ShapeDtypeStruct((depth, k, out_width), BF16),
        input_output_aliases={} if into is None else {2: 0},
        compiler_params=_cparams(("parallel", "parallel")),
        name=name,
    )(*args)


def _rope_apply(v, c, s):
    return v * c + pltpu.roll(v, 64, 1) * s


def _q_body(c_ref, nw_ref, w_ref, qnw_ref, ct_ref, st_ref, o_ref):
    cn = _rms(c_ref[...], nw_ref[...]).astype(BF16)
    acc = _dot(cn, w_ref[...])
    c = ct_ref[...]
    s = st_ref[...]
    wn = qnw_ref[:, :LANES]
    wrc = qnw_ref[:, LANES:2 * LANES] * c
    wxs = qnw_ref[:, 2 * LANES:] * s
    scale = QK_DIM ** -0.5 * LOG2_E
    for h in range(MLA_HEADS):
        qn = acc[:, Q_ACC_W * h:Q_ACC_W * h + LANES]
        qr = acc[:, Q_ACC_W * h + LANES:Q_ACC_W * h + 2 * LANES]
        qx = acc[:, Q_ACC_W * h + 2 * LANES:Q_ACC_W * (h + 1)]
        ssq = jnp.sum(qn * qn + qr * qr, axis=-1, keepdims=True)
        rs = lax.rsqrt(ssq / QK_DIM + EPS) * scale
        o_ref[:, HEAD_W * h:HEAD_W * h + LANES] = (qn * rs * wn).astype(BF16)
        o_ref[:, HEAD_W * h + LANES:HEAD_W * (h + 1)] = ((qr * wrc + qx * wxs) * rs).astype(BF16)


def _q_proj(small, nw, wq, qnw, ct, st, layer, tm):
    m = small.shape[0]
    n = MLA_HEADS * HEAD_W
    return pl.pallas_call(
        _q_body,
        grid=(m // tm,),
        in_specs=[pl.BlockSpec((tm, Q_LORA), lambda i: (i, 0)),
                  _lspec(layer, (1, Q_LORA)),
                  _lspec(layer, (Q_LORA, MLA_HEADS * Q_ACC_W)),
                  _lspec(layer, (1, Q_ACC_W)),
                  pl.BlockSpec((tm, LANES), lambda i: (i, 0)),
                  pl.BlockSpec((tm, LANES), lambda i: (i, 0))],
        out_specs=pl.BlockSpec((tm, n), lambda i: (i, 0)),
        out_shape=jax.ShapeDtypeStruct((m, n), BF16),
        compiler_params=_cparams(("parallel",)),
        name="q_proj",
    )(small, nw, wq, qnw, ct, st)


def _kv_body(c_ref, kr_ref, nw_ref, wk_ref, wvt_ref, kwn_ref, kwr_ref, ct_ref, st_ref, k_ref, vt_ref):
    cn = _rms(c_ref[...], nw_ref[...]).astype(BF16)
    acc = _dot(cn, wk_ref[...])
    vt_ref[0] = _dot_nt(wvt_ref[...], cn).astype(BF16)
    c = ct_ref[...]
    s = st_ref[...]
    kr = kr_ref[...]
    wn = kwn_ref[...]
    wr = kwr_ref[...]
    ssr = jnp.sum(kr * kr, axis=-1, keepdims=True)
    kr_rot = _rope_apply(kr * wr, c, s)
    for h in range(MLA_HEADS):
        kn = acc[:, LANES * h:LANES * (h + 1)]
        ssq = jnp.sum(kn * kn, axis=-1, keepdims=True) + ssr
        rs = lax.rsqrt(ssq / QK_DIM + EPS)
        k_ref[:, HEAD_W * h:HEAD_W * h + LANES] = (kn * rs * wn).astype(BF16)
        k_ref[:, HEAD_W * h + LANES:HEAD_W * (h + 1)] = (kr_rot * rs).astype(BF16)


def _kv_proj(small, nw, wk, wvt, kwn, kwr, ct, st, layer, tm):
    m = small.shape[0]
    nk = MLA_HEADS * HEAD_W
    nv = MLA_HEADS * V_DIM
    kr_blk = (Q_LORA + KV_LORA) // LANES
    return pl.pallas_call(
        _kv_body,
        grid=(m // tm,),
        in_specs=[pl.BlockSpec((tm, KV_LORA), lambda i: (i, 1)),
                  pl.BlockSpec((tm, LANES), lambda i: (i, kr_blk)),
                  _lspec(layer, (1, KV_LORA)),
                  _lspec(layer, (KV_LORA, MLA_HEADS * QK_NOPE)),
                  _lspec(layer, (nv, KV_LORA)),
                  _lspec(layer, (1, LANES)),
                  _lspec(layer, (1, LANES)),
                  pl.BlockSpec((tm, LANES), lambda i: (i, 0)),
                  pl.BlockSpec((tm, LANES), lambda i: (i, 0))],
        out_specs=[pl.BlockSpec((tm, nk), lambda i: (i, 0)),
                   pl.BlockSpec((1, nv, tm), lambda i: (i, 0, 0))],
        out_shape=[jax.ShapeDtypeStruct((m, nk), BF16), jax.ShapeDtypeStruct((m // tm, nv, tm), BF16)],
        compiler_params=_cparams(("parallel",)),
        name="kv_proj",
    )(small, small, nw, wk, wvt, kwn, kwr, ct, st)


def _attn_body(q_ref, k_ref, vt_ref, o_ref, s0_scr, s1_scr, p0_scr, p1_scr, m_scr, a_scr, acc_scr, *, tq, nq, hb):
    s_scr = (s0_scr, s1_scr)
    p_scr = (p0_scr, p1_scr)
    tk = tq // 2
    key_chunk = lax.broadcasted_iota(jnp.int32, (tk, tq), 0) // CHUNK
    qry_chunk = lax.broadcasted_iota(jnp.int32, (tk, tq), 1) // CHUNK
    diag_masks = [key_chunk + h * (tk // CHUNK) <= qry_chunk for h in range(2)]
    ones_rows = jnp.ones((ACC_ROWS - V_DIM, tk), BF16)

    def scores(qs, kpair, h, mask=None):
        koff = pl.multiple_of(kpair * tq + h * tk, tk)
        for hh in range(hb):
            s = _dot(k_ref[pl.ds(koff, tk), HEAD_W * hh:HEAD_W * (hh + 1)], qs[hh])
            s_scr[h][hh] = s if mask is None else jnp.where(mask, s, -jnp.inf)

    def pv(kpair, h):
        return [_dot(jnp.concatenate([vt_ref[kpair, V_DIM * hh:V_DIM * (hh + 1), h * tk:(h + 1) * tk],
                                      ones_rows], axis=0), p_scr[h][hh])
                for hh in range(hb)]

    def softmax(h, pvs):
        for hh in range(hb):
            m = m_scr[hh]
            s = s_scr[h][hh]
            m_new = jnp.maximum(m, jnp.max(s, axis=0, keepdims=True))
            p_scr[h][hh] = jnp.exp2((s - m_new).astype(BF16))
            alpha = jnp.exp2(m - m_new)
            acc_scr[hh] = alpha * (acc_scr[hh] + a_scr[hh] * pvs[hh])
            a_scr[hh] = alpha
            m_scr[hh] = m_new

    def q_block(qi, c):
        qoff = pl.multiple_of(qi * tq, tq)
        qs = [q_ref[pl.ds(qoff, tq), HEAD_W * hh:HEAD_W * (hh + 1)].astype(F32).T.astype(BF16) for hh in range(hb)]
        m_scr[...] = jnp.full(m_scr.shape, -jnp.inf, F32)
        a_scr[...] = jnp.zeros(a_scr.shape, F32)
        acc_scr[...] = jnp.zeros(acc_scr.shape, F32)
        for h in range(2):
            p_scr[h][...] = jnp.zeros(p_scr[h].shape, BF16)
            scores(qs, qi, h, diag_masks[h])

        def kpair_of(r):
            return jnp.where(r == 0, qi, r - 1)

        def body(r, c2):
            prev = kpair_of(jnp.maximum(r - 1, 0))
            for h in range(2):
                pvs = pv(prev, h)
                softmax(h, pvs)
                scores(qs, r, h)
            return c2

        lax.fori_loop(0, qi, body, 0)
        prev = kpair_of(jnp.maximum(qi - 1, 0))
        for h in range(2):
            softmax(h, pv(prev, h))
        last = kpair_of(qi)
        pv0, pv1 = pv(last, 0), pv(last, 1)
        for hh in range(hb):
            acc = acc_scr[hh] + a_scr[hh] * pv0[hh] + pv1[hh]
            o_ref[pl.ds(qoff, tq), V_DIM * hh:V_DIM * (hh + 1)] = (
                acc[:V_DIM] / acc[V_DIM:V_DIM + 1]).T.astype(BF16)
        return c

    lax.fori_loop(0, nq, q_block, 0)


def _attention(q, k, vt, batch, seq, tq, hb):
    m = q.shape[0]
    nq = seq // tq
    tk = tq // 2
    assert vt.shape[2] == tq
    return pl.pallas_call(
        functools.partial(_attn_body, tq=tq, nq=nq, hb=hb),
        grid=(batch, MLA_HEADS // hb),
        in_specs=[pl.BlockSpec((seq, hb * HEAD_W), lambda b, h: (b, h)),
                  pl.BlockSpec((seq, hb * HEAD_W), lambda b, h: (b, h)),
                  pl.BlockSpec((nq, hb * V_DIM, tq), lambda b, h: (b, h, 0))],
        out_specs=pl.BlockSpec((seq, hb * V_DIM), lambda b, h: (b, h)),
        out_shape=jax.ShapeDtypeStruct((m, MLA_HEADS * V_DIM), BF16),
        scratch_shapes=[pltpu.VMEM((hb, tk, tq), F32), pltpu.VMEM((hb, tk, tq), F32),
                        pltpu.VMEM((hb, tk, tq), BF16), pltpu.VMEM((hb, tk, tq), BF16),
                        pltpu.VMEM((hb, 1, tq), F32), pltpu.VMEM((hb, 1, tq), F32),
                        pltpu.VMEM((hb, ACC_ROWS, tq), F32)],
        compiler_params=_cparams(("parallel", "arbitrary")),
        name="attention",
    )(q, k, vt)


def _silu(y):
    h = 0.5 * y
    return h + h * jnp.tanh(h)


def _split3(v):
    hi = v.astype(BF16)
    r = v - hi.astype(F32)
    mid = r.astype(BF16)
    lo = (r - mid.astype(F32)).astype(BF16)
    return hi, mid, lo


def _ssd_body(z_ref, x_ref, bc_ref, dt_ref, cwx_ref, cwb_ref, cbx_ref, cbb_ref, dtb_ref, alog_ref,
              dsk_ref, nw_ref, sel_ref, eye_ref, tri_ref, tri64_ref, shift_ref, y_ref, state_ref, xhalo, bchalo,
              *, cpb):
    t = CHUNK
    gn = SSM_GROUPS * SSM_STATE

    @pl.when(pl.program_id(1) == 0)
    def _():
        state_ref[...] = jnp.zeros_like(state_ref)
        xhalo[...] = jnp.zeros_like(xhalo)
        bchalo[...] = jnp.zeros_like(bchalo)

    def conv_silu(raw_ref, halo, w_ref, b_ref):
        raws = [raw_ref[c * t:(c + 1) * t, :] for c in range(cpb)]
        prevs = [halo[...]] + raws[:-1]
        halo[...] = raws[-1]
        shifted = [_dot(shift_ref[...], jnp.concatenate([prevs[c], raws[c]], axis=0)) for c in range(cpb)]
        outs = []
        for c in range(cpb):
            y = None
            for tap in range(CONV_WIDTH):
                shift = CONV_WIDTH - 1 - tap
                xt = raws[c].astype(F32) if shift == 0 else shifted[c][(shift - 1) * t:shift * t, :]
                term = xt * w_ref[tap:tap + 1, :]
                y = term if y is None else y + term
            outs.append(_silu(y + b_ref[...]))
        return outs

    xs_all = conv_silu(x_ref, xhalo, cwx_ref, cbx_ref)
    bc_all = conv_silu(bc_ref, bchalo, cwb_ref, cbb_ref)

    head_lane = lax.broadcasted_iota(jnp.int32, (t, LANES), 1) < SSM_HEADS
    neg_a = -jnp.exp(alog_ref[...])
    tri64 = tri64_ref[...]
    stacks = []
    for c in range(cpb):
        dt = jnp.where(head_lane, jax.nn.softplus(dt_ref[c * t:(c + 1) * t, :] + dtb_ref[...]), 0.0)
        h3 = _split3(dt * neg_a)
        a_cum = (_dot(tri64, h3[0]) + _dot(tri64, h3[1]) + _dot(tri64, h3[2])) * LOG2_E
        ea_c = jnp.where(head_lane, jnp.exp2(a_cum), 0.0)
        dte_c = jnp.where(head_lane, jnp.exp2(a_cum[t - 1:t, :] - a_cum), 0.0)
        stacks += [a_cum, dt, ea_c, dte_c]
    s3 = [part.astype(F32) for part in _split3(jnp.concatenate(stacks, axis=0))]
    packed = jnp.concatenate([(s3[0] + pltpu.roll(s3[1], SSM_HEADS, 1)).astype(BF16), s3[2].astype(BF16)], axis=1)
    ex_all = _dot(packed, sel_ref[...])
    left = lax.broadcasted_iota(jnp.int32, (t, LANES), 1) < SSM_HEADDIM
    zero_b = jnp.zeros((t, LANES), BF16)

    pre = []
    for c in range(cpb):
        ex = ex_all[4 * t * c:4 * t * (c + 1)]
        a_col = ex[:t]
        ea = ex[2 * t:3 * t]
        a_row = jnp.sum(jnp.where(eye_ref[...] > 0, a_col, 0.0), axis=0, keepdims=True)
        decay = jnp.exp2(jnp.where(tri_ref[...] > 0, a_col - a_row, -jnp.inf))
        xdt = xs_all[c] * ex[t:2 * t]
        pre.append((decay, ea, xdt.astype(BF16), (xdt * ex[3 * t:]).astype(BF16)))

    for c in range(cpb):
        decay, ea, xdt_b, xw_b = pre[c]
        xs, bc = xs_all[c], bc_all[c]
        chunk_decay = ea[t - 1:t, :]
        r0 = c * t
        for g in range(SSM_GROUPS):
            lo, hi = GROUP_W * g, GROUP_W * (g + 1)
            bg = bc[:, SSM_STATE * g:SSM_STATE * (g + 1)]
            cg_b = bc[:, gn + SSM_STATE * g:gn + SSM_STATE * (g + 1)].astype(BF16)
            bg_b = bg.astype(BF16)
            cb2 = _dot_nt(cg_b, jnp.concatenate([bg_b, bg_b], axis=0))
            yd = []
            for j in range(GROUP_W // LANES):
                l0 = lo + LANES * j
                mm = (cb2 * decay[:, l0:l0 + LANES]).astype(BF16)
                xp = xdt_b[:, l0:l0 + LANES]
                rhs = jnp.concatenate([jnp.where(left, xp, zero_b), jnp.where(left, zero_b, xp)], axis=0)
                yd.append(_dot(mm, rhs))
            y_diag = jnp.concatenate(yd, axis=1)
            st = state_ref[g]
            y_off = _dot(cg_b, st.astype(BF16)) * ea[:, lo:hi]
            inc = _dot(bg.T.astype(BF16), xw_b[:, lo:hi])
            state_ref[g] = st * chunk_decay[:, lo:hi] + inc
            yg = y_diag + y_off + dsk_ref[:, lo:hi] * xs[:, lo:hi]
            yg = yg * _silu(z_ref[r0:r0 + t, lo:hi].astype(F32))
            ms = jnp.mean(yg * yg, axis=-1, keepdims=True)
            y_ref[r0:r0 + t, lo:hi] = (yg * lax.rsqrt(ms + EPS) * nw_ref[:, lo:hi]).astype(BF16)


def _ssd(big, small, cwx, cwb, cbx, cbb, dtb, alog, dsk, nw, sel, eye_t, tri_t, tri64, shift_m, layer, batch, seq,
         cpb):
    m = big.shape[0]
    t = CHUNK
    tb = cpb * t
    nb = seq // tb
    bcw = 2 * SSM_GROUPS * SSM_STATE
    bc_blk = (2 * D_INNER) // bcw
    dt_blk = (SMALL_N - LANES) // LANES
    row = lambda b, c: b * nb + c
    const = lambda shape: pl.BlockSpec(shape, lambda b, c: (0,) * len(shape))
    return pl.pallas_call(
        functools.partial(_ssd_body, cpb=cpb),
        grid=(batch, nb),
        in_specs=[pl.BlockSpec((tb, D_INNER), lambda b, c: (row(b, c), 0)),
                  pl.BlockSpec((tb, D_INNER), lambda b, c: (row(b, c), 1)),
                  pl.BlockSpec((tb, bcw), lambda b, c: (row(b, c), bc_blk)),
                  pl.BlockSpec((tb, LANES), lambda b, c: (row(b, c), dt_blk)),
                  _lspec(layer, (CONV_WIDTH, D_INNER)), _lspec(layer, (CONV_WIDTH, bcw)),
                  _lspec(layer, (1, D_INNER)), _lspec(layer, (1, bcw)),
                  _lspec(layer, (1, LANES)), _lspec(layer, (1, LANES)),
                  _lspec(layer, (1, D_INNER)), _lspec(layer, (1, D_INNER)),
                  const((2 * LANES, D_INNER)), const((t, D_INNER)), const((t, D_INNER)), const((t, t)),
                  const(((CONV_WIDTH - 1) * t, 2 * t))],
        out_specs=pl.BlockSpec((tb, D_INNER), lambda b, c: (row(b, c), 0)),
        out_shape=jax.ShapeDtypeStruct((m, D_INNER), BF16),
        scratch_shapes=[pltpu.VMEM((SSM_GROUPS, SSM_STATE, GROUP_W), F32),
                        pltpu.VMEM((t, D_INNER), BF16),
                        pltpu.VMEM((t, bcw), BF16)],
        compiler_params=_cparams(("parallel", "arbitrary")),
        name="ssd",
    )(big, big, big, small, cwx, cwb, cbx, cbb, dtb, alog, dsk, nw, sel, eye_t, tri_t, tri64, shift_m)


def _merge_body(o_ref, y_ref, wa_ref, wb_ref, g0_ref, g1_ref, out_ref):
    ya = _dot(o_ref[...], wa_ref[...])
    yb = _dot(y_ref[...], wb_ref[...])
    g0 = jax.nn.sigmoid(g0_ref[...].astype(F32))
    g1 = jax.nn.sigmoid(g1_ref[...].astype(F32))
    out_ref[...] = (g0 * ya + g1 * yb).astype(out_ref.dtype)


def _merge(o, y, wa, wb, big, layer, tm, tn):
    m = o.shape[0]
    g0_blk = (2 * D_INNER + 2 * SSM_GROUPS * SSM_STATE) // tn
    g1_blk = g0_blk + D_MODEL // tn
    return pl.pallas_call(
        _merge_body,
        grid=(m // tm, D_MODEL // tn),
        in_specs=[pl.BlockSpec((tm, o.shape[1]), lambda i, j: (i, 0)),
                  pl.BlockSpec((tm, y.shape[1]), lambda i, j: (i, 0)),
                  _lspec(layer, (wa.shape[1], tn), lambda i, j: (0, j)),
                  _lspec(layer, (wb.shape[1], tn), lambda i, j: (0, j)),
                  pl.BlockSpec((tm, tn), lambda i, j: (i, g0_blk + j)),
                  pl.BlockSpec((tm, tn), lambda i, j: (i, g1_blk + j))],
        out_specs=pl.BlockSpec((tm, tn), lambda i, j: (i, j)),
        out_shape=jax.ShapeDtypeStruct((m, D_MODEL), BF16),
        compiler_params=_cparams(("parallel", "arbitrary")),
        name="merge",
    )(o, y, wa, wb, big, big)


def _res_mm_body(x_ref, a_ref, w_ref, o_ref):
    o_ref[...] = x_ref[...] + _dot(a_ref[...], w_ref[...])


def _res_matmul(x, a, w, layer, tm):
    m, n = x.shape
    k = a.shape[1]
    return pl.pallas_call(
        _res_mm_body,
        grid=(m // tm,),
        in_specs=[pl.BlockSpec((tm, n), lambda i: (i, 0)),
                  pl.BlockSpec((tm, k), lambda i: (i, 0)),
                  _lspec(layer, (k, n))],
        out_specs=pl.BlockSpec((tm, n), lambda i: (i, 0)),
        out_shape=jax.ShapeDtypeStruct((m, n), F32),
        compiler_params=_cparams(("parallel",)),
        name="out_proj",
    )(x, a, w)


def _mlp_body(x_ref, nw_ref, wu_ref, wd_ref, o_ref, h_ref):
    @pl.when(pl.program_id(1) == 0)
    def _():
        h_ref[...] = _rms(x_ref[...], nw_ref[...]).astype(BF16)
        o_ref[...] = x_ref[...]

    u = _dot(h_ref[...], wu_ref[...])
    hid = jnp.square(jnp.maximum(u, 0.0)).astype(BF16)
    o_ref[...] += _dot(hid, wd_ref[...])


def _mlp(x, nw, wu, wd, layer, tm, tf):
    m, d = x.shape
    f = wu.shape[2]
    return pl.pallas_call(
        _mlp_body,
        grid=(m // tm, f // tf),
        in_specs=[pl.BlockSpec((tm, d), lambda i, j: (i, 0), pipeline_mode=pl.Buffered(1)),
                  _lspec(layer, (1, d)),
                  _lspec(layer, (d, tf), lambda i, j: (0, j)),
                  _lspec(layer, (tf, d), lambda i, j: (j, 0))],
        out_specs=pl.BlockSpec((tm, d), lambda i, j: (i, 0)),
        out_shape=jax.ShapeDtypeStruct((m, d), F32),
        scratch_shapes=[pltpu.VMEM((tm, d), BF16)],
        compiler_params=_cparams(("parallel", "arbitrary")),
        name="mlp",
    )(x, nw, wu, wd)


def _ple_body(x_ref, nw_ref, wg_ref, p_ref, wp_ref, o_ref):
    x = x_ref[...]
    h = _rms(x, nw_ref[...]).astype(BF16)
    gate = jax.nn.sigmoid(_dot(h, wg_ref[...]))
    pe = _dot(p_ref[...].astype(BF16), wp_ref[...])
    o_ref[...] = x + pe * gate


def _ple(x, nw, wg, p, wp, layer, tm):
    m, d = x.shape
    return pl.pallas_call(
        _ple_body,
        grid=(m // tm,),
        in_specs=[pl.BlockSpec((tm, d), lambda i: (i, 0)),
                  _lspec(layer, (1, d)),
                  _lspec(layer, (d, d)),
                  _lspec(layer, (tm, PLE_DIM), lambda i: (i, 0)),
                  _lspec(layer, (PLE_DIM, d))],
        out_specs=pl.BlockSpec((tm, d), lambda i: (i, 0)),
        out_shape=jax.ShapeDtypeStruct((m, d), F32),
        compiler_params=_cparams(("parallel",)),
        name="ple",
    )(x, nw, wg, p, wp)


def _rope_cols(w):
    half = QK_ROPE // 2
    z = jnp.zeros(w.shape[:-1] + (LANES // 2 - half,), w.dtype)
    return jnp.concatenate([w[..., :half], z, w[..., half:], z], axis=-1)


def kernel(x, p, positions, norm_mix_w, w_in, q_a_norm_w, w_uq, kv_a_norm_w, w_ukv, q_norm_w, k_norm_w,
           w_o_mla, conv_w, conv_b, dt_bias, a_log, d_skip, ssm_norm_w, w_o_ssm, w_out, norm_mlp_w, w_up,
           w_down, ple_norm_w, w_ple_gate, w_ple):
    batch, seq, d = x.shape
    m = batch * seq
    depth = w_in.shape[0]
    tm_big = min(1024, m)
    tm_mid = min(512, m)
    tm_small = min(256, m)
    tq = min(512, seq)

    inv_freq = 1.0 / (ROPE_THETA ** (jnp.arange(0, QK_ROPE, 2, dtype=F32) / QK_ROPE))
    freq = _rope_cols(jnp.concatenate([inv_freq, inv_freq]))[None, :]
    sign = _rope_cols(jnp.concatenate([-jnp.ones_like(inv_freq), jnp.ones_like(inv_freq)]))[None, :]
    ct, st = _rope_tables(positions.reshape(m, 1), freq, sign, tm_big)

    lane = jnp.arange(D_INNER)
    sel_row = jnp.arange(2 * LANES)[:, None]
    sel = ((sel_row % SSM_HEADS == (lane // SSM_HEADDIM)[None, :]) & (sel_row < 3 * SSM_HEADS)).astype(BF16)
    rows = jnp.arange(CHUNK)[:, None]
    eye_t = (rows == (lane % SSM_HEADDIM)[None, :]).astype(F32)
    tri_t = (rows >= (lane % SSM_HEADDIM)[None, :]).astype(F32)
    tri64 = (rows >= jnp.arange(CHUNK)[None, :]).astype(BF16)
    sh_rows = jnp.arange((CONV_WIDTH - 1) * CHUNK)[:, None]
    shift_m = (jnp.arange(2 * CHUNK)[None, :] == CHUNK + sh_rows % CHUNK - (sh_rows // CHUNK + 1)).astype(BF16)

    s0 = Q_LORA + KV_LORA + QK_ROPE
    s1 = s0 + D_INNER + (D_INNER + 2 * SSM_GROUPS * SSM_STATE)
    s2 = s1 + SSM_HEADS

    w_lat, w_kr, w_dt = lax.optimization_barrier(
        (w_in[..., :Q_LORA + KV_LORA], w_in[..., Q_LORA + KV_LORA:s0], w_in[..., s1:s2]))
    w_small = jnp.concatenate(
        [w_lat, _rope_cols(w_kr), w_dt, jnp.zeros((depth, d, LANES - SSM_HEADS), F32)], axis=-1).astype(BF16)
    big_n = (s1 - s0) + 2 * D_MODEL
    w_big = _shift_cast(w_in, s0 - QK_ROPE, QK_ROPE, s1 - s0, 512, big_n, 0, None, "relayout_zx")
    w_big = _shift_cast(w_in, s2 - LANES, LANES, 2 * D_MODEL, 512, big_n, s1 - s0, w_big, "relayout_gates")
    wq4 = w_uq.reshape(depth, Q_LORA, MLA_HEADS, QK_DIM)
    wq_rope = _rope_cols(wq4[..., QK_NOPE:])
    wq = jnp.concatenate([wq4[..., :QK_NOPE], wq_rope, jnp.roll(wq_rope, LANES // 2, axis=-1)], axis=-1)
    wq = wq.reshape(depth, Q_LORA, MLA_HEADS * Q_ACC_W).astype(BF16)
    wkv4 = w_ukv.reshape(depth, KV_LORA, MLA_HEADS, QK_NOPE + V_DIM)
    wk = wkv4[..., :QK_NOPE].reshape(depth, KV_LORA, -1).astype(BF16)
    wvt = jnp.swapaxes(wkv4[..., QK_NOPE:].reshape(depth, KV_LORA, -1), 1, 2).astype(BF16)
    w_o_mla_b = w_o_mla.astype(BF16)
    w_o_ssm_b = w_o_ssm.astype(BF16)
    w_out_b = w_out.astype(BF16)
    w_up_b = w_up.astype(BF16)
    w_down_b = w_down.astype(BF16)
    w_gate_b = w_ple_gate.astype(BF16)
    w_ple_b = w_ple.astype(BF16)

    rows3 = lambda v: v.reshape(depth, 1, -1).astype(F32)
    pad_h = jnp.zeros((depth, LANES - SSM_HEADS), F32)
    qw_rope = _rope_cols(q_norm_w[:, QK_NOPE:])
    qnw = rows3(jnp.concatenate([q_norm_w[:, :QK_NOPE], qw_rope, jnp.roll(qw_rope, LANES // 2, axis=-1)], axis=-1))
    kwn = rows3(k_norm_w[:, :QK_NOPE])
    kwr = rows3(_rope_cols(k_norm_w[:, QK_NOPE:]))
    dtb = rows3(jnp.concatenate([dt_bias, pad_h], axis=-1))
    alog = rows3(jnp.concatenate([a_log, pad_h], axis=-1))
    dsk = rows3(jnp.repeat(d_skip, SSM_HEADDIM, axis=-1))
    mix_nw, qa_nw, kva_nw = rows3(norm_mix_w), rows3(q_a_norm_w), rows3(kv_a_norm_w)
    ssm_nw, mlp_nw, ple_nw = rows3(ssm_norm_w), rows3(norm_mlp_w), rows3(ple_norm_w)
    cwx, cwb = conv_w[..., :D_INNER], conv_w[..., D_INNER:]
    cbx, cbb = rows3(conv_b[:, :D_INNER]), rows3(conv_b[:, D_INNER:])
    p3 = p.reshape(depth, m, PLE_DIM)

    xf = x.reshape(m, d)
    for i in range(depth):
        small = _norm_matmul(xf, mix_nw, w_small, i, F32, tm_mid, SMALL_N, "in_proj_small")
        big = _norm_matmul(xf, mix_nw, w_big, i, BF16, tm_big, 1024, "in_proj_big")
        q = _q_proj(small, qa_nw, wq, qnw, ct, st, i, tm_small)
        k, vt = _kv_proj(small, kva_nw, wk, wvt, kwn, kwr, ct, st, i, tq)
        o = _attention(q, k, vt, batch, seq, tq, 2)
        y = _ssd(big, small, cwx, cwb, cbx, cbb, dtb, alog, dsk, ssm_nw, sel, eye_t, tri_t, tri64, shift_m, i,
                 batch, seq, 2)
        merged = _merge(o, y, w_o_mla_b, w_o_ssm_b, big, i, tm_big, 512)
        xf = _res_matmul(xf, merged, w_out_b, i, tm_mid)
        xf = _mlp(xf, mlp_nw, w_up_b, w_down_b, i, tm_mid, 1024)
        xf = _ple(xf, ple_nw, w_gate_b, p3, w_ple_b, i, tm_mid)
    return xf.reshape(batch, seq, d)
```

```python
import functools

import jax
import jax.numpy as jnp
from jax import lax
from jax.experimental import pallas as pl
from jax.experimental.pallas import tpu as pltpu

F32 = jnp.float32
BF16 = jnp.bfloat16

D_MODEL = 2048
DEPTH = 4
CHUNK = 64
PLE_DIM = 256
EPS = 1e-6
MLA_HEADS = 16
Q_LORA = 512
KV_LORA = 512
QK_NOPE = 128
QK_ROPE = 64
V_DIM = 128
QK_DIM = QK_NOPE + QK_ROPE
ROPE_THETA = 10000.0
D_INNER = 2 * D_MODEL
SSM_HEADDIM = 64
SSM_HEADS = D_INNER // SSM_HEADDIM
SSM_GROUPS = 8
SSM_STATE = 128
GROUP_W = D_INNER // SSM_GROUPS
CONV_WIDTH = 4
D_FF = 4 * D_MODEL

LOG2_E = 1.4426950408889634
LANES = 128
ACC_ROWS = V_DIM + 16
Q_ACC_W = 3 * LANES
HEAD_W = 2 * LANES
SMALL_N = Q_LORA + KV_LORA + 2 * LANES
VMEM_LIMIT = 56 * 1024 * 1024


def _cparams(sem):
    return pltpu.CompilerParams(dimension_semantics=sem, vmem_limit_bytes=VMEM_LIMIT)


def _rms(xf, w):
    ms = jnp.mean(xf * xf, axis=-1, keepdims=True)
    return xf * lax.rsqrt(ms + EPS) * w


def _lspec(layer, block, imap=None):
    if imap is None:
        imap = lambda *g: (0,) * len(block)
    return pl.BlockSpec((None,) + tuple(block), lambda *g: (layer,) + tuple(imap(*g)))


def _dot(a, b):
    return jnp.dot(a, b, preferred_element_type=F32)


def _dot_nt(a, b):
    return lax.dot_general(a, b, (((1,), (1,)), ((), ())), preferred_element_type=F32)


def _rope_body(pos_ref, freq_ref, sign_ref, ct_ref, st_ref):
    ang = pos_ref[...].astype(F32) * freq_ref[...]
    ct_ref[...] = jnp.cos(ang)
    st_ref[...] = jnp.sin(ang) * sign_ref[...]


def _rope_tables(pos_col, freq, sign, tm):
    m = pos_col.shape[0]
    return pl.pallas_call(
        _rope_body,
        grid=(m // tm,),
        in_specs=[pl.BlockSpec((tm, 1), lambda i: (i, 0)),
                  pl.BlockSpec((1, LANES), lambda i: (0, 0)),
                  pl.BlockSpec((1, LANES), lambda i: (0, 0))],
        out_specs=[pl.BlockSpec((tm, LANES), lambda i: (i, 0)),
                   pl.BlockSpec((tm, LANES), lambda i: (i, 0))],
        out_shape=[jax.ShapeDtypeStruct((m, LANES), F32)] * 2,
        compiler_params=_cparams(("parallel",)),
        name="rope_tables",
    )(pos_col, freq, sign)


def _norm_mm_body(x_ref, nw_ref, w_ref, o_ref, h_ref):
    @pl.when(pl.program_id(1) == 0)
    def _():
        h_ref[...] = _rms(x_ref[...], nw_ref[...]).astype(BF16)

    o_ref[...] = _dot(h_ref[...], w_ref[...]).astype(o_ref.dtype)


def _norm_matmul(x, nw, w, layer, out_dtype, tm, tn, name):
    m, k = x.shape
    n = w.shape[2]
    return pl.pallas_call(
        _norm_mm_body,
        grid=(m // tm, n // tn),
        in_specs=[pl.BlockSpec((tm, k), lambda i, j: (i, 0)),
                  _lspec(layer, (1, k)),
                  _lspec(layer, (k, tn), lambda i, j: (0, j))],
        out_specs=pl.BlockSpec((tm, tn), lambda i, j: (i, j)),
        out_shape=jax.ShapeDtypeStruct((m, n), out_dtype),
        scratch_shapes=[pltpu.VMEM((tm, k), BF16)],
        compiler_params=_cparams(("parallel", "arbitrary")),
        name=name,
    )(x, nw, w)


def _shift_cast_body(a_ref, b_ref, *rest, shift):
    o_ref = rest[-1]
    v = jnp.concatenate([a_ref[...], b_ref[...]], axis=1)
    o_ref[...] = v[:, shift:shift + o_ref.shape[1]].astype(BF16)


def _shift_cast(w, base, shift, width, tn, out_width, col0, into, name):
    depth, k, _ = w.shape
    assert base % tn == 0 and width % tn == 0 and col0 % tn == 0 and 0 < shift <= LANES
    in_specs = [pl.BlockSpec((None, k, tn), lambda l, j: (l, 0, base // tn + j)),
                pl.BlockSpec((None, k, LANES), lambda l, j: (l, 0, (base + tn) // LANES + j * (tn // LANES)))]
    args = [w, w]
    if into is not None:
        in_specs.append(pl.BlockSpec(memory_space=pl.ANY))
        args.append(into)
    return pl.pallas_call(
        functools.partial(_shift_cast_body, shift=shift),
        grid=(depth, width // tn),
        in_specs=in_specs,
        out_specs=pl.BlockSpec((None, k, tn), lambda l, j: (l, 0, col0 // tn + j)),
        out_shape=jax.ShapeDtypeStruct((depth, k, out_width), BF16),
        input_output_aliases={} if into is None else {2: 0},
        compiler_params=_cparams(("parallel", "parallel")),
        name=name,
    )(*args)


def _rope_apply(v, c, s):
    return v * c + pltpu.roll(v, 64, 1) * s


def _q_body(c_ref, nw_ref, w_ref, qnw_ref, ct_ref, st_ref, o_ref):
    cn = _rms(c_ref[...], nw_ref[...]).astype(BF16)
    acc = _dot(cn, w_ref[...])
    c = ct_ref[...]
    s = st_ref[...]
    wn = qnw_ref[:, :LANES]
    wrc = qnw_ref[:, LANES:2 * LANES] * c
    wxs = qnw_ref[:, 2 * LANES:] * s
    scale = QK_DIM ** -0.5 * LOG2_E
    for h in range(MLA_HEADS):
        qn = acc[:, Q_ACC_W * h:Q_ACC_W * h + LANES]
        qr = acc[:, Q_ACC_W * h + LANES:Q_ACC_W * h + 2 * LANES]
        qx = acc[:, Q_ACC_W * h + 2 * LANES:Q_ACC_W * (h + 1)]
        ssq = jnp.sum(qn * qn + qr * qr, axis=-1, keepdims=True)
        rs = lax.rsqrt(ssq / QK_DIM + EPS) * scale
        o_ref[:, HEAD_W * h:HEAD_W * h + LANES] = (qn * rs * wn).astype(BF16)
        o_ref[:, HEAD_W * h + LANES:HEAD_W * (h + 1)] = ((qr * wrc + qx * wxs) * rs).astype(BF16)


def _q_proj(small, nw, wq, qnw, ct, st, layer, tm):
    m = small.shape[0]
    n = MLA_HEADS * HEAD_W
    return pl.pallas_call(
        _q_body,
        grid=(m // tm,),
        in_specs=[pl.BlockSpec((tm, Q_LORA), lambda i: (i, 0)),
                  _lspec(layer, (1, Q_LORA)),
                  _lspec(layer, (Q_LORA, MLA_HEADS * Q_ACC_W)),
                  _lspec(layer, (1, Q_ACC_W)),
                  pl.BlockSpec((tm, LANES), lambda i: (i, 0)),
                  pl.BlockSpec((tm, LANES), lambda i: (i, 0))],
        out_specs=pl.BlockSpec((tm, n), lambda i: (i, 0)),
        out_shape=jax.ShapeDtypeStruct((m, n), BF16),
        compiler_params=_cparams(("parallel",)),
        name="q_proj",
    )(small, nw, wq, qnw, ct, st)


def _kv_body(c_ref, kr_ref, nw_ref, wk_ref, wvt_ref, kwn_ref, kwr_ref, ct_ref, st_ref, k_ref, vt_ref):
    cn = _rms(c_ref[...], nw_ref[...]).astype(BF16)
    acc = _dot(cn, wk_ref[...])
    vt_ref[0] = _dot_nt(wvt_ref[...], cn).astype(BF16)
    c = ct_ref[...]
    s = st_ref[...]
    kr = kr_ref[...]
    wn = kwn_ref[...]
    wr = kwr_ref[...]
    ssr = jnp.sum(kr * kr, axis=-1, keepdims=True)
    kr_rot = _rope_apply(kr * wr, c, s)
    for h in range(MLA_HEADS):
        kn = acc[:, LANES * h:LANES * (h + 1)]
        ssq = jnp.sum(kn * kn, axis=-1, keepdims=True) + ssr
        rs = lax.rsqrt(ssq / QK_DIM + EPS)
        k_ref[:, HEAD_W * h:HEAD_W * h + LANES] = (kn * rs * wn).astype(BF16)
        k_ref[:, HEAD_W * h + LANES:HEAD_W * (h + 1)] = (kr_rot * rs).astype(BF16)


def _kv_proj(small, nw, wk, wvt, kwn, kwr, ct, st, layer, tm):
    m = small.shape[0]
    nk = MLA_HEADS * HEAD_W
    nv = MLA_HEADS * V_DIM
    kr_blk = (Q_LORA + KV_LORA) // LANES
    return pl.pallas_call(
        _kv_body,
        grid=(m // tm,),
        in_specs=[pl.BlockSpec((tm, KV_LORA), lambda i: (i, 1)),
                  pl.BlockSpec((tm, LANES), lambda i: (i, kr_blk)),
                  _lspec(layer, (1, KV_LORA)),
                  _lspec(layer, (KV_LORA, MLA_HEADS * QK_NOPE)),
                  _lspec(layer, (nv, KV_LORA)),
                  _lspec(layer, (1, LANES)),
                  _lspec(layer, (1, LANES)),
                  pl.BlockSpec((tm, LANES), lambda i: (i, 0)),
                  pl.BlockSpec((tm, LANES), lambda i: (i, 0))],
        out_specs=[pl.BlockSpec((tm, nk), lambda i: (i, 0)),
                   pl.BlockSpec((1, nv, tm), lambda i: (i, 0, 0))],
        out_shape=[jax.ShapeDtypeStruct((m, nk), BF16), jax.ShapeDtypeStruct((m // tm, nv, tm), BF16)],
        compiler_params=_cparams(("parallel",)),
        name="kv_proj",
    )(small, small, nw, wk, wvt, kwn, kwr, ct, st)


def _attn_body(q_ref, k_ref, vt_ref, o_ref, s0_scr, s1_scr, p0_scr, p1_scr, m_scr, a_scr, acc_scr, *, tq, nq, hb):
    s_scr = (s0_scr, s1_scr)
    p_scr = (p0_scr, p1_scr)
    tk = tq // 2
    key_chunk = lax.broadcasted_iota(jnp.int32, (tk, tq), 0) // CHUNK
    qry_chunk = lax.broadcasted_iota(jnp.int32, (tk, tq), 1) // CHUNK
    diag_masks = [key_chunk + h * (tk // CHUNK) <= qry_chunk for h in range(2)]
    ones_rows = jnp.ones((ACC_ROWS - V_DIM, tk), BF16)

    def scores(qs, kpair, h, mask=None):
        koff = pl.multiple_of(kpair * tq + h * tk, tk)
        for hh in range(hb):
            s = _dot(k_ref[pl.ds(koff, tk), HEAD_W * hh:HEAD_W * (hh + 1)], qs[hh])
            s_scr[h][hh] = s if mask is None else jnp.where(mask, s, -jnp.inf)

    def pv(kpair, h):
        return [_dot(jnp.concatenate([vt_ref[kpair, V_DIM * hh:V_DIM * (hh + 1), h * tk:(h + 1) * tk],
                                      ones_rows], axis=0), p_scr[h][hh])
                for hh in range(hb)]

    def softmax(h, pvs):
        for hh in range(hb):
            m = m_scr[hh]
            s = s_scr[h][hh]
            m_new = jnp.maximum(m, jnp.max(s, axis=0, keepdims=True))
            p_scr[h][hh] = jnp.exp2((s - m_new).astype(BF16))
            alpha = jnp.exp2(m - m_new)
            acc_scr[hh] = alpha * (acc_scr[hh] + a_scr[hh] * pvs[hh])
            a_scr[hh] = alpha
            m_scr[hh] = m_new

    def q_block(qi, c):
        qoff = pl.multiple_of(qi * tq, tq)
        qs = [q_ref[pl.ds(qoff, tq), HEAD_W * hh:HEAD_W * (hh + 1)].astype(F32).T.astype(BF16) for hh in range(hb)]
        m_scr[...] = jnp.full(m_scr.shape, -jnp.inf, F32)
        a_scr[...] = jnp.zeros(a_scr.shape, F32)
        acc_scr[...] = jnp.zeros(acc_scr.shape, F32)
        for h in range(2):
            p_scr[h][...] = jnp.zeros(p_scr[h].shape, BF16)
            scores(qs, qi, h, diag_masks[h])

        def kpair_of(r):
            return jnp.where(r == 0, qi, r - 1)

        def body(r, c2):
            prev = kpair_of(jnp.maximum(r - 1, 0))
            for h in range(2):
                pvs = pv(prev, h)
                softmax(h, pvs)
                scores(qs, r, h)
            return c2

        lax.fori_loop(0, qi, body, 0)
        prev = kpair_of(jnp.maximum(qi - 1, 0))
        for h in range(2):
            softmax(h, pv(prev, h))
        last = kpair_of(qi)
        pv0, pv1 = pv(last, 0), pv(last, 1)
        for hh in range(hb):
            acc = acc_scr[hh] + a_scr[hh] * pv0[hh] + pv1[hh]
            o_ref[pl.ds(qoff, tq), V_DIM * hh:V_DIM * (hh + 1)] = (
                acc[:V_DIM] / acc[V_DIM:V_DIM + 1]).T.astype(BF16)
        return c

    lax.fori_loop(0, nq, q_block, 0)


def _attention(q, k, vt, batch, seq, tq, hb):
    m = q.shape[0]
    nq = seq // tq
    tk = tq // 2
    assert vt.shape[2] == tq
    return pl.pallas_call(
        functools.partial(_attn_body, tq=tq, nq=nq, hb=hb),
        grid=(batch, MLA_HEADS // hb),
        in_specs=[pl.BlockSpec((seq, hb * HEAD_W), lambda b, h: (b, h)),
                  pl.BlockSpec((seq, hb * HEAD_W), lambda b, h: (b, h)),
                  pl.BlockSpec((nq, hb * V_DIM, tq), lambda b, h: (b, h, 0))],
        out_specs=pl.BlockSpec((seq, hb * V_DIM), lambda b, h: (b, h)),
        out_shape=jax.ShapeDtypeStruct((m, MLA_HEADS * V_DIM), BF16),
        scratch_shapes=[pltpu.VMEM((hb, tk, tq), F32), pltpu.VMEM((hb, tk, tq), F32),
                        pltpu.VMEM((hb, tk, tq), BF16), pltpu.VMEM((hb, tk, tq), BF16),
                        pltpu.VMEM((hb, 1, tq), F32), pltpu.VMEM((hb, 1, tq), F32),
                        pltpu.VMEM((hb, ACC_ROWS, tq), F32)],
        compiler_params=_cparams(("parallel", "arbitrary")),
        name="attention",
    )(q, k, vt)


def _silu_of_half(h):
    return h + h * jnp.tanh(h)


def _silu(y):
    return _silu_of_half(0.5 * y)


def _split3(v):
    hi = v.astype(BF16)
    r = v - hi.astype(F32)
    mid = r.astype(BF16)
    lo = (r - mid.astype(F32)).astype(BF16)
    return hi, mid, lo


def _ssd_body(z_ref, x_ref, bc_ref, dt_ref, cwx_ref, cwb_ref, cbx_ref, cbb_ref, dtb_ref, alog_ref,
              dsk_ref, nw_ref, sel_ref, eye_ref, tri_ref, tri64_ref, shift_ref, y_ref, state_ref, xhalo, bchalo,
              *, cpb):
    t = CHUNK
    gn = SSM_GROUPS * SSM_STATE

    @pl.when(pl.program_id(1) == 0)
    def _():
        state_ref[...] = jnp.zeros_like(state_ref)
        xhalo[...] = jnp.zeros_like(xhalo)
        bchalo[...] = jnp.zeros_like(bchalo)

    def conv_silu(raw_ref, halo, w_ref, b_ref):
        raws = [raw_ref[c * t:(c + 1) * t, :] for c in range(cpb)]
        prevs = [halo[...]] + raws[:-1]
        halo[...] = raws[-1]
        shifted = [_dot(shift_ref[...], jnp.concatenate([prevs[c], raws[c]], axis=0)) for c in range(cpb)]
        outs = []
        for c in range(cpb):
            y = None
            for tap in range(CONV_WIDTH):
                shift = CONV_WIDTH - 1 - tap
                xt = raws[c].astype(F32) if shift == 0 else shifted[c][(shift - 1) * t:shift * t, :]
                term = xt * w_ref[tap:tap + 1, :]
                y = term if y is None else y + term
            outs.append(_silu_of_half(y + b_ref[...]))
        return outs

    xs_all = conv_silu(x_ref, xhalo, cwx_ref, cbx_ref)
    bc_all = conv_silu(bc_ref, bchalo, cwb_ref, cbb_ref)

    head_lane = lax.broadcasted_iota(jnp.int32, (t, LANES), 1) < SSM_HEADS
    neg_a = -jnp.exp(alog_ref[...])
    tri64 = tri64_ref[...]
    stacks = []
    for c in range(cpb):
        dt = jnp.where(head_lane, jax.nn.softplus(dt_ref[c * t:(c + 1) * t, :] + dtb_ref[...]), 0.0)
        h3 = _split3(dt * neg_a)
        a_cum = (_dot(tri64, h3[0]) + _dot(tri64, h3[1]) + _dot(tri64, h3[2])) * LOG2_E
        stacks += [a_cum, dt]
    s3 = [part.astype(F32) for part in _split3(jnp.concatenate(stacks, axis=0))]
    packed = jnp.concatenate([(s3[0] + pltpu.roll(s3[1], SSM_HEADS, 1)).astype(BF16), s3[2].astype(BF16)], axis=1)
    ex_all = _dot(packed, sel_ref[...])
    left = lax.broadcasted_iota(jnp.int32, (t, LANES), 1) < SSM_HEADDIM
    zero_b = jnp.zeros((t, LANES), BF16)

    pre = []
    for c in range(cpb):
        ex = ex_all[2 * t * c:2 * t * (c + 1)]
        a_col = ex[:t]
        ea = jnp.exp2(a_col)
        a_row = jnp.sum(jnp.where(eye_ref[...] > 0, a_col, 0.0), axis=0, keepdims=True)
        decay = jnp.exp2(jnp.where(tri_ref[...] > 0, a_col - a_row, -jnp.inf))
        xdt = xs_all[c] * ex[t:2 * t]
        pre.append((decay, ea, xdt.astype(BF16), (xdt * jnp.exp2(a_col[t - 1:t, :] - a_col)).astype(BF16)))

    for c in range(cpb):
        decay, ea, xdt_b, xw_b = pre[c]
        xs, bc = xs_all[c], bc_all[c]
        chunk_decay = ea[t - 1:t, :]
        r0 = c * t
        for g in range(SSM_GROUPS):
            lo, hi = GROUP_W * g, GROUP_W * (g + 1)
            bg = bc[:, SSM_STATE * g:SSM_STATE * (g + 1)]
            cg_b = bc[:, gn + SSM_STATE * g:gn + SSM_STATE * (g + 1)].astype(BF16)
            bg_b = bg.astype(BF16)
            cb2 = _dot_nt(cg_b, jnp.concatenate([bg_b, bg_b], axis=0))
            yd = []
            for j in range(GROUP_W // LANES):
                l0 = lo + LANES * j
                mm = (cb2 * decay[:, l0:l0 + LANES]).astype(BF16)
                xp = xdt_b[:, l0:l0 + LANES]
                rhs = jnp.concatenate([jnp.where(left, xp, zero_b), jnp.where(left, zero_b, xp)], axis=0)
                yd.append(_dot(mm, rhs))
            y_diag = jnp.concatenate(yd, axis=1)
            st = state_ref[g]
            y_off = _dot(cg_b, st.astype(BF16)) * ea[:, lo:hi]
            inc = _dot(bg.T.astype(BF16), xw_b[:, lo:hi])
            state_ref[g] = st * chunk_decay[:, lo:hi] + inc
            yg = y_diag + y_off + dsk_ref[:, lo:hi] * xs[:, lo:hi]
            yg = yg * _silu(z_ref[r0:r0 + t, lo:hi].astype(F32))
            ms = jnp.mean(yg * yg, axis=-1, keepdims=True)
            y_ref[r0:r0 + t, lo:hi] = (yg * lax.rsqrt(ms + EPS) * nw_ref[:, lo:hi]).astype(BF16)


def _ssd(big, small, cwx, cwb, cbx, cbb, dtb, alog, dsk, nw, sel, eye_t, tri_t, tri64, shift_m, layer, batch, seq,
         cpb):
    m = big.shape[0]
    t = CHUNK
    tb = cpb * t
    nb = seq // tb
    bcw = 2 * SSM_GROUPS * SSM_STATE
    bc_blk = (2 * D_INNER) // bcw
    dt_blk = (SMALL_N - LANES) // LANES
    row = lambda b, c: b * nb + c
    const = lambda shape: pl.BlockSpec(shape, lambda b, c: (0,) * len(shape))
    return pl.pallas_call(
        functools.partial(_ssd_body, cpb=cpb),
        grid=(batch, nb),
        in_specs=[pl.BlockSpec((tb, D_INNER), lambda b, c: (row(b, c), 0)),
                  pl.BlockSpec((tb, D_INNER), lambda b, c: (row(b, c), 1)),
                  pl.BlockSpec((tb, bcw), lambda b, c: (row(b, c), bc_blk)),
                  pl.BlockSpec((tb, LANES), lambda b, c: (row(b, c), dt_blk)),
                  _lspec(layer, (CONV_WIDTH, D_INNER)), _lspec(layer, (CONV_WIDTH, bcw)),
                  _lspec(layer, (1, D_INNER)), _lspec(layer, (1, bcw)),
                  _lspec(layer, (1, LANES)), _lspec(layer, (1, LANES)),
                  _lspec(layer, (1, D_INNER)), _lspec(layer, (1, D_INNER)),
                  const((2 * LANES, D_INNER)), const((t, D_INNER)), const((t, D_INNER)), const((t, t)),
                  const(((CONV_WIDTH - 1) * t, 2 * t))],
        out_specs=pl.BlockSpec((tb, D_INNER), lambda b, c: (row(b, c), 0)),
        out_shape=jax.ShapeDtypeStruct((m, D_INNER), BF16),
        scratch_shapes=[pltpu.VMEM((SSM_GROUPS, SSM_STATE, GROUP_W), F32),
                        pltpu.VMEM((t, D_INNER), BF16),
                        pltpu.VMEM((t, bcw), BF16)],
        compiler_params=_cparams(("parallel", "arbitrary")),
        name="ssd",
    )(big, big, big, small, cwx, cwb, cbx, cbb, dtb, alog, dsk, nw, sel, eye_t, tri_t, tri64, shift_m)


def _merge_body(o_ref, y_ref, wa_ref, wb_ref, g0_ref, g1_ref, out_ref):
    ya = _dot(o_ref[...], wa_ref[...])
    yb = _dot(y_ref[...], wb_ref[...])
    g0 = jax.nn.sigmoid(g0_ref[...].astype(F32))
    g1 = jax.nn.sigmoid(g1_ref[...].astype(F32))
    out_ref[...] = (g0 * ya + g1 * yb).astype(out_ref.dtype)


def _merge(o, y, wa, wb, big, layer, tm, tn):
    m = o.shape[0]
    g0_blk = (2 * D_INNER + 2 * SSM_GROUPS * SSM_STATE) // tn
    g1_blk = g0_blk + D_MODEL // tn
    return pl.pallas_call(
        _merge_body,
        grid=(m // tm, D_MODEL // tn),
        in_specs=[pl.BlockSpec((tm, o.shape[1]), lambda i, j: (i, 0)),
                  pl.BlockSpec((tm, y.shape[1]), lambda i, j: (i, 0)),
                  _lspec(layer, (wa.shape[1], tn), lambda i, j: (0, j)),
                  _lspec(layer, (wb.shape[1], tn), lambda i, j: (0, j)),
                  pl.BlockSpec((tm, tn), lambda i, j: (i, g0_blk + j)),
                  pl.BlockSpec((tm, tn), lambda i, j: (i, g1_blk + j))],
        out_specs=pl.BlockSpec((tm, tn), lambda i, j: (i, j)),
        out_shape=jax.ShapeDtypeStruct((m, D_MODEL), BF16),
        compiler_params=_cparams(("parallel", "arbitrary")),
        name="merge",
    )(o, y, wa, wb, big, big)


def _res_mm_body(x_ref, a_ref, w_ref, o_ref):
    o_ref[...] = x_ref[...] + _dot(a_ref[...], w_ref[...])


def _res_matmul(x, a, w, layer, tm):
    m, n = x.shape
    k = a.shape[1]
    return pl.pallas_call(
        _res_mm_body,
        grid=(m // tm,),
        in_specs=[pl.BlockSpec((tm, n), lambda i: (i, 0)),
                  pl.BlockSpec((tm, k), lambda i: (i, 0)),
                  _lspec(layer, (k, n))],
        out_specs=pl.BlockSpec((tm, n), lambda i: (i, 0)),
        out_shape=jax.ShapeDtypeStruct((m, n), F32),
        compiler_params=_cparams(("parallel",)),
        name="out_proj",
    )(x, a, w)


def _mlp_body(x_ref, nw_ref, wu_ref, wd_ref, o_ref, h_ref):
    @pl.when(pl.program_id(1) == 0)
    def _():
        h_ref[...] = _rms(x_ref[...], nw_ref[...]).astype(BF16)
        o_ref[...] = x_ref[...]

    u = _dot(h_ref[...], wu_ref[...])
    hid = jnp.square(jnp.maximum(u, 0.0)).astype(BF16)
    o_ref[...] += _dot(hid, wd_ref[...])


def _mlp(x, nw, wu, wd, layer, tm, tf):
    m, d = x.shape
    f = wu.shape[2]
    return pl.pallas_call(
        _mlp_body,
        grid=(m // tm, f // tf),
        in_specs=[pl.BlockSpec((tm, d), lambda i, j: (i, 0), pipeline_mode=pl.Buffered(1)),
                  _lspec(layer, (1, d)),
                  _lspec(layer, (d, tf), lambda i, j: (0, j)),
                  _lspec(layer, (tf, d), lambda i, j: (j, 0))],
        out_specs=pl.BlockSpec((tm, d), lambda i, j: (i, 0)),
        out_shape=jax.ShapeDtypeStruct((m, d), F32),
        scratch_shapes=[pltpu.VMEM((tm, d), BF16)],
        compiler_params=_cparams(("parallel", "arbitrary")),
        name="mlp",
    )(x, nw, wu, wd)


def _ple_body(x_ref, nw_ref, wg_ref, p_ref, wp_ref, o_ref):
    x = x_ref[...]
    h = _rms(x, nw_ref[...]).astype(BF16)
    gate = jax.nn.sigmoid(_dot(h, wg_ref[...]))
    pe = _dot(p_ref[...].astype(BF16), wp_ref[...])
    o_ref[...] = x + pe * gate


def _ple(x, nw, wg, p, wp, layer, tm):
    m, d = x.shape
    return pl.pallas_call(
        _ple_body,
        grid=(m // tm,),
        in_specs=[pl.BlockSpec((tm, d), lambda i: (i, 0)),
                  _lspec(layer, (1, d)),
                  _lspec(layer, (d, d)),
                  _lspec(layer, (tm, PLE_DIM), lambda i: (i, 0)),
                  _lspec(layer, (PLE_DIM, d))],
        out_specs=pl.BlockSpec((tm, d), lambda i: (i, 0)),
        out_shape=jax.ShapeDtypeStruct((m, d), F32),
        compiler_params=_cparams(("parallel",)),
        name="ple",
    )(x, nw, wg, p, wp)


def _rope_cols(w):
    half = QK_ROPE // 2
    z = jnp.zeros(w.shape[:-1] + (LANES // 2 - half,), w.dtype)
    return jnp.concatenate([w[..., :half], z, w[..., half:], z], axis=-1)


def kernel(x, p, positions, norm_mix_w, w_in, q_a_norm_w, w_uq, kv_a_norm_w, w_ukv, q_norm_w, k_norm_w,
           w_o_mla, conv_w, conv_b, dt_bias, a_log, d_skip, ssm_norm_w, w_o_ssm, w_out, norm_mlp_w, w_up,
           w_down, ple_norm_w, w_ple_gate, w_ple):
    batch, seq, d = x.shape
    m = batch * seq
    depth = w_in.shape[0]
    tm_big = min(1024, m)
    tm_mid = min(512, m)
    tm_small = min(256, m)
    tq = min(512, seq)

    inv_freq = 1.0 / (ROPE_THETA ** (jnp.arange(0, QK_ROPE, 2, dtype=F32) / QK_ROPE))
    freq = _rope_cols(jnp.concatenate([inv_freq, inv_freq]))[None, :]
    sign = _rope_cols(jnp.concatenate([-jnp.ones_like(inv_freq), jnp.ones_like(inv_freq)]))[None, :]
    ct, st = _rope_tables(positions.reshape(m, 1), freq, sign, tm_big)

    lane = jnp.arange(D_INNER)
    sel_row = jnp.arange(2 * LANES)[:, None]
    sel = ((sel_row % SSM_HEADS == (lane // SSM_HEADDIM)[None, :]) & (sel_row < 3 * SSM_HEADS)).astype(BF16)
    rows = jnp.arange(CHUNK)[:, None]
    eye_t = (rows == (lane % SSM_HEADDIM)[None, :]).astype(F32)
    tri_t = (rows >= (lane % SSM_HEADDIM)[None, :]).astype(F32)
    tri64 = (rows >= jnp.arange(CHUNK)[None, :]).astype(BF16)
    sh_rows = jnp.arange((CONV_WIDTH - 1) * CHUNK)[:, None]
    shift_m = (jnp.arange(2 * CHUNK)[None, :] == CHUNK + sh_rows % CHUNK - (sh_rows // CHUNK + 1)).astype(BF16)

    s0 = Q_LORA + KV_LORA + QK_ROPE
    s1 = s0 + D_INNER + (D_INNER + 2 * SSM_GROUPS * SSM_STATE)
    s2 = s1 + SSM_HEADS

    w_lat, w_kr, w_dt = lax.optimization_barrier(
        (w_in[..., :Q_LORA + KV_LORA], w_in[..., Q_LORA + KV_LORA:s0], w_in[..., s1:s2]))
    w_small = jnp.concatenate(
        [w_lat, _rope_cols(w_kr), w_dt, jnp.zeros((depth, d, LANES - SSM_HEADS), F32)], axis=-1).astype(BF16)
    big_n = (s1 - s0) + 2 * D_MODEL
    w_big = _shift_cast(w_in, s0 - QK_ROPE, QK_ROPE, s1 - s0, 512, big_n, 0, None, "relayout_zx")
    w_big = _shift_cast(w_in, s2 - LANES, LANES, 2 * D_MODEL, 512, big_n, s1 - s0, w_big, "relayout_gates")
    wq4 = w_uq.reshape(depth, Q_LORA, MLA_HEADS, QK_DIM)
    wq_rope = _rope_cols(wq4[..., QK_NOPE:])
    wq = jnp.concatenate([wq4[..., :QK_NOPE], wq_rope, jnp.roll(wq_rope, LANES // 2, axis=-1)], axis=-1)
    wq = wq.reshape(depth, Q_LORA, MLA_HEADS * Q_ACC_W).astype(BF16)
    wkv4 = w_ukv.reshape(depth, KV_LORA, MLA_HEADS, QK_NOPE + V_DIM)
    wk = wkv4[..., :QK_NOPE].reshape(depth, KV_LORA, -1).astype(BF16)
    wvt = jnp.swapaxes(wkv4[..., QK_NOPE:].reshape(depth, KV_LORA, -1), 1, 2).astype(BF16)
    w_o_mla_b = w_o_mla.astype(BF16)
    w_o_ssm_b = w_o_ssm.astype(BF16)
    w_out_b = w_out.astype(BF16)
    w_up_b = w_up.astype(BF16)
    w_down_b = w_down.astype(BF16)
    w_gate_b = w_ple_gate.astype(BF16)
    w_ple_b = w_ple.astype(BF16)

    rows3 = lambda v: v.reshape(depth, 1, -1).astype(F32)
    pad_h = jnp.zeros((depth, LANES - SSM_HEADS), F32)
    qw_rope = _rope_cols(q_norm_w[:, QK_NOPE:])
    qnw = rows3(jnp.concatenate([q_norm_w[:, :QK_NOPE], qw_rope, jnp.roll(qw_rope, LANES // 2, axis=-1)], axis=-1))
    kwn = rows3(k_norm_w[:, :QK_NOPE])
    kwr = rows3(_rope_cols(k_norm_w[:, QK_NOPE:]))
    dtb = rows3(jnp.concatenate([dt_bias, pad_h], axis=-1))
    alog = rows3(jnp.concatenate([a_log, pad_h], axis=-1))
    dsk = rows3(jnp.repeat(d_skip, SSM_HEADDIM, axis=-1))
    mix_nw, qa_nw, kva_nw = rows3(norm_mix_w), rows3(q_a_norm_w), rows3(kv_a_norm_w)
    ssm_nw, mlp_nw, ple_nw = rows3(ssm_norm_w), rows3(norm_mlp_w), rows3(ple_norm_w)
    conv_wh, conv_bh = 0.5 * conv_w, 0.5 * conv_b
    cwx, cwb = conv_wh[..., :D_INNER], conv_wh[..., D_INNER:]
    cbx, cbb = rows3(conv_bh[:, :D_INNER]), rows3(conv_bh[:, D_INNER:])
    p3 = p.reshape(depth, m, PLE_DIM)

    xf = x.reshape(m, d)
    for i in range(depth):
        small = _norm_matmul(xf, mix_nw, w_small, i, F32, tm_mid, SMALL_N, "in_proj_small")
        big = _norm_matmul(xf, mix_nw, w_big, i, BF16, tm_big, 1024, "in_proj_big")
        q = _q_proj(small, qa_nw, wq, qnw, ct, st, i, tm_small)
        k, vt = _kv_proj(small, kva_nw, wk, wvt, kwn, kwr, ct, st, i, tq)
        o = _attention(q, k, vt, batch, seq, tq, 2)
        y = _ssd(big, small, cwx, cwb, cbx, cbb, dtb, alog, dsk, ssm_nw, sel, eye_t, tri_t, tri64, shift_m, i,
                 batch, seq, 4)
        merged = _merge(o, y, w_o_mla_b, w_o_ssm_b, big, i, tm_big, 512)
        xf = _res_matmul(xf, merged, w_out_b, i, tm_mid)
        xf = _mlp(xf, mlp_nw, w_up_b, w_down_b, i, tm_mid, 1024)
        xf = _ple(xf, ple_nw, w_gate_b, p3, w_ple_b, i, tm_mid)
    return xf.reshape(batch, seq, d)
```

```python
import functools

import jax
import jax.numpy as jnp
from jax import lax
from jax.experimental import pallas as pl
from jax.experimental.pallas import tpu as pltpu

F32 = jnp.float32
BF16 = jnp.bfloat16

D_MODEL = 2048
DEPTH = 4
CHUNK = 64
PLE_DIM = 256
EPS = 1e-6
MLA_HEADS = 16
Q_LORA = 512
KV_LORA = 512
QK_NOPE = 128
QK_ROPE = 64
V_DIM = 128
QK_DIM = QK_NOPE + QK_ROPE
ROPE_THETA = 10000.0
D_INNER = 2 * D_MODEL
SSM_HEADDIM = 64
SSM_HEADS = D_INNER // SSM_HEADDIM
SSM_GROUPS = 8
SSM_STATE = 128
GROUP_W = D_INNER // SSM_GROUPS
CONV_WIDTH = 4
D_FF = 4 * D_MODEL

LOG2_E = 1.4426950408889634
LANES = 128
ACC_ROWS = V_DIM + 16
Q_ACC_W = 3 * LANES
HEAD_W = 2 * LANES
SMALL_N = Q_LORA + KV_LORA + 2 * LANES
VMEM_LIMIT = 56 * 1024 * 1024


def _cparams(sem):
    return pltpu.CompilerParams(dimension_semantics=sem, vmem_limit_bytes=VMEM_LIMIT)


def _rms(xf, w):
    ms = jnp.mean(xf * xf, axis=-1, keepdims=True)
    return xf * lax.rsqrt(ms + EPS) * w


def _lspec(layer, block, imap=None):
    if imap is None:
        imap = lambda *g: (0,) * len(block)
    return pl.BlockSpec((None,) + tuple(block), lambda *g: (layer,) + tuple(imap(*g)))


def _dot(a, b):
    return jnp.dot(a, b, preferred_element_type=F32)


def _dot_nt(a, b):
    return lax.dot_general(a, b, (((1,), (1,)), ((), ())), preferred_element_type=F32)


def _rope_body(pos_ref, freq_ref, sign_ref, ct_ref, st_ref):
    ang = pos_ref[...].astype(F32) * freq_ref[...]
    ct_ref[...] = jnp.cos(ang)
    st_ref[...] = jnp.sin(ang) * sign_ref[...]


def _rope_tables(pos_col, freq, sign, tm):
    m = pos_col.shape[0]
    return pl.pallas_call(
        _rope_body,
        grid=(m // tm,),
        in_specs=[pl.BlockSpec((tm, 1), lambda i: (i, 0)),
                  pl.BlockSpec((1, LANES), lambda i: (0, 0)),
                  pl.BlockSpec((1, LANES), lambda i: (0, 0))],
        out_specs=[pl.BlockSpec((tm, LANES), lambda i: (i, 0)),
                   pl.BlockSpec((tm, LANES), lambda i: (i, 0))],
        out_shape=[jax.ShapeDtypeStruct((m, LANES), F32)] * 2,
        compiler_params=_cparams(("parallel",)),
        name="rope_tables",
    )(pos_col, freq, sign)


def _norm_mm_body(x_ref, nw_ref, w_ref, o_ref, h_ref, *, relu2):
    @pl.when(pl.program_id(1) == 0)
    def _():
        h_ref[...] = _rms(x_ref[...], nw_ref[...]).astype(BF16)

    y = _dot(h_ref[...], w_ref[...])
    if relu2:
        y = jnp.square(jnp.maximum(y, 0.0))
    o_ref[...] = y.astype(o_ref.dtype)


def _norm_matmul(x, nw, w, layer, out_dtype, tm, tn, name, relu2=False):
    m, k = x.shape
    n = w.shape[2]
    return pl.pallas_call(
        functools.partial(_norm_mm_body, relu2=relu2),
        grid=(m // tm, n // tn),
        in_specs=[pl.BlockSpec((tm, k), lambda i, j: (i, 0)),
                  _lspec(layer, (1, k)),
                  _lspec(layer, (k, tn), lambda i, j: (0, j))],
        out_specs=pl.BlockSpec((tm, tn), lambda i, j: (i, j)),
        out_shape=jax.ShapeDtypeStruct((m, n), out_dtype),
        scratch_shapes=[pltpu.VMEM((tm, k), BF16)],
        compiler_params=_cparams(("parallel", "arbitrary")),
        name=name,
    )(x, nw, w)


def _shift_cast_body(a_ref, b_ref, *rest, shift):
    o_ref = rest[-1]
    v = jnp.concatenate([a_ref[...], b_ref[...]], axis=1)
    o_ref[...] = v[:, shift:shift + o_ref.shape[1]].astype(BF16)


def _shift_cast(w, base, shift, width, tn, out_width, col0, into, name):
    depth, k, _ = w.shape
    assert base % tn == 0 and width % tn == 0 and col0 % tn == 0 and 0 < shift <= LANES
    in_specs = [pl.BlockSpec((None, k, tn), lambda l, j: (l, 0, base // tn + j)),
                pl.BlockSpec((None, k, LANES), lambda l, j: (l, 0, (base + tn) // LANES + j * (tn // LANES)))]
    args = [w, w]
    if into is not None:
        in_specs.append(pl.BlockSpec(memory_space=pl.ANY))
        args.append(into)
    return pl.pallas_call(
        functools.partial(_shift_cast_body, shift=shift),
        grid=(depth, width // tn),
        in_specs=in_specs,
        out_specs=pl.BlockSpec((None, k, tn), lambda l, j: (l, 0, col0 // tn + j)),
        out_shape=jax.ShapeDtypeStruct((depth, k, out_width), BF16),
        input_output_aliases={} if into is None else {2: 0},
        compiler_params=_cparams(("parallel", "parallel")),
        name=name,
    )(*args)


def _rope_apply(v, c, s):
    return v * c + pltpu.roll(v, 64, 1) * s


def _q_body(c_ref, nw_ref, w_ref, qnw_ref, ct_ref, st_ref, o_ref):
    cn = _rms(c_ref[...], nw_ref[...]).astype(BF16)
    acc = _dot(cn, w_ref[...])
    c = ct_ref[...]
    s = st_ref[...]
    wn = qnw_ref[:, :LANES]
    wrc = qnw_ref[:, LANES:2 * LANES] * c
    wxs = qnw_ref[:, 2 * LANES:] * s
    scale = QK_DIM ** -0.5 * LOG2_E
    for h in range(MLA_HEADS):
        qn = acc[:, Q_ACC_W * h:Q_ACC_W * h + LANES]
        qr = acc[:, Q_ACC_W * h + LANES:Q_ACC_W * h + 2 * LANES]
        qx = acc[:, Q_ACC_W * h + 2 * LANES:Q_ACC_W * (h + 1)]
        ssq = jnp.sum(qn * qn + qr * qr, axis=-1, keepdims=True)
        rs = lax.rsqrt(ssq / QK_DIM + EPS) * scale
        o_ref[:, HEAD_W * h:HEAD_W * h + LANES] = (qn * rs * wn).astype(BF16)
        o_ref[:, HEAD_W * h + LANES:HEAD_W * (h + 1)] = ((qr * wrc + qx * wxs) * rs).astype(BF16)


def _q_proj(small, nw, wq, qnw, ct, st, layer, tm):
    m = small.shape[0]
    n = MLA_HEADS * HEAD_W
    return pl.pallas_call(
        _q_body,
        grid=(m // tm,),
        in_specs=[pl.BlockSpec((tm, Q_LORA), lambda i: (i, 0)),
                  _lspec(layer, (1, Q_LORA)),
                  _lspec(layer, (Q_LORA, MLA_HEADS * Q_ACC_W)),
                  _lspec(layer, (1, Q_ACC_W)),
                  pl.BlockSpec((tm, LANES), lambda i: (i, 0)),
                  pl.BlockSpec((tm, LANES), lambda i: (i, 0))],
        out_specs=pl.BlockSpec((tm, n), lambda i: (i, 0)),
        out_shape=jax.ShapeDtypeStruct((m, n), BF16),
        compiler_params=_cparams(("parallel",)),
        name="q_proj",
    )(small, nw, wq, qnw, ct, st)


def _kv_body(c_ref, kr_ref, nw_ref, wk_ref, wvt_ref, kwn_ref, kwr_ref, ct_ref, st_ref, k_ref, vt_ref):
    cn = _rms(c_ref[...], nw_ref[...]).astype(BF16)
    acc = _dot(cn, wk_ref[...])
    vt_ref[0] = _dot_nt(wvt_ref[...], cn).astype(BF16)
    c = ct_ref[...]
    s = st_ref[...]
    kr = kr_ref[...]
    wn = kwn_ref[...]
    wr = kwr_ref[...]
    ssr = jnp.sum(kr * kr, axis=-1, keepdims=True)
    kr_rot = _rope_apply(kr * wr, c, s)
    for h in range(MLA_HEADS):
        kn = acc[:, LANES * h:LANES * (h + 1)]
        ssq = jnp.sum(kn * kn, axis=-1, keepdims=True) + ssr
        rs = lax.rsqrt(ssq / QK_DIM + EPS)
        k_ref[:, HEAD_W * h:HEAD_W * h + LANES] = (kn * rs * wn).astype(BF16)
        k_ref[:, HEAD_W * h + LANES:HEAD_W * (h + 1)] = (kr_rot * rs).astype(BF16)


def _kv_proj(small, nw, wk, wvt, kwn, kwr, ct, st, layer, tm):
    m = small.shape[0]
    nk = MLA_HEADS * HEAD_W
    nv = MLA_HEADS * V_DIM
    kr_blk = (Q_LORA + KV_LORA) // LANES
    return pl.pallas_call(
        _kv_body,
        grid=(m // tm,),
        in_specs=[pl.BlockSpec((tm, KV_LORA), lambda i: (i, 1)),
                  pl.BlockSpec((tm, LANES), lambda i: (i, kr_blk)),
                  _lspec(layer, (1, KV_LORA)),
                  _lspec(layer, (KV_LORA, MLA_HEADS * QK_NOPE)),
                  _lspec(layer, (nv, KV_LORA)),
                  _lspec(layer, (1, LANES)),
                  _lspec(layer, (1, LANES)),
                  pl.BlockSpec((tm, LANES), lambda i: (i, 0)),
                  pl.BlockSpec((tm, LANES), lambda i: (i, 0))],
        out_specs=[pl.BlockSpec((tm, nk), lambda i: (i, 0)),
                   pl.BlockSpec((1, nv, tm), lambda i: (i, 0, 0))],
        out_shape=[jax.ShapeDtypeStruct((m, nk), BF16), jax.ShapeDtypeStruct((m // tm, nv, tm), BF16)],
        compiler_params=_cparams(("parallel",)),
        name="kv_proj",
    )(small, small, nw, wk, wvt, kwn, kwr, ct, st)


def _attn_body(q_ref, k_ref, vt_ref, o_ref, s0_scr, s1_scr, p0_scr, p1_scr, m_scr, a_scr, acc_scr, *, tq, nq, hb):
    s_scr = (s0_scr, s1_scr)
    p_scr = (p0_scr, p1_scr)
    tk = tq // 2
    key_chunk = lax.broadcasted_iota(jnp.int32, (tk, tq), 0) // CHUNK
    qry_chunk = lax.broadcasted_iota(jnp.int32, (tk, tq), 1) // CHUNK
    diag_masks = [key_chunk + h * (tk // CHUNK) <= qry_chunk for h in range(2)]
    ones_rows = jnp.ones((ACC_ROWS - V_DIM, tk), BF16)

    def scores(qs, kpair, h, mask=None):
        koff = pl.multiple_of(kpair * tq + h * tk, tk)
        for hh in range(hb):
            s = _dot(k_ref[pl.ds(koff, tk), HEAD_W * hh:HEAD_W * (hh + 1)], qs[hh])
            s_scr[h][hh] = s if mask is None else jnp.where(mask, s, -jnp.inf)

    def pv(kpair, h):
        return [_dot(jnp.concatenate([vt_ref[kpair, V_DIM * hh:V_DIM * (hh + 1), h * tk:(h + 1) * tk],
                                      ones_rows], axis=0), p_scr[h][hh])
                for hh in range(hb)]

    def softmax(h, pvs):
        for hh in range(hb):
            m = m_scr[hh]
            s = s_scr[h][hh]
            m_new = jnp.maximum(m, jnp.max(s, axis=0, keepdims=True))
            p_scr[h][hh] = jnp.exp2((s - m_new).astype(BF16))
            alpha = jnp.exp2(m - m_new)
            acc_scr[hh] = alpha * (acc_scr[hh] + a_scr[hh] * pvs[hh])
            a_scr[hh] = alpha
            m_scr[hh] = m_new

    def q_block(qi, c):
        qoff = pl.multiple_of(qi * tq, tq)
        qs = [q_ref[pl.ds(qoff, tq), HEAD_W * hh:HEAD_W * (hh + 1)].astype(F32).T.astype(BF16) for hh in range(hb)]
        m_scr[...] = jnp.full(m_scr.shape, -jnp.inf, F32)
        a_scr[...] = jnp.zeros(a_scr.shape, F32)
        acc_scr[...] = jnp.zeros(acc_scr.shape, F32)
        for h in range(2):
            p_scr[h][...] = jnp.zeros(p_scr[h].shape, BF16)
            scores(qs, qi, h, diag_masks[h])

        def kpair_of(r):
            return jnp.where(r == 0, qi, r - 1)

        def body(r, c2):
            prev = kpair_of(jnp.maximum(r - 1, 0))
            for h in range(2):
                pvs = pv(prev, h)
                softmax(h, pvs)
                scores(qs, r, h)
            return c2

        lax.fori_loop(0, qi, body, 0)
        prev = kpair_of(jnp.maximum(qi - 1, 0))
        for h in range(2):
            softmax(h, pv(prev, h))
        last = kpair_of(qi)
        pv0, pv1 = pv(last, 0), pv(last, 1)
        for hh in range(hb):
            acc = acc_scr[hh] + a_scr[hh] * pv0[hh] + pv1[hh]
            o_ref[pl.ds(qoff, tq), V_DIM * hh:V_DIM * (hh + 1)] = (
                acc[:V_DIM] / acc[V_DIM:V_DIM + 1]).T.astype(BF16)
        return c

    lax.fori_loop(0, nq, q_block, 0)


def _attention(q, k, vt, batch, seq, tq, hb):
    m = q.shape[0]
    nq = seq // tq
    tk = tq // 2
    assert vt.shape[2] == tq
    return pl.pallas_call(
        functools.partial(_attn_body, tq=tq, nq=nq, hb=hb),
        grid=(batch, MLA_HEADS // hb),
        in_specs=[pl.BlockSpec((seq, hb * HEAD_W), lambda b, h: (b, h)),
                  pl.BlockSpec((seq, hb * HEAD_W), lambda b, h: (b, h)),
                  pl.BlockSpec((nq, hb * V_DIM, tq), lambda b, h: (b, h, 0))],
        out_specs=pl.BlockSpec((seq, hb * V_DIM), lambda b, h: (b, h)),
        out_shape=jax.ShapeDtypeStruct((m, MLA_HEADS * V_DIM), BF16),
        scratch_shapes=[pltpu.VMEM((hb, tk, tq), F32), pltpu.VMEM((hb, tk, tq), F32),
                        pltpu.VMEM((hb, tk, tq), BF16), pltpu.VMEM((hb, tk, tq), BF16),
                        pltpu.VMEM((hb, 1, tq), F32), pltpu.VMEM((hb, 1, tq), F32),
                        pltpu.VMEM((hb, ACC_ROWS, tq), F32)],
        compiler_params=_cparams(("parallel", "arbitrary")),
        name="attention",
    )(q, k, vt)


def _silu_of_half(h):
    return h + h * jnp.tanh(h)


def _silu(y):
    return _silu_of_half(0.5 * y)


def _split3(v):
    hi = v.astype(BF16)
    r = v - hi.astype(F32)
    mid = r.astype(BF16)
    lo = (r - mid.astype(F32)).astype(BF16)
    return hi, mid, lo


def _ssd_body(z_ref, x_ref, bc_ref, dt_ref, cwx_ref, cwb_ref, cbx_ref, cbb_ref, dtb_ref, alog_ref,
              dsk_ref, nw_ref, sel_ref, eye_ref, tri_ref, tri64_ref, shift_ref, y_ref, state_ref, xhalo, bchalo,
              *, cpb):
    t = CHUNK
    gn = SSM_GROUPS * SSM_STATE

    @pl.when(pl.program_id(1) == 0)
    def _():
        state_ref[...] = jnp.zeros_like(state_ref)
        xhalo[...] = jnp.zeros_like(xhalo)
        bchalo[...] = jnp.zeros_like(bchalo)

    def conv_silu(raw_ref, halo, w_ref, b_ref):
        raws = [raw_ref[c * t:(c + 1) * t, :] for c in range(cpb)]
        prevs = [halo[...]] + raws[:-1]
        halo[...] = raws[-1]
        shifted = [_dot(shift_ref[...], jnp.concatenate([prevs[c], raws[c]], axis=0)) for c in range(cpb)]
        outs = []
        for c in range(cpb):
            y = None
            for tap in range(CONV_WIDTH):
                shift = CONV_WIDTH - 1 - tap
                xt = raws[c].astype(F32) if shift == 0 else shifted[c][(shift - 1) * t:shift * t, :]
                term = xt * w_ref[tap:tap + 1, :]
                y = term if y is None else y + term
            outs.append(_silu_of_half(y + b_ref[...]))
        return outs

    xs_all = conv_silu(x_ref, xhalo, cwx_ref, cbx_ref)
    bc_all = conv_silu(bc_ref, bchalo, cwb_ref, cbb_ref)

    head_lane = lax.broadcasted_iota(jnp.int32, (t, LANES), 1) < SSM_HEADS
    neg_a = -jnp.exp(alog_ref[...])
    tri64 = tri64_ref[...]
    stacks = []
    for c in range(cpb):
        dt = jnp.where(head_lane, jax.nn.softplus(dt_ref[c * t:(c + 1) * t, :] + dtb_ref[...]), 0.0)
        h3 = _split3(dt * neg_a)
        a_cum = (_dot(tri64, h3[0]) + _dot(tri64, h3[1]) + _dot(tri64, h3[2])) * LOG2_E
        stacks += [a_cum, dt]
    s3 = [part.astype(F32) for part in _split3(jnp.concatenate(stacks, axis=0))]
    packed = jnp.concatenate([(s3[0] + pltpu.roll(s3[1], SSM_HEADS, 1)).astype(BF16), s3[2].astype(BF16)], axis=1)
    ex_all = _dot(packed, sel_ref[...])
    left = lax.broadcasted_iota(jnp.int32, (t, LANES), 1) < SSM_HEADDIM
    zero_b = jnp.zeros((t, LANES), BF16)

    pre = []
    for c in range(cpb):
        ex = ex_all[2 * t * c:2 * t * (c + 1)]
        a_col = ex[:t]
        ea = jnp.exp2(a_col)
        a_row = jnp.sum(jnp.where(eye_ref[...] > 0, a_col, 0.0), axis=0, keepdims=True)
        decay = jnp.exp2(jnp.where(tri_ref[...] > 0, a_col - a_row, -jnp.inf))
        xdt = xs_all[c] * ex[t:2 * t]
        pre.append((decay, ea, xdt.astype(BF16), (xdt * jnp.exp2(a_col[t - 1:t, :] - a_col)).astype(BF16)))

    for c in range(cpb):
        decay, ea, xdt_b, xw_b = pre[c]
        xs, bc = xs_all[c], bc_all[c]
        chunk_decay = ea[t - 1:t, :]
        r0 = c * t
        for g in range(SSM_GROUPS):
            lo, hi = GROUP_W * g, GROUP_W * (g + 1)
            bg = bc[:, SSM_STATE * g:SSM_STATE * (g + 1)]
            cg_b = bc[:, gn + SSM_STATE * g:gn + SSM_STATE * (g + 1)].astype(BF16)
            bg_b = bg.astype(BF16)
            cb2 = _dot_nt(cg_b, jnp.concatenate([bg_b, bg_b], axis=0))
            yd = []
            for j in range(GROUP_W // LANES):
                l0 = lo + LANES * j
                mm = (cb2 * decay[:, l0:l0 + LANES]).astype(BF16)
                xp = xdt_b[:, l0:l0 + LANES]
                rhs = jnp.concatenate([jnp.where(left, xp, zero_b), jnp.where(left, zero_b, xp)], axis=0)
                yd.append(_dot(mm, rhs))
            y_diag = jnp.concatenate(yd, axis=1)
            st = state_ref[g]
            y_off = _dot(cg_b, st.astype(BF16)) * ea[:, lo:hi]
            inc = _dot(bg.T.astype(BF16), xw_b[:, lo:hi])
            state_ref[g] = st * chunk_decay[:, lo:hi] + inc
            yg = y_diag + y_off + dsk_ref[:, lo:hi] * xs[:, lo:hi]
            yg = yg * _silu(z_ref[r0:r0 + t, lo:hi].astype(F32))
            ms = jnp.mean(yg * yg, axis=-1, keepdims=True)
            y_ref[r0:r0 + t, lo:hi] = (yg * lax.rsqrt(ms + EPS) * nw_ref[:, lo:hi]).astype(BF16)


def _ssd(big, small, cwx, cwb, cbx, cbb, dtb, alog, dsk, nw, sel, eye_t, tri_t, tri64, shift_m, layer, batch, seq,
         cpb):
    m = big.shape[0]
    t = CHUNK
    tb = cpb * t
    nb = seq // tb
    bcw = 2 * SSM_GROUPS * SSM_STATE
    bc_blk = (2 * D_INNER) // bcw
    dt_blk = (SMALL_N - LANES) // LANES
    row = lambda b, c: b * nb + c
    const = lambda shape: pl.BlockSpec(shape, lambda b, c: (0,) * len(shape))
    return pl.pallas_call(
        functools.partial(_ssd_body, cpb=cpb),
        grid=(batch, nb),
        in_specs=[pl.BlockSpec((tb, D_INNER), lambda b, c: (row(b, c), 0)),
                  pl.BlockSpec((tb, D_INNER), lambda b, c: (row(b, c), 1)),
                  pl.BlockSpec((tb, bcw), lambda b, c: (row(b, c), bc_blk)),
                  pl.BlockSpec((tb, LANES), lambda b, c: (row(b, c), dt_blk)),
                  _lspec(layer, (CONV_WIDTH, D_INNER)), _lspec(layer, (CONV_WIDTH, bcw)),
                  _lspec(layer, (1, D_INNER)), _lspec(layer, (1, bcw)),
                  _lspec(layer, (1, LANES)), _lspec(layer, (1, LANES)),
                  _lspec(layer, (1, D_INNER)), _lspec(layer, (1, D_INNER)),
                  const((2 * LANES, D_INNER)), const((t, D_INNER)), const((t, D_INNER)), const((t, t)),
                  const(((CONV_WIDTH - 1) * t, 2 * t))],
        out_specs=pl.BlockSpec((tb, D_INNER), lambda b, c: (row(b, c), 0)),
        out_shape=jax.ShapeDtypeStruct((m, D_INNER), BF16),
        scratch_shapes=[pltpu.VMEM((SSM_GROUPS, SSM_STATE, GROUP_W), F32),
                        pltpu.VMEM((t, D_INNER), BF16),
                        pltpu.VMEM((t, bcw), BF16)],
        compiler_params=_cparams(("parallel", "arbitrary")),
        name="ssd",
    )(big, big, big, small, cwx, cwb, cbx, cbb, dtb, alog, dsk, nw, sel, eye_t, tri_t, tri64, shift_m)


def _merge_body(o_ref, y_ref, wa_ref, wb_ref, g0_ref, g1_ref, out_ref):
    ya = _dot(o_ref[...], wa_ref[...])
    yb = _dot(y_ref[...], wb_ref[...])
    g0 = jax.nn.sigmoid(g0_ref[...].astype(F32))
    g1 = jax.nn.sigmoid(g1_ref[...].astype(F32))
    out_ref[...] = (g0 * ya + g1 * yb).astype(out_ref.dtype)


def _merge(o, y, wa, wb, big, layer, tm, tn):
    m = o.shape[0]
    g0_blk = (2 * D_INNER + 2 * SSM_GROUPS * SSM_STATE) // tn
    g1_blk = g0_blk + D_MODEL // tn
    return pl.pallas_call(
        _merge_body,
        grid=(m // tm, D_MODEL // tn),
        in_specs=[pl.BlockSpec((tm, o.shape[1]), lambda i, j: (i, 0)),
                  pl.BlockSpec((tm, y.shape[1]), lambda i, j: (i, 0)),
                  _lspec(layer, (wa.shape[1], tn), lambda i, j: (0, j)),
                  _lspec(layer, (wb.shape[1], tn), lambda i, j: (0, j)),
                  pl.BlockSpec((tm, tn), lambda i, j: (i, g0_blk + j)),
                  pl.BlockSpec((tm, tn), lambda i, j: (i, g1_blk + j))],
        out_specs=pl.BlockSpec((tm, tn), lambda i, j: (i, j)),
        out_shape=jax.ShapeDtypeStruct((m, D_MODEL), BF16),
        compiler_params=_cparams(("parallel", "arbitrary")),
        name="merge",
    )(o, y, wa, wb, big, big)


def _res_mm_body(x_ref, a_ref, w_ref, o_ref):
    o_ref[...] = x_ref[...] + _dot(a_ref[...], w_ref[...])


def _res_matmul(x, a, w, layer, tm):
    m, n = x.shape
    k = a.shape[1]
    return pl.pallas_call(
        _res_mm_body,
        grid=(m // tm,),
        in_specs=[pl.BlockSpec((tm, n), lambda i: (i, 0)),
                  pl.BlockSpec((tm, k), lambda i: (i, 0)),
                  _lspec(layer, (k, n))],
        out_specs=pl.BlockSpec((tm, n), lambda i: (i, 0)),
        out_shape=jax.ShapeDtypeStruct((m, n), F32),
        compiler_params=_cparams(("parallel",)),
        name="out_proj",
    )(x, a, w)


def _res_matmul_tiled(x, a, w, layer, tm, tn, name):
    m, n = x.shape
    k = a.shape[1]
    return pl.pallas_call(
        _res_mm_body,
        grid=(m // tm, n // tn),
        in_specs=[pl.BlockSpec((tm, tn), lambda i, j: (i, j)),
                  pl.BlockSpec((tm, k), lambda i, j: (i, 0)),
                  _lspec(layer, (k, tn), lambda i, j: (0, j))],
        out_specs=pl.BlockSpec((tm, tn), lambda i, j: (i, j)),
        out_shape=jax.ShapeDtypeStruct((m, n), F32),
        compiler_params=_cparams(("parallel", "arbitrary")),
        name=name,
    )(x, a, w)


def _mlp_body(x_ref, nw_ref, wu_ref, wd_ref, o_ref, h_ref):
    @pl.when(pl.program_id(1) == 0)
    def _():
        h_ref[...] = _rms(x_ref[...], nw_ref[...]).astype(BF16)
        o_ref[...] = x_ref[...]

    u = _dot(h_ref[...], wu_ref[...])
    hid = jnp.square(jnp.maximum(u, 0.0)).astype(BF16)
    o_ref[...] += _dot(hid, wd_ref[...])


def _mlp(x, nw, wu, wd, layer, tm, tf):
    m, d = x.shape
    f = wu.shape[2]
    return pl.pallas_call(
        _mlp_body,
        grid=(m // tm, f // tf),
        in_specs=[pl.BlockSpec((tm, d), lambda i, j: (i, 0), pipeline_mode=pl.Buffered(1)),
                  _lspec(layer, (1, d)),
                  _lspec(layer, (d, tf), lambda i, j: (0, j)),
                  _lspec(layer, (tf, d), lambda i, j: (j, 0))],
        out_specs=pl.BlockSpec((tm, d), lambda i, j: (i, 0)),
        out_shape=jax.ShapeDtypeStruct((m, d), F32),
        scratch_shapes=[pltpu.VMEM((tm, d), BF16)],
        compiler_params=_cparams(("parallel", "arbitrary")),
        name="mlp",
    )(x, nw, wu, wd)


def _ple_body(x_ref, nw_ref, wg_ref, p_ref, wp_ref, o_ref):
    x = x_ref[...]
    h = _rms(x, nw_ref[...]).astype(BF16)
    gate = jax.nn.sigmoid(_dot(h, wg_ref[...]))
    pe = _dot(p_ref[...].astype(BF16), wp_ref[...])
    o_ref[...] = x + pe * gate


def _ple(x, nw, wg, p, wp, layer, tm):
    m, d = x.shape
    return pl.pallas_call(
        _ple_body,
        grid=(m // tm,),
        in_specs=[pl.BlockSpec((tm, d), lambda i: (i, 0)),
                  _lspec(layer, (1, d)),
                  _lspec(layer, (d, d)),
                  _lspec(layer, (tm, PLE_DIM), lambda i: (i, 0)),
                  _lspec(layer, (PLE_DIM, d))],
        out_specs=pl.BlockSpec((tm, d), lambda i: (i, 0)),
        out_shape=jax.ShapeDtypeStruct((m, d), F32),
        compiler_params=_cparams(("parallel",)),
        name="ple",
    )(x, nw, wg, p, wp)


def _rope_cols(w):
    half = QK_ROPE // 2
    z = jnp.zeros(w.shape[:-1] + (LANES // 2 - half,), w.dtype)
    return jnp.concatenate([w[..., :half], z, w[..., half:], z], axis=-1)


def kernel(x, p, positions, norm_mix_w, w_in, q_a_norm_w, w_uq, kv_a_norm_w, w_ukv, q_norm_w, k_norm_w,
           w_o_mla, conv_w, conv_b, dt_bias, a_log, d_skip, ssm_norm_w, w_o_ssm, w_out, norm_mlp_w, w_up,
           w_down, ple_norm_w, w_ple_gate, w_ple):
    batch, seq, d = x.shape
    m = batch * seq
    depth = w_in.shape[0]
    tm_big = min(1024, m)
    tm_mid = min(512, m)
    tm_small = min(256, m)
    tq = min(512, seq)

    inv_freq = 1.0 / (ROPE_THETA ** (jnp.arange(0, QK_ROPE, 2, dtype=F32) / QK_ROPE))
    freq = _rope_cols(jnp.concatenate([inv_freq, inv_freq]))[None, :]
    sign = _rope_cols(jnp.concatenate([-jnp.ones_like(inv_freq), jnp.ones_like(inv_freq)]))[None, :]
    ct, st = _rope_tables(positions.reshape(m, 1), freq, sign, tm_big)

    lane = jnp.arange(D_INNER)
    sel_row = jnp.arange(2 * LANES)[:, None]
    sel = ((sel_row % SSM_HEADS == (lane // SSM_HEADDIM)[None, :]) & (sel_row < 3 * SSM_HEADS)).astype(BF16)
    rows = jnp.arange(CHUNK)[:, None]
    eye_t = (rows == (lane % SSM_HEADDIM)[None, :]).astype(F32)
    tri_t = (rows >= (lane % SSM_HEADDIM)[None, :]).astype(F32)
    tri64 = (rows >= jnp.arange(CHUNK)[None, :]).astype(BF16)
    sh_rows = jnp.arange((CONV_WIDTH - 1) * CHUNK)[:, None]
    shift_m = (jnp.arange(2 * CHUNK)[None, :] == CHUNK + sh_rows % CHUNK - (sh_rows // CHUNK + 1)).astype(BF16)

    s0 = Q_LORA + KV_LORA + QK_ROPE
    s1 = s0 + D_INNER + (D_INNER + 2 * SSM_GROUPS * SSM_STATE)
    s2 = s1 + SSM_HEADS

    w_lat, w_kr, w_dt = lax.optimization_barrier(
        (w_in[..., :Q_LORA + KV_LORA], w_in[..., Q_LORA + KV_LORA:s0], w_in[..., s1:s2]))
    w_small = jnp.concatenate(
        [w_lat, _rope_cols(w_kr), w_dt, jnp.zeros((depth, d, LANES - SSM_HEADS), F32)], axis=-1).astype(BF16)
    big_n = (s1 - s0) + 2 * D_MODEL
    w_big = _shift_cast(w_in, s0 - QK_ROPE, QK_ROPE, s1 - s0, 512, big_n, 0, None, "relayout_zx")
    w_big = _shift_cast(w_in, s2 - LANES, LANES, 2 * D_MODEL, 512, big_n, s1 - s0, w_big, "relayout_gates")
    wq4 = w_uq.reshape(depth, Q_LORA, MLA_HEADS, QK_DIM)
    wq_rope = _rope_cols(wq4[..., QK_NOPE:])
    wq = jnp.concatenate([wq4[..., :QK_NOPE], wq_rope, jnp.roll(wq_rope, LANES // 2, axis=-1)], axis=-1)
    wq = wq.reshape(depth, Q_LORA, MLA_HEADS * Q_ACC_W).astype(BF16)
    wkv4 = w_ukv.reshape(depth, KV_LORA, MLA_HEADS, QK_NOPE + V_DIM)
    wk = wkv4[..., :QK_NOPE].reshape(depth, KV_LORA, -1).astype(BF16)
    wvt = jnp.swapaxes(wkv4[..., QK_NOPE:].reshape(depth, KV_LORA, -1), 1, 2).astype(BF16)
    w_o_mla_b = w_o_mla.astype(BF16)
    w_o_ssm_b = w_o_ssm.astype(BF16)
    w_out_b = w_out.astype(BF16)
    w_up_b = w_up.astype(BF16)
    w_down_b = w_down.astype(BF16)
    w_gate_b = w_ple_gate.astype(BF16)
    w_ple_b = w_ple.astype(BF16)

    rows3 = lambda v: v.reshape(depth, 1, -1).astype(F32)
    pad_h = jnp.zeros((depth, LANES - SSM_HEADS), F32)
    qw_rope = _rope_cols(q_norm_w[:, QK_NOPE:])
    qnw = rows3(jnp.concatenate([q_norm_w[:, :QK_NOPE], qw_rope, jnp.roll(qw_rope, LANES // 2, axis=-1)], axis=-1))
    kwn = rows3(k_norm_w[:, :QK_NOPE])
    kwr = rows3(_rope_cols(k_norm_w[:, QK_NOPE:]))
    dtb = rows3(jnp.concatenate([dt_bias, pad_h], axis=-1))
    alog = rows3(jnp.concatenate([a_log, pad_h], axis=-1))
    dsk = rows3(jnp.repeat(d_skip, SSM_HEADDIM, axis=-1))
    mix_nw, qa_nw, kva_nw = rows3(norm_mix_w), rows3(q_a_norm_w), rows3(kv_a_norm_w)
    ssm_nw, mlp_nw, ple_nw = rows3(ssm_norm_w), rows3(norm_mlp_w), rows3(ple_norm_w)
    conv_wh, conv_bh = 0.5 * conv_w, 0.5 * conv_b
    cwx, cwb = conv_wh[..., :D_INNER], conv_wh[..., D_INNER:]
    cbx, cbb = rows3(conv_bh[:, :D_INNER]), rows3(conv_bh[:, D_INNER:])
    p3 = p.reshape(depth, m, PLE_DIM)

    xf = x.reshape(m, d)
    for i in range(depth):
        small = _norm_matmul(xf, mix_nw, w_small, i, F32, tm_mid, SMALL_N, "in_proj_small")
        big = _norm_matmul(xf, mix_nw, w_big, i, BF16, tm_big, 1024, "in_proj_big")
        q = _q_proj(small, qa_nw, wq, qnw, ct, st, i, tm_small)
        k, vt = _kv_proj(small, kva_nw, wk, wvt, kwn, kwr, ct, st, i, tq)
        o = _attention(q, k, vt, batch, seq, tq, 2)
        y = _ssd(big, small, cwx, cwb, cbx, cbb, dtb, alog, dsk, ssm_nw, sel, eye_t, tri_t, tri64, shift_m, i,
                 batch, seq, 4)
        merged = _merge(o, y, w_o_mla_b, w_o_ssm_b, big, i, tm_big, 512)
        xf = _res_matmul(xf, merged, w_out_b, i, tm_mid)
        hidden = _norm_matmul(xf, mlp_nw, w_up_b, i, BF16, tm_big, 1024, "mlp_up", relu2=True)
        xf = _res_matmul_tiled(xf, hidden, w_down_b, i, tm_mid, 512, "mlp_down")
        xf = _ple(xf, ple_nw, w_gate_b, p3, w_ple_b, i, tm_mid)
    return xf.reshape(batch, seq, d)
```

```python
import functools

import jax
import jax.numpy as jnp
from jax import lax
from jax.experimental import pallas as pl
from jax.experimental.pallas import tpu as pltpu

F32 = jnp.float32
BF16 = jnp.bfloat16

D_MODEL = 2048
DEPTH = 4
CHUNK = 64
PLE_DIM = 256
EPS = 1e-6
MLA_HEADS = 16
Q_LORA = 512
KV_LORA = 512
QK_NOPE = 128
QK_ROPE = 64
V_DIM = 128
QK_DIM = QK_NOPE + QK_ROPE
ROPE_THETA = 10000.0
D_INNER = 2 * D_MODEL
SSM_HEADDIM = 64
SSM_HEADS = D_INNER // SSM_HEADDIM
SSM_GROUPS = 8
SSM_STATE = 128
GROUP_W = D_INNER // SSM_GROUPS
CONV_WIDTH = 4
D_FF = 4 * D_MODEL

LOG2_E = 1.4426950408889634
LANES = 128
ACC_ROWS = V_DIM + 16
Q_ACC_W = 3 * LANES
HEAD_W = 2 * LANES
SMALL_N = Q_LORA + KV_LORA + 2 * LANES
VMEM_LIMIT = 56 * 1024 * 1024


def _cparams(sem):
    return pltpu.CompilerParams(dimension_semantics=sem, vmem_limit_bytes=VMEM_LIMIT)


def _rms(xf, w):
    ms = jnp.mean(xf * xf, axis=-1, keepdims=True)
    return xf * lax.rsqrt(ms + EPS) * w


def _lspec(layer, block, imap=None):
    if imap is None:
        imap = lambda *g: (0,) * len(block)
    return pl.BlockSpec((None,) + tuple(block), lambda *g: (layer,) + tuple(imap(*g)))


def _dot(a, b):
    return jnp.dot(a, b, preferred_element_type=F32)


def _dot_nt(a, b):
    return lax.dot_general(a, b, (((1,), (1,)), ((), ())), preferred_element_type=F32)


def _rope_body(pos_ref, freq_ref, sign_ref, ct_ref, st_ref):
    ang = pos_ref[...].astype(F32) * freq_ref[...]
    ct_ref[...] = jnp.cos(ang)
    st_ref[...] = jnp.sin(ang) * sign_ref[...]


def _rope_tables(pos_col, freq, sign, tm):
    m = pos_col.shape[0]
    return pl.pallas_call(
        _rope_body,
        grid=(m // tm,),
        in_specs=[pl.BlockSpec((tm, 1), lambda i: (i, 0)),
                  pl.BlockSpec((1, LANES), lambda i: (0, 0)),
                  pl.BlockSpec((1, LANES), lambda i: (0, 0))],
        out_specs=[pl.BlockSpec((tm, LANES), lambda i: (i, 0)),
                   pl.BlockSpec((tm, LANES), lambda i: (i, 0))],
        out_shape=[jax.ShapeDtypeStruct((m, LANES), F32)] * 2,
        compiler_params=_cparams(("parallel",)),
        name="rope_tables",
    )(pos_col, freq, sign)


def _norm_mm_body(x_ref, nw_ref, w_ref, o_ref, h_ref, *, relu2):
    @pl.when(pl.program_id(1) == 0)
    def _():
        h_ref[...] = _rms(x_ref[...], nw_ref[...]).astype(BF16)

    y = _dot(h_ref[...], w_ref[...])
    if relu2:
        y = jnp.square(jnp.maximum(y, 0.0))
    o_ref[...] = y.astype(o_ref.dtype)


def _norm_matmul(x, nw, w, layer, out_dtype, tm, tn, name, relu2=False):
    m, k = x.shape
    n = w.shape[2]
    return pl.pallas_call(
        functools.partial(_norm_mm_body, relu2=relu2),
        grid=(m // tm, n // tn),
        in_specs=[pl.BlockSpec((tm, k), lambda i, j: (i, 0)),
                  _lspec(layer, (1, k)),
                  _lspec(layer, (k, tn), lambda i, j: (0, j))],
        out_specs=pl.BlockSpec((tm, tn), lambda i, j: (i, j)),
        out_shape=jax.ShapeDtypeStruct((m, n), out_dtype),
        scratch_shapes=[pltpu.VMEM((tm, k), BF16)],
        compiler_params=_cparams(("parallel", "arbitrary")),
        name=name,
    )(x, nw, w)


def _shift_cast_body(a_ref, b_ref, *rest, shift):
    o_ref = rest[-1]
    v = jnp.concatenate([a_ref[...], b_ref[...]], axis=1)
    o_ref[...] = v[:, shift:shift + o_ref.shape[1]].astype(BF16)


def _shift_cast(w, base, shift, width, tn, out_width, col0, into, name):
    depth, k, _ = w.shape
    assert base % tn == 0 and width % tn == 0 and col0 % tn == 0 and 0 < shift <= LANES
    in_specs = [pl.BlockSpec((None, k, tn), lambda l, j: (l, 0, base // tn + j)),
                pl.BlockSpec((None, k, LANES), lambda l, j: (l, 0, (base + tn) // LANES + j * (tn // LANES)))]
    args = [w, w]
    if into is not None:
        in_specs.append(pl.BlockSpec(memory_space=pl.ANY))
        args.append(into)
    return pl.pallas_call(
        functools.partial(_shift_cast_body, shift=shift),
        grid=(depth, width // tn),
        in_specs=in_specs,
        out_specs=pl.BlockSpec((None, k, tn), lambda l, j: (l, 0, col0 // tn + j)),
        out_shape=jax.ShapeDtypeStruct((depth, k, out_width), BF16),
        input_output_aliases={} if into is None else {2: 0},
        compiler_params=_cparams(("parallel", "parallel")),
        name=name,
    )(*args)


def _rope_apply(v, c, s):
    return v * c + pltpu.roll(v, 64, 1) * s


def _q_body(c_ref, nw_ref, wt_ref, qnw_ref, ct_ref, st_ref, o_ref):
    cn = _rms(c_ref[...], nw_ref[...]).astype(BF16)
    acc = _dot_nt(wt_ref[...], cn)
    wn = qnw_ref[:LANES, :]
    wrc = qnw_ref[LANES:2 * LANES, :] * ct_ref[...]
    wxs = qnw_ref[2 * LANES:, :] * st_ref[...]
    scale = QK_DIM ** -0.5 * LOG2_E
    for h in range(MLA_HEADS):
        qn = acc[Q_ACC_W * h:Q_ACC_W * h + LANES, :]
        qr = acc[Q_ACC_W * h + LANES:Q_ACC_W * h + 2 * LANES, :]
        qx = acc[Q_ACC_W * h + 2 * LANES:Q_ACC_W * (h + 1), :]
        ssq = jnp.sum(qn * qn + qr * qr, axis=0, keepdims=True)
        rs = lax.rsqrt(ssq / QK_DIM + EPS) * scale
        o_ref[0, HEAD_W * h:HEAD_W * h + LANES, :] = (qn * rs * wn).astype(BF16)
        o_ref[0, HEAD_W * h + LANES:HEAD_W * (h + 1), :] = ((qr * wrc + qx * wxs) * rs).astype(BF16)


def _q_proj(small, nw, wqt, qnw, ct_t, st_t, layer, tm):
    m = small.shape[0]
    n = MLA_HEADS * HEAD_W
    return pl.pallas_call(
        _q_body,
        grid=(m // tm,),
        in_specs=[pl.BlockSpec((tm, Q_LORA), lambda i: (i, 0)),
                  _lspec(layer, (1, Q_LORA)),
                  _lspec(layer, (MLA_HEADS * Q_ACC_W, Q_LORA)),
                  _lspec(layer, (Q_ACC_W, tm)),
                  pl.BlockSpec((LANES, tm), lambda i: (0, i)),
                  pl.BlockSpec((LANES, tm), lambda i: (0, i))],
        out_specs=pl.BlockSpec((1, n, tm), lambda i: (i, 0, 0)),
        out_shape=jax.ShapeDtypeStruct((m // tm, n, tm), BF16),
        compiler_params=_cparams(("parallel",)),
        name="q_proj",
    )(small, nw, wqt, qnw, ct_t, st_t)


def _kv_body(c_ref, kr_ref, nw_ref, wk_ref, wvt_ref, kwn_ref, kwr_ref, ct_ref, st_ref, k_ref, vt_ref):
    cn = _rms(c_ref[...], nw_ref[...]).astype(BF16)
    acc = _dot(cn, wk_ref[...])
    vt_ref[0] = _dot_nt(wvt_ref[...], cn).astype(BF16)
    c = ct_ref[...]
    s = st_ref[...]
    kr = kr_ref[...]
    wn = kwn_ref[...]
    wr = kwr_ref[...]
    ssr = jnp.sum(kr * kr, axis=-1, keepdims=True)
    kr_rot = _rope_apply(kr * wr, c, s)
    for h in range(MLA_HEADS):
        kn = acc[:, LANES * h:LANES * (h + 1)]
        ssq = jnp.sum(kn * kn, axis=-1, keepdims=True) + ssr
        rs = lax.rsqrt(ssq / QK_DIM + EPS)
        k_ref[:, HEAD_W * h:HEAD_W * h + LANES] = (kn * rs * wn).astype(BF16)
        k_ref[:, HEAD_W * h + LANES:HEAD_W * (h + 1)] = (kr_rot * rs).astype(BF16)


def _kv_proj(small, nw, wk, wvt, kwn, kwr, ct, st, layer, tm):
    m = small.shape[0]
    nk = MLA_HEADS * HEAD_W
    nv = MLA_HEADS * V_DIM
    kr_blk = (Q_LORA + KV_LORA) // LANES
    return pl.pallas_call(
        _kv_body,
        grid=(m // tm,),
        in_specs=[pl.BlockSpec((tm, KV_LORA), lambda i: (i, 1)),
                  pl.BlockSpec((tm, LANES), lambda i: (i, kr_blk)),
                  _lspec(layer, (1, KV_LORA)),
                  _lspec(layer, (KV_LORA, MLA_HEADS * QK_NOPE)),
                  _lspec(layer, (nv, KV_LORA)),
                  _lspec(layer, (1, LANES)),
                  _lspec(layer, (1, LANES)),
                  pl.BlockSpec((tm, LANES), lambda i: (i, 0)),
                  pl.BlockSpec((tm, LANES), lambda i: (i, 0))],
        out_specs=[pl.BlockSpec((tm, nk), lambda i: (i, 0)),
                   pl.BlockSpec((1, nv, tm), lambda i: (i, 0, 0))],
        out_shape=[jax.ShapeDtypeStruct((m, nk), BF16), jax.ShapeDtypeStruct((m // tm, nv, tm), BF16)],
        compiler_params=_cparams(("parallel",)),
        name="kv_proj",
    )(small, small, nw, wk, wvt, kwn, kwr, ct, st)


def _attn_body(q_ref, k_ref, vt_ref, o_ref, s0_scr, s1_scr, p0_scr, p1_scr, m_scr, a_scr, acc_scr, *, tq, nq, hb):
    s_scr = (s0_scr, s1_scr)
    p_scr = (p0_scr, p1_scr)
    tk = tq // 2
    key_chunk = lax.broadcasted_iota(jnp.int32, (tk, tq), 0) // CHUNK
    qry_chunk = lax.broadcasted_iota(jnp.int32, (tk, tq), 1) // CHUNK
    diag_masks = [key_chunk + h * (tk // CHUNK) <= qry_chunk for h in range(2)]
    ones_rows = jnp.ones((ACC_ROWS - V_DIM, tk), BF16)

    def scores(qs, kpair, h, mask=None):
        koff = pl.multiple_of(kpair * tq + h * tk, tk)
        for hh in range(hb):
            s = _dot(k_ref[pl.ds(koff, tk), HEAD_W * hh:HEAD_W * (hh + 1)], qs[hh])
            s_scr[h][hh] = s if mask is None else jnp.where(mask, s, -jnp.inf)

    def pv(kpair, h):
        return [_dot(jnp.concatenate([vt_ref[kpair, V_DIM * hh:V_DIM * (hh + 1), h * tk:(h + 1) * tk],
                                      ones_rows], axis=0), p_scr[h][hh])
                for hh in range(hb)]

    def softmax(h, pvs):
        for hh in range(hb):
            m = m_scr[hh]
            s = s_scr[h][hh]
            m_new = jnp.maximum(m, jnp.max(s, axis=0, keepdims=True))
            p_scr[h][hh] = jnp.exp2((s - m_new).astype(BF16))
            alpha = jnp.exp2(m - m_new)
            acc_scr[hh] = alpha * (acc_scr[hh] + a_scr[hh] * pvs[hh])
            a_scr[hh] = alpha
            m_scr[hh] = m_new

    def q_tiles(qi):
        return [q_ref[qi, HEAD_W * hh:HEAD_W * (hh + 1), :] for hh in range(hb)]

    def diag_scores(qi):
        qs = q_tiles(qi)
        for h in range(2):
            scores(qs, qi, h, diag_masks[h])

    diag_scores(0)

    def q_block(qi, c):
        qoff = pl.multiple_of(qi * tq, tq)
        qs = q_tiles(qi)
        m_scr[...] = jnp.full(m_scr.shape, -jnp.inf, F32)
        a_scr[...] = jnp.zeros(a_scr.shape, F32)
        acc_scr[...] = jnp.zeros(acc_scr.shape, F32)
        for h in range(2):
            p_scr[h][...] = jnp.zeros(p_scr[h].shape, BF16)

        def kpair_of(r):
            return jnp.where(r == 0, qi, r - 1)

        def body(r, c2):
            prev = kpair_of(jnp.maximum(r - 1, 0))
            for h in range(2):
                pvs = pv(prev, h)
                softmax(h, pvs)
                scores(qs, r, h)
            return c2

        lax.fori_loop(0, qi, body, 0)
        prev = kpair_of(jnp.maximum(qi - 1, 0))
        nxt = jnp.minimum(qi + 1, nq - 1)
        qn = q_tiles(nxt)
        for h in range(2):
            softmax(h, pv(prev, h))
            scores(qn, nxt, h, diag_masks[h])
        last = kpair_of(qi)
        pv0, pv1 = pv(last, 0), pv(last, 1)
        for hh in range(hb):
            acc = acc_scr[hh] + a_scr[hh] * pv0[hh] + pv1[hh]
            o_ref[pl.ds(qoff, tq), V_DIM * hh:V_DIM * (hh + 1)] = (
                acc[:V_DIM] / acc[V_DIM:V_DIM + 1]).T.astype(BF16)
        return c

    lax.fori_loop(0, nq, q_block, 0)


def _attention(qt, k, vt, batch, seq, tq, hb):
    m = k.shape[0]
    nq = seq // tq
    tk = tq // 2
    assert vt.shape[2] == tq
    return pl.pallas_call(
        functools.partial(_attn_body, tq=tq, nq=nq, hb=hb),
        grid=(batch, MLA_HEADS // hb),
        in_specs=[pl.BlockSpec((nq, hb * HEAD_W, tq), lambda b, h: (b, h, 0)),
                  pl.BlockSpec((seq, hb * HEAD_W), lambda b, h: (b, h)),
                  pl.BlockSpec((nq, hb * V_DIM, tq), lambda b, h: (b, h, 0))],
        out_specs=pl.BlockSpec((seq, hb * V_DIM), lambda b, h: (b, h)),
        out_shape=jax.ShapeDtypeStruct((m, MLA_HEADS * V_DIM), BF16),
        scratch_shapes=[pltpu.VMEM((hb, tk, tq), F32), pltpu.VMEM((hb, tk, tq), F32),
                        pltpu.VMEM((hb, tk, tq), BF16), pltpu.VMEM((hb, tk, tq), BF16),
                        pltpu.VMEM((hb, 1, tq), F32), pltpu.VMEM((hb, 1, tq), F32),
                        pltpu.VMEM((hb, ACC_ROWS, tq), F32)],
        compiler_params=_cparams(("parallel", "arbitrary")),
        name="attention",
    )(qt, k, vt)


def _silu_of_half(h):
    return h + h * jnp.tanh(h)


def _silu(y):
    return _silu_of_half(0.5 * y)


def _split3(v):
    hi = v.astype(BF16)
    r = v - hi.astype(F32)
    mid = r.astype(BF16)
    lo = (r - mid.astype(F32)).astype(BF16)
    return hi, mid, lo


def _ssd_body(z_ref, x_ref, bc_ref, dt_ref, cwx_ref, cwb_ref, cbx_ref, cbb_ref, dtb_ref, alog_ref,
              dsk_ref, nw_ref, sel_ref, eye_ref, tri_ref, tri64_ref, shift_ref, y_ref, state_ref, xhalo, bchalo,
              *, cpb):
    t = CHUNK
    gn = SSM_GROUPS * SSM_STATE

    @pl.when(pl.program_id(1) == 0)
    def _():
        state_ref[...] = jnp.zeros_like(state_ref)
        xhalo[...] = jnp.zeros_like(xhalo)
        bchalo[...] = jnp.zeros_like(bchalo)

    def conv_silu(raw_ref, halo, w_ref, b_ref):
        raws = [raw_ref[c * t:(c + 1) * t, :] for c in range(cpb)]
        prevs = [halo[...]] + raws[:-1]
        halo[...] = raws[-1]
        shifted = [_dot(shift_ref[...], jnp.concatenate([prevs[c], raws[c]], axis=0)) for c in range(cpb)]
        outs = []
        for c in range(cpb):
            y = None
            for tap in range(CONV_WIDTH):
                shift = CONV_WIDTH - 1 - tap
                xt = raws[c].astype(F32) if shift == 0 else shifted[c][(shift - 1) * t:shift * t, :]
                term = xt * w_ref[tap:tap + 1, :]
                y = term if y is None else y + term
            outs.append(_silu_of_half(y + b_ref[...]))
        return outs

    xs_all = conv_silu(x_ref, xhalo, cwx_ref, cbx_ref)
    bc_all = conv_silu(bc_ref, bchalo, cwb_ref, cbb_ref)

    head_lane = lax.broadcasted_iota(jnp.int32, (t, LANES), 1) < SSM_HEADS
    neg_a = -jnp.exp(alog_ref[...])
    tri64 = tri64_ref[...]
    stacks = []
    for c in range(cpb):
        dt = jnp.where(head_lane, jax.nn.softplus(dt_ref[c * t:(c + 1) * t, :] + dtb_ref[...]), 0.0)
        h3 = _split3(dt * neg_a)
        a_cum = (_dot(tri64, h3[0]) + _dot(tri64, h3[1]) + _dot(tri64, h3[2])) * LOG2_E
        stacks += [a_cum, dt]
    s3 = [part.astype(F32) for part in _split3(jnp.concatenate(stacks, axis=0))]
    packed = jnp.concatenate([(s3[0] + pltpu.roll(s3[1], SSM_HEADS, 1)).astype(BF16), s3[2].astype(BF16)], axis=1)
    ex_all = _dot(packed, sel_ref[...])
    left = lax.broadcasted_iota(jnp.int32, (t, LANES), 1) < SSM_HEADDIM
    zero_b = jnp.zeros((t, LANES), BF16)

    pre = []
    for c in range(cpb):
        ex = ex_all[2 * t * c:2 * t * (c + 1)]
        a_col = ex[:t]
        ea = jnp.exp2(a_col)
        a_row = jnp.sum(jnp.where(eye_ref[...] > 0, a_col, 0.0), axis=0, keepdims=True)
        decay = jnp.exp2(jnp.where(tri_ref[...] > 0, a_col - a_row, -jnp.inf))
        xdt = xs_all[c] * ex[t:2 * t]
        pre.append((decay, ea, xdt.astype(BF16), (xdt * jnp.exp2(a_col[t - 1:t, :] - a_col)).astype(BF16)))

    for c in range(cpb):
        decay, ea, xdt_b, xw_b = pre[c]
        xs, bc = xs_all[c], bc_all[c]
        chunk_decay = ea[t - 1:t, :]
        r0 = c * t
        for g in range(SSM_GROUPS):
            lo, hi = GROUP_W * g, GROUP_W * (g + 1)
            bg = bc[:, SSM_STATE * g:SSM_STATE * (g + 1)]
            cg_b = bc[:, gn + SSM_STATE * g:gn + SSM_STATE * (g + 1)].astype(BF16)
            bg_b = bg.astype(BF16)
            cb2 = _dot_nt(cg_b, jnp.concatenate([bg_b, bg_b], axis=0))
            yd = []
            for j in range(GROUP_W // LANES):
                l0 = lo + LANES * j
                mm = (cb2 * decay[:, l0:l0 + LANES]).astype(BF16)
                xp = xdt_b[:, l0:l0 + LANES]
                rhs = jnp.concatenate([jnp.where(left, xp, zero_b), jnp.where(left, zero_b, xp)], axis=0)
                yd.append(_dot(mm, rhs))
            y_diag = jnp.concatenate(yd, axis=1)
            st = state_ref[g]
            y_off = _dot(cg_b, st.astype(BF16)) * ea[:, lo:hi]
            inc = _dot(bg.T.astype(BF16), xw_b[:, lo:hi])
            state_ref[g] = st * chunk_decay[:, lo:hi] + inc
            yg = y_diag + y_off + dsk_ref[:, lo:hi] * xs[:, lo:hi]
            yg = yg * _silu(z_ref[r0:r0 + t, lo:hi].astype(F32))
            ms = jnp.mean(yg * yg, axis=-1, keepdims=True)
            y_ref[r0:r0 + t, lo:hi] = (yg * lax.rsqrt(ms + EPS) * nw_ref[:, lo:hi]).astype(BF16)


def _ssd(big, small, cwx, cwb, cbx, cbb, dtb, alog, dsk, nw, sel, eye_t, tri_t, tri64, shift_m, layer, batch, seq,
         cpb):
    m = big.shape[0]
    t = CHUNK
    tb = cpb * t
    nb = seq // tb
    bcw = 2 * SSM_GROUPS * SSM_STATE
    bc_blk = (2 * D_INNER) // bcw
    dt_blk = (SMALL_N - LANES) // LANES
    row = lambda b, c: b * nb + c
    const = lambda shape: pl.BlockSpec(shape, lambda b, c: (0,) * len(shape))
    return pl.pallas_call(
        functools.partial(_ssd_body, cpb=cpb),
        grid=(batch, nb),
        in_specs=[pl.BlockSpec((tb, D_INNER), lambda b, c: (row(b, c), 0)),
                  pl.BlockSpec((tb, D_INNER), lambda b, c: (row(b, c), 1)),
                  pl.BlockSpec((tb, bcw), lambda b, c: (row(b, c), bc_blk)),
                  pl.BlockSpec((tb, LANES), lambda b, c: (row(b, c), dt_blk)),
                  _lspec(layer, (CONV_WIDTH, D_INNER)), _lspec(layer, (CONV_WIDTH, bcw)),
                  _lspec(layer, (1, D_INNER)), _lspec(layer, (1, bcw)),
                  _lspec(layer, (1, LANES)), _lspec(layer, (1, LANES)),
                  _lspec(layer, (1, D_INNER)), _lspec(layer, (1, D_INNER)),
                  const((2 * LANES, D_INNER)), const((t, D_INNER)), const((t, D_INNER)), const((t, t)),
                  const(((CONV_WIDTH - 1) * t, 2 * t))],
        out_specs=pl.BlockSpec((tb, D_INNER), lambda b, c: (row(b, c), 0)),
        out_shape=jax.ShapeDtypeStruct((m, D_INNER), BF16),
        scratch_shapes=[pltpu.VMEM((SSM_GROUPS, SSM_STATE, GROUP_W), F32),
                        pltpu.VMEM((t, D_INNER), BF16),
                        pltpu.VMEM((t, bcw), BF16)],
        compiler_params=_cparams(("parallel", "arbitrary")),
        name="ssd",
    )(big, big, big, small, cwx, cwb, cbx, cbb, dtb, alog, dsk, nw, sel, eye_t, tri_t, tri64, shift_m)


def _merge_body(o_ref, y_ref, wa_ref, wb_ref, g0_ref, g1_ref, out_ref):
    ya = _dot(o_ref[...], wa_ref[...])
    yb = _dot(y_ref[...], wb_ref[...])
    g0 = jax.nn.sigmoid(g0_ref[...].astype(F32))
    g1 = jax.nn.sigmoid(g1_ref[...].astype(F32))
    out_ref[...] = (g0 * ya + g1 * yb).astype(out_ref.dtype)


def _merge(o, y, wa, wb, big, layer, tm, tn):
    m = o.shape[0]
    g0_blk = (2 * D_INNER + 2 * SSM_GROUPS * SSM_STATE) // tn
    g1_blk = g0_blk + D_MODEL // tn
    return pl.pallas_call(
        _merge_body,
        grid=(m // tm, D_MODEL // tn),
        in_specs=[pl.BlockSpec((tm, o.shape[1]), lambda i, j: (i, 0)),
                  pl.BlockSpec((tm, y.shape[1]), lambda i, j: (i, 0)),
                  _lspec(layer, (wa.shape[1], tn), lambda i, j: (0, j)),
                  _lspec(layer, (wb.shape[1], tn), lambda i, j: (0, j)),
                  pl.BlockSpec((tm, tn), lambda i, j: (i, g0_blk + j)),
                  pl.BlockSpec((tm, tn), lambda i, j: (i, g1_blk + j))],
        out_specs=pl.BlockSpec((tm, tn), lambda i, j: (i, j)),
        out_shape=jax.ShapeDtypeStruct((m, D_MODEL), BF16),
        compiler_params=_cparams(("parallel", "arbitrary")),
        name="merge",
    )(o, y, wa, wb, big, big)


def _res_mm_body(x_ref, a_ref, w_ref, o_ref):
    o_ref[...] = x_ref[...] + _dot(a_ref[...], w_ref[...])


def _res_matmul(x, a, w, layer, tm):
    m, n = x.shape
    k = a.shape[1]
    return pl.pallas_call(
        _res_mm_body,
        grid=(m // tm,),
        in_specs=[pl.BlockSpec((tm, n), lambda i: (i, 0)),
                  pl.BlockSpec((tm, k), lambda i: (i, 0)),
                  _lspec(layer, (k, n))],
        out_specs=pl.BlockSpec((tm, n), lambda i: (i, 0)),
        out_shape=jax.ShapeDtypeStruct((m, n), F32),
        compiler_params=_cparams(("parallel",)),
        name="out_proj",
    )(x, a, w)


def _res_matmul_tiled(x, a, w, layer, tm, tn, name):
    m, n = x.shape
    k = a.shape[1]
    return pl.pallas_call(
        _res_mm_body,
        grid=(m // tm, n // tn),
        in_specs=[pl.BlockSpec((tm, tn), lambda i, j: (i, j)),
                  pl.BlockSpec((tm, k), lambda i, j: (i, 0)),
                  _lspec(layer, (k, tn), lambda i, j: (0, j))],
        out_specs=pl.BlockSpec((tm, tn), lambda i, j: (i, j)),
        out_shape=jax.ShapeDtypeStruct((m, n), F32),
        compiler_params=_cparams(("parallel", "arbitrary")),
        name=name,
    )(x, a, w)


def _mlp_body(x_ref, nw_ref, wu_ref, wd_ref, o_ref, h_ref):
    @pl.when(pl.program_id(1) == 0)
    def _():
        h_ref[...] = _rms(x_ref[...], nw_ref[...]).astype(BF16)
        o_ref[...] = x_ref[...]

    u = _dot(h_ref[...], wu_ref[...])
    hid = jnp.square(jnp.maximum(u, 0.0)).astype(BF16)
    o_ref[...] += _dot(hid, wd_ref[...])


def _mlp(x, nw, wu, wd, layer, tm, tf):
    m, d = x.shape
    f = wu.shape[2]
    return pl.pallas_call(
        _mlp_body,
        grid=(m // tm, f // tf),
        in_specs=[pl.BlockSpec((tm, d), lambda i, j: (i, 0), pipeline_mode=pl.Buffered(1)),
                  _lspec(layer, (1, d)),
                  _lspec(layer, (d, tf), lambda i, j: (0, j)),
                  _lspec(layer, (tf, d), lambda i, j: (j, 0))],
        out_specs=pl.BlockSpec((tm, d), lambda i, j: (i, 0)),
        out_shape=jax.ShapeDtypeStruct((m, d), F32),
        scratch_shapes=[pltpu.VMEM((tm, d), BF16)],
        compiler_params=_cparams(("parallel", "arbitrary")),
        name="mlp",
    )(x, nw, wu, wd)


def _ple_body(x_ref, nw_ref, wg_ref, p_ref, wp_ref, o_ref):
    x = x_ref[...]
    h = _rms(x, nw_ref[...]).astype(BF16)
    gate = jax.nn.sigmoid(_dot(h, wg_ref[...]))
    pe = _dot(p_ref[...].astype(BF16), wp_ref[...])
    o_ref[...] = x + pe * gate


def _ple(x, nw, wg, p, wp, layer, tm):
    m, d = x.shape
    return pl.pallas_call(
        _ple_body,
        grid=(m // tm,),
        in_specs=[pl.BlockSpec((tm, d), lambda i: (i, 0)),
                  _lspec(layer, (1, d)),
                  _lspec(layer, (d, d)),
                  _lspec(layer, (tm, PLE_DIM), lambda i: (i, 0)),
                  _lspec(layer, (PLE_DIM, d))],
        out_specs=pl.BlockSpec((tm, d), lambda i: (i, 0)),
        out_shape=jax.ShapeDtypeStruct((m, d), F32),
        compiler_params=_cparams(("parallel",)),
        name="ple",
    )(x, nw, wg, p, wp)


def _rope_cols(w):
    half = QK_ROPE // 2
    z = jnp.zeros(w.shape[:-1] + (LANES // 2 - half,), w.dtype)
    return jnp.concatenate([w[..., :half], z, w[..., half:], z], axis=-1)


def kernel(x, p, positions, norm_mix_w, w_in, q_a_norm_w, w_uq, kv_a_norm_w, w_ukv, q_norm_w, k_norm_w,
           w_o_mla, conv_w, conv_b, dt_bias, a_log, d_skip, ssm_norm_w, w_o_ssm, w_out, norm_mlp_w, w_up,
           w_down, ple_norm_w, w_ple_gate, w_ple):
    batch, seq, d = x.shape
    m = batch * seq
    depth = w_in.shape[0]
    tm_big = min(1024, m)
    tm_mid = min(512, m)
    tm_small = min(256, m)
    tq = min(512, seq)

    inv_freq = 1.0 / (ROPE_THETA ** (jnp.arange(0, QK_ROPE, 2, dtype=F32) / QK_ROPE))
    freq = _rope_cols(jnp.concatenate([inv_freq, inv_freq]))[None, :]
    sign = _rope_cols(jnp.concatenate([-jnp.ones_like(inv_freq), jnp.ones_like(inv_freq)]))[None, :]
    ct, st = _rope_tables(positions.reshape(m, 1), freq, sign, tm_big)
    ct_t, st_t = ct.T, st.T

    lane = jnp.arange(D_INNER)
    sel_row = jnp.arange(2 * LANES)[:, None]
    sel = ((sel_row % SSM_HEADS == (lane // SSM_HEADDIM)[None, :]) & (sel_row < 3 * SSM_HEADS)).astype(BF16)
    rows = jnp.arange(CHUNK)[:, None]
    eye_t = (rows == (lane % SSM_HEADDIM)[None, :]).astype(F32)
    tri_t = (rows >= (lane % SSM_HEADDIM)[None, :]).astype(F32)
    tri64 = (rows >= jnp.arange(CHUNK)[None, :]).astype(BF16)
    sh_rows = jnp.arange((CONV_WIDTH - 1) * CHUNK)[:, None]
    shift_m = (jnp.arange(2 * CHUNK)[None, :] == CHUNK + sh_rows % CHUNK - (sh_rows // CHUNK + 1)).astype(BF16)

    s0 = Q_LORA + KV_LORA + QK_ROPE
    s1 = s0 + D_INNER + (D_INNER + 2 * SSM_GROUPS * SSM_STATE)
    s2 = s1 + SSM_HEADS

    w_lat, w_kr, w_dt = lax.optimization_barrier(
        (w_in[..., :Q_LORA + KV_LORA], w_in[..., Q_LORA + KV_LORA:s0], w_in[..., s1:s2]))
    w_small = jnp.concatenate(
        [w_lat, _rope_cols(w_kr), w_dt, jnp.zeros((depth, d, LANES - SSM_HEADS), F32)], axis=-1).astype(BF16)
    big_n = (s1 - s0) + 2 * D_MODEL
    w_big = _shift_cast(w_in, s0 - QK_ROPE, QK_ROPE, s1 - s0, 512, big_n, 0, None, "relayout_zx")
    w_big = _shift_cast(w_in, s2 - LANES, LANES, 2 * D_MODEL, 512, big_n, s1 - s0, w_big, "relayout_gates")
    wq4 = w_uq.reshape(depth, Q_LORA, MLA_HEADS, QK_DIM)
    wq_rope = _rope_cols(wq4[..., QK_NOPE:])
    wq = jnp.concatenate([wq4[..., :QK_NOPE], wq_rope, jnp.roll(wq_rope, LANES // 2, axis=-1)], axis=-1)
    wqt = jnp.swapaxes(wq.reshape(depth, Q_LORA, MLA_HEADS * Q_ACC_W), 1, 2).astype(BF16)
    wkv4 = w_ukv.reshape(depth, KV_LORA, MLA_HEADS, QK_NOPE + V_DIM)
    wk = wkv4[..., :QK_NOPE].reshape(depth, KV_LORA, -1).astype(BF16)
    wvt = jnp.swapaxes(wkv4[..., QK_NOPE:].reshape(depth, KV_LORA, -1), 1, 2).astype(BF16)
    w_o_mla_b = w_o_mla.astype(BF16)
    w_o_ssm_b = w_o_ssm.astype(BF16)
    w_out_b = w_out.astype(BF16)
    w_up_b = w_up.astype(BF16)
    w_down_b = w_down.astype(BF16)
    w_gate_b = w_ple_gate.astype(BF16)
    w_ple_b = w_ple.astype(BF16)

    rows3 = lambda v: v.reshape(depth, 1, -1).astype(F32)
    pad_h = jnp.zeros((depth, LANES - SSM_HEADS), F32)
    qw_rope = _rope_cols(q_norm_w[:, QK_NOPE:])
    qnw = jnp.concatenate([q_norm_w[:, :QK_NOPE], qw_rope, jnp.roll(qw_rope, LANES // 2, axis=-1)], axis=-1)
    qnw = jnp.broadcast_to(qnw[:, :, None], (depth, Q_ACC_W, tq)).astype(F32)
    kwn = rows3(k_norm_w[:, :QK_NOPE])
    kwr = rows3(_rope_cols(k_norm_w[:, QK_NOPE:]))
    dtb = rows3(jnp.concatenate([dt_bias, pad_h], axis=-1))
    alog = rows3(jnp.concatenate([a_log, pad_h], axis=-1))
    dsk = rows3(jnp.repeat(d_skip, SSM_HEADDIM, axis=-1))
    mix_nw, qa_nw, kva_nw = rows3(norm_mix_w), rows3(q_a_norm_w), rows3(kv_a_norm_w)
    ssm_nw, mlp_nw, ple_nw = rows3(ssm_norm_w), rows3(norm_mlp_w), rows3(ple_norm_w)
    conv_wh, conv_bh = 0.5 * conv_w, 0.5 * conv_b
    cwx, cwb = conv_wh[..., :D_INNER], conv_wh[..., D_INNER:]
    cbx, cbb = rows3(conv_bh[:, :D_INNER]), rows3(conv_bh[:, D_INNER:])
    p3 = p.reshape(depth, m, PLE_DIM)

    xf = x.reshape(m, d)
    for i in range(depth):
        small = _norm_matmul(xf, mix_nw, w_small, i, F32, tm_mid, SMALL_N, "in_proj_small")
        big = _norm_matmul(xf, mix_nw, w_big, i, BF16, tm_big, 1024, "in_proj_big")
        qt = _q_proj(small, qa_nw, wqt, qnw, ct_t, st_t, i, tq)
        k, vt = _kv_proj(small, kva_nw, wk, wvt, kwn, kwr, ct, st, i, tq)
        o = _attention(qt, k, vt, batch, seq, tq, 2)
        y = _ssd(big, small, cwx, cwb, cbx, cbb, dtb, alog, dsk, ssm_nw, sel, eye_t, tri_t, tri64, shift_m, i,
                 batch, seq, 4)
        merged = _merge(o, y, w_o_mla_b, w_o_ssm_b, big, i, tm_big, 512)
        xf = _res_matmul(xf, merged, w_out_b, i, tm_mid)
        hidden = _norm_matmul(xf, mlp_nw, w_up_b, i, BF16, tm_big, 1024, "mlp_up", relu2=True)
        xf = _res_matmul_tiled(xf, hidden, w_down_b, i, tm_mid, 512, "mlp_down")
        xf = _ple(xf, ple_nw, w_gate_b, p3, w_ple_b, i, tm_mid)
    return xf.reshape(batch, seq, d)
```

```python
import functools

import jax
import jax.numpy as jnp
from jax import lax
from jax.experimental import pallas as pl
from jax.experimental.pallas import tpu as pltpu

F32 = jnp.float32
BF16 = jnp.bfloat16

D_MODEL = 2048
DEPTH = 4
CHUNK = 64
PLE_DIM = 256
EPS = 1e-6
MLA_HEADS = 16
Q_LORA = 512
KV_LORA = 512
QK_NOPE = 128
QK_ROPE = 64
V_DIM = 128
QK_DIM = QK_NOPE + QK_ROPE
ROPE_THETA = 10000.0
D_INNER = 2 * D_MODEL
SSM_HEADDIM = 64
SSM_HEADS = D_INNER // SSM_HEADDIM
SSM_GROUPS = 8
SSM_STATE = 128
GROUP_W = D_INNER // SSM_GROUPS
CONV_WIDTH = 4
D_FF = 4 * D_MODEL

LOG2_E = 1.4426950408889634
LANES = 128
ACC_ROWS = V_DIM + 16
Q_ACC_W = 3 * LANES
HEAD_W = 2 * LANES
SMALL_N = Q_LORA + KV_LORA + 2 * LANES
VMEM_LIMIT = 56 * 1024 * 1024


def _cparams(sem):
    return pltpu.CompilerParams(dimension_semantics=sem, vmem_limit_bytes=VMEM_LIMIT)


def _rms(xf, w):
    ms = jnp.mean(xf * xf, axis=-1, keepdims=True)
    return xf * lax.rsqrt(ms + EPS) * w


def _lspec(layer, block, imap=None, single=False):
    if imap is None:
        imap = lambda *g: (0,) * len(block)
    return pl.BlockSpec((None,) + tuple(block), lambda *g: (layer,) + tuple(imap(*g)),
                        pipeline_mode=pl.Buffered(1) if single else None)


def _dot(a, b):
    return jnp.dot(a, b, preferred_element_type=F32)


def _dot_nt(a, b):
    return lax.dot_general(a, b, (((1,), (1,)), ((), ())), preferred_element_type=F32)


def _rope_body(pos_ref, freq_ref, sign_ref, ct_ref, st_ref):
    ang = pos_ref[...].astype(F32) * freq_ref[...]
    ct_ref[...] = jnp.cos(ang)
    st_ref[...] = jnp.sin(ang) * sign_ref[...]


def _rope_tables(pos_col, freq, sign, tm):
    m = pos_col.shape[0]
    return pl.pallas_call(
        _rope_body,
        grid=(m // tm,),
        in_specs=[pl.BlockSpec((tm, 1), lambda i: (i, 0)),
                  pl.BlockSpec((1, LANES), lambda i: (0, 0)),
                  pl.BlockSpec((1, LANES), lambda i: (0, 0))],
        out_specs=[pl.BlockSpec((tm, LANES), lambda i: (i, 0)),
                   pl.BlockSpec((tm, LANES), lambda i: (i, 0))],
        out_shape=[jax.ShapeDtypeStruct((m, LANES), F32)] * 2,
        compiler_params=_cparams(("parallel",)),
        name="rope_tables",
    )(pos_col, freq, sign)


def _norm_mm_body(x_ref, nw_ref, w_ref, o_ref, h_ref, *, relu2):
    @pl.when(pl.program_id(1) == 0)
    def _():
        h_ref[...] = _rms(x_ref[...], nw_ref[...]).astype(BF16)

    y = _dot(h_ref[...], w_ref[...].astype(BF16))
    if relu2:
        y = jnp.square(jnp.maximum(y, 0.0))
    o_ref[...] = y.astype(o_ref.dtype)


def _norm_matmul(x, nw, w, layer, out_dtype, tm, tn, name, relu2=False, single_x=False):
    m, k = x.shape
    n = w.shape[2]
    return pl.pallas_call(
        functools.partial(_norm_mm_body, relu2=relu2),
        grid=(m // tm, n // tn),
        in_specs=[pl.BlockSpec((tm, k), lambda i, j: (i, 0), pipeline_mode=pl.Buffered(1) if single_x else None),
                  _lspec(layer, (1, k)),
                  _lspec(layer, (k, tn), lambda i, j: (0, j))],
        out_specs=pl.BlockSpec((tm, tn), lambda i, j: (i, j)),
        out_shape=jax.ShapeDtypeStruct((m, n), out_dtype),
        scratch_shapes=[pltpu.VMEM((tm, k), BF16)],
        compiler_params=_cparams(("parallel", "arbitrary")),
        name=name,
    )(x, nw, w)


def _shift_cast_body(a_ref, b_ref, *rest, shift):
    o_ref = rest[-1]
    v = jnp.concatenate([a_ref[...], b_ref[...]], axis=1)
    o_ref[...] = v[:, shift:shift + o_ref.shape[1]].astype(BF16)


def _shift_cast(w, base, shift, width, tn, out_width, col0, into, name):
    depth, k, _ = w.shape
    assert base % tn == 0 and width % tn == 0 and col0 % tn == 0 and 0 < shift <= LANES
    in_specs = [pl.BlockSpec((None, k, tn), lambda l, j: (l, 0, base // tn + j)),
                pl.BlockSpec((None, k, LANES), lambda l, j: (l, 0, (base + tn) // LANES + j * (tn // LANES)))]
    args = [w, w]
    if into is not None:
        in_specs.append(pl.BlockSpec(memory_space=pl.ANY))
        args.append(into)
    return pl.pallas_call(
        functools.partial(_shift_cast_body, shift=shift),
        grid=(depth, width // tn),
        in_specs=in_specs,
        out_specs=pl.BlockSpec((None, k, tn), lambda l, j: (l, 0, col0 // tn + j)),
        out_shape=jax.ShapeDtypeStruct((depth, k, out_width), BF16),
        input_output_aliases={} if into is None else {2: 0},
        compiler_params=_cparams(("parallel", "parallel")),
        name=name,
    )(*args)


def _rope_apply(v, c, s):
    return v * c + pltpu.roll(v, 64, 1) * s


def _q_body(c_ref, nw_ref, wt_ref, qnw_ref, ct_ref, st_ref, o_ref):
    cn = _rms(c_ref[...], nw_ref[...]).astype(BF16)
    acc = _dot_nt(wt_ref[...], cn)
    wn = qnw_ref[:LANES, :]
    wrc = qnw_ref[LANES:2 * LANES, :] * ct_ref[...]
    wxs = qnw_ref[2 * LANES:, :] * st_ref[...]
    scale = QK_DIM ** -0.5 * LOG2_E
    for h in range(MLA_HEADS):
        qn = acc[Q_ACC_W * h:Q_ACC_W * h + LANES, :]
        qr = acc[Q_ACC_W * h + LANES:Q_ACC_W * h + 2 * LANES, :]
        qx = acc[Q_ACC_W * h + 2 * LANES:Q_ACC_W * (h + 1), :]
        ssq = jnp.sum(qn * qn + qr * qr, axis=0, keepdims=True)
        rs = lax.rsqrt(ssq / QK_DIM + EPS) * scale
        o_ref[0, HEAD_W * h:HEAD_W * h + LANES, :] = (qn * rs * wn).astype(BF16)
        o_ref[0, HEAD_W * h + LANES:HEAD_W * (h + 1), :] = ((qr * wrc + qx * wxs) * rs).astype(BF16)


def _q_proj(small, nw, wqt, qnw, ct_t, st_t, layer, tm):
    m = small.shape[0]
    n = MLA_HEADS * HEAD_W
    return pl.pallas_call(
        _q_body,
        grid=(m // tm,),
        in_specs=[pl.BlockSpec((tm, Q_LORA), lambda i: (i, 0)),
                  _lspec(layer, (1, Q_LORA)),
                  _lspec(layer, (MLA_HEADS * Q_ACC_W, Q_LORA)),
                  _lspec(layer, (Q_ACC_W, tm)),
                  pl.BlockSpec((LANES, tm), lambda i: (0, i)),
                  pl.BlockSpec((LANES, tm), lambda i: (0, i))],
        out_specs=pl.BlockSpec((1, n, tm), lambda i: (i, 0, 0)),
        out_shape=jax.ShapeDtypeStruct((m // tm, n, tm), BF16),
        compiler_params=_cparams(("parallel",)),
        name="q_proj",
    )(small, nw, wqt, qnw, ct_t, st_t)


def _kv_body(c_ref, kr_ref, nw_ref, wk_ref, wvt_ref, kwn_ref, kwr_ref, ct_ref, st_ref, k_ref, vt_ref):
    cn = _rms(c_ref[...], nw_ref[...]).astype(BF16)
    acc = _dot(cn, wk_ref[...])
    vt_ref[0] = _dot_nt(wvt_ref[...], cn).astype(BF16)
    c = ct_ref[...]
    s = st_ref[...]
    kr = kr_ref[...]
    wn = kwn_ref[...]
    wr = kwr_ref[...]
    ssr = jnp.sum(kr * kr, axis=-1, keepdims=True)
    kr_rot = _rope_apply(kr * wr, c, s)
    for h in range(MLA_HEADS):
        kn = acc[:, LANES * h:LANES * (h + 1)]
        ssq = jnp.sum(kn * kn, axis=-1, keepdims=True) + ssr
        rs = lax.rsqrt(ssq / QK_DIM + EPS)
        k_ref[:, HEAD_W * h:HEAD_W * h + LANES] = (kn * rs * wn).astype(BF16)
        k_ref[:, HEAD_W * h + LANES:HEAD_W * (h + 1)] = (kr_rot * rs).astype(BF16)


def _kv_proj(small, nw, wk, wvt, kwn, kwr, ct, st, layer, tm):
    m = small.shape[0]
    nk = MLA_HEADS * HEAD_W
    nv = MLA_HEADS * V_DIM
    kr_blk = (Q_LORA + KV_LORA) // LANES
    return pl.pallas_call(
        _kv_body,
        grid=(m // tm,),
        in_specs=[pl.BlockSpec((tm, KV_LORA), lambda i: (i, 1)),
                  pl.BlockSpec((tm, LANES), lambda i: (i, kr_blk)),
                  _lspec(layer, (1, KV_LORA)),
                  _lspec(layer, (KV_LORA, MLA_HEADS * QK_NOPE)),
                  _lspec(layer, (nv, KV_LORA)),
                  _lspec(layer, (1, LANES)),
                  _lspec(layer, (1, LANES)),
                  pl.BlockSpec((tm, LANES), lambda i: (i, 0)),
                  pl.BlockSpec((tm, LANES), lambda i: (i, 0))],
        out_specs=[pl.BlockSpec((tm, nk), lambda i: (i, 0)),
                   pl.BlockSpec((1, nv, tm), lambda i: (i, 0, 0))],
        out_shape=[jax.ShapeDtypeStruct((m, nk), BF16), jax.ShapeDtypeStruct((m // tm, nv, tm), BF16)],
        compiler_params=_cparams(("parallel",)),
        name="kv_proj",
    )(small, small, nw, wk, wvt, kwn, kwr, ct, st)


def _attn_body(q_ref, k_ref, vt_ref, o_ref, s0_scr, s1_scr, p0_scr, p1_scr, m_scr, a_scr, acc_scr, *, tq, nq, hb):
    s_scr = (s0_scr, s1_scr)
    p_scr = (p0_scr, p1_scr)
    tk = tq // 2
    key_chunk = lax.broadcasted_iota(jnp.int32, (tk, tq), 0) // CHUNK
    qry_chunk = lax.broadcasted_iota(jnp.int32, (tk, tq), 1) // CHUNK
    diag_masks = [key_chunk + h * (tk // CHUNK) <= qry_chunk for h in range(2)]
    ones_rows = jnp.ones((ACC_ROWS - V_DIM, tk), BF16)

    def scores(qs, kpair, h, mask=None):
        koff = pl.multiple_of(kpair * tq + h * tk, tk)
        for hh in range(hb):
            s = _dot(k_ref[pl.ds(koff, tk), HEAD_W * hh:HEAD_W * (hh + 1)], qs[hh])
            s_scr[h][hh] = s if mask is None else jnp.where(mask, s, -jnp.inf)

    def pv(kpair, h):
        return [_dot(jnp.concatenate([vt_ref[kpair, V_DIM * hh:V_DIM * (hh + 1), h * tk:(h + 1) * tk],
                                      ones_rows], axis=0), p_scr[h][hh])
                for hh in range(hb)]

    def softmax(h, pvs):
        for hh in range(hb):
            m = m_scr[hh]
            s = s_scr[h][hh]
            m_new = jnp.maximum(m, jnp.max(s, axis=0, keepdims=True))
            p_scr[h][hh] = jnp.exp2((s - m_new).astype(BF16))
            alpha = jnp.exp2(m - m_new)
            acc_scr[hh] = alpha * (acc_scr[hh] + a_scr[hh] * pvs[hh])
            a_scr[hh] = alpha
            m_scr[hh] = m_new

    def q_tiles(qi):
        return [q_ref[qi, HEAD_W * hh:HEAD_W * (hh + 1), :] for hh in range(hb)]

    def diag_scores(qi):
        qs = q_tiles(qi)
        for h in range(2):
            scores(qs, qi, h, diag_masks[h])

    diag_scores(0)

    def q_block(qi, c):
        qoff = pl.multiple_of(qi * tq, tq)
        qs = q_tiles(qi)
        m_scr[...] = jnp.full(m_scr.shape, -jnp.inf, F32)
        a_scr[...] = jnp.zeros(a_scr.shape, F32)
        acc_scr[...] = jnp.zeros(acc_scr.shape, F32)
        for h in range(2):
            p_scr[h][...] = jnp.zeros(p_scr[h].shape, BF16)

        def kpair_of(r):
            return jnp.where(r == 0, qi, r - 1)

        def body(r, c2):
            prev = kpair_of(jnp.maximum(r - 1, 0))
            for h in range(2):
                pvs = pv(prev, h)
                softmax(h, pvs)
                scores(qs, r, h)
            return c2

        lax.fori_loop(0, qi, body, 0)
        prev = kpair_of(jnp.maximum(qi - 1, 0))
        nxt = jnp.minimum(qi + 1, nq - 1)
        qn = q_tiles(nxt)
        for h in range(2):
            softmax(h, pv(prev, h))
            scores(qn, nxt, h, diag_masks[h])
        last = kpair_of(qi)
        pv0, pv1 = pv(last, 0), pv(last, 1)
        for hh in range(hb):
            acc = acc_scr[hh] + a_scr[hh] * pv0[hh] + pv1[hh]
            o_ref[pl.ds(qoff, tq), V_DIM * hh:V_DIM * (hh + 1)] = (
                acc[:V_DIM] / acc[V_DIM:V_DIM + 1]).T.astype(BF16)
        return c

    lax.fori_loop(0, nq, q_block, 0)


def _attention(qt, k, vt, batch, seq, tq, hb):
    m = k.shape[0]
    nq = seq // tq
    tk = tq // 2
    assert vt.shape[2] == tq
    return pl.pallas_call(
        functools.partial(_attn_body, tq=tq, nq=nq, hb=hb),
        grid=(batch, MLA_HEADS // hb),
        in_specs=[pl.BlockSpec((nq, hb * HEAD_W, tq), lambda b, h: (b, h, 0)),
                  pl.BlockSpec((seq, hb * HEAD_W), lambda b, h: (b, h)),
                  pl.BlockSpec((nq, hb * V_DIM, tq), lambda b, h: (b, h, 0))],
        out_specs=pl.BlockSpec((seq, hb * V_DIM), lambda b, h: (b, h)),
        out_shape=jax.ShapeDtypeStruct((m, MLA_HEADS * V_DIM), BF16),
        scratch_shapes=[pltpu.VMEM((hb, tk, tq), F32), pltpu.VMEM((hb, tk, tq), F32),
                        pltpu.VMEM((hb, tk, tq), BF16), pltpu.VMEM((hb, tk, tq), BF16),
                        pltpu.VMEM((hb, 1, tq), F32), pltpu.VMEM((hb, 1, tq), F32),
                        pltpu.VMEM((hb, ACC_ROWS, tq), F32)],
        compiler_params=_cparams(("parallel", "arbitrary")),
        name="attention",
    )(qt, k, vt)


def _silu_of_half(h):
    return h + h * jnp.tanh(h)


def _silu(y):
    return _silu_of_half(0.5 * y)


def _split3(v):
    hi = v.astype(BF16)
    r = v - hi.astype(F32)
    mid = r.astype(BF16)
    lo = (r - mid.astype(F32)).astype(BF16)
    return hi, mid, lo


def _ssd_body(z_ref, x_ref, bc_ref, dt_ref, cwx_ref, cwb_ref, cbx_ref, cbb_ref, dtb_ref, alog_ref,
              dsk_ref, nw_ref, sel_ref, eye_ref, tri_ref, tri64_ref, shift_ref, y_ref, state_ref, xhalo, bchalo,
              *, cpb):
    t = CHUNK
    gn = SSM_GROUPS * SSM_STATE

    @pl.when(pl.program_id(1) == 0)
    def _():
        state_ref[...] = jnp.zeros_like(state_ref)
        xhalo[...] = jnp.zeros_like(xhalo)
        bchalo[...] = jnp.zeros_like(bchalo)

    def conv_silu(raw_ref, halo, w_ref, b_ref):
        raws = [raw_ref[c * t:(c + 1) * t, :] for c in range(cpb)]
        prevs = [halo[...]] + raws[:-1]
        halo[...] = raws[-1]
        shifted = [_dot(shift_ref[...], jnp.concatenate([prevs[c], raws[c]], axis=0)) for c in range(cpb)]
        outs = []
        for c in range(cpb):
            y = None
            for tap in range(CONV_WIDTH):
                shift = CONV_WIDTH - 1 - tap
                xt = raws[c].astype(F32) if shift == 0 else shifted[c][(shift - 1) * t:shift * t, :]
                term = xt * w_ref[tap:tap + 1, :]
                y = term if y is None else y + term
            outs.append(_silu_of_half(y + b_ref[...]))
        return outs

    xs_all = conv_silu(x_ref, xhalo, cwx_ref, cbx_ref)
    bc_all = conv_silu(bc_ref, bchalo, cwb_ref, cbb_ref)

    head_lane = lax.broadcasted_iota(jnp.int32, (t, LANES), 1) < SSM_HEADS
    neg_a = -jnp.exp(alog_ref[...])
    tri64 = tri64_ref[...]
    stacks = []
    for c in range(cpb):
        dt = jnp.where(head_lane, jax.nn.softplus(dt_ref[c * t:(c + 1) * t, :] + dtb_ref[...]), 0.0)
        h3 = _split3(dt * neg_a)
        a_cum = (_dot(tri64, h3[0]) + _dot(tri64, h3[1]) + _dot(tri64, h3[2])) * LOG2_E
        stacks += [a_cum, dt]
    s3 = [part.astype(F32) for part in _split3(jnp.concatenate(stacks, axis=0))]
    packed = jnp.concatenate([(s3[0] + pltpu.roll(s3[1], SSM_HEADS, 1)).astype(BF16), s3[2].astype(BF16)], axis=1)
    ex_all = _dot(packed, sel_ref[...])
    left = lax.broadcasted_iota(jnp.int32, (t, LANES), 1) < SSM_HEADDIM
    zero_b = jnp.zeros((t, LANES), BF16)

    pre = []
    for c in range(cpb):
        ex = ex_all[2 * t * c:2 * t * (c + 1)]
        a_col = ex[:t]
        ea = jnp.exp2(a_col)
        a_row = jnp.sum(jnp.where(eye_ref[...] > 0, a_col, 0.0), axis=0, keepdims=True)
        decay = jnp.exp2(jnp.where(tri_ref[...] > 0, a_col - a_row, -jnp.inf))
        xdt = xs_all[c] * ex[t:2 * t]
        pre.append((decay, ea, xdt.astype(BF16), (xdt * jnp.exp2(a_col[t - 1:t, :] - a_col)).astype(BF16)))

    for c in range(cpb):
        decay, ea, xdt_b, xw_b = pre[c]
        xs, bc = xs_all[c], bc_all[c]
        chunk_decay = ea[t - 1:t, :]
        r0 = c * t
        for g in range(SSM_GROUPS):
            lo, hi = GROUP_W * g, GROUP_W * (g + 1)
            bg = bc[:, SSM_STATE * g:SSM_STATE * (g + 1)]
            cg_b = bc[:, gn + SSM_STATE * g:gn + SSM_STATE * (g + 1)].astype(BF16)
            bg_b = bg.astype(BF16)
            cb2 = _dot_nt(cg_b, jnp.concatenate([bg_b, bg_b], axis=0))
            yd = []
            for j in range(GROUP_W // LANES):
                l0 = lo + LANES * j
                mm = (cb2 * decay[:, l0:l0 + LANES]).astype(BF16)
                xp = xdt_b[:, l0:l0 + LANES]
                rhs = jnp.concatenate([jnp.where(left, xp, zero_b), jnp.where(left, zero_b, xp)], axis=0)
                yd.append(_dot(mm, rhs))
            y_diag = jnp.concatenate(yd, axis=1)
            st = state_ref[g]
            y_off = _dot(cg_b, st.astype(BF16)) * ea[:, lo:hi]
            inc = _dot(bg.T.astype(BF16), xw_b[:, lo:hi])
            state_ref[g] = st * chunk_decay[:, lo:hi] + inc
            yg = y_diag + y_off + dsk_ref[:, lo:hi] * xs[:, lo:hi]
            yg = yg * _silu(z_ref[r0:r0 + t, lo:hi].astype(F32))
            ms = jnp.mean(yg * yg, axis=-1, keepdims=True)
            y_ref[r0:r0 + t, lo:hi] = (yg * lax.rsqrt(ms + EPS) * nw_ref[:, lo:hi]).astype(BF16)


def _ssd(big, small, cwx, cwb, cbx, cbb, dtb, alog, dsk, nw, sel, eye_t, tri_t, tri64, shift_m, layer, batch, seq,
         cpb):
    m = big.shape[0]
    t = CHUNK
    tb = cpb * t
    nb = seq // tb
    bcw = 2 * SSM_GROUPS * SSM_STATE
    bc_blk = (2 * D_INNER) // bcw
    dt_blk = (SMALL_N - LANES) // LANES
    row = lambda b, c: b * nb + c
    const = lambda shape: pl.BlockSpec(shape, lambda b, c: (0,) * len(shape))
    return pl.pallas_call(
        functools.partial(_ssd_body, cpb=cpb),
        grid=(batch, nb),
        in_specs=[pl.BlockSpec((tb, D_INNER), lambda b, c: (row(b, c), 0)),
                  pl.BlockSpec((tb, D_INNER), lambda b, c: (row(b, c), 1)),
                  pl.BlockSpec((tb, bcw), lambda b, c: (row(b, c), bc_blk)),
                  pl.BlockSpec((tb, LANES), lambda b, c: (row(b, c), dt_blk)),
                  _lspec(layer, (CONV_WIDTH, D_INNER)), _lspec(layer, (CONV_WIDTH, bcw)),
                  _lspec(layer, (1, D_INNER)), _lspec(layer, (1, bcw)),
                  _lspec(layer, (1, LANES)), _lspec(layer, (1, LANES)),
                  _lspec(layer, (1, D_INNER)), _lspec(layer, (1, D_INNER)),
                  const((2 * LANES, D_INNER)), const((t, D_INNER)), const((t, D_INNER)), const((t, t)),
                  const(((CONV_WIDTH - 1) * t, 2 * t))],
        out_specs=pl.BlockSpec((tb, D_INNER), lambda b, c: (row(b, c), 0)),
        out_shape=jax.ShapeDtypeStruct((m, D_INNER), BF16),
        scratch_shapes=[pltpu.VMEM((SSM_GROUPS, SSM_STATE, GROUP_W), F32),
                        pltpu.VMEM((t, D_INNER), BF16),
                        pltpu.VMEM((t, bcw), BF16)],
        compiler_params=_cparams(("parallel", "arbitrary")),
        name="ssd",
    )(big, big, big, small, cwx, cwb, cbx, cbb, dtb, alog, dsk, nw, sel, eye_t, tri_t, tri64, shift_m)


def _merge_body(o_ref, y_ref, wa_ref, wb_ref, g0_ref, g1_ref, out_ref):
    ya = _dot(o_ref[...], wa_ref[...])
    yb = _dot(y_ref[...], wb_ref[...])
    g0 = jax.nn.sigmoid(g0_ref[...].astype(F32))
    g1 = jax.nn.sigmoid(g1_ref[...].astype(F32))
    out_ref[...] = (g0 * ya + g1 * yb).astype(out_ref.dtype)


def _merge(o, y, wa, wb, big, layer, tm, tn):
    m = o.shape[0]
    g0_blk = (2 * D_INNER + 2 * SSM_GROUPS * SSM_STATE) // tn
    g1_blk = g0_blk + D_MODEL // tn
    return pl.pallas_call(
        _merge_body,
        grid=(m // tm, D_MODEL // tn),
        in_specs=[pl.BlockSpec((tm, o.shape[1]), lambda i, j: (i, 0)),
                  pl.BlockSpec((tm, y.shape[1]), lambda i, j: (i, 0)),
                  _lspec(layer, (wa.shape[1], tn), lambda i, j: (0, j)),
                  _lspec(layer, (wb.shape[1], tn), lambda i, j: (0, j)),
                  pl.BlockSpec((tm, tn), lambda i, j: (i, g0_blk + j)),
                  pl.BlockSpec((tm, tn), lambda i, j: (i, g1_blk + j))],
        out_specs=pl.BlockSpec((tm, tn), lambda i, j: (i, j)),
        out_shape=jax.ShapeDtypeStruct((m, D_MODEL), BF16),
        compiler_params=_cparams(("parallel", "arbitrary")),
        name="merge",
    )(o, y, wa, wb, big, big)


def _res_mm_body(x_ref, a_ref, w_ref, o_ref):
    o_ref[...] = x_ref[...] + _dot(a_ref[...], w_ref[...])


def _res_mm_f32w_body(x_ref, a_ref, w_ref, o_ref, wb_ref):
    @pl.when(pl.program_id(0) == 0)
    def _():
        wb_ref[...] = w_ref[...].astype(BF16)

    o_ref[...] = x_ref[...] + _dot(a_ref[...], wb_ref[...])


def _res_matmul(x, a, w, layer, tm):
    m, n = x.shape
    k = a.shape[1]
    return pl.pallas_call(
        _res_mm_f32w_body,
        grid=(m // tm,),
        in_specs=[pl.BlockSpec((tm, n), lambda i: (i, 0)),
                  pl.BlockSpec((tm, k), lambda i: (i, 0)),
                  _lspec(layer, (k, n), single=True)],
        out_specs=pl.BlockSpec((tm, n), lambda i: (i, 0)),
        out_shape=jax.ShapeDtypeStruct((m, n), F32),
        scratch_shapes=[pltpu.VMEM((k, n), BF16)],
        compiler_params=_cparams(("arbitrary",)),
        name="out_proj",
    )(x, a, w)


def _res_matmul_tiled(x, a, w, layer, tm, tn, name):
    m, n = x.shape
    k = a.shape[1]
    return pl.pallas_call(
        _res_mm_body,
        grid=(m // tm, n // tn),
        in_specs=[pl.BlockSpec((tm, tn), lambda i, j: (i, j)),
                  pl.BlockSpec((tm, k), lambda i, j: (i, 0)),
                  _lspec(layer, (k, tn), lambda i, j: (0, j))],
        out_specs=pl.BlockSpec((tm, tn), lambda i, j: (i, j)),
        out_shape=jax.ShapeDtypeStruct((m, n), F32),
        compiler_params=_cparams(("parallel", "arbitrary")),
        name=name,
    )(x, a, w)


def _ple_body(x_ref, nw_ref, wg_ref, p_ref, wp_ref, o_ref, wgb_ref):
    @pl.when(pl.program_id(0) == 0)
    def _():
        wgb_ref[...] = wg_ref[...].astype(BF16)

    x = x_ref[...]
    h = _rms(x, nw_ref[...]).astype(BF16)
    gate = jax.nn.sigmoid(_dot(h, wgb_ref[...]))
    pe = _dot(p_ref[...].astype(BF16), wp_ref[...])
    o_ref[...] = x + pe * gate


def _ple(x, nw, wg, p, wp, layer, tm):
    m, d = x.shape
    return pl.pallas_call(
        _ple_body,
        grid=(m // tm,),
        in_specs=[pl.BlockSpec((tm, d), lambda i: (i, 0)),
                  _lspec(layer, (1, d)),
                  _lspec(layer, (d, d), single=True),
                  _lspec(layer, (tm, PLE_DIM), lambda i: (i, 0)),
                  _lspec(layer, (PLE_DIM, d))],
        out_specs=pl.BlockSpec((tm, d), lambda i: (i, 0)),
        out_shape=jax.ShapeDtypeStruct((m, d), F32),
        scratch_shapes=[pltpu.VMEM((d, d), BF16)],
        compiler_params=_cparams(("arbitrary",)),
        name="ple",
    )(x, nw, wg, p, wp)


def _rope_cols(w):
    half = QK_ROPE // 2
    z = jnp.zeros(w.shape[:-1] + (LANES // 2 - half,), w.dtype)
    return jnp.concatenate([w[..., :half], z, w[..., half:], z], axis=-1)


def kernel(x, p, positions, norm_mix_w, w_in, q_a_norm_w, w_uq, kv_a_norm_w, w_ukv, q_norm_w, k_norm_w,
           w_o_mla, conv_w, conv_b, dt_bias, a_log, d_skip, ssm_norm_w, w_o_ssm, w_out, norm_mlp_w, w_up,
           w_down, ple_norm_w, w_ple_gate, w_ple):
    batch, seq, d = x.shape
    m = batch * seq
    depth = w_in.shape[0]
    tm_big = min(1024, m)
    tm_mid = min(512, m)
    tm_small = min(256, m)
    tq = min(512, seq)

    inv_freq = 1.0 / (ROPE_THETA ** (jnp.arange(0, QK_ROPE, 2, dtype=F32) / QK_ROPE))
    freq = _rope_cols(jnp.concatenate([inv_freq, inv_freq]))[None, :]
    sign = _rope_cols(jnp.concatenate([-jnp.ones_like(inv_freq), jnp.ones_like(inv_freq)]))[None, :]
    ct, st = _rope_tables(positions.reshape(m, 1), freq, sign, tm_big)
    ct_t, st_t = ct.T, st.T

    lane = jnp.arange(D_INNER)
    sel_row = jnp.arange(2 * LANES)[:, None]
    sel = ((sel_row % SSM_HEADS == (lane // SSM_HEADDIM)[None, :]) & (sel_row < 3 * SSM_HEADS)).astype(BF16)
    rows = jnp.arange(CHUNK)[:, None]
    eye_t = (rows == (lane % SSM_HEADDIM)[None, :]).astype(F32)
    tri_t = (rows >= (lane % SSM_HEADDIM)[None, :]).astype(F32)
    tri64 = (rows >= jnp.arange(CHUNK)[None, :]).astype(BF16)
    sh_rows = jnp.arange((CONV_WIDTH - 1) * CHUNK)[:, None]
    shift_m = (jnp.arange(2 * CHUNK)[None, :] == CHUNK + sh_rows % CHUNK - (sh_rows // CHUNK + 1)).astype(BF16)

    s0 = Q_LORA + KV_LORA + QK_ROPE
    s1 = s0 + D_INNER + (D_INNER + 2 * SSM_GROUPS * SSM_STATE)
    s2 = s1 + SSM_HEADS

    w_lat, w_kr, w_dt = lax.optimization_barrier(
        (w_in[..., :Q_LORA + KV_LORA], w_in[..., Q_LORA + KV_LORA:s0], w_in[..., s1:s2]))
    w_small = jnp.concatenate(
        [w_lat, _rope_cols(w_kr), w_dt, jnp.zeros((depth, d, LANES - SSM_HEADS), F32)], axis=-1).astype(BF16)
    big_n = (s1 - s0) + 2 * D_MODEL
    w_big = _shift_cast(w_in, s0 - QK_ROPE, QK_ROPE, s1 - s0, 512, big_n, 0, None, "relayout_zx")
    w_big = _shift_cast(w_in, s2 - LANES, LANES, 2 * D_MODEL, 512, big_n, s1 - s0, w_big, "relayout_gates")
    wq4 = w_uq.reshape(depth, Q_LORA, MLA_HEADS, QK_DIM)
    wq_rope = _rope_cols(wq4[..., QK_NOPE:])
    wq = jnp.concatenate([wq4[..., :QK_NOPE], wq_rope, jnp.roll(wq_rope, LANES // 2, axis=-1)], axis=-1)
    wqt = jnp.swapaxes(wq.reshape(depth, Q_LORA, MLA_HEADS * Q_ACC_W), 1, 2).astype(BF16)
    wkv4 = w_ukv.reshape(depth, KV_LORA, MLA_HEADS, QK_NOPE + V_DIM)
    wk = wkv4[..., :QK_NOPE].reshape(depth, KV_LORA, -1).astype(BF16)
    wvt = jnp.swapaxes(wkv4[..., QK_NOPE:].reshape(depth, KV_LORA, -1), 1, 2).astype(BF16)
    w_o_mla_b = w_o_mla.astype(BF16)
    w_o_ssm_b = w_o_ssm.astype(BF16)
    w_down_b = w_down.astype(BF16)
    w_ple_b = w_ple.astype(BF16)

    rows3 = lambda v: v.reshape(depth, 1, -1).astype(F32)
    pad_h = jnp.zeros((depth, LANES - SSM_HEADS), F32)
    qw_rope = _rope_cols(q_norm_w[:, QK_NOPE:])
    qnw = jnp.concatenate([q_norm_w[:, :QK_NOPE], qw_rope, jnp.roll(qw_rope, LANES // 2, axis=-1)], axis=-1)
    qnw = jnp.broadcast_to(qnw[:, :, None], (depth, Q_ACC_W, tq)).astype(F32)
    kwn = rows3(k_norm_w[:, :QK_NOPE])
    kwr = rows3(_rope_cols(k_norm_w[:, QK_NOPE:]))
    dtb = rows3(jnp.concatenate([dt_bias, pad_h], axis=-1))
    alog = rows3(jnp.concatenate([a_log, pad_h], axis=-1))
    dsk = rows3(jnp.repeat(d_skip, SSM_HEADDIM, axis=-1))
    mix_nw, qa_nw, kva_nw = rows3(norm_mix_w), rows3(q_a_norm_w), rows3(kv_a_norm_w)
    ssm_nw, mlp_nw, ple_nw = rows3(ssm_norm_w), rows3(norm_mlp_w), rows3(ple_norm_w)
    conv_wh, conv_bh = 0.5 * conv_w, 0.5 * conv_b
    cwx, cwb = conv_wh[..., :D_INNER], conv_wh[..., D_INNER:]
    cbx, cbb = rows3(conv_bh[:, :D_INNER]), rows3(conv_bh[:, D_INNER:])
    p3 = p.reshape(depth, m, PLE_DIM)

    xf = x.reshape(m, d)
    for i in range(depth):
        small = _norm_matmul(xf, mix_nw, w_small, i, F32, tm_mid, SMALL_N, "in_proj_small")
        big = _norm_matmul(xf, mix_nw, w_big, i, BF16, tm_big, 1024, "in_proj_big")
        qt = _q_proj(small, qa_nw, wqt, qnw, ct_t, st_t, i, tq)
        k, vt = _kv_proj(small, kva_nw, wk, wvt, kwn, kwr, ct, st, i, tq)
        o = _attention(qt, k, vt, batch, seq, tq, 2)
        y = _ssd(big, small, cwx, cwb, cbx, cbb, dtb, alog, dsk, ssm_nw, sel, eye_t, tri_t, tri64, shift_m, i,
                 batch, seq, 4)
        merged = _merge(o, y, w_o_mla_b, w_o_ssm_b, big, i, tm_big, 512)
        xf = _res_matmul(xf, merged, w_out, i, tm_mid)
        hidden = _norm_matmul(xf, mlp_nw, w_up, i, BF16, min(2048, m), 512, "mlp_up", relu2=True, single_x=True)
        xf = _res_matmul_tiled(xf, hidden, w_down_b, i, tm_mid, 512, "mlp_down")
        xf = _ple(xf, ple_nw, w_ple_gate, p3, w_ple_b, i, tm_mid)
    return xf.reshape(batch, seq, d)
```

```python
import functools

import jax
import jax.numpy as jnp
from jax import lax
from jax.experimental import pallas as pl
from jax.experimental.pallas import tpu as pltpu

F32 = jnp.float32
BF16 = jnp.bfloat16

D_MODEL = 2048
DEPTH = 4
CHUNK = 64
PLE_DIM = 256
EPS = 1e-6
MLA_HEADS = 16
Q_LORA = 512
KV_LORA = 512
QK_NOPE = 128
QK_ROPE = 64
V_DIM = 128
QK_DIM = QK_NOPE + QK_ROPE
ROPE_THETA = 10000.0
D_INNER = 2 * D_MODEL
SSM_HEADDIM = 64
SSM_HEADS = D_INNER // SSM_HEADDIM
SSM_GROUPS = 8
SSM_STATE = 128
GROUP_W = D_INNER // SSM_GROUPS
CONV_WIDTH = 4
D_FF = 4 * D_MODEL

LOG2_E = 1.4426950408889634
LANES = 128
ACC_ROWS = V_DIM + 16
Q_ACC_W = 3 * LANES
HEAD_W = 2 * LANES
SMALL_N = Q_LORA + KV_LORA + 2 * LANES
VMEM_LIMIT = 56 * 1024 * 1024


def _cparams(sem):
    return pltpu.CompilerParams(dimension_semantics=sem, vmem_limit_bytes=VMEM_LIMIT)


def _rms(xf, w):
    ms = jnp.mean(xf * xf, axis=-1, keepdims=True)
    return xf * lax.rsqrt(ms + EPS) * w


def _lspec(layer, block, imap=None, single=False):
    if imap is None:
        imap = lambda *g: (0,) * len(block)
    return pl.BlockSpec((None,) + tuple(block), lambda *g: (layer,) + tuple(imap(*g)),
                        pipeline_mode=pl.Buffered(1) if single else None)


def _dot(a, b):
    return jnp.dot(a, b, preferred_element_type=F32)


def _dot_nt(a, b):
    return lax.dot_general(a, b, (((1,), (1,)), ((), ())), preferred_element_type=F32)


def _rope_body(pos_ref, freq_ref, sign_ref, ct_ref, st_ref):
    ang = pos_ref[...].astype(F32) * freq_ref[...]
    ct_ref[...] = jnp.cos(ang)
    st_ref[...] = jnp.sin(ang) * sign_ref[...]


def _rope_tables(pos_col, freq, sign, tm):
    m = pos_col.shape[0]
    return pl.pallas_call(
        _rope_body,
        grid=(m // tm,),
        in_specs=[pl.BlockSpec((tm, 1), lambda i: (i, 0)),
                  pl.BlockSpec((1, LANES), lambda i: (0, 0)),
                  pl.BlockSpec((1, LANES), lambda i: (0, 0))],
        out_specs=[pl.BlockSpec((tm, LANES), lambda i: (i, 0)),
                   pl.BlockSpec((tm, LANES), lambda i: (i, 0))],
        out_shape=[jax.ShapeDtypeStruct((m, LANES), F32)] * 2,
        compiler_params=_cparams(("parallel",)),
        name="rope_tables",
    )(pos_col, freq, sign)


def _norm_mm_body(x_ref, nw_ref, w_ref, o_ref, h_ref, *, relu2):
    @pl.when(pl.program_id(1) == 0)
    def _():
        h_ref[...] = _rms(x_ref[...], nw_ref[...]).astype(BF16)

    y = _dot(h_ref[...], w_ref[...].astype(BF16))
    if relu2:
        y = jnp.square(jnp.maximum(y, 0.0))
    o_ref[...] = y.astype(o_ref.dtype)


def _norm_matmul(x, nw, w, layer, out_dtype, tm, tn, name, relu2=False, single_x=False):
    m, k = x.shape
    n = w.shape[2]
    return pl.pallas_call(
        functools.partial(_norm_mm_body, relu2=relu2),
        grid=(m // tm, n // tn),
        in_specs=[pl.BlockSpec((tm, k), lambda i, j: (i, 0), pipeline_mode=pl.Buffered(1) if single_x else None),
                  _lspec(layer, (1, k)),
                  _lspec(layer, (k, tn), lambda i, j: (0, j))],
        out_specs=pl.BlockSpec((tm, tn), lambda i, j: (i, j)),
        out_shape=jax.ShapeDtypeStruct((m, n), out_dtype),
        scratch_shapes=[pltpu.VMEM((tm, k), BF16)],
        compiler_params=_cparams(("parallel", "arbitrary")),
        name=name,
    )(x, nw, w)


def _shift_cast_body(a_ref, b_ref, o_ref, *, tiles0, shifts):
    v = jnp.concatenate([a_ref[...], b_ref[...]], axis=1)
    tn = o_ref.shape[1]
    first = pl.program_id(1) < tiles0

    @pl.when(first)
    def _():
        o_ref[...] = v[:, shifts[0]:shifts[0] + tn].astype(BF16)

    @pl.when(jnp.logical_not(first))
    def _():
        o_ref[...] = v[:, shifts[1]:shifts[1] + tn].astype(BF16)


def _shift_cast(w, bases, shifts, widths, tn):
    depth, k, _ = w.shape
    assert all(b % tn == 0 and n % tn == 0 and 0 < s <= LANES for b, s, n in zip(bases, shifts, widths))
    tiles0 = widths[0] // tn
    sub = tn // LANES

    def a_col(l, j):
        return jnp.where(j < tiles0, bases[0] // tn + j, bases[1] // tn + j - tiles0)

    def b_col(l, j):
        return jnp.where(j < tiles0, (bases[0] + tn) // LANES + j * sub, (bases[1] + tn) // LANES + (j - tiles0) * sub)

    return pl.pallas_call(
        functools.partial(_shift_cast_body, tiles0=tiles0, shifts=tuple(shifts)),
        grid=(depth, (widths[0] + widths[1]) // tn),
        in_specs=[pl.BlockSpec((None, k, tn), lambda l, j: (l, 0, a_col(l, j))),
                  pl.BlockSpec((None, k, LANES), lambda l, j: (l, 0, b_col(l, j)))],
        out_specs=pl.BlockSpec((None, k, tn), lambda l, j: (l, 0, j)),
        out_shape=jax.ShapeDtypeStruct((depth, k, widths[0] + widths[1]), BF16),
        compiler_params=_cparams(("parallel", "parallel")),
        name="relayout_w_in",
    )(w, w)


def _rope_apply(v, c, s):
    return v * c + pltpu.roll(v, 64, 1) * s


def _q_body(c_ref, nw_ref, wt_ref, qnw_ref, ct_ref, st_ref, o_ref):
    cn = _rms(c_ref[...], nw_ref[...]).astype(BF16)
    acc = _dot_nt(wt_ref[...], cn)
    wn = qnw_ref[:LANES, :]
    wrc = qnw_ref[LANES:2 * LANES, :] * ct_ref[...]
    wxs = qnw_ref[2 * LANES:, :] * st_ref[...]
    scale = QK_DIM ** -0.5 * LOG2_E
    for h in range(MLA_HEADS):
        qn = acc[Q_ACC_W * h:Q_ACC_W * h + LANES, :]
        qr = acc[Q_ACC_W * h + LANES:Q_ACC_W * h + 2 * LANES, :]
        qx = acc[Q_ACC_W * h + 2 * LANES:Q_ACC_W * (h + 1), :]
        ssq = jnp.sum(qn * qn + qr * qr, axis=0, keepdims=True)
        rs = lax.rsqrt(ssq / QK_DIM + EPS) * scale
        o_ref[0, HEAD_W * h:HEAD_W * h + LANES, :] = (qn * rs * wn).astype(BF16)
        o_ref[0, HEAD_W * h + LANES:HEAD_W * (h + 1), :] = ((qr * wrc + qx * wxs) * rs).astype(BF16)


def _q_proj(small, nw, wqt, qnw, ct_t, st_t, layer, tm):
    m = small.shape[0]
    n = MLA_HEADS * HEAD_W
    return pl.pallas_call(
        _q_body,
        grid=(m // tm,),
        in_specs=[pl.BlockSpec((tm, Q_LORA), lambda i: (i, 0)),
                  _lspec(layer, (1, Q_LORA)),
                  _lspec(layer, (MLA_HEADS * Q_ACC_W, Q_LORA)),
                  _lspec(layer, (Q_ACC_W, tm)),
                  pl.BlockSpec((LANES, tm), lambda i: (0, i)),
                  pl.BlockSpec((LANES, tm), lambda i: (0, i))],
        out_specs=pl.BlockSpec((1, n, tm), lambda i: (i, 0, 0)),
        out_shape=jax.ShapeDtypeStruct((m // tm, n, tm), BF16),
        compiler_params=_cparams(("parallel",)),
        name="q_proj",
    )(small, nw, wqt, qnw, ct_t, st_t)


def _kv_body(c_ref, kr_ref, nw_ref, wk_ref, wvt_ref, kwn_ref, kwr_ref, ct_ref, st_ref, k_ref, vt_ref):
    cn = _rms(c_ref[...], nw_ref[...]).astype(BF16)
    acc = _dot(cn, wk_ref[...])
    vt_ref[0] = _dot_nt(wvt_ref[...], cn).astype(BF16)
    c = ct_ref[...]
    s = st_ref[...]
    kr = kr_ref[...]
    wn = kwn_ref[...]
    wr = kwr_ref[...]
    ssr = jnp.sum(kr * kr, axis=-1, keepdims=True)
    kr_rot = _rope_apply(kr * wr, c, s)
    for h in range(MLA_HEADS):
        kn = acc[:, LANES * h:LANES * (h + 1)]
        ssq = jnp.sum(kn * kn, axis=-1, keepdims=True) + ssr
        rs = lax.rsqrt(ssq / QK_DIM + EPS)
        k_ref[:, HEAD_W * h:HEAD_W * h + LANES] = (kn * rs * wn).astype(BF16)
        k_ref[:, HEAD_W * h + LANES:HEAD_W * (h + 1)] = (kr_rot * rs).astype(BF16)


def _kv_proj(small, nw, wk, wvt, kwn, kwr, ct, st, layer, tm):
    m = small.shape[0]
    nk = MLA_HEADS * HEAD_W
    nv = MLA_HEADS * V_DIM
    kr_blk = (Q_LORA + KV_LORA) // LANES
    return pl.pallas_call(
        _kv_body,
        grid=(m // tm,),
        in_specs=[pl.BlockSpec((tm, KV_LORA), lambda i: (i, 1)),
                  pl.BlockSpec((tm, LANES), lambda i: (i, kr_blk)),
                  _lspec(layer, (1, KV_LORA)),
                  _lspec(layer, (KV_LORA, MLA_HEADS * QK_NOPE)),
                  _lspec(layer, (nv, KV_LORA)),
                  _lspec(layer, (1, LANES)),
                  _lspec(layer, (1, LANES)),
                  pl.BlockSpec((tm, LANES), lambda i: (i, 0)),
                  pl.BlockSpec((tm, LANES), lambda i: (i, 0))],
        out_specs=[pl.BlockSpec((tm, nk), lambda i: (i, 0)),
                   pl.BlockSpec((1, nv, tm), lambda i: (i, 0, 0))],
        out_shape=[jax.ShapeDtypeStruct((m, nk), BF16), jax.ShapeDtypeStruct((m // tm, nv, tm), BF16)],
        compiler_params=_cparams(("parallel",)),
        name="kv_proj",
    )(small, small, nw, wk, wvt, kwn, kwr, ct, st)


def _attn_body(q_ref, k_ref, vt_ref, o_ref, s0_scr, s1_scr, p0_scr, p1_scr, m_scr, a_scr, acc_scr, *, tq, nq, hb):
    s_scr = (s0_scr, s1_scr)
    p_scr = (p0_scr, p1_scr)
    tk = tq // 2
    key_chunk = lax.broadcasted_iota(jnp.int32, (tk, tq), 0) // CHUNK
    qry_chunk = lax.broadcasted_iota(jnp.int32, (tk, tq), 1) // CHUNK
    diag_masks = [key_chunk + h * (tk // CHUNK) <= qry_chunk for h in range(2)]
    ones_rows = jnp.ones((ACC_ROWS - V_DIM, tk), BF16)

    def scores(qs, kpair, h, mask=None):
        koff = pl.multiple_of(kpair * tq + h * tk, tk)
        for hh in range(hb):
            s = _dot(k_ref[pl.ds(koff, tk), HEAD_W * hh:HEAD_W * (hh + 1)], qs[hh])
            s_scr[h][hh] = s if mask is None else jnp.where(mask, s, -jnp.inf)

    def pv(kpair, h):
        return [_dot(jnp.concatenate([vt_ref[kpair, V_DIM * hh:V_DIM * (hh + 1), h * tk:(h + 1) * tk],
                                      ones_rows], axis=0), p_scr[h][hh])
                for hh in range(hb)]

    def softmax(h, pvs):
        for hh in range(hb):
            m = m_scr[hh]
            s = s_scr[h][hh]
            m_new = jnp.maximum(m, jnp.max(s, axis=0, keepdims=True))
            p_scr[h][hh] = jnp.exp2((s - m_new).astype(BF16))
            alpha = jnp.exp2(m - m_new)
            acc_scr[hh] = alpha * (acc_scr[hh] + a_scr[hh] * pvs[hh])
            a_scr[hh] = alpha
            m_scr[hh] = m_new

    def q_tiles(qi):
        return [q_ref[qi, HEAD_W * hh:HEAD_W * (hh + 1), :] for hh in range(hb)]

    def diag_scores(qi):
        qs = q_tiles(qi)
        for h in range(2):
            scores(qs, qi, h, diag_masks[h])

    diag_scores(0)

    def q_block(qi, c):
        qoff = pl.multiple_of(qi * tq, tq)
        qs = q_tiles(qi)
        m_scr[...] = jnp.full(m_scr.shape, -jnp.inf, F32)
        a_scr[...] = jnp.zeros(a_scr.shape, F32)
        acc_scr[...] = jnp.zeros(acc_scr.shape, F32)
        for h in range(2):
            p_scr[h][...] = jnp.zeros(p_scr[h].shape, BF16)

        def kpair_of(r):
            return jnp.where(r == 0, qi, r - 1)

        def body(r, c2):
            prev = kpair_of(jnp.maximum(r - 1, 0))
            for h in range(2):
                pvs = pv(prev, h)
                softmax(h, pvs)
                scores(qs, r, h)
            return c2

        lax.fori_loop(0, qi, body, 0)
        prev = kpair_of(jnp.maximum(qi - 1, 0))
        nxt = jnp.minimum(qi + 1, nq - 1)
        qn = q_tiles(nxt)
        for h in range(2):
            softmax(h, pv(prev, h))
            scores(qn, nxt, h, diag_masks[h])
        last = kpair_of(qi)
        pv0, pv1 = pv(last, 0), pv(last, 1)
        for hh in range(hb):
            acc = acc_scr[hh] + a_scr[hh] * pv0[hh] + pv1[hh]
            o_ref[pl.ds(qoff, tq), V_DIM * hh:V_DIM * (hh + 1)] = (
                acc[:V_DIM] / acc[V_DIM:V_DIM + 1]).T.astype(BF16)
        return c

    lax.fori_loop(0, nq, q_block, 0)


def _attention(qt, k, vt, batch, seq, tq, hb):
    m = k.shape[0]
    nq = seq // tq
    tk = tq // 2
    assert vt.shape[2] == tq
    return pl.pallas_call(
        functools.partial(_attn_body, tq=tq, nq=nq, hb=hb),
        grid=(batch, MLA_HEADS // hb),
        in_specs=[pl.BlockSpec((nq, hb * HEAD_W, tq), lambda b, h: (b, h, 0)),
                  pl.BlockSpec((seq, hb * HEAD_W), lambda b, h: (b, h)),
                  pl.BlockSpec((nq, hb * V_DIM, tq), lambda b, h: (b, h, 0))],
        out_specs=pl.BlockSpec((seq, hb * V_DIM), lambda b, h: (b, h)),
        out_shape=jax.ShapeDtypeStruct((m, MLA_HEADS * V_DIM), BF16),
        scratch_shapes=[pltpu.VMEM((hb, tk, tq), F32), pltpu.VMEM((hb, tk, tq), F32),
                        pltpu.VMEM((hb, tk, tq), BF16), pltpu.VMEM((hb, tk, tq), BF16),
                        pltpu.VMEM((hb, 1, tq), F32), pltpu.VMEM((hb, 1, tq), F32),
                        pltpu.VMEM((hb, ACC_ROWS, tq), F32)],
        compiler_params=_cparams(("parallel", "arbitrary")),
        name="attention",
    )(qt, k, vt)


def _silu_of_half(h):
    return h + h * jnp.tanh(h)


def _silu(y):
    return _silu_of_half(0.5 * y)


def _split3(v):
    hi = v.astype(BF16)
    r = v - hi.astype(F32)
    mid = r.astype(BF16)
    lo = (r - mid.astype(F32)).astype(BF16)
    return hi, mid, lo


def _ssd_body(z_ref, x_ref, bc_ref, dt_ref, cwx_ref, cwb_ref, cbx_ref, cbb_ref, dtb_ref, alog_ref,
              dsk_ref, nw_ref, sel_ref, eye_ref, tri_ref, tri64_ref, shift_ref, y_ref, state_ref, xhalo, bchalo,
              *, cpb):
    t = CHUNK
    gn = SSM_GROUPS * SSM_STATE

    @pl.when(pl.program_id(1) == 0)
    def _():
        state_ref[...] = jnp.zeros_like(state_ref)
        xhalo[...] = jnp.zeros_like(xhalo)
        bchalo[...] = jnp.zeros_like(bchalo)

    def conv_silu(raw_ref, halo, w_ref, b_ref):
        raws = [raw_ref[c * t:(c + 1) * t, :] for c in range(cpb)]
        prevs = [halo[...]] + raws[:-1]
        halo[...] = raws[-1]
        shifted = [_dot(shift_ref[...], jnp.concatenate([prevs[c], raws[c]], axis=0)) for c in range(cpb)]
        outs = []
        for c in range(cpb):
            y = None
            for tap in range(CONV_WIDTH):
                shift = CONV_WIDTH - 1 - tap
                xt = raws[c].astype(F32) if shift == 0 else shifted[c][(shift - 1) * t:shift * t, :]
                term = xt * w_ref[tap:tap + 1, :]
                y = term if y is None else y + term
            outs.append(_silu_of_half(y + b_ref[...]))
        return outs

    xs_all = conv_silu(x_ref, xhalo, cwx_ref, cbx_ref)
    bc_all = conv_silu(bc_ref, bchalo, cwb_ref, cbb_ref)

    head_lane = lax.broadcasted_iota(jnp.int32, (t, LANES), 1) < SSM_HEADS
    neg_a = -jnp.exp(alog_ref[...])
    tri64 = tri64_ref[...]
    stacks = []
    for c in range(cpb):
        dt = jnp.where(head_lane, jax.nn.softplus(dt_ref[c * t:(c + 1) * t, :] + dtb_ref[...]), 0.0)
        h3 = _split3(dt * neg_a)
        a_cum = (_dot(tri64, h3[0]) + _dot(tri64, h3[1]) + _dot(tri64, h3[2])) * LOG2_E
        stacks += [a_cum, dt]
    s3 = [part.astype(F32) for part in _split3(jnp.concatenate(stacks, axis=0))]
    packed = jnp.concatenate([(s3[0] + pltpu.roll(s3[1], SSM_HEADS, 1)).astype(BF16), s3[2].astype(BF16)], axis=1)
    ex_all = _dot(packed, sel_ref[...])
    left = lax.broadcasted_iota(jnp.int32, (t, LANES), 1) < SSM_HEADDIM
    zero_b = jnp.zeros((t, LANES), BF16)

    pre = []
    for c in range(cpb):
        ex = ex_all[2 * t * c:2 * t * (c + 1)]
        a_col = ex[:t]
        ea = jnp.exp2(a_col)
        a_row = jnp.sum(jnp.where(eye_ref[...] > 0, a_col, 0.0), axis=0, keepdims=True)
        decay = jnp.exp2(jnp.where(tri_ref[...] > 0, a_col - a_row, -jnp.inf))
        xdt = xs_all[c] * ex[t:2 * t]
        pre.append((decay, ea, xdt.astype(BF16), (xdt * jnp.exp2(a_col[t - 1:t, :] - a_col)).astype(BF16)))

    for c in range(cpb):
        decay, ea, xdt_b, xw_b = pre[c]
        xs, bc = xs_all[c], bc_all[c]
        chunk_decay = ea[t - 1:t, :]
        r0 = c * t
        for g in range(SSM_GROUPS):
            lo, hi = GROUP_W * g, GROUP_W * (g + 1)
            bg = bc[:, SSM_STATE * g:SSM_STATE * (g + 1)]
            cg_b = bc[:, gn + SSM_STATE * g:gn + SSM_STATE * (g + 1)].astype(BF16)
            bg_b = bg.astype(BF16)
            cb2 = _dot_nt(cg_b, jnp.concatenate([bg_b, bg_b], axis=0))
            yd = []
            for j in range(GROUP_W // LANES):
                l0 = lo + LANES * j
                mm = (cb2 * decay[:, l0:l0 + LANES]).astype(BF16)
                xp = xdt_b[:, l0:l0 + LANES]
                rhs = jnp.concatenate([jnp.where(left, xp, zero_b), jnp.where(left, zero_b, xp)], axis=0)
                yd.append(_dot(mm, rhs))
            y_diag = jnp.concatenate(yd, axis=1)
            st = state_ref[g]
            y_off = _dot(cg_b, st.astype(BF16)) * ea[:, lo:hi]
            inc = _dot(bg.T.astype(BF16), xw_b[:, lo:hi])
            state_ref[g] = st * chunk_decay[:, lo:hi] + inc
            yg = y_diag + y_off + dsk_ref[:, lo:hi] * xs[:, lo:hi]
            yg = yg * _silu(z_ref[r0:r0 + t, lo:hi].astype(F32))
            ms = jnp.mean(yg * yg, axis=-1, keepdims=True)
            y_ref[r0:r0 + t, lo:hi] = (yg * lax.rsqrt(ms + EPS) * nw_ref[:, lo:hi]).astype(BF16)


def _ssd(big, small, cwx, cwb, cbx, cbb, dtb, alog, dsk, nw, sel, eye_t, tri_t, tri64, shift_m, layer, batch, seq,
         cpb):
    m = big.shape[0]
    t = CHUNK
    tb = cpb * t
    nb = seq // tb
    bcw = 2 * SSM_GROUPS * SSM_STATE
    bc_blk = (2 * D_INNER) // bcw
    dt_blk = (SMALL_N - LANES) // LANES
    row = lambda b, c: b * nb + c
    const = lambda shape: pl.BlockSpec(shape, lambda b, c: (0,) * len(shape))
    return pl.pallas_call(
        functools.partial(_ssd_body, cpb=cpb),
        grid=(batch, nb),
        in_specs=[pl.BlockSpec((tb, D_INNER), lambda b, c: (row(b, c), 0)),
                  pl.BlockSpec((tb, D_INNER), lambda b, c: (row(b, c), 1)),
                  pl.BlockSpec((tb, bcw), lambda b, c: (row(b, c), bc_blk)),
                  pl.BlockSpec((tb, LANES), lambda b, c: (row(b, c), dt_blk)),
                  _lspec(layer, (CONV_WIDTH, D_INNER)), _lspec(layer, (CONV_WIDTH, bcw)),
                  _lspec(layer, (1, D_INNER)), _lspec(layer, (1, bcw)),
                  _lspec(layer, (1, LANES)), _lspec(layer, (1, LANES)),
                  _lspec(layer, (1, D_INNER)), _lspec(layer, (1, D_INNER)),
                  const((2 * LANES, D_INNER)), const((t, D_INNER)), const((t, D_INNER)), const((t, t)),
                  const(((CONV_WIDTH - 1) * t, 2 * t))],
        out_specs=pl.BlockSpec((tb, D_INNER), lambda b, c: (row(b, c), 0)),
        out_shape=jax.ShapeDtypeStruct((m, D_INNER), BF16),
        scratch_shapes=[pltpu.VMEM((SSM_GROUPS, SSM_STATE, GROUP_W), F32),
                        pltpu.VMEM((t, D_INNER), BF16),
                        pltpu.VMEM((t, bcw), BF16)],
        compiler_params=_cparams(("parallel", "arbitrary")),
        name="ssd",
    )(big, big, big, small, cwx, cwb, cbx, cbb, dtb, alog, dsk, nw, sel, eye_t, tri_t, tri64, shift_m)


def _merge_body(o_ref, y_ref, wa_ref, wb_ref, g0_ref, g1_ref, out_ref):
    ya = _dot(o_ref[...], wa_ref[...])
    yb = _dot(y_ref[...], wb_ref[...])
    g0 = jax.nn.sigmoid(g0_ref[...].astype(F32))
    g1 = jax.nn.sigmoid(g1_ref[...].astype(F32))
    out_ref[...] = (g0 * ya + g1 * yb).astype(out_ref.dtype)


def _merge(o, y, wa, wb, big, layer, tm, tn):
    m = o.shape[0]
    g0_blk = (2 * D_INNER + 2 * SSM_GROUPS * SSM_STATE) // tn
    g1_blk = g0_blk + D_MODEL // tn
    return pl.pallas_call(
        _merge_body,
        grid=(m // tm, D_MODEL // tn),
        in_specs=[pl.BlockSpec((tm, o.shape[1]), lambda i, j: (i, 0)),
                  pl.BlockSpec((tm, y.shape[1]), lambda i, j: (i, 0)),
                  _lspec(layer, (wa.shape[1], tn), lambda i, j: (0, j)),
                  _lspec(layer, (wb.shape[1], tn), lambda i, j: (0, j)),
                  pl.BlockSpec((tm, tn), lambda i, j: (i, g0_blk + j)),
                  pl.BlockSpec((tm, tn), lambda i, j: (i, g1_blk + j))],
        out_specs=pl.BlockSpec((tm, tn), lambda i, j: (i, j)),
        out_shape=jax.ShapeDtypeStruct((m, D_MODEL), BF16),
        compiler_params=_cparams(("parallel", "arbitrary")),
        name="merge",
    )(o, y, wa, wb, big, big)


def _res_mm_body(x_ref, a_ref, w_ref, o_ref):
    o_ref[...] = x_ref[...] + _dot(a_ref[...], w_ref[...])


def _res_mm_f32w_body(x_ref, a_ref, w_ref, o_ref, wb_ref):
    @pl.when(pl.program_id(0) == 0)
    def _():
        wb_ref[...] = w_ref[...].astype(BF16)

    o_ref[...] = x_ref[...] + _dot(a_ref[...], wb_ref[...])


def _res_matmul(x, a, w, layer, tm):
    m, n = x.shape
    k = a.shape[1]
    return pl.pallas_call(
        _res_mm_f32w_body,
        grid=(m // tm,),
        in_specs=[pl.BlockSpec((tm, n), lambda i: (i, 0)),
                  pl.BlockSpec((tm, k), lambda i: (i, 0)),
                  _lspec(layer, (k, n), single=True)],
        out_specs=pl.BlockSpec((tm, n), lambda i: (i, 0)),
        out_shape=jax.ShapeDtypeStruct((m, n), F32),
        scratch_shapes=[pltpu.VMEM((k, n), BF16)],
        compiler_params=_cparams(("arbitrary",)),
        name="out_proj",
    )(x, a, w)


def _res_matmul_tiled(x, a, w, layer, tm, tn, name):
    m, n = x.shape
    k = a.shape[1]
    return pl.pallas_call(
        _res_mm_body,
        grid=(m // tm, n // tn),
        in_specs=[pl.BlockSpec((tm, tn), lambda i, j: (i, j)),
                  pl.BlockSpec((tm, k), lambda i, j: (i, 0)),
                  _lspec(layer, (k, tn), lambda i, j: (0, j))],
        out_specs=pl.BlockSpec((tm, tn), lambda i, j: (i, j)),
        out_shape=jax.ShapeDtypeStruct((m, n), F32),
        compiler_params=_cparams(("parallel", "arbitrary")),
        name=name,
    )(x, a, w)


def _ple_body(x_ref, nw_ref, wg_ref, p_ref, wp_ref, o_ref, wgb_ref):
    @pl.when(pl.program_id(0) == 0)
    def _():
        wgb_ref[...] = wg_ref[...].astype(BF16)

    x = x_ref[...]
    h = _rms(x, nw_ref[...]).astype(BF16)
    gate = jax.nn.sigmoid(_dot(h, wgb_ref[...]))
    pe = _dot(p_ref[...].astype(BF16), wp_ref[...])
    o_ref[...] = x + pe * gate


def _ple(x, nw, wg, p, wp, layer, tm):
    m, d = x.shape
    return pl.pallas_call(
        _ple_body,
        grid=(m // tm,),
        in_specs=[pl.BlockSpec((tm, d), lambda i: (i, 0)),
                  _lspec(layer, (1, d)),
                  _lspec(layer, (d, d), single=True),
                  _lspec(layer, (tm, PLE_DIM), lambda i: (i, 0)),
                  _lspec(layer, (PLE_DIM, d))],
        out_specs=pl.BlockSpec((tm, d), lambda i: (i, 0)),
        out_shape=jax.ShapeDtypeStruct((m, d), F32),
        scratch_shapes=[pltpu.VMEM((d, d), BF16)],
        compiler_params=_cparams(("arbitrary",)),
        name="ple",
    )(x, nw, wg, p, wp)


def _rope_cols(w):
    half = QK_ROPE // 2
    z = jnp.zeros(w.shape[:-1] + (LANES // 2 - half,), w.dtype)
    return jnp.concatenate([w[..., :half], z, w[..., half:], z], axis=-1)


def kernel(x, p, positions, norm_mix_w, w_in, q_a_norm_w, w_uq, kv_a_norm_w, w_ukv, q_norm_w, k_norm_w,
           w_o_mla, conv_w, conv_b, dt_bias, a_log, d_skip, ssm_norm_w, w_o_ssm, w_out, norm_mlp_w, w_up,
           w_down, ple_norm_w, w_ple_gate, w_ple):
    batch, seq, d = x.shape
    m = batch * seq
    depth = w_in.shape[0]
    tm_big = min(1024, m)
    tm_mid = min(512, m)
    tm_small = min(256, m)
    tq = min(512, seq)

    inv_freq = 1.0 / (ROPE_THETA ** (jnp.arange(0, QK_ROPE, 2, dtype=F32) / QK_ROPE))
    freq = _rope_cols(jnp.concatenate([inv_freq, inv_freq]))[None, :]
    sign = _rope_cols(jnp.concatenate([-jnp.ones_like(inv_freq), jnp.ones_like(inv_freq)]))[None, :]
    ct, st = _rope_tables(positions.reshape(m, 1), freq, sign, tm_big)
    ct_t, st_t = ct.T, st.T

    lane = jnp.arange(D_INNER)
    sel_row = jnp.arange(2 * LANES)[:, None]
    sel = ((sel_row % SSM_HEADS == (lane // SSM_HEADDIM)[None, :]) & (sel_row < 3 * SSM_HEADS)).astype(BF16)
    rows = jnp.arange(CHUNK)[:, None]
    eye_t = (rows == (lane % SSM_HEADDIM)[None, :]).astype(F32)
    tri_t = (rows >= (lane % SSM_HEADDIM)[None, :]).astype(F32)
    tri64 = (rows >= jnp.arange(CHUNK)[None, :]).astype(BF16)
    sh_rows = jnp.arange((CONV_WIDTH - 1) * CHUNK)[:, None]
    shift_m = (jnp.arange(2 * CHUNK)[None, :] == CHUNK + sh_rows % CHUNK - (sh_rows // CHUNK + 1)).astype(BF16)

    s0 = Q_LORA + KV_LORA + QK_ROPE
    s1 = s0 + D_INNER + (D_INNER + 2 * SSM_GROUPS * SSM_STATE)
    s2 = s1 + SSM_HEADS

    w_lat, w_kr, w_dt = lax.optimization_barrier(
        (w_in[..., :Q_LORA + KV_LORA], w_in[..., Q_LORA + KV_LORA:s0], w_in[..., s1:s2]))
    w_small = jnp.concatenate(
        [w_lat, _rope_cols(w_kr), w_dt, jnp.zeros((depth, d, LANES - SSM_HEADS), F32)], axis=-1).astype(BF16)
    w_big = _shift_cast(w_in, (s0 - QK_ROPE, s2 - LANES), (QK_ROPE, LANES), (s1 - s0, 2 * D_MODEL), 512)
    wq4 = w_uq.reshape(depth, Q_LORA, MLA_HEADS, QK_DIM)
    wq_rope = _rope_cols(wq4[..., QK_NOPE:])
    wq = jnp.concatenate([wq4[..., :QK_NOPE], wq_rope, jnp.roll(wq_rope, LANES // 2, axis=-1)], axis=-1)
    wqt = jnp.swapaxes(wq.reshape(depth, Q_LORA, MLA_HEADS * Q_ACC_W), 1, 2).astype(BF16)
    wkv4 = w_ukv.reshape(depth, KV_LORA, MLA_HEADS, QK_NOPE + V_DIM)
    wk = wkv4[..., :QK_NOPE].reshape(depth, KV_LORA, -1).astype(BF16)
    wvt = jnp.swapaxes(wkv4[..., QK_NOPE:].reshape(depth, KV_LORA, -1), 1, 2).astype(BF16)
    w_o_mla_b = w_o_mla.astype(BF16)
    w_o_ssm_b = w_o_ssm.astype(BF16)
    w_down_b = w_down.astype(BF16)
    w_ple_b = w_ple.astype(BF16)

    rows3 = lambda v: v.reshape(depth, 1, -1).astype(F32)
    pad_h = jnp.zeros((depth, LANES - SSM_HEADS), F32)
    qw_rope = _rope_cols(q_norm_w[:, QK_NOPE:])
    qnw = jnp.concatenate([q_norm_w[:, :QK_NOPE], qw_rope, jnp.roll(qw_rope, LANES // 2, axis=-1)], axis=-1)
    qnw = jnp.broadcast_to(qnw[:, :, None], (depth, Q_ACC_W, tq)).astype(F32)
    kwn = rows3(k_norm_w[:, :QK_NOPE])
    kwr = rows3(_rope_cols(k_norm_w[:, QK_NOPE:]))
    dtb = rows3(jnp.concatenate([dt_bias, pad_h], axis=-1))
    alog = rows3(jnp.concatenate([a_log, pad_h], axis=-1))
    dsk = rows3(jnp.repeat(d_skip, SSM_HEADDIM, axis=-1))
    mix_nw, qa_nw, kva_nw = rows3(norm_mix_w), rows3(q_a_norm_w), rows3(kv_a_norm_w)
    ssm_nw, mlp_nw, ple_nw = rows3(ssm_norm_w), rows3(norm_mlp_w), rows3(ple_norm_w)
    conv_wh, conv_bh = 0.5 * conv_w, 0.5 * conv_b
    cwx, cwb = conv_wh[..., :D_INNER], conv_wh[..., D_INNER:]
    cbx, cbb = rows3(conv_bh[:, :D_INNER]), rows3(conv_bh[:, D_INNER:])
    p3 = p.reshape(depth, m, PLE_DIM)

    xf = x.reshape(m, d)
    for i in range(depth):
        small = _norm_matmul(xf, mix_nw, w_small, i, F32, tm_mid, SMALL_N, "in_proj_small")
        big = _norm_matmul(xf, mix_nw, w_big, i, BF16, tm_big, 1024, "in_proj_big")
        qt = _q_proj(small, qa_nw, wqt, qnw, ct_t, st_t, i, tq)
        k, vt = _kv_proj(small, kva_nw, wk, wvt, kwn, kwr, ct, st, i, tq)
        o = _attention(qt, k, vt, batch, seq, tq, 2)
        y = _ssd(big, small, cwx, cwb, cbx, cbb, dtb, alog, dsk, ssm_nw, sel, eye_t, tri_t, tri64, shift_m, i,
                 batch, seq, 4)
        merged = _merge(o, y, w_o_mla_b, w_o_ssm_b, big, i, tm_big, 512)
        xf = _res_matmul(xf, merged, w_out, i, tm_mid)
        hidden = _norm_matmul(xf, mlp_nw, w_up, i, BF16, min(2048, m), 512, "mlp_up", relu2=True, single_x=True)
        xf = _res_matmul_tiled(xf, hidden, w_down_b, i, tm_mid, 512, "mlp_down")
        xf = _ple(xf, ple_nw, w_ple_gate, p3, w_ple_b, i, tm_mid)
    return xf.reshape(batch, seq, d)
```

```python
import functools

import jax
import jax.numpy as jnp
from jax import lax
from jax.experimental import pallas as pl
from jax.experimental.pallas import tpu as pltpu

F32 = jnp.float32
BF16 = jnp.bfloat16

D_MODEL = 2048
DEPTH = 4
CHUNK = 64
PLE_DIM = 256
EPS = 1e-6
MLA_HEADS = 16
Q_LORA = 512
KV_LORA = 512
QK_NOPE = 128
QK_ROPE = 64
V_DIM = 128
QK_DIM = QK_NOPE + QK_ROPE
ROPE_THETA = 10000.0
D_INNER = 2 * D_MODEL
SSM_HEADDIM = 64
SSM_HEADS = D_INNER // SSM_HEADDIM
SSM_GROUPS = 8
SSM_STATE = 128
GROUP_W = D_INNER // SSM_GROUPS
CONV_WIDTH = 4
D_FF = 4 * D_MODEL

LOG2_E = 1.4426950408889634
LANES = 128
ACC_ROWS = V_DIM + 16
Q_ACC_W = 3 * LANES
HEAD_W = 2 * LANES
SMALL_N = Q_LORA + KV_LORA + 2 * LANES
VMEM_LIMIT = 56 * 1024 * 1024


def _cparams(sem):
    return pltpu.CompilerParams(dimension_semantics=sem, vmem_limit_bytes=VMEM_LIMIT)


def _rms(xf, w):
    ms = jnp.mean(xf * xf, axis=-1, keepdims=True)
    return xf * lax.rsqrt(ms + EPS) * w


def _lspec(layer, block, imap=None, single=False):
    if imap is None:
        imap = lambda *g: (0,) * len(block)
    return pl.BlockSpec((None,) + tuple(block), lambda *g: (layer,) + tuple(imap(*g)),
                        pipeline_mode=pl.Buffered(1) if single else None)


def _dot(a, b):
    return jnp.dot(a, b, preferred_element_type=F32)


def _dot_nt(a, b):
    return lax.dot_general(a, b, (((1,), (1,)), ((), ())), preferred_element_type=F32)


def _rope_body(pos_ref, freq_ref, sign_ref, ct_ref, st_ref):
    ang = pos_ref[...].astype(F32) * freq_ref[...]
    ct_ref[...] = jnp.cos(ang)
    st_ref[...] = jnp.sin(ang) * sign_ref[...]


def _rope_tables(pos_col, freq, sign, tm):
    m = pos_col.shape[0]
    return pl.pallas_call(
        _rope_body,
        grid=(m // tm,),
        in_specs=[pl.BlockSpec((tm, 1), lambda i: (i, 0)),
                  pl.BlockSpec((1, LANES), lambda i: (0, 0)),
                  pl.BlockSpec((1, LANES), lambda i: (0, 0))],
        out_specs=[pl.BlockSpec((tm, LANES), lambda i: (i, 0)),
                   pl.BlockSpec((tm, LANES), lambda i: (i, 0))],
        out_shape=[jax.ShapeDtypeStruct((m, LANES), F32)] * 2,
        compiler_params=_cparams(("parallel",)),
        name="rope_tables",
    )(pos_col, freq, sign)


def _norm_mm_body(x_ref, nw_ref, w_ref, o_ref, h_ref, *, relu2):
    @pl.when(pl.program_id(1) == 0)
    def _():
        h_ref[...] = _rms(x_ref[...], nw_ref[...]).astype(BF16)

    y = _dot(h_ref[...], w_ref[...].astype(BF16))
    if relu2:
        y = jnp.square(jnp.maximum(y, 0.0))
    o_ref[...] = y.astype(o_ref.dtype)


def _norm_matmul(x, nw, w, layer, out_dtype, tm, tn, name, relu2=False, single_x=False):
    m, k = x.shape
    n = w.shape[2]
    return pl.pallas_call(
        functools.partial(_norm_mm_body, relu2=relu2),
        grid=(m // tm, n // tn),
        in_specs=[pl.BlockSpec((tm, k), lambda i, j: (i, 0), pipeline_mode=pl.Buffered(1) if single_x else None),
                  _lspec(layer, (1, k)),
                  _lspec(layer, (k, tn), lambda i, j: (0, j))],
        out_specs=pl.BlockSpec((tm, tn), lambda i, j: (i, j)),
        out_shape=jax.ShapeDtypeStruct((m, n), out_dtype),
        scratch_shapes=[pltpu.VMEM((tm, k), BF16)],
        compiler_params=_cparams(("parallel", "arbitrary")),
        name=name,
    )(x, nw, w)


def _shift_cast_body(a_ref, b_ref, o_ref, *, tiles0, shifts):
    v = jnp.concatenate([a_ref[...], b_ref[...]], axis=1)
    tn = o_ref.shape[1]
    first = pl.program_id(1) < tiles0

    @pl.when(first)
    def _():
        o_ref[...] = v[:, shifts[0]:shifts[0] + tn].astype(BF16)

    @pl.when(jnp.logical_not(first))
    def _():
        o_ref[...] = v[:, shifts[1]:shifts[1] + tn].astype(BF16)


def _shift_cast(w, bases, shifts, widths, tn):
    depth, k, _ = w.shape
    assert all(b % tn == 0 and n % tn == 0 and 0 < s <= LANES for b, s, n in zip(bases, shifts, widths))
    tiles0 = widths[0] // tn
    sub = tn // LANES

    def a_col(l, j):
        return jnp.where(j < tiles0, bases[0] // tn + j, bases[1] // tn + j - tiles0)

    def b_col(l, j):
        return jnp.where(j < tiles0, (bases[0] + tn) // LANES + j * sub, (bases[1] + tn) // LANES + (j - tiles0) * sub)

    return pl.pallas_call(
        functools.partial(_shift_cast_body, tiles0=tiles0, shifts=tuple(shifts)),
        grid=(depth, (widths[0] + widths[1]) // tn),
        in_specs=[pl.BlockSpec((None, k, tn), lambda l, j: (l, 0, a_col(l, j))),
                  pl.BlockSpec((None, k, LANES), lambda l, j: (l, 0, b_col(l, j)))],
        out_specs=pl.BlockSpec((None, k, tn), lambda l, j: (l, 0, j)),
        out_shape=jax.ShapeDtypeStruct((depth, k, widths[0] + widths[1]), BF16),
        compiler_params=_cparams(("parallel", "parallel")),
        name="relayout_w_in",
    )(w, w)


def _rope_apply(v, c, s):
    return v * c + pltpu.roll(v, 64, 1) * s


def _q_body(c_ref, nw_ref, wt_ref, qnw_ref, ct_ref, st_ref, o_ref):
    cn = _rms(c_ref[...], nw_ref[...]).astype(BF16)
    acc = _dot_nt(wt_ref[...], cn)
    wn = qnw_ref[:LANES, :]
    wrc = qnw_ref[LANES:2 * LANES, :] * ct_ref[...]
    wxs = qnw_ref[2 * LANES:, :] * st_ref[...]
    scale = QK_DIM ** -0.5 * LOG2_E
    for h in range(MLA_HEADS):
        qn = acc[Q_ACC_W * h:Q_ACC_W * h + LANES, :]
        qr = acc[Q_ACC_W * h + LANES:Q_ACC_W * h + 2 * LANES, :]
        qx = acc[Q_ACC_W * h + 2 * LANES:Q_ACC_W * (h + 1), :]
        ssq = jnp.sum(qn * qn + qr * qr, axis=0, keepdims=True)
        rs = lax.rsqrt(ssq / QK_DIM + EPS) * scale
        o_ref[0, HEAD_W * h:HEAD_W * h + LANES, :] = (qn * rs * wn).astype(BF16)
        o_ref[0, HEAD_W * h + LANES:HEAD_W * (h + 1), :] = ((qr * wrc + qx * wxs) * rs).astype(BF16)


def _q_proj(small, nw, wqt, qnw, ct_t, st_t, layer, tm):
    m = small.shape[0]
    n = MLA_HEADS * HEAD_W
    return pl.pallas_call(
        _q_body,
        grid=(m // tm,),
        in_specs=[pl.BlockSpec((tm, Q_LORA), lambda i: (i, 0)),
                  _lspec(layer, (1, Q_LORA)),
                  _lspec(layer, (MLA_HEADS * Q_ACC_W, Q_LORA)),
                  _lspec(layer, (Q_ACC_W, tm)),
                  pl.BlockSpec((LANES, tm), lambda i: (0, i)),
                  pl.BlockSpec((LANES, tm), lambda i: (0, i))],
        out_specs=pl.BlockSpec((1, n, tm), lambda i: (i, 0, 0)),
        out_shape=jax.ShapeDtypeStruct((m // tm, n, tm), BF16),
        compiler_params=_cparams(("parallel",)),
        name="q_proj",
    )(small, nw, wqt, qnw, ct_t, st_t)


def _kv_body(c_ref, kr_ref, nw_ref, wk_ref, wvt_ref, kwn_ref, kwr_ref, ct_ref, st_ref, k_ref, vt_ref):
    cn = _rms(c_ref[...], nw_ref[...]).astype(BF16)
    acc = _dot(cn, wk_ref[...])
    vt_ref[0] = _dot_nt(wvt_ref[...], cn).astype(BF16)
    c = ct_ref[...]
    s = st_ref[...]
    kr = kr_ref[...]
    wn = kwn_ref[...]
    wr = kwr_ref[...]
    ssr = jnp.sum(kr * kr, axis=-1, keepdims=True)
    kr_rot = _rope_apply(kr * wr, c, s)
    for h in range(MLA_HEADS):
        kn = acc[:, LANES * h:LANES * (h + 1)]
        ssq = jnp.sum(kn * kn, axis=-1, keepdims=True) + ssr
        rs = lax.rsqrt(ssq / QK_DIM + EPS)
        k_ref[:, HEAD_W * h:HEAD_W * h + LANES] = (kn * rs * wn).astype(BF16)
        k_ref[:, HEAD_W * h + LANES:HEAD_W * (h + 1)] = (kr_rot * rs).astype(BF16)


def _kv_proj(small, nw, wk, wvt, kwn, kwr, ct, st, layer, tm):
    m = small.shape[0]
    nk = MLA_HEADS * HEAD_W
    nv = MLA_HEADS * V_DIM
    kr_blk = (Q_LORA + KV_LORA) // LANES
    return pl.pallas_call(
        _kv_body,
        grid=(m // tm,),
        in_specs=[pl.BlockSpec((tm, KV_LORA), lambda i: (i, 1)),
                  pl.BlockSpec((tm, LANES), lambda i: (i, kr_blk)),
                  _lspec(layer, (1, KV_LORA)),
                  _lspec(layer, (KV_LORA, MLA_HEADS * QK_NOPE)),
                  _lspec(layer, (nv, KV_LORA)),
                  _lspec(layer, (1, LANES)),
                  _lspec(layer, (1, LANES)),
                  pl.BlockSpec((tm, LANES), lambda i: (i, 0)),
                  pl.BlockSpec((tm, LANES), lambda i: (i, 0))],
        out_specs=[pl.BlockSpec((tm, nk), lambda i: (i, 0)),
                   pl.BlockSpec((1, nv, tm), lambda i: (i, 0, 0))],
        out_shape=[jax.ShapeDtypeStruct((m, nk), BF16), jax.ShapeDtypeStruct((m // tm, nv, tm), BF16)],
        compiler_params=_cparams(("parallel",)),
        name="kv_proj",
    )(small, small, nw, wk, wvt, kwn, kwr, ct, st)


def _attn_body(q_ref, k_ref, vt_ref, o_ref, s0_scr, s1_scr, p0_scr, p1_scr, m_scr, a_scr, acc_scr, *, tq, nq, hb):
    s_scr = (s0_scr, s1_scr)
    p_scr = (p0_scr, p1_scr)
    tk = tq // 2
    key_chunk = lax.broadcasted_iota(jnp.int32, (tk, tq), 0) // CHUNK
    qry_chunk = lax.broadcasted_iota(jnp.int32, (tk, tq), 1) // CHUNK
    diag_masks = [key_chunk + h * (tk // CHUNK) <= qry_chunk for h in range(2)]
    ones_rows = jnp.ones((ACC_ROWS - V_DIM, tk), BF16)

    def scores(qs, kpair, h, mask=None):
        koff = pl.multiple_of(kpair * tq + h * tk, tk)
        for hh in range(hb):
            s = _dot(k_ref[pl.ds(koff, tk), HEAD_W * hh:HEAD_W * (hh + 1)], qs[hh])
            s_scr[h][hh] = s if mask is None else jnp.where(mask, s, -jnp.inf)

    def pv(kpair, h):
        return [_dot(jnp.concatenate([vt_ref[kpair, V_DIM * hh:V_DIM * (hh + 1), h * tk:(h + 1) * tk],
                                      ones_rows], axis=0), p_scr[h][hh])
                for hh in range(hb)]

    def softmax(h, pvs):
        for hh in range(hb):
            m = m_scr[hh]
            s = s_scr[h][hh]
            m_new = jnp.maximum(m, jnp.max(s, axis=0, keepdims=True))
            p_scr[h][hh] = jnp.exp2((s - m_new).astype(BF16))
            alpha = jnp.exp2(m - m_new)
            acc_scr[hh] = alpha * (acc_scr[hh] + a_scr[hh] * pvs[hh])
            a_scr[hh] = alpha
            m_scr[hh] = m_new

    def q_tiles(qi):
        return [q_ref[qi, HEAD_W * hh:HEAD_W * (hh + 1), :] for hh in range(hb)]

    def diag_scores(qi):
        qs = q_tiles(qi)
        for h in range(2):
            scores(qs, qi, h, diag_masks[h])

    diag_scores(0)
    a_scr[...] = jnp.zeros(a_scr.shape, F32)
    acc_scr[...] = jnp.zeros(acc_scr.shape, F32)
    for h in range(2):
        p_scr[h][...] = jnp.zeros(p_scr[h].shape, BF16)

    def q_block(qi, c):
        qoff = pl.multiple_of(qi * tq, tq)
        qs = q_tiles(qi)
        m_scr[...] = jnp.full(m_scr.shape, -jnp.inf, F32)

        def kpair_of(r):
            return jnp.where(r == 0, qi, r - 1)

        def body(r, c2):
            prev = kpair_of(jnp.maximum(r - 1, 0))
            for h in range(2):
                pvs = pv(prev, h)
                softmax(h, pvs)
                scores(qs, r, h)
            return c2

        lax.fori_loop(0, qi, body, 0)
        prev = kpair_of(jnp.maximum(qi - 1, 0))
        nxt = jnp.minimum(qi + 1, nq - 1)
        qn = q_tiles(nxt)
        for h in range(2):
            softmax(h, pv(prev, h))
            scores(qn, nxt, h, diag_masks[h])
        last = kpair_of(qi)
        pv0, pv1 = pv(last, 0), pv(last, 1)
        for hh in range(hb):
            acc = acc_scr[hh] + a_scr[hh] * pv0[hh] + pv1[hh]
            o_ref[pl.ds(qoff, tq), V_DIM * hh:V_DIM * (hh + 1)] = (
                acc[:V_DIM] / acc[V_DIM:V_DIM + 1]).T.astype(BF16)
        return c

    lax.fori_loop(0, nq, q_block, 0)


def _attention(qt, k, vt, batch, seq, tq, hb):
    m = k.shape[0]
    nq = seq // tq
    tk = tq // 2
    assert vt.shape[2] == tq
    return pl.pallas_call(
        functools.partial(_attn_body, tq=tq, nq=nq, hb=hb),
        grid=(batch, MLA_HEADS // hb),
        in_specs=[pl.BlockSpec((nq, hb * HEAD_W, tq), lambda b, h: (b, h, 0)),
                  pl.BlockSpec((seq, hb * HEAD_W), lambda b, h: (b, h)),
                  pl.BlockSpec((nq, hb * V_DIM, tq), lambda b, h: (b, h, 0))],
        out_specs=pl.BlockSpec((seq, hb * V_DIM), lambda b, h: (b, h)),
        out_shape=jax.ShapeDtypeStruct((m, MLA_HEADS * V_DIM), BF16),
        scratch_shapes=[pltpu.VMEM((hb, tk, tq), F32), pltpu.VMEM((hb, tk, tq), F32),
                        pltpu.VMEM((hb, tk, tq), BF16), pltpu.VMEM((hb, tk, tq), BF16),
                        pltpu.VMEM((hb, 1, tq), F32), pltpu.VMEM((hb, 1, tq), F32),
                        pltpu.VMEM((hb, ACC_ROWS, tq), F32)],
        compiler_params=_cparams(("parallel", "arbitrary")),
        name="attention",
    )(qt, k, vt)


def _silu_of_half(h):
    return h + h * jnp.tanh(h)


def _silu(y):
    return _silu_of_half(0.5 * y)


def _split3(v):
    hi = v.astype(BF16)
    r = v - hi.astype(F32)
    mid = r.astype(BF16)
    lo = (r - mid.astype(F32)).astype(BF16)
    return hi, mid, lo


def _ssd_body(z_ref, x_ref, bc_ref, dt_ref, cwx_ref, cwb_ref, cbx_ref, cbb_ref, dtb_ref, alog_ref,
              dsk_ref, nw_ref, sel_ref, eye_ref, tri_ref, tri64_ref, shift_ref, y_ref, state_ref, xhalo, bchalo,
              *, cpb):
    t = CHUNK
    gn = SSM_GROUPS * SSM_STATE

    @pl.when(pl.program_id(1) == 0)
    def _():
        state_ref[...] = jnp.zeros_like(state_ref)
        xhalo[...] = jnp.zeros_like(xhalo)
        bchalo[...] = jnp.zeros_like(bchalo)

    def conv_silu(raw_ref, halo, w_ref, b_ref):
        raws = [raw_ref[c * t:(c + 1) * t, :] for c in range(cpb)]
        prevs = [halo[...]] + raws[:-1]
        halo[...] = raws[-1]
        shifted = [_dot(shift_ref[...], jnp.concatenate([prevs[c], raws[c]], axis=0)) for c in range(cpb)]
        outs = []
        for c in range(cpb):
            y = None
            for tap in range(CONV_WIDTH):
                shift = CONV_WIDTH - 1 - tap
                xt = raws[c].astype(F32) if shift == 0 else shifted[c][(shift - 1) * t:shift * t, :]
                term = xt * w_ref[tap:tap + 1, :]
                y = term if y is None else y + term
            outs.append(_silu_of_half(y + b_ref[...]))
        return outs

    xs_all = conv_silu(x_ref, xhalo, cwx_ref, cbx_ref)
    bc_all = conv_silu(bc_ref, bchalo, cwb_ref, cbb_ref)

    head_lane = lax.broadcasted_iota(jnp.int32, (t, LANES), 1) < SSM_HEADS
    neg_a = -jnp.exp(alog_ref[...])
    tri64 = tri64_ref[...]
    stacks = []
    for c in range(cpb):
        dt = jnp.where(head_lane, jax.nn.softplus(dt_ref[c * t:(c + 1) * t, :] + dtb_ref[...]), 0.0)
        h3 = _split3(dt * neg_a)
        a_cum = (_dot(tri64, h3[0]) + _dot(tri64, h3[1]) + _dot(tri64, h3[2])) * LOG2_E
        stacks += [a_cum, dt]
    s3 = [part.astype(F32) for part in _split3(jnp.concatenate(stacks, axis=0))]
    packed = jnp.concatenate([(s3[0] + pltpu.roll(s3[1], SSM_HEADS, 1)).astype(BF16), s3[2].astype(BF16)], axis=1)
    ex_all = _dot(packed, sel_ref[...])
    left = lax.broadcasted_iota(jnp.int32, (t, LANES), 1) < SSM_HEADDIM
    zero_b = jnp.zeros((t, LANES), BF16)

    pre = []
    for c in range(cpb):
        ex = ex_all[2 * t * c:2 * t * (c + 1)]
        a_col = ex[:t]
        ea = jnp.exp2(a_col)
        a_row = jnp.sum(jnp.where(eye_ref[...] > 0, a_col, 0.0), axis=0, keepdims=True)
        decay = jnp.exp2(a_col - a_row + tri_ref[...])
        xdt = xs_all[c] * ex[t:2 * t]
        pre.append((decay, ea, xdt.astype(BF16), (xdt * jnp.exp2(a_col[t - 1:t, :] - a_col)).astype(BF16)))

    for c in range(cpb):
        decay, ea, xdt_b, xw_b = pre[c]
        xs, bc = xs_all[c], bc_all[c]
        chunk_decay = ea[t - 1:t, :]
        r0 = c * t
        for g in range(SSM_GROUPS):
            lo, hi = GROUP_W * g, GROUP_W * (g + 1)
            bg = bc[:, SSM_STATE * g:SSM_STATE * (g + 1)]
            cg_b = bc[:, gn + SSM_STATE * g:gn + SSM_STATE * (g + 1)].astype(BF16)
            bg_b = bg.astype(BF16)
            cb2 = _dot_nt(cg_b, jnp.concatenate([bg_b, bg_b], axis=0))
            yd = []
            for j in range(GROUP_W // LANES):
                l0 = lo + LANES * j
                mm = (cb2 * decay[:, l0:l0 + LANES]).astype(BF16)
                xp = xdt_b[:, l0:l0 + LANES]
                rhs = jnp.concatenate([jnp.where(left, xp, zero_b), jnp.where(left, zero_b, xp)], axis=0)
                yd.append(_dot(mm, rhs))
            y_diag = jnp.concatenate(yd, axis=1)
            st = state_ref[g]
            y_off = _dot(cg_b, st.astype(BF16)) * ea[:, lo:hi]
            inc = _dot(bg.T.astype(BF16), xw_b[:, lo:hi])
            state_ref[g] = st * chunk_decay[:, lo:hi] + inc
            yg = y_diag + y_off + dsk_ref[:, lo:hi] * xs[:, lo:hi]
            yg = yg * _silu(z_ref[r0:r0 + t, lo:hi].astype(F32))
            ms = jnp.mean(yg * yg, axis=-1, keepdims=True)
            y_ref[r0:r0 + t, lo:hi] = (yg * lax.rsqrt(ms + EPS) * nw_ref[:, lo:hi]).astype(BF16)


def _ssd(big, small, cwx, cwb, cbx, cbb, dtb, alog, dsk, nw, sel, eye_t, tri_t, tri64, shift_m, layer, batch, seq,
         cpb):
    m = big.shape[0]
    t = CHUNK
    tb = cpb * t
    nb = seq // tb
    bcw = 2 * SSM_GROUPS * SSM_STATE
    bc_blk = (2 * D_INNER) // bcw
    dt_blk = (SMALL_N - LANES) // LANES
    row = lambda b, c: b * nb + c
    const = lambda shape: pl.BlockSpec(shape, lambda b, c: (0,) * len(shape))
    return pl.pallas_call(
        functools.partial(_ssd_body, cpb=cpb),
        grid=(batch, nb),
        in_specs=[pl.BlockSpec((tb, D_INNER), lambda b, c: (row(b, c), 0)),
                  pl.BlockSpec((tb, D_INNER), lambda b, c: (row(b, c), 1)),
                  pl.BlockSpec((tb, bcw), lambda b, c: (row(b, c), bc_blk)),
                  pl.BlockSpec((tb, LANES), lambda b, c: (row(b, c), dt_blk)),
                  _lspec(layer, (CONV_WIDTH, D_INNER)), _lspec(layer, (CONV_WIDTH, bcw)),
                  _lspec(layer, (1, D_INNER)), _lspec(layer, (1, bcw)),
                  _lspec(layer, (1, LANES)), _lspec(layer, (1, LANES)),
                  _lspec(layer, (1, D_INNER)), _lspec(layer, (1, D_INNER)),
                  const((2 * LANES, D_INNER)), const((t, D_INNER)), const((t, D_INNER)), const((t, t)),
                  const(((CONV_WIDTH - 1) * t, 2 * t))],
        out_specs=pl.BlockSpec((tb, D_INNER), lambda b, c: (row(b, c), 0)),
        out_shape=jax.ShapeDtypeStruct((m, D_INNER), BF16),
        scratch_shapes=[pltpu.VMEM((SSM_GROUPS, SSM_STATE, GROUP_W), F32),
                        pltpu.VMEM((t, D_INNER), BF16),
                        pltpu.VMEM((t, bcw), BF16)],
        compiler_params=_cparams(("parallel", "arbitrary")),
        name="ssd",
    )(big, big, big, small, cwx, cwb, cbx, cbb, dtb, alog, dsk, nw, sel, eye_t, tri_t, tri64, shift_m)


def _merge_body(o_ref, y_ref, wa_ref, wb_ref, g0_ref, g1_ref, out_ref):
    ya = _dot(o_ref[...], wa_ref[...])
    yb = _dot(y_ref[...], wb_ref[...])
    g0 = jax.nn.sigmoid(g0_ref[...].astype(F32))
    g1 = jax.nn.sigmoid(g1_ref[...].astype(F32))
    out_ref[...] = (g0 * ya + g1 * yb).astype(out_ref.dtype)


def _merge(o, y, wa, wb, big, layer, tm, tn):
    m = o.shape[0]
    g0_blk = (2 * D_INNER + 2 * SSM_GROUPS * SSM_STATE) // tn
    g1_blk = g0_blk + D_MODEL // tn
    return pl.pallas_call(
        _merge_body,
        grid=(m // tm, D_MODEL // tn),
        in_specs=[pl.BlockSpec((tm, o.shape[1]), lambda i, j: (i, 0)),
                  pl.BlockSpec((tm, y.shape[1]), lambda i, j: (i, 0)),
                  _lspec(layer, (wa.shape[1], tn), lambda i, j: (0, j)),
                  _lspec(layer, (wb.shape[1], tn), lambda i, j: (0, j)),
                  pl.BlockSpec((tm, tn), lambda i, j: (i, g0_blk + j)),
                  pl.BlockSpec((tm, tn), lambda i, j: (i, g1_blk + j))],
        out_specs=pl.BlockSpec((tm, tn), lambda i, j: (i, j)),
        out_shape=jax.ShapeDtypeStruct((m, D_MODEL), BF16),
        compiler_params=_cparams(("parallel", "arbitrary")),
        name="merge",
    )(o, y, wa, wb, big, big)


def _res_mm_body(x_ref, a_ref, w_ref, o_ref):
    o_ref[...] = x_ref[...] + _dot(a_ref[...], w_ref[...])


def _res_mm_f32w_body(x_ref, a_ref, w_ref, o_ref, wb_ref):
    @pl.when(pl.program_id(0) == 0)
    def _():
        wb_ref[...] = w_ref[...].astype(BF16)

    o_ref[...] = x_ref[...] + _dot(a_ref[...], wb_ref[...])


def _res_matmul(x, a, w, layer, tm):
    m, n = x.shape
    k = a.shape[1]
    return pl.pallas_call(
        _res_mm_f32w_body,
        grid=(m // tm,),
        in_specs=[pl.BlockSpec((tm, n), lambda i: (i, 0)),
                  pl.BlockSpec((tm, k), lambda i: (i, 0)),
                  _lspec(layer, (k, n), single=True)],
        out_specs=pl.BlockSpec((tm, n), lambda i: (i, 0)),
        out_shape=jax.ShapeDtypeStruct((m, n), F32),
        scratch_shapes=[pltpu.VMEM((k, n), BF16)],
        compiler_params=_cparams(("arbitrary",)),
        name="out_proj",
    )(x, a, w)


def _res_matmul_tiled(x, a, w, layer, tm, tn, name):
    m, n = x.shape
    k = a.shape[1]
    return pl.pallas_call(
        _res_mm_body,
        grid=(m // tm, n // tn),
        in_specs=[pl.BlockSpec((tm, tn), lambda i, j: (i, j)),
                  pl.BlockSpec((tm, k), lambda i, j: (i, 0)),
                  _lspec(layer, (k, tn), lambda i, j: (0, j))],
        out_specs=pl.BlockSpec((tm, tn), lambda i, j: (i, j)),
        out_shape=jax.ShapeDtypeStruct((m, n), F32),
        compiler_params=_cparams(("parallel", "arbitrary")),
        name=name,
    )(x, a, w)


def _ple_body(x_ref, nw_ref, wg_ref, p_ref, wp_ref, o_ref, wgb_ref):
    @pl.when(pl.program_id(0) == 0)
    def _():
        wgb_ref[...] = wg_ref[...].astype(BF16)

    x = x_ref[...]
    h = _rms(x, nw_ref[...]).astype(BF16)
    gate = jax.nn.sigmoid(_dot(h, wgb_ref[...]))
    pe = _dot(p_ref[...].astype(BF16), wp_ref[...])
    o_ref[...] = x + pe * gate


def _ple(x, nw, wg, p, wp, layer, tm):
    m, d = x.shape
    return pl.pallas_call(
        _ple_body,
        grid=(m // tm,),
        in_specs=[pl.BlockSpec((tm, d), lambda i: (i, 0)),
                  _lspec(layer, (1, d)),
                  _lspec(layer, (d, d), single=True),
                  _lspec(layer, (tm, PLE_DIM), lambda i: (i, 0)),
                  _lspec(layer, (PLE_DIM, d))],
        out_specs=pl.BlockSpec((tm, d), lambda i: (i, 0)),
        out_shape=jax.ShapeDtypeStruct((m, d), F32),
        scratch_shapes=[pltpu.VMEM((d, d), BF16)],
        compiler_params=_cparams(("arbitrary",)),
        name="ple",
    )(x, nw, wg, p, wp)


def _rope_cols(w):
    half = QK_ROPE // 2
    z = jnp.zeros(w.shape[:-1] + (LANES // 2 - half,), w.dtype)
    return jnp.concatenate([w[..., :half], z, w[..., half:], z], axis=-1)


def kernel(x, p, positions, norm_mix_w, w_in, q_a_norm_w, w_uq, kv_a_norm_w, w_ukv, q_norm_w, k_norm_w,
           w_o_mla, conv_w, conv_b, dt_bias, a_log, d_skip, ssm_norm_w, w_o_ssm, w_out, norm_mlp_w, w_up,
           w_down, ple_norm_w, w_ple_gate, w_ple):
    batch, seq, d = x.shape
    m = batch * seq
    depth = w_in.shape[0]
    tm_big = min(1024, m)
    tm_mid = min(512, m)
    tm_small = min(256, m)
    tq = min(512, seq)

    inv_freq = 1.0 / (ROPE_THETA ** (jnp.arange(0, QK_ROPE, 2, dtype=F32) / QK_ROPE))
    freq = _rope_cols(jnp.concatenate([inv_freq, inv_freq]))[None, :]
    sign = _rope_cols(jnp.concatenate([-jnp.ones_like(inv_freq), jnp.ones_like(inv_freq)]))[None, :]
    ct, st = _rope_tables(positions.reshape(m, 1), freq, sign, tm_big)
    ct_t, st_t = ct.T, st.T

    lane = jnp.arange(D_INNER)
    sel_row = jnp.arange(2 * LANES)[:, None]
    sel = ((sel_row % SSM_HEADS == (lane // SSM_HEADDIM)[None, :]) & (sel_row < 3 * SSM_HEADS)).astype(BF16)
    rows = jnp.arange(CHUNK)[:, None]
    eye_t = (rows == (lane % SSM_HEADDIM)[None, :]).astype(F32)
    tri_t = jnp.where(rows >= (lane % SSM_HEADDIM)[None, :], 0.0, -jnp.inf).astype(F32)
    tri64 = (rows >= jnp.arange(CHUNK)[None, :]).astype(BF16)
    sh_rows = jnp.arange((CONV_WIDTH - 1) * CHUNK)[:, None]
    shift_m = (jnp.arange(2 * CHUNK)[None, :] == CHUNK + sh_rows % CHUNK - (sh_rows // CHUNK + 1)).astype(BF16)

    s0 = Q_LORA + KV_LORA + QK_ROPE
    s1 = s0 + D_INNER + (D_INNER + 2 * SSM_GROUPS * SSM_STATE)
    s2 = s1 + SSM_HEADS

    w_lat, w_kr, w_dt = lax.optimization_barrier(
        (w_in[..., :Q_LORA + KV_LORA], w_in[..., Q_LORA + KV_LORA:s0], w_in[..., s1:s2]))
    w_small = jnp.concatenate(
        [w_lat, _rope_cols(w_kr), w_dt, jnp.zeros((depth, d, LANES - SSM_HEADS), F32)], axis=-1).astype(BF16)
    w_big = _shift_cast(w_in, (s0 - QK_ROPE, s2 - LANES), (QK_ROPE, LANES), (s1 - s0, 2 * D_MODEL), 512)
    wq4 = w_uq.reshape(depth, Q_LORA, MLA_HEADS, QK_DIM)
    wq_rope = _rope_cols(wq4[..., QK_NOPE:])
    wq = jnp.concatenate([wq4[..., :QK_NOPE], wq_rope, jnp.roll(wq_rope, LANES // 2, axis=-1)], axis=-1)
    wqt = jnp.swapaxes(wq.reshape(depth, Q_LORA, MLA_HEADS * Q_ACC_W), 1, 2).astype(BF16)
    wkv4 = w_ukv.reshape(depth, KV_LORA, MLA_HEADS, QK_NOPE + V_DIM)
    wk = wkv4[..., :QK_NOPE].reshape(depth, KV_LORA, -1).astype(BF16)
    wvt = jnp.swapaxes(wkv4[..., QK_NOPE:].reshape(depth, KV_LORA, -1), 1, 2).astype(BF16)
    w_o_mla_b = w_o_mla.astype(BF16)
    w_o_ssm_b = w_o_ssm.astype(BF16)
    w_down_b = w_down.astype(BF16)
    w_ple_b = w_ple.astype(BF16)

    rows3 = lambda v: v.reshape(depth, 1, -1).astype(F32)
    pad_h = jnp.zeros((depth, LANES - SSM_HEADS), F32)
    qw_rope = _rope_cols(q_norm_w[:, QK_NOPE:])
    qnw = jnp.concatenate([q_norm_w[:, :QK_NOPE], qw_rope, jnp.roll(qw_rope, LANES // 2, axis=-1)], axis=-1)
    qnw = jnp.broadcast_to(qnw[:, :, None], (depth, Q_ACC_W, tq)).astype(F32)
    kwn = rows3(k_norm_w[:, :QK_NOPE])
    kwr = rows3(_rope_cols(k_norm_w[:, QK_NOPE:]))
    dtb = rows3(jnp.concatenate([dt_bias, pad_h], axis=-1))
    alog = rows3(jnp.concatenate([a_log, pad_h], axis=-1))
    dsk = rows3(jnp.repeat(d_skip, SSM_HEADDIM, axis=-1))
    mix_nw, qa_nw, kva_nw = rows3(norm_mix_w), rows3(q_a_norm_w), rows3(kv_a_norm_w)
    ssm_nw, mlp_nw, ple_nw = rows3(ssm_norm_w), rows3(norm_mlp_w), rows3(ple_norm_w)
    conv_wh, conv_bh = 0.5 * conv_w, 0.5 * conv_b
    cwx, cwb = conv_wh[..., :D_INNER], conv_wh[..., D_INNER:]
    cbx, cbb = rows3(conv_bh[:, :D_INNER]), rows3(conv_bh[:, D_INNER:])
    p3 = p.reshape(depth, m, PLE_DIM)

    xf = x.reshape(m, d)
    for i in range(depth):
        small = _norm_matmul(xf, mix_nw, w_small, i, F32, tm_mid, SMALL_N, "in_proj_small")
        big = _norm_matmul(xf, mix_nw, w_big, i, BF16, tm_big, 1024, "in_proj_big")
        qt = _q_proj(small, qa_nw, wqt, qnw, ct_t, st_t, i, tq)
        k, vt = _kv_proj(small, kva_nw, wk, wvt, kwn, kwr, ct, st, i, tq)
        o = _attention(qt, k, vt, batch, seq, tq, 2)
        y = _ssd(big, small, cwx, cwb, cbx, cbb, dtb, alog, dsk, ssm_nw, sel, eye_t, tri_t, tri64, shift_m, i,
                 batch, seq, 4)
        merged = _merge(o, y, w_o_mla_b, w_o_ssm_b, big, i, tm_big, 512)
        xf = _res_matmul(xf, merged, w_out, i, tm_mid)
        hidden = _norm_matmul(xf, mlp_nw, w_up, i, BF16, min(2048, m), 512, "mlp_up", relu2=True, single_x=True)
        xf = _res_matmul_tiled(xf, hidden, w_down_b, i, tm_mid, 512, "mlp_down")
        xf = _ple(xf, ple_nw, w_ple_gate, p3, w_ple_b, i, tm_mid)
    return xf.reshape(batch, seq, d)
```

```python
import functools

import jax
import jax.numpy as jnp
from jax import lax
from jax.experimental import pallas as pl
from jax.experimental.pallas import tpu as pltpu

F32 = jnp.float32
BF16 = jnp.bfloat16

D_MODEL = 2048
DEPTH = 4
CHUNK = 64
PLE_DIM = 256
EPS = 1e-6
MLA_HEADS = 16
Q_LORA = 512
KV_LORA = 512
QK_NOPE = 128
QK_ROPE = 64
V_DIM = 128
QK_DIM = QK_NOPE + QK_ROPE
ROPE_THETA = 10000.0
D_INNER = 2 * D_MODEL
SSM_HEADDIM = 64
SSM_HEADS = D_INNER // SSM_HEADDIM
SSM_GROUPS = 8
SSM_STATE = 128
GROUP_W = D_INNER // SSM_GROUPS
CONV_WIDTH = 4
D_FF = 4 * D_MODEL

LOG2_E = 1.4426950408889634
LANES = 128
ACC_ROWS = V_DIM + 16
Q_ACC_W = 3 * LANES
HEAD_W = 2 * LANES
SMALL_N = Q_LORA + KV_LORA + 2 * LANES
VMEM_LIMIT = 56 * 1024 * 1024


def _cparams(sem):
    return pltpu.CompilerParams(dimension_semantics=sem, vmem_limit_bytes=VMEM_LIMIT)


def _rms(xf, w):
    ms = jnp.mean(xf * xf, axis=-1, keepdims=True)
    return xf * lax.rsqrt(ms + EPS) * w


def _lspec(layer, block, imap=None, single=False):
    if imap is None:
        imap = lambda *g: (0,) * len(block)
    return pl.BlockSpec((None,) + tuple(block), lambda *g: (layer,) + tuple(imap(*g)),
                        pipeline_mode=pl.Buffered(1) if single else None)


def _dot(a, b):
    return jnp.dot(a, b, preferred_element_type=F32)


def _dot_nt(a, b):
    return lax.dot_general(a, b, (((1,), (1,)), ((), ())), preferred_element_type=F32)


def _rope_body(pos_ref, freq_ref, sign_ref, ct_ref, st_ref):
    ang = pos_ref[...].astype(F32) * freq_ref[...]
    ct_ref[...] = jnp.cos(ang)
    st_ref[...] = jnp.sin(ang) * sign_ref[...]


def _rope_tables(pos_col, freq, sign, tm):
    m = pos_col.shape[0]
    return pl.pallas_call(
        _rope_body,
        grid=(m // tm,),
        in_specs=[pl.BlockSpec((tm, 1), lambda i: (i, 0)),
                  pl.BlockSpec((1, LANES), lambda i: (0, 0)),
                  pl.BlockSpec((1, LANES), lambda i: (0, 0))],
        out_specs=[pl.BlockSpec((tm, LANES), lambda i: (i, 0)),
                   pl.BlockSpec((tm, LANES), lambda i: (i, 0))],
        out_shape=[jax.ShapeDtypeStruct((m, LANES), F32)] * 2,
        compiler_params=_cparams(("parallel",)),
        name="rope_tables",
    )(pos_col, freq, sign)


def _norm_mm_body(x_ref, nw_ref, w_ref, o_ref, h_ref, *, relu2):
    @pl.when(pl.program_id(1) == 0)
    def _():
        h_ref[...] = _rms(x_ref[...], nw_ref[...]).astype(BF16)

    y = _dot(h_ref[...], w_ref[...].astype(BF16))
    if relu2:
        y = jnp.square(jnp.maximum(y, 0.0))
    o_ref[...] = y.astype(o_ref.dtype)


def _norm_matmul(x, nw, w, layer, out_dtype, tm, tn, name, relu2=False, single_x=False):
    m, k = x.shape
    n = w.shape[2]
    return pl.pallas_call(
        functools.partial(_norm_mm_body, relu2=relu2),
        grid=(m // tm, n // tn),
        in_specs=[pl.BlockSpec((tm, k), lambda i, j: (i, 0), pipeline_mode=pl.Buffered(1) if single_x else None),
                  _lspec(layer, (1, k)),
                  _lspec(layer, (k, tn), lambda i, j: (0, j))],
        out_specs=pl.BlockSpec((tm, tn), lambda i, j: (i, j)),
        out_shape=jax.ShapeDtypeStruct((m, n), out_dtype),
        scratch_shapes=[pltpu.VMEM((tm, k), BF16)],
        compiler_params=_cparams(("parallel", "arbitrary")),
        name=name,
    )(x, nw, w)


def _shift_cast_body(a_ref, b_ref, o_ref, *, tiles0, shifts):
    v = jnp.concatenate([a_ref[...], b_ref[...]], axis=1)
    tn = o_ref.shape[1]
    first = pl.program_id(1) < tiles0

    @pl.when(first)
    def _():
        o_ref[...] = v[:, shifts[0]:shifts[0] + tn].astype(BF16)

    @pl.when(jnp.logical_not(first))
    def _():
        o_ref[...] = v[:, shifts[1]:shifts[1] + tn].astype(BF16)


def _shift_cast(w, bases, shifts, widths, tn):
    depth, k, _ = w.shape
    assert all(b % tn == 0 and n % tn == 0 and 0 < s <= LANES for b, s, n in zip(bases, shifts, widths))
    tiles0 = widths[0] // tn
    sub = tn // LANES

    def a_col(l, j):
        return jnp.where(j < tiles0, bases[0] // tn + j, bases[1] // tn + j - tiles0)

    def b_col(l, j):
        return jnp.where(j < tiles0, (bases[0] + tn) // LANES + j * sub, (bases[1] + tn) // LANES + (j - tiles0) * sub)

    return pl.pallas_call(
        functools.partial(_shift_cast_body, tiles0=tiles0, shifts=tuple(shifts)),
        grid=(depth, (widths[0] + widths[1]) // tn),
        in_specs=[pl.BlockSpec((None, k, tn), lambda l, j: (l, 0, a_col(l, j))),
                  pl.BlockSpec((None, k, LANES), lambda l, j: (l, 0, b_col(l, j)))],
        out_specs=pl.BlockSpec((None, k, tn), lambda l, j: (l, 0, j)),
        out_shape=jax.ShapeDtypeStruct((depth, k, widths[0] + widths[1]), BF16),
        compiler_params=_cparams(("parallel", "parallel")),
        name="relayout_w_in",
    )(w, w)


def _rope_apply(v, c, s):
    return v * c + pltpu.roll(v, 64, 1) * s


def _q_body(c_ref, nw_ref, wt_ref, qnw_ref, ct_ref, st_ref, o_ref):
    cn = _rms(c_ref[...], nw_ref[...]).astype(BF16)
    acc = _dot_nt(wt_ref[...], cn)
    wn = qnw_ref[:LANES, :]
    wrc = qnw_ref[LANES:2 * LANES, :] * ct_ref[...]
    wxs = qnw_ref[2 * LANES:, :] * st_ref[...]
    scale = QK_DIM ** -0.5 * LOG2_E
    for h in range(MLA_HEADS):
        qn = acc[Q_ACC_W * h:Q_ACC_W * h + LANES, :]
        qr = acc[Q_ACC_W * h + LANES:Q_ACC_W * h + 2 * LANES, :]
        qx = acc[Q_ACC_W * h + 2 * LANES:Q_ACC_W * (h + 1), :]
        ssq = jnp.sum(qn * qn + qr * qr, axis=0, keepdims=True)
        rs = lax.rsqrt(ssq / QK_DIM + EPS) * scale
        o_ref[0, HEAD_W * h:HEAD_W * h + LANES, :] = (qn * rs * wn).astype(BF16)
        o_ref[0, HEAD_W * h + LANES:HEAD_W * (h + 1), :] = ((qr * wrc + qx * wxs) * rs).astype(BF16)


def _q_proj(small, nw, wqt, qnw, ct_t, st_t, layer, tm):
    m = small.shape[0]
    n = MLA_HEADS * HEAD_W
    return pl.pallas_call(
        _q_body,
        grid=(m // tm,),
        in_specs=[pl.BlockSpec((tm, Q_LORA), lambda i: (i, 0)),
                  _lspec(layer, (1, Q_LORA)),
                  _lspec(layer, (MLA_HEADS * Q_ACC_W, Q_LORA)),
                  _lspec(layer, (Q_ACC_W, tm)),
                  pl.BlockSpec((LANES, tm), lambda i: (0, i)),
                  pl.BlockSpec((LANES, tm), lambda i: (0, i))],
        out_specs=pl.BlockSpec((1, n, tm), lambda i: (i, 0, 0)),
        out_shape=jax.ShapeDtypeStruct((m // tm, n, tm), BF16),
        compiler_params=_cparams(("parallel",)),
        name="q_proj",
    )(small, nw, wqt, qnw, ct_t, st_t)


def _kv_body(c_ref, kr_ref, nw_ref, wk_ref, wvt_ref, kwn_ref, kwr_ref, ct_ref, st_ref, k_ref, vt_ref):
    cn = _rms(c_ref[...], nw_ref[...]).astype(BF16)
    acc = _dot(cn, wk_ref[...])
    vt_ref[0] = _dot_nt(wvt_ref[...], cn).astype(BF16)
    c = ct_ref[...]
    s = st_ref[...]
    kr = kr_ref[...]
    wn = kwn_ref[...]
    wr = kwr_ref[...]
    ssr = jnp.sum(kr * kr, axis=-1, keepdims=True)
    kr_rot = _rope_apply(kr * wr, c, s)
    for h in range(MLA_HEADS):
        kn = acc[:, LANES * h:LANES * (h + 1)]
        ssq = jnp.sum(kn * kn, axis=-1, keepdims=True) + ssr
        rs = lax.rsqrt(ssq / QK_DIM + EPS)
        k_ref[:, HEAD_W * h:HEAD_W * h + LANES] = (kn * rs * wn).astype(BF16)
        k_ref[:, HEAD_W * h + LANES:HEAD_W * (h + 1)] = (kr_rot * rs).astype(BF16)


def _kv_proj(small, nw, wk, wvt, kwn, kwr, ct, st, layer, tm):
    m = small.shape[0]
    nk = MLA_HEADS * HEAD_W
    nv = MLA_HEADS * V_DIM
    kr_blk = (Q_LORA + KV_LORA) // LANES
    return pl.pallas_call(
        _kv_body,
        grid=(m // tm,),
        in_specs=[pl.BlockSpec((tm, KV_LORA), lambda i: (i, 1)),
                  pl.BlockSpec((tm, LANES), lambda i: (i, kr_blk)),
                  _lspec(layer, (1, KV_LORA)),
                  _lspec(layer, (KV_LORA, MLA_HEADS * QK_NOPE)),
                  _lspec(layer, (nv, KV_LORA)),
                  _lspec(layer, (1, LANES)),
                  _lspec(layer, (1, LANES)),
                  pl.BlockSpec((tm, LANES), lambda i: (i, 0)),
                  pl.BlockSpec((tm, LANES), lambda i: (i, 0))],
        out_specs=[pl.BlockSpec((tm, nk), lambda i: (i, 0)),
                   pl.BlockSpec((1, nv, tm), lambda i: (i, 0, 0))],
        out_shape=[jax.ShapeDtypeStruct((m, nk), BF16), jax.ShapeDtypeStruct((m // tm, nv, tm), BF16)],
        compiler_params=_cparams(("parallel",)),
        name="kv_proj",
    )(small, small, nw, wk, wvt, kwn, kwr, ct, st)


def _attn_body(q_ref, k_ref, vt_ref, o_ref, s0_scr, s1_scr, p0_scr, p1_scr, m_scr, a_scr, acc_scr, *, tq, nq, hb, qsub):
    s_scr = (s0_scr, s1_scr)
    p_scr = (p0_scr, p1_scr)
    tk = tq // 2
    key_chunk = lax.broadcasted_iota(jnp.int32, (tk, tq), 0) // CHUNK
    qry_chunk = lax.broadcasted_iota(jnp.int32, (tk, tq), 1) // CHUNK
    diag_masks = [key_chunk + h * (tk // CHUNK) <= qry_chunk for h in range(2)]
    ones_rows = jnp.ones((ACC_ROWS - V_DIM, tk), BF16)

    def scores(qs, kpair, h, mask=None):
        koff = pl.multiple_of(kpair * tq + h * tk, tk)
        for hh in range(hb):
            s = _dot(k_ref[pl.ds(koff, tk), HEAD_W * hh:HEAD_W * (hh + 1)], qs[hh])
            s_scr[h][hh] = s if mask is None else jnp.where(mask, s, -jnp.inf)

    def pv(kpair, h):
        return [_dot(jnp.concatenate([vt_ref[kpair, V_DIM * hh:V_DIM * (hh + 1), h * tk:(h + 1) * tk],
                                      ones_rows], axis=0), p_scr[h][hh])
                for hh in range(hb)]

    def softmax(h, pvs):
        for hh in range(hb):
            m = m_scr[hh]
            s = s_scr[h][hh]
            m_new = jnp.maximum(m, jnp.max(s, axis=0, keepdims=True))
            p_scr[h][hh] = jnp.exp2((s - m_new).astype(BF16))
            alpha = jnp.exp2(m - m_new)
            acc_scr[hh] = alpha * (acc_scr[hh] + a_scr[hh] * pvs[hh])
            a_scr[hh] = alpha
            m_scr[hh] = m_new

    def q_tiles(qi):
        return [jnp.concatenate([q_ref[qsub * qi + u, HEAD_W * hh:HEAD_W * (hh + 1), :] for u in range(qsub)], axis=1)
                for hh in range(hb)]

    def diag_scores(qi):
        qs = q_tiles(qi)
        for h in range(2):
            scores(qs, qi, h, diag_masks[h])

    diag_scores(0)
    a_scr[...] = jnp.zeros(a_scr.shape, F32)
    acc_scr[...] = jnp.zeros(acc_scr.shape, F32)
    for h in range(2):
        p_scr[h][...] = jnp.zeros(p_scr[h].shape, BF16)

    def q_block(qi, c):
        qoff = pl.multiple_of(qi * tq, tq)
        qs = q_tiles(qi)
        m_scr[...] = jnp.full(m_scr.shape, -jnp.inf, F32)

        def kpair_of(r):
            return jnp.where(r == 0, qi, r - 1)

        def body(r, c2):
            prev = kpair_of(jnp.maximum(r - 1, 0))
            for h in range(2):
                pvs = pv(prev, h)
                softmax(h, pvs)
                scores(qs, r, h)
            return c2

        lax.fori_loop(0, qi, body, 0)
        prev = kpair_of(jnp.maximum(qi - 1, 0))
        nxt = jnp.minimum(qi + 1, nq - 1)
        qn = q_tiles(nxt)
        for h in range(2):
            softmax(h, pv(prev, h))
            scores(qn, nxt, h, diag_masks[h])
        last = kpair_of(qi)
        pv0, pv1 = pv(last, 0), pv(last, 1)
        for hh in range(hb):
            acc = acc_scr[hh] + a_scr[hh] * pv0[hh] + pv1[hh]
            o_ref[pl.ds(qoff, tq), V_DIM * hh:V_DIM * (hh + 1)] = (
                acc[:V_DIM] / acc[V_DIM:V_DIM + 1]).T.astype(BF16)
        return c

    lax.fori_loop(0, nq, q_block, 0)


def _attention(qt, k, vt, batch, seq, tq, hb):
    m = k.shape[0]
    nq = seq // tq
    tk = tq // 2
    assert vt.shape[2] == tq
    qsub = tq // qt.shape[2]
    return pl.pallas_call(
        functools.partial(_attn_body, tq=tq, nq=nq, hb=hb, qsub=qsub),
        grid=(batch, MLA_HEADS // hb),
        in_specs=[pl.BlockSpec((nq * qsub, hb * HEAD_W, tq // qsub), lambda b, h: (b, h, 0)),
                  pl.BlockSpec((seq, hb * HEAD_W), lambda b, h: (b, h)),
                  pl.BlockSpec((nq, hb * V_DIM, tq), lambda b, h: (b, h, 0))],
        out_specs=pl.BlockSpec((seq, hb * V_DIM), lambda b, h: (b, h)),
        out_shape=jax.ShapeDtypeStruct((m, MLA_HEADS * V_DIM), BF16),
        scratch_shapes=[pltpu.VMEM((hb, tk, tq), F32), pltpu.VMEM((hb, tk, tq), F32),
                        pltpu.VMEM((hb, tk, tq), BF16), pltpu.VMEM((hb, tk, tq), BF16),
                        pltpu.VMEM((hb, 1, tq), F32), pltpu.VMEM((hb, 1, tq), F32),
                        pltpu.VMEM((hb, ACC_ROWS, tq), F32)],
        compiler_params=_cparams(("parallel", "arbitrary")),
        name="attention",
    )(qt, k, vt)


def _silu_of_half(h):
    return h + h * jnp.tanh(h)


def _silu(y):
    return _silu_of_half(0.5 * y)


def _split3(v):
    hi = v.astype(BF16)
    r = v - hi.astype(F32)
    mid = r.astype(BF16)
    lo = (r - mid.astype(F32)).astype(BF16)
    return hi, mid, lo


def _ssd_body(z_ref, x_ref, bc_ref, dt_ref, cwx_ref, cwb_ref, cbx_ref, cbb_ref, dtb_ref, alog_ref,
              dsk_ref, nw_ref, sel_ref, eye_ref, tri_ref, tri64_ref, shift_ref, y_ref, state_ref, xhalo, bchalo,
              *, cpb):
    t = CHUNK
    gn = SSM_GROUPS * SSM_STATE

    @pl.when(pl.program_id(1) == 0)
    def _():
        state_ref[...] = jnp.zeros_like(state_ref)
        xhalo[...] = jnp.zeros_like(xhalo)
        bchalo[...] = jnp.zeros_like(bchalo)

    def conv_silu(raw_ref, halo, w_ref, b_ref):
        raws = [raw_ref[c * t:(c + 1) * t, :] for c in range(cpb)]
        prevs = [halo[...]] + raws[:-1]
        halo[...] = raws[-1]
        shifted = [_dot(shift_ref[...], jnp.concatenate([prevs[c], raws[c]], axis=0)) for c in range(cpb)]
        outs = []
        for c in range(cpb):
            y = None
            for tap in range(CONV_WIDTH):
                shift = CONV_WIDTH - 1 - tap
                xt = raws[c].astype(F32) if shift == 0 else shifted[c][(shift - 1) * t:shift * t, :]
                term = xt * w_ref[tap:tap + 1, :]
                y = term if y is None else y + term
            outs.append(_silu_of_half(y + b_ref[...]))
        return outs

    xs_all = conv_silu(x_ref, xhalo, cwx_ref, cbx_ref)
    bc_all = conv_silu(bc_ref, bchalo, cwb_ref, cbb_ref)

    head_lane = lax.broadcasted_iota(jnp.int32, (t, LANES), 1) < SSM_HEADS
    neg_a = -jnp.exp(alog_ref[...])
    tri64 = tri64_ref[...]
    stacks = []
    for c in range(cpb):
        dt = jnp.where(head_lane, jax.nn.softplus(dt_ref[c * t:(c + 1) * t, :] + dtb_ref[...]), 0.0)
        h3 = _split3(dt * neg_a)
        a_cum = (_dot(tri64, h3[0]) + _dot(tri64, h3[1]) + _dot(tri64, h3[2])) * LOG2_E
        stacks += [a_cum, dt]
    s3 = [part.astype(F32) for part in _split3(jnp.concatenate(stacks, axis=0))]
    packed = jnp.concatenate([(s3[0] + pltpu.roll(s3[1], SSM_HEADS, 1)).astype(BF16), s3[2].astype(BF16)], axis=1)
    ex_all = _dot(packed, sel_ref[...])
    left = lax.broadcasted_iota(jnp.int32, (t, LANES), 1) < SSM_HEADDIM
    zero_b = jnp.zeros((t, LANES), BF16)

    pre = []
    for c in range(cpb):
        ex = ex_all[2 * t * c:2 * t * (c + 1)]
        a_col = ex[:t]
        ea = jnp.exp2(a_col)
        a_row = jnp.sum(jnp.where(eye_ref[...] > 0, a_col, 0.0), axis=0, keepdims=True)
        decay = jnp.exp2(a_col - a_row + tri_ref[...])
        xdt = xs_all[c] * ex[t:2 * t]
        pre.append((decay, ea, xdt.astype(BF16), (xdt * jnp.exp2(a_col[t - 1:t, :] - a_col)).astype(BF16)))

    for c in range(cpb):
        decay, ea, xdt_b, xw_b = pre[c]
        xs, bc = xs_all[c], bc_all[c]
        chunk_decay = ea[t - 1:t, :]
        r0 = c * t
        for g in range(SSM_GROUPS):
            lo, hi = GROUP_W * g, GROUP_W * (g + 1)
            bg = bc[:, SSM_STATE * g:SSM_STATE * (g + 1)]
            cg_b = bc[:, gn + SSM_STATE * g:gn + SSM_STATE * (g + 1)].astype(BF16)
            bg_b = bg.astype(BF16)
            cb2 = _dot_nt(cg_b, jnp.concatenate([bg_b, bg_b], axis=0))
            yd = []
            for j in range(GROUP_W // LANES):
                l0 = lo + LANES * j
                mm = (cb2 * decay[:, l0:l0 + LANES]).astype(BF16)
                xp = xdt_b[:, l0:l0 + LANES]
                rhs = jnp.concatenate([jnp.where(left, xp, zero_b), jnp.where(left, zero_b, xp)], axis=0)
                yd.append(_dot(mm, rhs))
            y_diag = jnp.concatenate(yd, axis=1)
            st = state_ref[g]
            y_off = _dot(cg_b, st.astype(BF16)) * ea[:, lo:hi]
            inc = _dot(bg.T.astype(BF16), xw_b[:, lo:hi])
            state_ref[g] = st * chunk_decay[:, lo:hi] + inc
            yg = y_diag + y_off + dsk_ref[:, lo:hi] * xs[:, lo:hi]
            yg = yg * _silu(z_ref[r0:r0 + t, lo:hi].astype(F32))
            ms = jnp.mean(yg * yg, axis=-1, keepdims=True)
            y_ref[r0:r0 + t, lo:hi] = (yg * lax.rsqrt(ms + EPS) * nw_ref[:, lo:hi]).astype(BF16)


def _ssd(big, small, cwx, cwb, cbx, cbb, dtb, alog, dsk, nw, sel, eye_t, tri_t, tri64, shift_m, layer, batch, seq,
         cpb):
    m = big.shape[0]
    t = CHUNK
    tb = cpb * t
    nb = seq // tb
    bcw = 2 * SSM_GROUPS * SSM_STATE
    bc_blk = (2 * D_INNER) // bcw
    dt_blk = (SMALL_N - LANES) // LANES
    row = lambda b, c: b * nb + c
    const = lambda shape: pl.BlockSpec(shape, lambda b, c: (0,) * len(shape))
    return pl.pallas_call(
        functools.partial(_ssd_body, cpb=cpb),
        grid=(batch, nb),
        in_specs=[pl.BlockSpec((tb, D_INNER), lambda b, c: (row(b, c), 0)),
                  pl.BlockSpec((tb, D_INNER), lambda b, c: (row(b, c), 1)),
                  pl.BlockSpec((tb, bcw), lambda b, c: (row(b, c), bc_blk)),
                  pl.BlockSpec((tb, LANES), lambda b, c: (row(b, c), dt_blk)),
                  _lspec(layer, (CONV_WIDTH, D_INNER)), _lspec(layer, (CONV_WIDTH, bcw)),
                  _lspec(layer, (1, D_INNER)), _lspec(layer, (1, bcw)),
                  _lspec(layer, (1, LANES)), _lspec(layer, (1, LANES)),
                  _lspec(layer, (1, D_INNER)), _lspec(layer, (1, D_INNER)),
                  const((2 * LANES, D_INNER)), const((t, D_INNER)), const((t, D_INNER)), const((t, t)),
                  const(((CONV_WIDTH - 1) * t, 2 * t))],
        out_specs=pl.BlockSpec((tb, D_INNER), lambda b, c: (row(b, c), 0)),
        out_shape=jax.ShapeDtypeStruct((m, D_INNER), BF16),
        scratch_shapes=[pltpu.VMEM((SSM_GROUPS, SSM_STATE, GROUP_W), F32),
                        pltpu.VMEM((t, D_INNER), BF16),
                        pltpu.VMEM((t, bcw), BF16)],
        compiler_params=_cparams(("parallel", "arbitrary")),
        name="ssd",
    )(big, big, big, small, cwx, cwb, cbx, cbb, dtb, alog, dsk, nw, sel, eye_t, tri_t, tri64, shift_m)


def _merge_body(o_ref, y_ref, wa_ref, wb_ref, g0_ref, g1_ref, out_ref):
    ya = _dot(o_ref[...], wa_ref[...])
    yb = _dot(y_ref[...], wb_ref[...])
    g0 = jax.nn.sigmoid(g0_ref[...].astype(F32))
    g1 = jax.nn.sigmoid(g1_ref[...].astype(F32))
    out_ref[...] = (g0 * ya + g1 * yb).astype(out_ref.dtype)


def _merge(o, y, wa, wb, big, layer, tm, tn):
    m = o.shape[0]
    g0_blk = (2 * D_INNER + 2 * SSM_GROUPS * SSM_STATE) // tn
    g1_blk = g0_blk + D_MODEL // tn
    return pl.pallas_call(
        _merge_body,
        grid=(m // tm, D_MODEL // tn),
        in_specs=[pl.BlockSpec((tm, o.shape[1]), lambda i, j: (i, 0)),
                  pl.BlockSpec((tm, y.shape[1]), lambda i, j: (i, 0)),
                  _lspec(layer, (wa.shape[1], tn), lambda i, j: (0, j)),
                  _lspec(layer, (wb.shape[1], tn), lambda i, j: (0, j)),
                  pl.BlockSpec((tm, tn), lambda i, j: (i, g0_blk + j)),
                  pl.BlockSpec((tm, tn), lambda i, j: (i, g1_blk + j))],
        out_specs=pl.BlockSpec((tm, tn), lambda i, j: (i, j)),
        out_shape=jax.ShapeDtypeStruct((m, D_MODEL), BF16),
        compiler_params=_cparams(("parallel", "arbitrary")),
        name="merge",
    )(o, y, wa, wb, big, big)


def _res_mm_body(x_ref, a_ref, w_ref, o_ref):
    o_ref[...] = x_ref[...] + _dot(a_ref[...], w_ref[...])


def _res_mm_f32w_body(x_ref, a_ref, w_ref, o_ref, wb_ref):
    @pl.when(pl.program_id(0) == 0)
    def _():
        wb_ref[...] = w_ref[...].astype(BF16)

    o_ref[...] = x_ref[...] + _dot(a_ref[...], wb_ref[...])


def _res_matmul(x, a, w, layer, tm):
    m, n = x.shape
    k = a.shape[1]
    return pl.pallas_call(
        _res_mm_f32w_body,
        grid=(m // tm,),
        in_specs=[pl.BlockSpec((tm, n), lambda i: (i, 0)),
                  pl.BlockSpec((tm, k), lambda i: (i, 0)),
                  _lspec(layer, (k, n), single=True)],
        out_specs=pl.BlockSpec((tm, n), lambda i: (i, 0)),
        out_shape=jax.ShapeDtypeStruct((m, n), F32),
        scratch_shapes=[pltpu.VMEM((k, n), BF16)],
        compiler_params=_cparams(("arbitrary",)),
        name="out_proj",
    )(x, a, w)


def _res_matmul_tiled(x, a, w, layer, tm, tn, name):
    m, n = x.shape
    k = a.shape[1]
    return pl.pallas_call(
        _res_mm_body,
        grid=(m // tm, n // tn),
        in_specs=[pl.BlockSpec((tm, tn), lambda i, j: (i, j)),
                  pl.BlockSpec((tm, k), lambda i, j: (i, 0)),
                  _lspec(layer, (k, tn), lambda i, j: (0, j))],
        out_specs=pl.BlockSpec((tm, tn), lambda i, j: (i, j)),
        out_shape=jax.ShapeDtypeStruct((m, n), F32),
        compiler_params=_cparams(("parallel", "arbitrary")),
        name=name,
    )(x, a, w)


def _ple_body(x_ref, nw_ref, wg_ref, p_ref, wp_ref, o_ref, wgb_ref):
    @pl.when(pl.program_id(0) == 0)
    def _():
        wgb_ref[...] = wg_ref[...].astype(BF16)

    x = x_ref[...]
    h = _rms(x, nw_ref[...]).astype(BF16)
    gate = jax.nn.sigmoid(_dot(h, wgb_ref[...]))
    pe = _dot(p_ref[...].astype(BF16), wp_ref[...])
    o_ref[...] = x + pe * gate


def _ple(x, nw, wg, p, wp, layer, tm):
    m, d = x.shape
    return pl.pallas_call(
        _ple_body,
        grid=(m // tm,),
        in_specs=[pl.BlockSpec((tm, d), lambda i: (i, 0)),
                  _lspec(layer, (1, d)),
                  _lspec(layer, (d, d), single=True),
                  _lspec(layer, (tm, PLE_DIM), lambda i: (i, 0)),
                  _lspec(layer, (PLE_DIM, d))],
        out_specs=pl.BlockSpec((tm, d), lambda i: (i, 0)),
        out_shape=jax.ShapeDtypeStruct((m, d), F32),
        scratch_shapes=[pltpu.VMEM((d, d), BF16)],
        compiler_params=_cparams(("arbitrary",)),
        name="ple",
    )(x, nw, wg, p, wp)


def _rope_cols(w):
    half = QK_ROPE // 2
    z = jnp.zeros(w.shape[:-1] + (LANES // 2 - half,), w.dtype)
    return jnp.concatenate([w[..., :half], z, w[..., half:], z], axis=-1)


def kernel(x, p, positions, norm_mix_w, w_in, q_a_norm_w, w_uq, kv_a_norm_w, w_ukv, q_norm_w, k_norm_w,
           w_o_mla, conv_w, conv_b, dt_bias, a_log, d_skip, ssm_norm_w, w_o_ssm, w_out, norm_mlp_w, w_up,
           w_down, ple_norm_w, w_ple_gate, w_ple):
    batch, seq, d = x.shape
    m = batch * seq
    depth = w_in.shape[0]
    tm_big = min(1024, m)
    tm_mid = min(512, m)
    tm_small = min(256, m)
    tq = min(512, seq)

    inv_freq = 1.0 / (ROPE_THETA ** (jnp.arange(0, QK_ROPE, 2, dtype=F32) / QK_ROPE))
    freq = _rope_cols(jnp.concatenate([inv_freq, inv_freq]))[None, :]
    sign = _rope_cols(jnp.concatenate([-jnp.ones_like(inv_freq), jnp.ones_like(inv_freq)]))[None, :]
    ct, st = _rope_tables(positions.reshape(m, 1), freq, sign, tm_big)
    ct_t, st_t = ct.T, st.T

    lane = jnp.arange(D_INNER)
    sel_row = jnp.arange(2 * LANES)[:, None]
    sel = ((sel_row % SSM_HEADS == (lane // SSM_HEADDIM)[None, :]) & (sel_row < 3 * SSM_HEADS)).astype(BF16)
    rows = jnp.arange(CHUNK)[:, None]
    eye_t = (rows == (lane % SSM_HEADDIM)[None, :]).astype(F32)
    tri_t = jnp.where(rows >= (lane % SSM_HEADDIM)[None, :], 0.0, -jnp.inf).astype(F32)
    tri64 = (rows >= jnp.arange(CHUNK)[None, :]).astype(BF16)
    sh_rows = jnp.arange((CONV_WIDTH - 1) * CHUNK)[:, None]
    shift_m = (jnp.arange(2 * CHUNK)[None, :] == CHUNK + sh_rows % CHUNK - (sh_rows // CHUNK + 1)).astype(BF16)

    s0 = Q_LORA + KV_LORA + QK_ROPE
    s1 = s0 + D_INNER + (D_INNER + 2 * SSM_GROUPS * SSM_STATE)
    s2 = s1 + SSM_HEADS

    w_lat, w_kr, w_dt = lax.optimization_barrier(
        (w_in[..., :Q_LORA + KV_LORA], w_in[..., Q_LORA + KV_LORA:s0], w_in[..., s1:s2]))
    w_small = jnp.concatenate(
        [w_lat, _rope_cols(w_kr), w_dt, jnp.zeros((depth, d, LANES - SSM_HEADS), F32)], axis=-1).astype(BF16)
    w_big = _shift_cast(w_in, (s0 - QK_ROPE, s2 - LANES), (QK_ROPE, LANES), (s1 - s0, 2 * D_MODEL), 512)
    wq4 = w_uq.reshape(depth, Q_LORA, MLA_HEADS, QK_DIM)
    wq_rope = _rope_cols(wq4[..., QK_NOPE:])
    wq = jnp.concatenate([wq4[..., :QK_NOPE], wq_rope, jnp.roll(wq_rope, LANES // 2, axis=-1)], axis=-1)
    wqt = jnp.swapaxes(wq.reshape(depth, Q_LORA, MLA_HEADS * Q_ACC_W), 1, 2).astype(BF16)
    wkv4 = w_ukv.reshape(depth, KV_LORA, MLA_HEADS, QK_NOPE + V_DIM)
    wk = wkv4[..., :QK_NOPE].reshape(depth, KV_LORA, -1).astype(BF16)
    wvt = jnp.swapaxes(wkv4[..., QK_NOPE:].reshape(depth, KV_LORA, -1), 1, 2).astype(BF16)
    w_o_mla_b = w_o_mla.astype(BF16)
    w_o_ssm_b = w_o_ssm.astype(BF16)
    w_down_b = w_down.astype(BF16)
    w_ple_b = w_ple.astype(BF16)

    rows3 = lambda v: v.reshape(depth, 1, -1).astype(F32)
    pad_h = jnp.zeros((depth, LANES - SSM_HEADS), F32)
    qw_rope = _rope_cols(q_norm_w[:, QK_NOPE:])
    qnw = jnp.concatenate([q_norm_w[:, :QK_NOPE], qw_rope, jnp.roll(qw_rope, LANES // 2, axis=-1)], axis=-1)
    qnw = jnp.broadcast_to(qnw[:, :, None], (depth, Q_ACC_W, tm_small)).astype(F32)
    kwn = rows3(k_norm_w[:, :QK_NOPE])
    kwr = rows3(_rope_cols(k_norm_w[:, QK_NOPE:]))
    dtb = rows3(jnp.concatenate([dt_bias, pad_h], axis=-1))
    alog = rows3(jnp.concatenate([a_log, pad_h], axis=-1))
    dsk = rows3(jnp.repeat(d_skip, SSM_HEADDIM, axis=-1))
    mix_nw, qa_nw, kva_nw = rows3(norm_mix_w), rows3(q_a_norm_w), rows3(kv_a_norm_w)
    ssm_nw, mlp_nw, ple_nw = rows3(ssm_norm_w), rows3(norm_mlp_w), rows3(ple_norm_w)
    conv_wh, conv_bh = 0.5 * conv_w, 0.5 * conv_b
    cwx, cwb = conv_wh[..., :D_INNER], conv_wh[..., D_INNER:]
    cbx, cbb = rows3(conv_bh[:, :D_INNER]), rows3(conv_bh[:, D_INNER:])
    p3 = p.reshape(depth, m, PLE_DIM)

    xf = x.reshape(m, d)
    for i in range(depth):
        small = _norm_matmul(xf, mix_nw, w_small, i, F32, tm_mid, SMALL_N, "in_proj_small")
        big = _norm_matmul(xf, mix_nw, w_big, i, BF16, tm_big, 1024, "in_proj_big")
        qt = _q_proj(small, qa_nw, wqt, qnw, ct_t, st_t, i, tm_small)
        k, vt = _kv_proj(small, kva_nw, wk, wvt, kwn, kwr, ct, st, i, tq)
        o = _attention(qt, k, vt, batch, seq, tq, 2)
        y = _ssd(big, small, cwx, cwb, cbx, cbb, dtb, alog, dsk, ssm_nw, sel, eye_t, tri_t, tri64, shift_m, i,
                 batch, seq, 4)
        merged = _merge(o, y, w_o_mla_b, w_o_ssm_b, big, i, tm_big, 512)
        xf = _res_matmul(xf, merged, w_out, i, tm_mid)
        hidden = _norm_matmul(xf, mlp_nw, w_up, i, BF16, min(2048, m), 512, "mlp_up", relu2=True, single_x=True)
        xf = _res_matmul_tiled(xf, hidden, w_down_b, i, tm_mid, 512, "mlp_down")
        xf = _ple(xf, ple_nw, w_ple_gate, p3, w_ple_b, i, tm_mid)
    return xf.reshape(batch, seq, d)
```

```python
import functools

import jax
import jax.numpy as jnp
from jax import lax
from jax.experimental import pallas as pl
from jax.experimental.pallas import tpu as pltpu

F32 = jnp.float32
BF16 = jnp.bfloat16

D_MODEL = 2048
DEPTH = 4
CHUNK = 64
PLE_DIM = 256
EPS = 1e-6
MLA_HEADS = 16
Q_LORA = 512
KV_LORA = 512
QK_NOPE = 128
QK_ROPE = 64
V_DIM = 128
QK_DIM = QK_NOPE + QK_ROPE
ROPE_THETA = 10000.0
D_INNER = 2 * D_MODEL
SSM_HEADDIM = 64
SSM_HEADS = D_INNER // SSM_HEADDIM
SSM_GROUPS = 8
SSM_STATE = 128
GROUP_W = D_INNER // SSM_GROUPS
CONV_WIDTH = 4
D_FF = 4 * D_MODEL

LOG2_E = 1.4426950408889634
LANES = 128
ACC_ROWS = V_DIM + 16
Q_ACC_W = 3 * LANES
HEAD_W = 2 * LANES
SMALL_N = Q_LORA + KV_LORA + 2 * LANES
VMEM_LIMIT = 56 * 1024 * 1024


def _cparams(sem):
    return pltpu.CompilerParams(dimension_semantics=sem, vmem_limit_bytes=VMEM_LIMIT)


def _rms(xf, w):
    ms = jnp.mean(xf * xf, axis=-1, keepdims=True)
    return xf * lax.rsqrt(ms + EPS) * w


def _lspec(layer, block, imap=None, single=False):
    if imap is None:
        imap = lambda *g: (0,) * len(block)
    return pl.BlockSpec((None,) + tuple(block), lambda *g: (layer,) + tuple(imap(*g)),
                        pipeline_mode=pl.Buffered(1) if single else None)


def _dot(a, b):
    return jnp.dot(a, b, preferred_element_type=F32)


def _dot_nt(a, b):
    return lax.dot_general(a, b, (((1,), (1,)), ((), ())), preferred_element_type=F32)


def _rope_body(pos_ref, freq_ref, sign_ref, ct_ref, st_ref):
    ang = pos_ref[...].astype(F32) * freq_ref[...]
    ct_ref[...] = jnp.cos(ang)
    st_ref[...] = jnp.sin(ang) * sign_ref[...]


def _rope_tables(pos_col, freq, sign, tm):
    m = pos_col.shape[0]
    return pl.pallas_call(
        _rope_body,
        grid=(m // tm,),
        in_specs=[pl.BlockSpec((tm, 1), lambda i: (i, 0)),
                  pl.BlockSpec((1, LANES), lambda i: (0, 0)),
                  pl.BlockSpec((1, LANES), lambda i: (0, 0))],
        out_specs=[pl.BlockSpec((tm, LANES), lambda i: (i, 0)),
                   pl.BlockSpec((tm, LANES), lambda i: (i, 0))],
        out_shape=[jax.ShapeDtypeStruct((m, LANES), F32)] * 2,
        compiler_params=_cparams(("parallel",)),
        name="rope_tables",
    )(pos_col, freq, sign)


def _norm_mm_body(x_ref, nw_ref, w_ref, o_ref, h_ref, *, relu2):
    @pl.when(pl.program_id(1) == 0)
    def _():
        h_ref[...] = _rms(x_ref[...], nw_ref[...]).astype(BF16)

    y = _dot(h_ref[...], w_ref[...].astype(BF16))
    if relu2:
        y = jnp.square(jnp.maximum(y, 0.0))
    o_ref[...] = y.astype(o_ref.dtype)


def _norm_matmul(x, nw, w, layer, out_dtype, tm, tn, name, relu2=False, single_x=False):
    m, k = x.shape
    n = w.shape[2]
    return pl.pallas_call(
        functools.partial(_norm_mm_body, relu2=relu2),
        grid=(m // tm, n // tn),
        in_specs=[pl.BlockSpec((tm, k), lambda i, j: (i, 0), pipeline_mode=pl.Buffered(1) if single_x else None),
                  _lspec(layer, (1, k)),
                  _lspec(layer, (k, tn), lambda i, j: (0, j))],
        out_specs=pl.BlockSpec((tm, tn), lambda i, j: (i, j)),
        out_shape=jax.ShapeDtypeStruct((m, n), out_dtype),
        scratch_shapes=[pltpu.VMEM((tm, k), BF16)],
        compiler_params=_cparams(("parallel", "arbitrary")),
        name=name,
    )(x, nw, w)


def _shift_cast_body(a_ref, b_ref, o_ref, *, tiles0, shifts):
    v = jnp.concatenate([a_ref[...], b_ref[...]], axis=1)
    tn = o_ref.shape[1]
    first = pl.program_id(1) < tiles0

    @pl.when(first)
    def _():
        o_ref[...] = v[:, shifts[0]:shifts[0] + tn].astype(BF16)

    @pl.when(jnp.logical_not(first))
    def _():
        o_ref[...] = v[:, shifts[1]:shifts[1] + tn].astype(BF16)


def _shift_cast(w, bases, shifts, widths, tn):
    depth, k, _ = w.shape
    assert all(b % tn == 0 and n % tn == 0 and 0 < s <= LANES for b, s, n in zip(bases, shifts, widths))
    tiles0 = widths[0] // tn
    sub = tn // LANES

    def a_col(l, j):
        return jnp.where(j < tiles0, bases[0] // tn + j, bases[1] // tn + j - tiles0)

    def b_col(l, j):
        return jnp.where(j < tiles0, (bases[0] + tn) // LANES + j * sub, (bases[1] + tn) // LANES + (j - tiles0) * sub)

    return pl.pallas_call(
        functools.partial(_shift_cast_body, tiles0=tiles0, shifts=tuple(shifts)),
        grid=(depth, (widths[0] + widths[1]) // tn),
        in_specs=[pl.BlockSpec((None, k, tn), lambda l, j: (l, 0, a_col(l, j))),
                  pl.BlockSpec((None, k, LANES), lambda l, j: (l, 0, b_col(l, j)))],
        out_specs=pl.BlockSpec((None, k, tn), lambda l, j: (l, 0, j)),
        out_shape=jax.ShapeDtypeStruct((depth, k, widths[0] + widths[1]), BF16),
        compiler_params=_cparams(("parallel", "parallel")),
        name="relayout_w_in",
    )(w, w)


def _rope_apply(v, c, s):
    return v * c + pltpu.roll(v, 64, 1) * s


def _q_body(c_ref, nw_ref, wt_ref, qnw_ref, ct_ref, st_ref, o_ref):
    cn = _rms(c_ref[...], nw_ref[...]).astype(BF16)
    acc = _dot_nt(wt_ref[...], cn)
    wn = qnw_ref[:LANES, :]
    wrc = qnw_ref[LANES:2 * LANES, :] * ct_ref[...]
    wxs = qnw_ref[2 * LANES:, :] * st_ref[...]
    scale = QK_DIM ** -0.5 * LOG2_E
    for h in range(MLA_HEADS):
        qn = acc[Q_ACC_W * h:Q_ACC_W * h + LANES, :]
        qr = acc[Q_ACC_W * h + LANES:Q_ACC_W * h + 2 * LANES, :]
        qx = acc[Q_ACC_W * h + 2 * LANES:Q_ACC_W * (h + 1), :]
        ssq = jnp.sum(qn * qn + qr * qr, axis=0, keepdims=True)
        rs = lax.rsqrt(ssq / QK_DIM + EPS) * scale
        o_ref[0, HEAD_W * h:HEAD_W * h + LANES, :] = (qn * rs * wn).astype(BF16)
        o_ref[0, HEAD_W * h + LANES:HEAD_W * (h + 1), :] = ((qr * wrc + qx * wxs) * rs).astype(BF16)


def _q_proj(small, nw, wqt, qnw, ct_t, st_t, layer, tm):
    m = small.shape[0]
    n = MLA_HEADS * HEAD_W
    return pl.pallas_call(
        _q_body,
        grid=(m // tm,),
        in_specs=[pl.BlockSpec((tm, Q_LORA), lambda i: (i, 0)),
                  _lspec(layer, (1, Q_LORA)),
                  _lspec(layer, (MLA_HEADS * Q_ACC_W, Q_LORA)),
                  _lspec(layer, (Q_ACC_W, tm)),
                  pl.BlockSpec((LANES, tm), lambda i: (0, i)),
                  pl.BlockSpec((LANES, tm), lambda i: (0, i))],
        out_specs=pl.BlockSpec((1, n, tm), lambda i: (i, 0, 0)),
        out_shape=jax.ShapeDtypeStruct((m // tm, n, tm), BF16),
        compiler_params=_cparams(("parallel",)),
        name="q_proj",
    )(small, nw, wqt, qnw, ct_t, st_t)


def _kv_body(c_ref, kr_ref, nw_ref, wk_ref, wvt_ref, kwn_ref, kwr_ref, ct_ref, st_ref, k_ref, vt_ref):
    cn = _rms(c_ref[...], nw_ref[...]).astype(BF16)
    acc = _dot(cn, wk_ref[...])
    vt_ref[0] = _dot_nt(wvt_ref[...], cn).astype(BF16)
    c = ct_ref[...]
    s = st_ref[...]
    kr = kr_ref[...]
    wn = kwn_ref[...]
    wr = kwr_ref[...]
    ssr = jnp.sum(kr * kr, axis=-1, keepdims=True)
    kr_rot = _rope_apply(kr * wr, c, s)
    for h in range(MLA_HEADS):
        kn = acc[:, LANES * h:LANES * (h + 1)]
        ssq = jnp.sum(kn * kn, axis=-1, keepdims=True) + ssr
        rs = lax.rsqrt(ssq / QK_DIM + EPS)
        k_ref[:, HEAD_W * h:HEAD_W * h + LANES] = (kn * rs * wn).astype(BF16)
        k_ref[:, HEAD_W * h + LANES:HEAD_W * (h + 1)] = (kr_rot * rs).astype(BF16)


def _kv_proj(small, nw, wk, wvt, kwn, kwr, ct, st, layer, tm):
    m = small.shape[0]
    nk = MLA_HEADS * HEAD_W
    nv = MLA_HEADS * V_DIM
    kr_blk = (Q_LORA + KV_LORA) // LANES
    return pl.pallas_call(
        _kv_body,
        grid=(m // tm,),
        in_specs=[pl.BlockSpec((tm, KV_LORA), lambda i: (i, 1)),
                  pl.BlockSpec((tm, LANES), lambda i: (i, kr_blk)),
                  _lspec(layer, (1, KV_LORA)),
                  _lspec(layer, (KV_LORA, MLA_HEADS * QK_NOPE)),
                  _lspec(layer, (nv, KV_LORA)),
                  _lspec(layer, (1, LANES)),
                  _lspec(layer, (1, LANES)),
                  pl.BlockSpec((tm, LANES), lambda i: (i, 0)),
                  pl.BlockSpec((tm, LANES), lambda i: (i, 0))],
        out_specs=[pl.BlockSpec((tm, nk), lambda i: (i, 0)),
                   pl.BlockSpec((1, nv, tm), lambda i: (i, 0, 0))],
        out_shape=[jax.ShapeDtypeStruct((m, nk), BF16), jax.ShapeDtypeStruct((m // tm, nv, tm), BF16)],
        compiler_params=_cparams(("parallel",)),
        name="kv_proj",
    )(small, small, nw, wk, wvt, kwn, kwr, ct, st)


def _attn_body(q_ref, k_ref, vt_ref, o_ref, s0_scr, s1_scr, p0_scr, p1_scr, m_scr, a_scr, acc_scr, *, tq, nq, hb, qsub):
    s_scr = (s0_scr, s1_scr)
    p_scr = (p0_scr, p1_scr)
    tk = tq // 2
    key_chunk = lax.broadcasted_iota(jnp.int32, (tk, tq), 0) // CHUNK
    qry_chunk = lax.broadcasted_iota(jnp.int32, (tk, tq), 1) // CHUNK
    diag_masks = [key_chunk + h * (tk // CHUNK) <= qry_chunk for h in range(2)]
    ones_rows = jnp.ones((ACC_ROWS - V_DIM, tk), BF16)

    def scores(qs, kpair, h, mask=None):
        koff = pl.multiple_of(kpair * tq + h * tk, tk)
        for hh in range(hb):
            s = _dot(k_ref[pl.ds(koff, tk), HEAD_W * hh:HEAD_W * (hh + 1)], qs[hh])
            s_scr[h][hh] = s if mask is None else jnp.where(mask, s, -jnp.inf)

    def pv(kpair, h):
        return [_dot(jnp.concatenate([vt_ref[kpair, V_DIM * hh:V_DIM * (hh + 1), h * tk:(h + 1) * tk],
                                      ones_rows], axis=0), p_scr[h][hh])
                for hh in range(hb)]

    def softmax(h, pvs):
        for hh in range(hb):
            m = m_scr[hh]
            s = s_scr[h][hh]
            m_new = jnp.maximum(m, jnp.max(s, axis=0, keepdims=True))
            p_scr[h][hh] = jnp.exp2((s - m_new).astype(BF16))
            alpha = jnp.exp2(m - m_new)
            acc_scr[hh] = alpha * (acc_scr[hh] + a_scr[hh] * pvs[hh])
            a_scr[hh] = alpha
            m_scr[hh] = m_new

    def q_tiles(qi):
        return [jnp.concatenate([q_ref[qsub * qi + u, HEAD_W * hh:HEAD_W * (hh + 1), :] for u in range(qsub)], axis=1)
                for hh in range(hb)]

    def diag_scores(qi):
        qs = q_tiles(qi)
        for h in range(2):
            scores(qs, qi, h, diag_masks[h])

    diag_scores(0)
    a_scr[...] = jnp.zeros(a_scr.shape, F32)
    acc_scr[...] = jnp.zeros(acc_scr.shape, F32)
    for h in range(2):
        p_scr[h][...] = jnp.zeros(p_scr[h].shape, BF16)

    def q_block(qi, c):
        qoff = pl.multiple_of(qi * tq, tq)
        qs = q_tiles(qi)
        m_scr[...] = jnp.full(m_scr.shape, -jnp.inf, F32)

        def kpair_of(r):
            return jnp.where(r == 0, qi, r - 1)

        def body(r, c2):
            prev = kpair_of(jnp.maximum(r - 1, 0))
            for h in range(2):
                pvs = pv(prev, h)
                softmax(h, pvs)
                scores(qs, r, h)
            return c2

        lax.fori_loop(0, qi, body, 0)
        prev = kpair_of(jnp.maximum(qi - 1, 0))
        nxt = jnp.minimum(qi + 1, nq - 1)
        qn = q_tiles(nxt)
        for h in range(2):
            softmax(h, pv(prev, h))
            scores(qn, nxt, h, diag_masks[h])
        last = kpair_of(qi)
        pv0, pv1 = pv(last, 0), pv(last, 1)
        for hh in range(hb):
            acc = acc_scr[hh] + a_scr[hh] * pv0[hh] + pv1[hh]
            o_ref[pl.ds(qoff, tq), V_DIM * hh:V_DIM * (hh + 1)] = (
                acc[:V_DIM] / acc[V_DIM:V_DIM + 1]).T.astype(BF16)
        return c

    lax.fori_loop(0, nq, q_block, 0)


def _attention(qt, k, vt, batch, seq, tq, hb):
    m = k.shape[0]
    nq = seq // tq
    tk = tq // 2
    assert vt.shape[2] == tq
    qsub = tq // qt.shape[2]
    return pl.pallas_call(
        functools.partial(_attn_body, tq=tq, nq=nq, hb=hb, qsub=qsub),
        grid=(batch, MLA_HEADS // hb),
        in_specs=[pl.BlockSpec((nq * qsub, hb * HEAD_W, tq // qsub), lambda b, h: (b, h, 0)),
                  pl.BlockSpec((seq, hb * HEAD_W), lambda b, h: (b, h)),
                  pl.BlockSpec((nq, hb * V_DIM, tq), lambda b, h: (b, h, 0))],
        out_specs=pl.BlockSpec((seq, hb * V_DIM), lambda b, h: (b, h)),
        out_shape=jax.ShapeDtypeStruct((m, MLA_HEADS * V_DIM), BF16),
        scratch_shapes=[pltpu.VMEM((hb, tk, tq), F32), pltpu.VMEM((hb, tk, tq), F32),
                        pltpu.VMEM((hb, tk, tq), BF16), pltpu.VMEM((hb, tk, tq), BF16),
                        pltpu.VMEM((hb, 1, tq), F32), pltpu.VMEM((hb, 1, tq), F32),
                        pltpu.VMEM((hb, ACC_ROWS, tq), F32)],
        compiler_params=_cparams(("parallel", "arbitrary")),
        name="attention",
    )(qt, k, vt)


def _silu_of_half(h):
    return h + h * jnp.tanh(h)


def _silu(y):
    return _silu_of_half(0.5 * y)


def _split3(v):
    hi = v.astype(BF16)
    r = v - hi.astype(F32)
    mid = r.astype(BF16)
    lo = (r - mid.astype(F32)).astype(BF16)
    return hi, mid, lo


def _ssd_body(z_ref, x_ref, bc_ref, dt_ref, cwx_ref, cwb_ref, cbx_ref, cbb_ref, dtb_ref, alog_ref,
              dsk_ref, nw_ref, sel_ref, eye_ref, tri_ref, tri64_ref, shift_ref, y_ref, state_ref, xhalo, bchalo,
              *, cpb):
    t = CHUNK
    gn = SSM_GROUPS * SSM_STATE

    @pl.when(pl.program_id(1) == 0)
    def _():
        state_ref[...] = jnp.zeros_like(state_ref)
        xhalo[...] = jnp.zeros_like(xhalo)
        bchalo[...] = jnp.zeros_like(bchalo)

    def conv_silu(raw_ref, halo, w_ref, b_ref):
        raws = [raw_ref[c * t:(c + 1) * t, :] for c in range(cpb)]
        prevs = [halo[...]] + raws[:-1]
        halo[...] = raws[-1]
        shifted = [_dot(shift_ref[...], jnp.concatenate([prevs[c], raws[c]], axis=0)) for c in range(cpb)]
        outs = []
        for c in range(cpb):
            y = None
            for tap in range(CONV_WIDTH):
                shift = CONV_WIDTH - 1 - tap
                xt = raws[c].astype(F32) if shift == 0 else shifted[c][(shift - 1) * t:shift * t, :]
                term = xt * w_ref[tap:tap + 1, :]
                y = term if y is None else y + term
            outs.append(_silu_of_half(y + b_ref[...]))
        return outs

    xs_all = conv_silu(x_ref, xhalo, cwx_ref, cbx_ref)
    bc_all = conv_silu(bc_ref, bchalo, cwb_ref, cbb_ref)

    head_lane = lax.broadcasted_iota(jnp.int32, (t, LANES), 1) < SSM_HEADS
    neg_a = -jnp.exp(alog_ref[...])
    tri64 = tri64_ref[...]
    stacks = []
    for c in range(cpb):
        dt = jnp.where(head_lane, jax.nn.softplus(dt_ref[c * t:(c + 1) * t, :] + dtb_ref[...]), 0.0)
        h3 = _split3(dt * neg_a)
        a_cum = (_dot(tri64, h3[0]) + _dot(tri64, h3[1]) + _dot(tri64, h3[2])) * LOG2_E
        stacks += [a_cum, dt]
    s3 = [part.astype(F32) for part in _split3(jnp.concatenate(stacks, axis=0))]
    packed = jnp.concatenate([(s3[0] + pltpu.roll(s3[1], SSM_HEADS, 1)).astype(BF16), s3[2].astype(BF16)], axis=1)
    ex_all = _dot(packed, sel_ref[...])
    left = lax.broadcasted_iota(jnp.int32, (t, LANES), 1) < SSM_HEADDIM
    zero_b = jnp.zeros((t, LANES), BF16)

    pre = []
    for c in range(cpb):
        ex = ex_all[2 * t * c:2 * t * (c + 1)]
        a_col = ex[:t]
        ea = jnp.exp2(a_col)
        a_row = jnp.sum(jnp.where(eye_ref[...] > 0, a_col, 0.0), axis=0, keepdims=True)
        decay = jnp.exp2(a_col - a_row + tri_ref[...])
        xdt = xs_all[c] * ex[t:2 * t]
        pre.append((decay, ea, xdt.astype(BF16), (xdt * jnp.exp2(a_col[t - 1:t, :] - a_col)).astype(BF16)))

    for c in range(cpb):
        decay, ea, xdt_b, xw_b = pre[c]
        xs, bc = xs_all[c], bc_all[c]
        chunk_decay = ea[t - 1:t, :]
        r0 = c * t
        for g in range(SSM_GROUPS):
            lo, hi = GROUP_W * g, GROUP_W * (g + 1)
            bg = bc[:, SSM_STATE * g:SSM_STATE * (g + 1)]
            cg_b = bc[:, gn + SSM_STATE * g:gn + SSM_STATE * (g + 1)].astype(BF16)
            bg_b = bg.astype(BF16)
            cb2 = _dot_nt(cg_b, jnp.concatenate([bg_b, bg_b], axis=0))
            yd = []
            for j in range(GROUP_W // LANES):
                l0 = lo + LANES * j
                mm = (cb2 * decay[:, l0:l0 + LANES]).astype(BF16)
                xp = xdt_b[:, l0:l0 + LANES]
                rhs = jnp.concatenate([jnp.where(left, xp, zero_b), jnp.where(left, zero_b, xp)], axis=0)
                yd.append(_dot(mm, rhs))
            y_diag = jnp.concatenate(yd, axis=1)
            st = state_ref[g]
            y_off = _dot(cg_b, st.astype(BF16)) * ea[:, lo:hi]
            inc = _dot(bg.T.astype(BF16), xw_b[:, lo:hi])
            state_ref[g] = st * chunk_decay[:, lo:hi] + inc
            yg = y_diag + y_off + dsk_ref[:, lo:hi] * xs[:, lo:hi]
            yg = yg * _silu(z_ref[r0:r0 + t, lo:hi].astype(F32))
            ms = jnp.mean(yg * yg, axis=-1, keepdims=True)
            y_ref[r0:r0 + t, lo:hi] = (yg * lax.rsqrt(ms + EPS) * nw_ref[:, lo:hi]).astype(BF16)


def _ssd(big, small, cwx, cwb, cbx, cbb, dtb, alog, dsk, nw, sel, eye_t, tri_t, tri64, shift_m, layer, batch, seq,
         cpb):
    m = big.shape[0]
    t = CHUNK
    tb = cpb * t
    nb = seq // tb
    bcw = 2 * SSM_GROUPS * SSM_STATE
    bc_blk = (2 * D_INNER) // bcw
    dt_blk = (SMALL_N - LANES) // LANES
    row = lambda b, c: b * nb + c
    const = lambda shape: pl.BlockSpec(shape, lambda b, c: (0,) * len(shape))
    return pl.pallas_call(
        functools.partial(_ssd_body, cpb=cpb),
        grid=(batch, nb),
        in_specs=[pl.BlockSpec((tb, D_INNER), lambda b, c: (row(b, c), 0)),
                  pl.BlockSpec((tb, D_INNER), lambda b, c: (row(b, c), 1)),
                  pl.BlockSpec((tb, bcw), lambda b, c: (row(b, c), bc_blk)),
                  pl.BlockSpec((tb, LANES), lambda b, c: (row(b, c), dt_blk)),
                  _lspec(layer, (CONV_WIDTH, D_INNER)), _lspec(layer, (CONV_WIDTH, bcw)),
                  _lspec(layer, (1, D_INNER)), _lspec(layer, (1, bcw)),
                  _lspec(layer, (1, LANES)), _lspec(layer, (1, LANES)),
                  _lspec(layer, (1, D_INNER)), _lspec(layer, (1, D_INNER)),
                  const((2 * LANES, D_INNER)), const((t, D_INNER)), const((t, D_INNER)), const((t, t)),
                  const(((CONV_WIDTH - 1) * t, 2 * t))],
        out_specs=pl.BlockSpec((tb, D_INNER), lambda b, c: (row(b, c), 0)),
        out_shape=jax.ShapeDtypeStruct((m, D_INNER), BF16),
        scratch_shapes=[pltpu.VMEM((SSM_GROUPS, SSM_STATE, GROUP_W), F32),
                        pltpu.VMEM((t, D_INNER), BF16),
                        pltpu.VMEM((t, bcw), BF16)],
        compiler_params=_cparams(("parallel", "arbitrary")),
        name="ssd",
    )(big, big, big, small, cwx, cwb, cbx, cbb, dtb, alog, dsk, nw, sel, eye_t, tri_t, tri64, shift_m)


def _merge_body(o_ref, y_ref, wa_ref, wb_ref, g0_ref, g1_ref, out_ref):
    ya = _dot(o_ref[...], wa_ref[...])
    yb = _dot(y_ref[...], wb_ref[...])
    g0 = jax.nn.sigmoid(g0_ref[...].astype(F32))
    g1 = jax.nn.sigmoid(g1_ref[...].astype(F32))
    out_ref[...] = (g0 * ya + g1 * yb).astype(out_ref.dtype)


def _merge(o, y, wa, wb, big, layer, tm, tn):
    m = o.shape[0]
    g0_blk = (2 * D_INNER + 2 * SSM_GROUPS * SSM_STATE) // tn
    g1_blk = g0_blk + D_MODEL // tn
    return pl.pallas_call(
        _merge_body,
        grid=(m // tm, D_MODEL // tn),
        in_specs=[pl.BlockSpec((tm, o.shape[1]), lambda i, j: (i, 0)),
                  pl.BlockSpec((tm, y.shape[1]), lambda i, j: (i, 0)),
                  _lspec(layer, (wa.shape[1], tn), lambda i, j: (0, j)),
                  _lspec(layer, (wb.shape[1], tn), lambda i, j: (0, j)),
                  pl.BlockSpec((tm, tn), lambda i, j: (i, g0_blk + j)),
                  pl.BlockSpec((tm, tn), lambda i, j: (i, g1_blk + j))],
        out_specs=pl.BlockSpec((tm, tn), lambda i, j: (i, j)),
        out_shape=jax.ShapeDtypeStruct((m, D_MODEL), BF16),
        compiler_params=_cparams(("parallel", "arbitrary")),
        name="merge",
    )(o, y, wa, wb, big, big)


def _res_mm_body(x_ref, a_ref, w_ref, o_ref):
    o_ref[...] = x_ref[...] + _dot(a_ref[...], w_ref[...])


def _res_mm_f32w_body(x_ref, a_ref, w_ref, o_ref, wb_ref):
    @pl.when(pl.program_id(0) == 0)
    def _():
        wb_ref[...] = w_ref[...].astype(BF16)

    o_ref[...] = x_ref[...] + _dot(a_ref[...], wb_ref[...])


def _res_matmul(x, a, w, layer, tm):
    m, n = x.shape
    k = a.shape[1]
    return pl.pallas_call(
        _res_mm_f32w_body,
        grid=(m // tm,),
        in_specs=[pl.BlockSpec((tm, n), lambda i: (i, 0)),
                  pl.BlockSpec((tm, k), lambda i: (i, 0)),
                  _lspec(layer, (k, n), single=True)],
        out_specs=pl.BlockSpec((tm, n), lambda i: (i, 0)),
        out_shape=jax.ShapeDtypeStruct((m, n), F32),
        scratch_shapes=[pltpu.VMEM((k, n), BF16)],
        compiler_params=_cparams(("arbitrary",)),
        name="out_proj",
    )(x, a, w)


def _res_matmul_tiled(x, a, w, layer, tm, tn, name):
    m, n = x.shape
    k = a.shape[1]
    return pl.pallas_call(
        _res_mm_body,
        grid=(m // tm, n // tn),
        in_specs=[pl.BlockSpec((tm, tn), lambda i, j: (i, j)),
                  pl.BlockSpec((tm, k), lambda i, j: (i, 0)),
                  _lspec(layer, (k, tn), lambda i, j: (0, j))],
        out_specs=pl.BlockSpec((tm, tn), lambda i, j: (i, j)),
        out_shape=jax.ShapeDtypeStruct((m, n), F32),
        compiler_params=_cparams(("parallel", "arbitrary")),
        name=name,
    )(x, a, w)


def _ple_body(x_ref, nw_ref, wg_ref, p_ref, wp_ref, o_ref, wgb_ref):
    @pl.when(pl.program_id(0) == 0)
    def _():
        wgb_ref[...] = wg_ref[...].astype(BF16)

    x = x_ref[...]
    h = _rms(x, nw_ref[...]).astype(BF16)
    gate = jax.nn.sigmoid(_dot(h, wgb_ref[...]))
    pe = _dot(p_ref[...].astype(BF16), wp_ref[...])
    o_ref[...] = x + pe * gate


def _ple(x, nw, wg, p, wp, layer, tm):
    m, d = x.shape
    return pl.pallas_call(
        _ple_body,
        grid=(m // tm,),
        in_specs=[pl.BlockSpec((tm, d), lambda i: (i, 0)),
                  _lspec(layer, (1, d)),
                  _lspec(layer, (d, d), single=True),
                  _lspec(layer, (tm, PLE_DIM), lambda i: (i, 0)),
                  _lspec(layer, (PLE_DIM, d))],
        out_specs=pl.BlockSpec((tm, d), lambda i: (i, 0)),
        out_shape=jax.ShapeDtypeStruct((m, d), F32),
        scratch_shapes=[pltpu.VMEM((d, d), BF16)],
        compiler_params=_cparams(("arbitrary",)),
        name="ple",
    )(x, nw, wg, p, wp)


def _rope_cols(w):
    half = QK_ROPE // 2
    z = jnp.zeros(w.shape[:-1] + (LANES // 2 - half,), w.dtype)
    return jnp.concatenate([w[..., :half], z, w[..., half:], z], axis=-1)


def kernel(x, p, positions, norm_mix_w, w_in, q_a_norm_w, w_uq, kv_a_norm_w, w_ukv, q_norm_w, k_norm_w,
           w_o_mla, conv_w, conv_b, dt_bias, a_log, d_skip, ssm_norm_w, w_o_ssm, w_out, norm_mlp_w, w_up,
           w_down, ple_norm_w, w_ple_gate, w_ple):
    batch, seq, d = x.shape
    m = batch * seq
    depth = w_in.shape[0]
    tm_big = min(1024, m)
    tm_mid = min(512, m)
    tm_small = min(256, m)
    tq = min(512, seq)

    inv_freq = 1.0 / (ROPE_THETA ** (jnp.arange(0, QK_ROPE, 2, dtype=F32) / QK_ROPE))
    freq = _rope_cols(jnp.concatenate([inv_freq, inv_freq]))[None, :]
    sign = _rope_cols(jnp.concatenate([-jnp.ones_like(inv_freq), jnp.ones_like(inv_freq)]))[None, :]
    ct, st = _rope_tables(positions.reshape(m, 1), freq, sign, tm_big)
    ct_t, st_t = ct.T, st.T

    lane = jnp.arange(D_INNER)
    sel_row = jnp.arange(2 * LANES)[:, None]
    sel = ((sel_row % SSM_HEADS == (lane // SSM_HEADDIM)[None, :]) & (sel_row < 3 * SSM_HEADS)).astype(BF16)
    rows = jnp.arange(CHUNK)[:, None]
    eye_t = (rows == (lane % SSM_HEADDIM)[None, :]).astype(F32)
    tri_t = jnp.where(rows >= (lane % SSM_HEADDIM)[None, :], 0.0, -jnp.inf).astype(F32)
    tri64 = (rows >= jnp.arange(CHUNK)[None, :]).astype(BF16)
    sh_rows = jnp.arange((CONV_WIDTH - 1) * CHUNK)[:, None]
    shift_m = (jnp.arange(2 * CHUNK)[None, :] == CHUNK + sh_rows % CHUNK - (sh_rows // CHUNK + 1)).astype(BF16)

    s0 = Q_LORA + KV_LORA + QK_ROPE
    s1 = s0 + D_INNER + (D_INNER + 2 * SSM_GROUPS * SSM_STATE)
    s2 = s1 + SSM_HEADS

    w_lat, w_kr, w_dt = lax.optimization_barrier(
        (w_in[..., :Q_LORA + KV_LORA], w_in[..., Q_LORA + KV_LORA:s0], w_in[..., s1:s2]))
    w_small = jnp.concatenate(
        [w_lat, _rope_cols(w_kr), w_dt, jnp.zeros((depth, d, LANES - SSM_HEADS), F32)], axis=-1).astype(BF16)
    w_big = _shift_cast(w_in, (s0 - QK_ROPE, s2 - LANES), (QK_ROPE, LANES), (s1 - s0, 2 * D_MODEL), 512)
    wq4 = w_uq.reshape(depth, Q_LORA, MLA_HEADS, QK_DIM)
    wq_rope = _rope_cols(wq4[..., QK_NOPE:])
    wq = jnp.concatenate([wq4[..., :QK_NOPE], wq_rope, jnp.roll(wq_rope, LANES // 2, axis=-1)], axis=-1)
    wqt = jnp.swapaxes(wq.reshape(depth, Q_LORA, MLA_HEADS * Q_ACC_W), 1, 2).astype(BF16)
    wkv4 = w_ukv.reshape(depth, KV_LORA, MLA_HEADS, QK_NOPE + V_DIM)
    wk = wkv4[..., :QK_NOPE].reshape(depth, KV_LORA, -1).astype(BF16)
    wvt = jnp.swapaxes(wkv4[..., QK_NOPE:].reshape(depth, KV_LORA, -1), 1, 2).astype(BF16)
    w_o_mla_b = w_o_mla.astype(BF16)
    w_o_ssm_b = w_o_ssm.astype(BF16)
    w_down_b = w_down.astype(BF16)
    w_ple_b = w_ple.astype(BF16)

    rows3 = lambda v: v.reshape(depth, 1, -1).astype(F32)
    pad_h = jnp.zeros((depth, LANES - SSM_HEADS), F32)
    qw_rope = _rope_cols(q_norm_w[:, QK_NOPE:])
    qnw = jnp.concatenate([q_norm_w[:, :QK_NOPE], qw_rope, jnp.roll(qw_rope, LANES // 2, axis=-1)], axis=-1)
    qnw = jnp.broadcast_to(qnw[:, :, None], (depth, Q_ACC_W, tm_small)).astype(F32)
    kwn = rows3(k_norm_w[:, :QK_NOPE])
    kwr = rows3(_rope_cols(k_norm_w[:, QK_NOPE:]))
    dtb = rows3(jnp.concatenate([dt_bias, pad_h], axis=-1))
    alog = rows3(jnp.concatenate([a_log, pad_h], axis=-1))
    dsk = rows3(jnp.repeat(d_skip, SSM_HEADDIM, axis=-1))
    mix_nw, qa_nw, kva_nw = rows3(norm_mix_w), rows3(q_a_norm_w), rows3(kv_a_norm_w)
    ssm_nw, mlp_nw, ple_nw = rows3(ssm_norm_w), rows3(norm_mlp_w), rows3(ple_norm_w)
    conv_wh, conv_bh = 0.5 * conv_w, 0.5 * conv_b
    cwx, cwb = conv_wh[..., :D_INNER], conv_wh[..., D_INNER:]
    cbx, cbb = rows3(conv_bh[:, :D_INNER]), rows3(conv_bh[:, D_INNER:])
    p3 = p.reshape(depth, m, PLE_DIM)

    xf = x.reshape(m, d)
    for i in range(depth):
        small = _norm_matmul(xf, mix_nw, w_small, i, F32, tm_big, SMALL_N, "in_proj_small")
        big = _norm_matmul(xf, mix_nw, w_big, i, BF16, tm_big, 2048, "in_proj_big")
        qt = _q_proj(small, qa_nw, wqt, qnw, ct_t, st_t, i, tm_small)
        k, vt = _kv_proj(small, kva_nw, wk, wvt, kwn, kwr, ct, st, i, tq)
        o = _attention(qt, k, vt, batch, seq, tq, 2)
        y = _ssd(big, small, cwx, cwb, cbx, cbb, dtb, alog, dsk, ssm_nw, sel, eye_t, tri_t, tri64, shift_m, i,
                 batch, seq, 4)
        merged = _merge(o, y, w_o_mla_b, w_o_ssm_b, big, i, tm_big, 512)
        xf = _res_matmul(xf, merged, w_out, i, tm_mid)
        hidden = _norm_matmul(xf, mlp_nw, w_up, i, BF16, min(2048, m), 512, "mlp_up", relu2=True, single_x=True)
        xf = _res_matmul_tiled(xf, hidden, w_down_b, i, tm_mid, 512, "mlp_down")
        xf = _ple(xf, ple_nw, w_ple_gate, p3, w_ple_b, i, tm_mid)
    return xf.reshape(batch, seq, d)
```

```python
import functools

import jax
import jax.numpy as jnp
from jax import lax
from jax.experimental import pallas as pl
from jax.experimental.pallas import tpu as pltpu

F32 = jnp.float32
BF16 = jnp.bfloat16

D_MODEL = 2048
DEPTH = 4
CHUNK = 64
PLE_DIM = 256
EPS = 1e-6
MLA_HEADS = 16
Q_LORA = 512
KV_LORA = 512
QK_NOPE = 128
QK_ROPE = 64
V_DIM = 128
QK_DIM = QK_NOPE + QK_ROPE
ROPE_THETA = 10000.0
D_INNER = 2 * D_MODEL
SSM_HEADDIM = 64
SSM_HEADS = D_INNER // SSM_HEADDIM
SSM_GROUPS = 8
SSM_STATE = 128
GROUP_W = D_INNER // SSM_GROUPS
CONV_WIDTH = 4
D_FF = 4 * D_MODEL

LOG2_E = 1.4426950408889634
LANES = 128
ACC_ROWS = V_DIM + 16
Q_ACC_W = 3 * LANES
HEAD_W = 2 * LANES
SMALL_N = Q_LORA + KV_LORA + 2 * LANES
VMEM_LIMIT = 56 * 1024 * 1024


def _cparams(sem):
    return pltpu.CompilerParams(dimension_semantics=sem, vmem_limit_bytes=VMEM_LIMIT)


def _rms(xf, w):
    ms = jnp.mean(xf * xf, axis=-1, keepdims=True)
    return xf * lax.rsqrt(ms + EPS) * w


def _lspec(layer, block, imap=None, single=False):
    if imap is None:
        imap = lambda *g: (0,) * len(block)
    return pl.BlockSpec((None,) + tuple(block), lambda *g: (layer,) + tuple(imap(*g)),
                        pipeline_mode=pl.Buffered(1) if single else None)


def _dot(a, b):
    return jnp.dot(a, b, preferred_element_type=F32)


def _dot_nt(a, b):
    return lax.dot_general(a, b, (((1,), (1,)), ((), ())), preferred_element_type=F32)


def _rope_body(pos_ref, freq_ref, sign_ref, ct_ref, st_ref):
    ang = pos_ref[...].astype(F32) * freq_ref[...]
    ct_ref[...] = jnp.cos(ang)
    st_ref[...] = jnp.sin(ang) * sign_ref[...]


def _rope_tables(pos_col, freq, sign, tm):
    m = pos_col.shape[0]
    return pl.pallas_call(
        _rope_body,
        grid=(m // tm,),
        in_specs=[pl.BlockSpec((tm, 1), lambda i: (i, 0)),
                  pl.BlockSpec((1, LANES), lambda i: (0, 0)),
                  pl.BlockSpec((1, LANES), lambda i: (0, 0))],
        out_specs=[pl.BlockSpec((tm, LANES), lambda i: (i, 0)),
                   pl.BlockSpec((tm, LANES), lambda i: (i, 0))],
        out_shape=[jax.ShapeDtypeStruct((m, LANES), F32)] * 2,
        compiler_params=_cparams(("parallel",)),
        name="rope_tables",
    )(pos_col, freq, sign)


def _norm_mm_body(x_ref, nw_ref, w_ref, o_ref, h_ref, *, relu2):
    @pl.when(pl.program_id(1) == 0)
    def _():
        h_ref[...] = _rms(x_ref[...], nw_ref[...]).astype(BF16)

    y = _dot(h_ref[...], w_ref[...].astype(BF16))
    if relu2:
        y = jnp.square(jnp.maximum(y, 0.0))
    o_ref[...] = y.astype(o_ref.dtype)


def _norm_matmul(x, nw, w, layer, out_dtype, tm, tn, name, relu2=False, single_x=False):
    m, k = x.shape
    n = w.shape[2]
    return pl.pallas_call(
        functools.partial(_norm_mm_body, relu2=relu2),
        grid=(m // tm, n // tn),
        in_specs=[pl.BlockSpec((tm, k), lambda i, j: (i, 0), pipeline_mode=pl.Buffered(1) if single_x else None),
                  _lspec(layer, (1, k)),
                  _lspec(layer, (k, tn), lambda i, j: (0, j))],
        out_specs=pl.BlockSpec((tm, tn), lambda i, j: (i, j)),
        out_shape=jax.ShapeDtypeStruct((m, n), out_dtype),
        scratch_shapes=[pltpu.VMEM((tm, k), BF16)],
        compiler_params=_cparams(("parallel", "arbitrary")),
        name=name,
    )(x, nw, w)


def _shift_cast_body(a_ref, b_ref, o_ref, *, tiles0, shifts):
    v = jnp.concatenate([a_ref[...], b_ref[...]], axis=1)
    tn = o_ref.shape[1]
    first = pl.program_id(1) < tiles0

    @pl.when(first)
    def _():
        o_ref[...] = v[:, shifts[0]:shifts[0] + tn].astype(BF16)

    @pl.when(jnp.logical_not(first))
    def _():
        o_ref[...] = v[:, shifts[1]:shifts[1] + tn].astype(BF16)


def _shift_cast(w, bases, shifts, widths, tn):
    depth, k, _ = w.shape
    assert all(b % tn == 0 and n % tn == 0 and 0 < s <= LANES for b, s, n in zip(bases, shifts, widths))
    tiles0 = widths[0] // tn
    sub = tn // LANES

    def a_col(l, j):
        return jnp.where(j < tiles0, bases[0] // tn + j, bases[1] // tn + j - tiles0)

    def b_col(l, j):
        return jnp.where(j < tiles0, (bases[0] + tn) // LANES + j * sub, (bases[1] + tn) // LANES + (j - tiles0) * sub)

    return pl.pallas_call(
        functools.partial(_shift_cast_body, tiles0=tiles0, shifts=tuple(shifts)),
        grid=(depth, (widths[0] + widths[1]) // tn),
        in_specs=[pl.BlockSpec((None, k, tn), lambda l, j: (l, 0, a_col(l, j))),
                  pl.BlockSpec((None, k, LANES), lambda l, j: (l, 0, b_col(l, j)))],
        out_specs=pl.BlockSpec((None, k, tn), lambda l, j: (l, 0, j)),
        out_shape=jax.ShapeDtypeStruct((depth, k, widths[0] + widths[1]), BF16),
        compiler_params=_cparams(("parallel", "parallel")),
        name="relayout_w_in",
    )(w, w)


def _rope_apply(v, c, s):
    return v * c + pltpu.roll(v, 64, 1) * s


def _q_body(c_ref, nw_ref, wt_ref, qnw_ref, ct_ref, st_ref, o_ref):
    cn = _rms(c_ref[...], nw_ref[...]).astype(BF16)
    acc = _dot_nt(wt_ref[...], cn)
    wn = qnw_ref[:LANES, :]
    wrc = qnw_ref[LANES:2 * LANES, :] * ct_ref[...]
    wxs = qnw_ref[2 * LANES:, :] * st_ref[...]
    scale = QK_DIM ** -0.5 * LOG2_E
    for h in range(MLA_HEADS):
        qn = acc[Q_ACC_W * h:Q_ACC_W * h + LANES, :]
        qr = acc[Q_ACC_W * h + LANES:Q_ACC_W * h + 2 * LANES, :]
        qx = acc[Q_ACC_W * h + 2 * LANES:Q_ACC_W * (h + 1), :]
        ssq = jnp.sum(qn * qn + qr * qr, axis=0, keepdims=True)
        rs = lax.rsqrt(ssq / QK_DIM + EPS) * scale
        o_ref[0, HEAD_W * h:HEAD_W * h + LANES, :] = (qn * rs * wn).astype(BF16)
        o_ref[0, HEAD_W * h + LANES:HEAD_W * (h + 1), :] = ((qr * wrc + qx * wxs) * rs).astype(BF16)


def _q_proj(small, nw, wqt, qnw, ct_t, st_t, layer, tm):
    m = small.shape[0]
    n = MLA_HEADS * HEAD_W
    return pl.pallas_call(
        _q_body,
        grid=(m // tm,),
        in_specs=[pl.BlockSpec((tm, Q_LORA), lambda i: (i, 0)),
                  _lspec(layer, (1, Q_LORA)),
                  _lspec(layer, (MLA_HEADS * Q_ACC_W, Q_LORA)),
                  _lspec(layer, (Q_ACC_W, tm)),
                  pl.BlockSpec((LANES, tm), lambda i: (0, i)),
                  pl.BlockSpec((LANES, tm), lambda i: (0, i))],
        out_specs=pl.BlockSpec((1, n, tm), lambda i: (i, 0, 0)),
        out_shape=jax.ShapeDtypeStruct((m // tm, n, tm), BF16),
        compiler_params=_cparams(("parallel",)),
        name="q_proj",
    )(small, nw, wqt, qnw, ct_t, st_t)


def _kv_body(c_ref, kr_ref, nw_ref, wk_ref, wvt_ref, kwn_ref, kwr_ref, ct_ref, st_ref, k_ref, vt_ref):
    cn = _rms(c_ref[...], nw_ref[...]).astype(BF16)
    acc = _dot(cn, wk_ref[...])
    vt_ref[0] = _dot_nt(wvt_ref[...], cn).astype(BF16)
    c = ct_ref[...]
    s = st_ref[...]
    kr = kr_ref[...]
    wn = kwn_ref[...]
    wr = kwr_ref[...]
    ssr = jnp.sum(kr * kr, axis=-1, keepdims=True)
    kr_rot = _rope_apply(kr * wr, c, s)
    for h in range(MLA_HEADS):
        kn = acc[:, LANES * h:LANES * (h + 1)]
        ssq = jnp.sum(kn * kn, axis=-1, keepdims=True) + ssr
        rs = lax.rsqrt(ssq / QK_DIM + EPS)
        k_ref[:, HEAD_W * h:HEAD_W * h + LANES] = (kn * rs * wn).astype(BF16)
        k_ref[:, HEAD_W * h + LANES:HEAD_W * (h + 1)] = (kr_rot * rs).astype(BF16)


def _kv_proj(small, nw, wk, wvt, kwn, kwr, ct, st, layer, tm):
    m = small.shape[0]
    nk = MLA_HEADS * HEAD_W
    nv = MLA_HEADS * V_DIM
    kr_blk = (Q_LORA + KV_LORA) // LANES
    return pl.pallas_call(
        _kv_body,
        grid=(m // tm,),
        in_specs=[pl.BlockSpec((tm, KV_LORA), lambda i: (i, 1)),
                  pl.BlockSpec((tm, LANES), lambda i: (i, kr_blk)),
                  _lspec(layer, (1, KV_LORA)),
                  _lspec(layer, (KV_LORA, MLA_HEADS * QK_NOPE)),
                  _lspec(layer, (nv, KV_LORA)),
                  _lspec(layer, (1, LANES)),
                  _lspec(layer, (1, LANES)),
                  pl.BlockSpec((tm, LANES), lambda i: (i, 0)),
                  pl.BlockSpec((tm, LANES), lambda i: (i, 0))],
        out_specs=[pl.BlockSpec((tm, nk), lambda i: (i, 0)),
                   pl.BlockSpec((1, nv, tm), lambda i: (i, 0, 0))],
        out_shape=[jax.ShapeDtypeStruct((m, nk), BF16), jax.ShapeDtypeStruct((m // tm, nv, tm), BF16)],
        compiler_params=_cparams(("parallel",)),
        name="kv_proj",
    )(small, small, nw, wk, wvt, kwn, kwr, ct, st)


def _attn_body(q_ref, k_ref, vt_ref, o_ref, s0_scr, s1_scr, p0_scr, p1_scr, m_scr, a_scr, acc_scr, *, tq, nq, hb, qsub):
    s_scr = (s0_scr, s1_scr)
    p_scr = (p0_scr, p1_scr)
    tk = tq // 2
    key_chunk = lax.broadcasted_iota(jnp.int32, (tk, tq), 0) // CHUNK
    qry_chunk = lax.broadcasted_iota(jnp.int32, (tk, tq), 1) // CHUNK
    diag_masks = [key_chunk + h * (tk // CHUNK) <= qry_chunk for h in range(2)]
    ones_rows = jnp.ones((ACC_ROWS - V_DIM, tk), BF16)

    def scores(qs, kpair, h, mask=None):
        koff = pl.multiple_of(kpair * tq + h * tk, tk)
        for hh in range(hb):
            s = _dot(k_ref[pl.ds(koff, tk), HEAD_W * hh:HEAD_W * (hh + 1)], qs[hh])
            s_scr[h][hh] = s if mask is None else jnp.where(mask, s, -jnp.inf)

    def pv(kpair, h):
        return [_dot(jnp.concatenate([vt_ref[kpair, V_DIM * hh:V_DIM * (hh + 1), h * tk:(h + 1) * tk],
                                      ones_rows], axis=0), p_scr[h][hh])
                for hh in range(hb)]

    def softmax(h, pvs):
        for hh in range(hb):
            m = m_scr[hh]
            s = s_scr[h][hh]
            m_new = jnp.maximum(m, jnp.max(s, axis=0, keepdims=True))
            p_scr[h][hh] = jnp.exp2((s - m_new).astype(BF16))
            alpha = jnp.exp2(m - m_new)
            acc_scr[hh] = alpha * (acc_scr[hh] + a_scr[hh] * pvs[hh])
            a_scr[hh] = alpha
            m_scr[hh] = m_new

    def q_tiles(qi):
        return [jnp.concatenate([q_ref[qsub * qi + u, HEAD_W * hh:HEAD_W * (hh + 1), :] for u in range(qsub)], axis=1)
                for hh in range(hb)]

    def diag_scores(qi):
        qs = q_tiles(qi)
        for h in range(2):
            scores(qs, qi, h, diag_masks[h])

    diag_scores(0)
    a_scr[...] = jnp.zeros(a_scr.shape, F32)
    acc_scr[...] = jnp.zeros(acc_scr.shape, F32)
    for h in range(2):
        p_scr[h][...] = jnp.zeros(p_scr[h].shape, BF16)

    def q_block(qi, c):
        qoff = pl.multiple_of(qi * tq, tq)
        qs = q_tiles(qi)
        m_scr[...] = jnp.full(m_scr.shape, -jnp.inf, F32)

        def kpair_of(r):
            return jnp.where(r == 0, qi, r - 1)

        def body(r, c2):
            prev = kpair_of(jnp.maximum(r - 1, 0))
            for h in range(2):
                pvs = pv(prev, h)
                softmax(h, pvs)
                scores(qs, r, h)
            return c2

        lax.fori_loop(0, qi, body, 0)
        prev = kpair_of(jnp.maximum(qi - 1, 0))
        nxt = jnp.minimum(qi + 1, nq - 1)
        qn = q_tiles(nxt)
        for h in range(2):
            softmax(h, pv(prev, h))
            scores(qn, nxt, h, diag_masks[h])
        last = kpair_of(qi)
        pv0, pv1 = pv(last, 0), pv(last, 1)
        for hh in range(hb):
            acc = acc_scr[hh] + a_scr[hh] * pv0[hh] + pv1[hh]
            o_ref[pl.ds(qoff, tq), V_DIM * hh:V_DIM * (hh + 1)] = (
                acc[:V_DIM] / acc[V_DIM:V_DIM + 1]).T.astype(BF16)
        return c

    lax.fori_loop(0, nq, q_block, 0)


def _attention(qt, k, vt, batch, seq, tq, hb):
    m = k.shape[0]
    nq = seq // tq
    tk = tq // 2
    assert vt.shape[2] == tq
    qsub = tq // qt.shape[2]
    return pl.pallas_call(
        functools.partial(_attn_body, tq=tq, nq=nq, hb=hb, qsub=qsub),
        grid=(batch, MLA_HEADS // hb),
        in_specs=[pl.BlockSpec((nq * qsub, hb * HEAD_W, tq // qsub), lambda b, h: (b, h, 0)),
                  pl.BlockSpec((seq, hb * HEAD_W), lambda b, h: (b, h)),
                  pl.BlockSpec((nq, hb * V_DIM, tq), lambda b, h: (b, h, 0))],
        out_specs=pl.BlockSpec((seq, hb * V_DIM), lambda b, h: (b, h)),
        out_shape=jax.ShapeDtypeStruct((m, MLA_HEADS * V_DIM), BF16),
        scratch_shapes=[pltpu.VMEM((hb, tk, tq), F32), pltpu.VMEM((hb, tk, tq), F32),
                        pltpu.VMEM((hb, tk, tq), BF16), pltpu.VMEM((hb, tk, tq), BF16),
                        pltpu.VMEM((hb, 1, tq), F32), pltpu.VMEM((hb, 1, tq), F32),
                        pltpu.VMEM((hb, ACC_ROWS, tq), F32)],
        compiler_params=_cparams(("parallel", "arbitrary")),
        name="attention",
    )(qt, k, vt)


def _silu_of_half(h):
    return h + h * jnp.tanh(h)


def _silu(y):
    return _silu_of_half(0.5 * y)


def _split3(v):
    hi = v.astype(BF16)
    r = v - hi.astype(F32)
    mid = r.astype(BF16)
    lo = (r - mid.astype(F32)).astype(BF16)
    return hi, mid, lo


def _ssd_body(z_ref, x_ref, bc_ref, dt_ref, cwx_ref, cwb_ref, cbx_ref, cbb_ref, dtb_ref, alog_ref,
              dsk_ref, nw_ref, sel_ref, eye_ref, tri_ref, tri64_ref, shift_ref, y_ref, state_ref, xhalo, bchalo,
              *, cpb):
    t = CHUNK
    gn = SSM_GROUPS * SSM_STATE

    @pl.when(pl.program_id(1) == 0)
    def _():
        state_ref[...] = jnp.zeros_like(state_ref)
        xhalo[...] = jnp.zeros_like(xhalo)
        bchalo[...] = jnp.zeros_like(bchalo)

    def conv_silu(raw_ref, halo, w_ref, b_ref):
        raws = [raw_ref[c * t:(c + 1) * t, :] for c in range(cpb)]
        prevs = [halo[...]] + raws[:-1]
        halo[...] = raws[-1]
        shifted = [_dot(shift_ref[...], jnp.concatenate([prevs[c], raws[c]], axis=0)) for c in range(cpb)]
        outs = []
        for c in range(cpb):
            y = None
            for tap in range(CONV_WIDTH):
                shift = CONV_WIDTH - 1 - tap
                xt = raws[c].astype(F32) if shift == 0 else shifted[c][(shift - 1) * t:shift * t, :]
                term = xt * w_ref[tap:tap + 1, :]
                y = term if y is None else y + term
            outs.append(_silu_of_half(y + b_ref[...]))
        return outs

    xs_all = conv_silu(x_ref, xhalo, cwx_ref, cbx_ref)
    bc_all = conv_silu(bc_ref, bchalo, cwb_ref, cbb_ref)

    head_lane = lax.broadcasted_iota(jnp.int32, (t, LANES), 1) < SSM_HEADS
    neg_a = -jnp.exp(alog_ref[...])
    tri64 = tri64_ref[...]
    stacks = []
    for c in range(cpb):
        dt = jnp.where(head_lane, jax.nn.softplus(dt_ref[c * t:(c + 1) * t, :] + dtb_ref[...]), 0.0)
        h3 = _split3(dt * neg_a)
        a_cum = (_dot(tri64, h3[0]) + _dot(tri64, h3[1]) + _dot(tri64, h3[2])) * LOG2_E
        stacks += [a_cum, dt]
    s3 = [part.astype(F32) for part in _split3(jnp.concatenate(stacks, axis=0))]
    packed = jnp.concatenate([(s3[0] + pltpu.roll(s3[1], SSM_HEADS, 1)).astype(BF16), s3[2].astype(BF16)], axis=1)
    ex_all = _dot(packed, sel_ref[...])
    left = lax.broadcasted_iota(jnp.int32, (t, LANES), 1) < SSM_HEADDIM
    zero_b = jnp.zeros((t, LANES), BF16)

    pre = []
    for c in range(cpb):
        ex = ex_all[2 * t * c:2 * t * (c + 1)]
        a_col = ex[:t]
        ea = jnp.exp2(a_col)
        a_row = jnp.sum(jnp.where(eye_ref[...] > 0, a_col, 0.0), axis=0, keepdims=True)
        decay = jnp.exp2(a_col - a_row + tri_ref[...])
        xdt = xs_all[c] * ex[t:2 * t]
        pre.append((decay, ea, xdt.astype(BF16), (xdt * jnp.exp2(a_col[t - 1:t, :] - a_col)).astype(BF16)))

    for c in range(cpb):
        decay, ea, xdt_b, xw_b = pre[c]
        xs, bc = xs_all[c], bc_all[c]
        chunk_decay = ea[t - 1:t, :]
        r0 = c * t
        for g in range(SSM_GROUPS):
            lo, hi = GROUP_W * g, GROUP_W * (g + 1)
            bg = bc[:, SSM_STATE * g:SSM_STATE * (g + 1)]
            cg_b = bc[:, gn + SSM_STATE * g:gn + SSM_STATE * (g + 1)].astype(BF16)
            bg_b = bg.astype(BF16)
            cb2 = _dot_nt(cg_b, jnp.concatenate([bg_b, bg_b], axis=0))
            yd = []
            for j in range(GROUP_W // LANES):
                l0 = lo + LANES * j
                mm = (cb2 * decay[:, l0:l0 + LANES]).astype(BF16)
                xp = xdt_b[:, l0:l0 + LANES]
                rhs = jnp.concatenate([jnp.where(left, xp, zero_b), jnp.where(left, zero_b, xp)], axis=0)
                yd.append(_dot(mm, rhs))
            y_diag = jnp.concatenate(yd, axis=1)
            st = state_ref[g]
            y_off = _dot(cg_b, st.astype(BF16)) * ea[:, lo:hi]
            inc = _dot(bg.T.astype(BF16), xw_b[:, lo:hi])
            state_ref[g] = st * chunk_decay[:, lo:hi] + inc
            yg = y_diag + y_off + dsk_ref[:, lo:hi] * xs[:, lo:hi]
            yg = yg * _silu(z_ref[r0:r0 + t, lo:hi].astype(F32))
            ms = jnp.mean(yg * yg, axis=-1, keepdims=True)
            y_ref[r0:r0 + t, lo:hi] = (yg * lax.rsqrt(ms + EPS) * nw_ref[:, lo:hi]).astype(BF16)


def _ssd(big, small, cwx, cwb, cbx, cbb, dtb, alog, dsk, nw, sel, eye_t, tri_t, tri64, shift_m, layer, batch, seq,
         cpb):
    m = big.shape[0]
    t = CHUNK
    tb = cpb * t
    nb = seq // tb
    bcw = 2 * SSM_GROUPS * SSM_STATE
    bc_blk = (2 * D_INNER) // bcw
    dt_blk = (SMALL_N - LANES) // LANES
    row = lambda b, c: b * nb + c
    const = lambda shape: pl.BlockSpec(shape, lambda b, c: (0,) * len(shape))
    return pl.pallas_call(
        functools.partial(_ssd_body, cpb=cpb),
        grid=(batch, nb),
        in_specs=[pl.BlockSpec((tb, D_INNER), lambda b, c: (row(b, c), 0)),
                  pl.BlockSpec((tb, D_INNER), lambda b, c: (row(b, c), 1)),
                  pl.BlockSpec((tb, bcw), lambda b, c: (row(b, c), bc_blk)),
                  pl.BlockSpec((tb, LANES), lambda b, c: (row(b, c), dt_blk)),
                  _lspec(layer, (CONV_WIDTH, D_INNER)), _lspec(layer, (CONV_WIDTH, bcw)),
                  _lspec(layer, (1, D_INNER)), _lspec(layer, (1, bcw)),
                  _lspec(layer, (1, LANES)), _lspec(layer, (1, LANES)),
                  _lspec(layer, (1, D_INNER)), _lspec(layer, (1, D_INNER)),
                  const((2 * LANES, D_INNER)), const((t, D_INNER)), const((t, D_INNER)), const((t, t)),
                  const(((CONV_WIDTH - 1) * t, 2 * t))],
        out_specs=pl.BlockSpec((tb, D_INNER), lambda b, c: (row(b, c), 0)),
        out_shape=jax.ShapeDtypeStruct((m, D_INNER), BF16),
        scratch_shapes=[pltpu.VMEM((SSM_GROUPS, SSM_STATE, GROUP_W), F32),
                        pltpu.VMEM((t, D_INNER), BF16),
                        pltpu.VMEM((t, bcw), BF16)],
        compiler_params=_cparams(("parallel", "arbitrary")),
        name="ssd",
    )(big, big, big, small, cwx, cwb, cbx, cbb, dtb, alog, dsk, nw, sel, eye_t, tri_t, tri64, shift_m)


def _merge_body(o_ref, y_ref, wa_ref, wb_ref, g0_ref, g1_ref, out_ref):
    ya = _dot(o_ref[...], wa_ref[...])
    yb = _dot(y_ref[...], wb_ref[...])
    g0 = jax.nn.sigmoid(g0_ref[...].astype(F32))
    g1 = jax.nn.sigmoid(g1_ref[...].astype(F32))
    out_ref[...] = (g0 * ya + g1 * yb).astype(out_ref.dtype)


def _merge(o, y, wa, wb, big, layer, tm, tn):
    m = o.shape[0]
    g0_blk = (2 * D_INNER + 2 * SSM_GROUPS * SSM_STATE) // tn
    g1_blk = g0_blk + D_MODEL // tn
    return pl.pallas_call(
        _merge_body,
        grid=(m // tm, D_MODEL // tn),
        in_specs=[pl.BlockSpec((tm, o.shape[1]), lambda i, j: (i, 0)),
                  pl.BlockSpec((tm, y.shape[1]), lambda i, j: (i, 0)),
                  _lspec(layer, (wa.shape[1], tn), lambda i, j: (0, j)),
                  _lspec(layer, (wb.shape[1], tn), lambda i, j: (0, j)),
                  pl.BlockSpec((tm, tn), lambda i, j: (i, g0_blk + j)),
                  pl.BlockSpec((tm, tn), lambda i, j: (i, g1_blk + j))],
        out_specs=pl.BlockSpec((tm, tn), lambda i, j: (i, j)),
        out_shape=jax.ShapeDtypeStruct((m, D_MODEL), BF16),
        compiler_params=_cparams(("parallel", "arbitrary")),
        name="merge",
    )(o, y, wa, wb, big, big)


def _res_mm_body(x_ref, a_ref, w_ref, o_ref):
    o_ref[...] = x_ref[...] + _dot(a_ref[...], w_ref[...])


def _res_mm_f32w_body(x_ref, a_ref, w_ref, o_ref, wb_ref):
    @pl.when(pl.program_id(0) == 0)
    def _():
        wb_ref[...] = w_ref[...].astype(BF16)

    o_ref[...] = x_ref[...] + _dot(a_ref[...], wb_ref[...])


def _res_matmul(x, a, w, layer, tm):
    m, n = x.shape
    k = a.shape[1]
    return pl.pallas_call(
        _res_mm_f32w_body,
        grid=(m // tm,),
        in_specs=[pl.BlockSpec((tm, n), lambda i: (i, 0)),
                  pl.BlockSpec((tm, k), lambda i: (i, 0)),
                  _lspec(layer, (k, n), single=True)],
        out_specs=pl.BlockSpec((tm, n), lambda i: (i, 0)),
        out_shape=jax.ShapeDtypeStruct((m, n), F32),
        scratch_shapes=[pltpu.VMEM((k, n), BF16)],
        compiler_params=_cparams(("arbitrary",)),
        name="out_proj",
    )(x, a, w)


def _res_matmul_tiled(x, a, w, layer, tm, tn, name):
    m, n = x.shape
    k = a.shape[1]
    return pl.pallas_call(
        _res_mm_body,
        grid=(m // tm, n // tn),
        in_specs=[pl.BlockSpec((tm, tn), lambda i, j: (i, j)),
                  pl.BlockSpec((tm, k), lambda i, j: (i, 0)),
                  _lspec(layer, (k, tn), lambda i, j: (0, j))],
        out_specs=pl.BlockSpec((tm, tn), lambda i, j: (i, j)),
        out_shape=jax.ShapeDtypeStruct((m, n), F32),
        compiler_params=_cparams(("parallel", "arbitrary")),
        name=name,
    )(x, a, w)


def _ple_body(x_ref, nw_ref, wg_ref, p_ref, wp_ref, o_ref, wgb_ref):
    @pl.when(pl.program_id(0) == 0)
    def _():
        wgb_ref[...] = wg_ref[...].astype(BF16)

    x = x_ref[...]
    h = _rms(x, nw_ref[...]).astype(BF16)
    gate = jax.nn.sigmoid(_dot(h, wgb_ref[...]))
    pe = _dot(p_ref[...].astype(BF16), wp_ref[...])
    o_ref[...] = x + pe * gate


def _ple(x, nw, wg, p, wp, layer, tm):
    m, d = x.shape
    return pl.pallas_call(
        _ple_body,
        grid=(m // tm,),
        in_specs=[pl.BlockSpec((tm, d), lambda i: (i, 0)),
                  _lspec(layer, (1, d)),
                  _lspec(layer, (d, d), single=True),
                  _lspec(layer, (tm, PLE_DIM), lambda i: (i, 0)),
                  _lspec(layer, (PLE_DIM, d))],
        out_specs=pl.BlockSpec((tm, d), lambda i: (i, 0)),
        out_shape=jax.ShapeDtypeStruct((m, d), F32),
        scratch_shapes=[pltpu.VMEM((d, d), BF16)],
        compiler_params=_cparams(("arbitrary",)),
        name="ple",
    )(x, nw, wg, p, wp)


def _rope_cols(w):
    half = QK_ROPE // 2
    z = jnp.zeros(w.shape[:-1] + (LANES // 2 - half,), w.dtype)
    return jnp.concatenate([w[..., :half], z, w[..., half:], z], axis=-1)


def kernel(x, p, positions, norm_mix_w, w_in, q_a_norm_w, w_uq, kv_a_norm_w, w_ukv, q_norm_w, k_norm_w,
           w_o_mla, conv_w, conv_b, dt_bias, a_log, d_skip, ssm_norm_w, w_o_ssm, w_out, norm_mlp_w, w_up,
           w_down, ple_norm_w, w_ple_gate, w_ple):
    batch, seq, d = x.shape
    m = batch * seq
    depth = w_in.shape[0]
    tm_big = min(1024, m)
    tm_mid = min(512, m)
    tm_small = min(256, m)
    tq = min(512, seq)

    inv_freq = 1.0 / (ROPE_THETA ** (jnp.arange(0, QK_ROPE, 2, dtype=F32) / QK_ROPE))
    freq = _rope_cols(jnp.concatenate([inv_freq, inv_freq]))[None, :]
    sign = _rope_cols(jnp.concatenate([-jnp.ones_like(inv_freq), jnp.ones_like(inv_freq)]))[None, :]
    ct, st = _rope_tables(positions.reshape(m, 1), freq, sign, tm_big)
    ct_t, st_t = ct.T, st.T

    lane = jnp.arange(D_INNER)
    sel_row = jnp.arange(2 * LANES)[:, None]
    sel = ((sel_row % SSM_HEADS == (lane // SSM_HEADDIM)[None, :]) & (sel_row < 3 * SSM_HEADS)).astype(BF16)
    rows = jnp.arange(CHUNK)[:, None]
    eye_t = (rows == (lane % SSM_HEADDIM)[None, :]).astype(F32)
    tri_t = jnp.where(rows >= (lane % SSM_HEADDIM)[None, :], 0.0, -jnp.inf).astype(F32)
    tri64 = (rows >= jnp.arange(CHUNK)[None, :]).astype(BF16)
    sh_rows = jnp.arange((CONV_WIDTH - 1) * CHUNK)[:, None]
    shift_m = (jnp.arange(2 * CHUNK)[None, :] == CHUNK + sh_rows % CHUNK - (sh_rows // CHUNK + 1)).astype(BF16)

    s0 = Q_LORA + KV_LORA + QK_ROPE
    s1 = s0 + D_INNER + (D_INNER + 2 * SSM_GROUPS * SSM_STATE)
    s2 = s1 + SSM_HEADS

    w_lat, w_kr, w_dt = lax.optimization_barrier(
        (w_in[..., :Q_LORA + KV_LORA], w_in[..., Q_LORA + KV_LORA:s0], w_in[..., s1:s2]))
    w_small = jnp.concatenate(
        [w_lat, _rope_cols(w_kr), w_dt, jnp.zeros((depth, d, LANES - SSM_HEADS), F32)], axis=-1).astype(BF16)
    w_big = _shift_cast(w_in, (s0 - QK_ROPE, s2 - LANES), (QK_ROPE, LANES), (s1 - s0, 2 * D_MODEL), 1024)
    wq4 = w_uq.reshape(depth, Q_LORA, MLA_HEADS, QK_DIM)
    wq_rope = _rope_cols(wq4[..., QK_NOPE:])
    wq = jnp.concatenate([wq4[..., :QK_NOPE], wq_rope, jnp.roll(wq_rope, LANES // 2, axis=-1)], axis=-1)
    wqt = jnp.swapaxes(wq.reshape(depth, Q_LORA, MLA_HEADS * Q_ACC_W), 1, 2).astype(BF16)
    wkv4 = w_ukv.reshape(depth, KV_LORA, MLA_HEADS, QK_NOPE + V_DIM)
    wk = wkv4[..., :QK_NOPE].reshape(depth, KV_LORA, -1).astype(BF16)
    wvt = jnp.swapaxes(wkv4[..., QK_NOPE:].reshape(depth, KV_LORA, -1), 1, 2).astype(BF16)
    w_o_mla_b = w_o_mla.astype(BF16)
    w_o_ssm_b = w_o_ssm.astype(BF16)
    w_down_b = w_down.astype(BF16)
    w_ple_b = w_ple.astype(BF16)

    rows3 = lambda v: v.reshape(depth, 1, -1).astype(F32)
    pad_h = jnp.zeros((depth, LANES - SSM_HEADS), F32)
    qw_rope = _rope_cols(q_norm_w[:, QK_NOPE:])
    qnw = jnp.concatenate([q_norm_w[:, :QK_NOPE], qw_rope, jnp.roll(qw_rope, LANES // 2, axis=-1)], axis=-1)
    qnw = jnp.broadcast_to(qnw[:, :, None], (depth, Q_ACC_W, tm_small)).astype(F32)
    kwn = rows3(k_norm_w[:, :QK_NOPE])
    kwr = rows3(_rope_cols(k_norm_w[:, QK_NOPE:]))
    dtb = rows3(jnp.concatenate([dt_bias, pad_h], axis=-1))
    alog = rows3(jnp.concatenate([a_log, pad_h], axis=-1))
    dsk = rows3(jnp.repeat(d_skip, SSM_HEADDIM, axis=-1))
    mix_nw, qa_nw, kva_nw = rows3(norm_mix_w), rows3(q_a_norm_w), rows3(kv_a_norm_w)
    ssm_nw, mlp_nw, ple_nw = rows3(ssm_norm_w), rows3(norm_mlp_w), rows3(ple_norm_w)
    conv_wh, conv_bh = 0.5 * conv_w, 0.5 * conv_b
    cwx, cwb = conv_wh[..., :D_INNER], conv_wh[..., D_INNER:]
    cbx, cbb = rows3(conv_bh[:, :D_INNER]), rows3(conv_bh[:, D_INNER:])
    p3 = p.reshape(depth, m, PLE_DIM)

    xf = x.reshape(m, d)
    for i in range(depth):
        small = _norm_matmul(xf, mix_nw, w_small, i, F32, tm_big, SMALL_N, "in_proj_small")
        big = _norm_matmul(xf, mix_nw, w_big, i, BF16, tm_big, 2048, "in_proj_big")
        qt = _q_proj(small, qa_nw, wqt, qnw, ct_t, st_t, i, tm_small)
        k, vt = _kv_proj(small, kva_nw, wk, wvt, kwn, kwr, ct, st, i, tq)
        o = _attention(qt, k, vt, batch, seq, tq, 2)
        y = _ssd(big, small, cwx, cwb, cbx, cbb, dtb, alog, dsk, ssm_nw, sel, eye_t, tri_t, tri64, shift_m, i,
                 batch, seq, 4)
        merged = _merge(o, y, w_o_mla_b, w_o_ssm_b, big, i, tm_big, 512)
        xf = _res_matmul(xf, merged, w_out, i, tm_mid)
        hidden = _norm_matmul(xf, mlp_nw, w_up, i, BF16, min(2048, m), 512, "mlp_up", relu2=True, single_x=True)
        xf = _res_matmul_tiled(xf, hidden, w_down_b, i, tm_mid, 512, "mlp_down")
        xf = _ple(xf, ple_nw, w_ple_gate, p3, w_ple_b, i, tm_mid)
    return xf.reshape(batch, seq, d)
```

```python
import functools

import jax
import jax.numpy as jnp
from jax import lax
from jax.experimental import pallas as pl
from jax.experimental.pallas import tpu as pltpu

F32 = jnp.float32
BF16 = jnp.bfloat16

D_MODEL = 2048
DEPTH = 4
CHUNK = 64
PLE_DIM = 256
EPS = 1e-6
MLA_HEADS = 16
Q_LORA = 512
KV_LORA = 512
QK_NOPE = 128
QK_ROPE = 64
V_DIM = 128
QK_DIM = QK_NOPE + QK_ROPE
ROPE_THETA = 10000.0
D_INNER = 2 * D_MODEL
SSM_HEADDIM = 64
SSM_HEADS = D_INNER // SSM_HEADDIM
SSM_GROUPS = 8
SSM_STATE = 128
GROUP_W = D_INNER // SSM_GROUPS
CONV_WIDTH = 4
D_FF = 4 * D_MODEL

LOG2_E = 1.4426950408889634
LANES = 128
ACC_ROWS = V_DIM + 16
Q_ACC_W = 3 * LANES
HEAD_W = 2 * LANES
SMALL_N = Q_LORA + KV_LORA + 2 * LANES
VMEM_LIMIT = 56 * 1024 * 1024


def _cparams(sem):
    return pltpu.CompilerParams(dimension_semantics=sem, vmem_limit_bytes=VMEM_LIMIT)


def _rms(xf, w):
    ms = jnp.mean(xf * xf, axis=-1, keepdims=True)
    return xf * lax.rsqrt(ms + EPS) * w


def _lspec(layer, block, imap=None, single=False):
    if imap is None:
        imap = lambda *g: (0,) * len(block)
    return pl.BlockSpec((None,) + tuple(block), lambda *g: (layer,) + tuple(imap(*g)),
                        pipeline_mode=pl.Buffered(1) if single else None)


def _dot(a, b):
    return jnp.dot(a, b, preferred_element_type=F32)


def _dot_nt(a, b):
    return lax.dot_general(a, b, (((1,), (1,)), ((), ())), preferred_element_type=F32)


def _rope_body(pos_ref, freq_ref, sign_ref, ct_ref, st_ref):
    ang = pos_ref[...].astype(F32) * freq_ref[...]
    ct_ref[...] = jnp.cos(ang)
    st_ref[...] = jnp.sin(ang) * sign_ref[...]


def _rope_tables(pos_col, freq, sign, tm):
    m = pos_col.shape[0]
    return pl.pallas_call(
        _rope_body,
        grid=(m // tm,),
        in_specs=[pl.BlockSpec((tm, 1), lambda i: (i, 0)),
                  pl.BlockSpec((1, LANES), lambda i: (0, 0)),
                  pl.BlockSpec((1, LANES), lambda i: (0, 0))],
        out_specs=[pl.BlockSpec((tm, LANES), lambda i: (i, 0)),
                   pl.BlockSpec((tm, LANES), lambda i: (i, 0))],
        out_shape=[jax.ShapeDtypeStruct((m, LANES), F32)] * 2,
        compiler_params=_cparams(("parallel",)),
        name="rope_tables",
    )(pos_col, freq, sign)


def _norm_mm_body(x_ref, nw_ref, w_ref, o_ref, h_ref, *, relu2):
    @pl.when(pl.program_id(1) == 0)
    def _():
        h_ref[...] = _rms(x_ref[...], nw_ref[...]).astype(BF16)

    y = _dot(h_ref[...], w_ref[...].astype(BF16))
    if relu2:
        y = jnp.square(jnp.maximum(y, 0.0))
    o_ref[...] = y.astype(o_ref.dtype)


def _norm_matmul(x, nw, w, layer, out_dtype, tm, tn, name, relu2=False, single_x=False):
    m, k = x.shape
    n = w.shape[2]
    return pl.pallas_call(
        functools.partial(_norm_mm_body, relu2=relu2),
        grid=(m // tm, n // tn),
        in_specs=[pl.BlockSpec((tm, k), lambda i, j: (i, 0), pipeline_mode=pl.Buffered(1) if single_x else None),
                  _lspec(layer, (1, k)),
                  _lspec(layer, (k, tn), lambda i, j: (0, j))],
        out_specs=pl.BlockSpec((tm, tn), lambda i, j: (i, j)),
        out_shape=jax.ShapeDtypeStruct((m, n), out_dtype),
        scratch_shapes=[pltpu.VMEM((tm, k), BF16)],
        compiler_params=_cparams(("parallel", "arbitrary")),
        name=name,
    )(x, nw, w)


def _shift_cast_body(a_ref, b_ref, o_ref, *, tiles0, shifts):
    v = jnp.concatenate([a_ref[...], b_ref[...]], axis=1)
    tn = o_ref.shape[1]
    first = pl.program_id(1) < tiles0

    @pl.when(first)
    def _():
        o_ref[...] = v[:, shifts[0]:shifts[0] + tn].astype(BF16)

    @pl.when(jnp.logical_not(first))
    def _():
        o_ref[...] = v[:, shifts[1]:shifts[1] + tn].astype(BF16)


def _shift_cast(w, bases, shifts, widths, tn):
    depth, k, _ = w.shape
    assert all(b % tn == 0 and n % tn == 0 and 0 < s <= LANES for b, s, n in zip(bases, shifts, widths))
    tiles0 = widths[0] // tn
    sub = tn // LANES

    def a_col(l, j):
        return jnp.where(j < tiles0, bases[0] // tn + j, bases[1] // tn + j - tiles0)

    def b_col(l, j):
        return jnp.where(j < tiles0, (bases[0] + tn) // LANES + j * sub, (bases[1] + tn) // LANES + (j - tiles0) * sub)

    return pl.pallas_call(
        functools.partial(_shift_cast_body, tiles0=tiles0, shifts=tuple(shifts)),
        grid=(depth, (widths[0] + widths[1]) // tn),
        in_specs=[pl.BlockSpec((None, k, tn), lambda l, j: (l, 0, a_col(l, j))),
                  pl.BlockSpec((None, k, LANES), lambda l, j: (l, 0, b_col(l, j)))],
        out_specs=pl.BlockSpec((None, k, tn), lambda l, j: (l, 0, j)),
        out_shape=jax.ShapeDtypeStruct((depth, k, widths[0] + widths[1]), BF16),
        compiler_params=_cparams(("parallel", "parallel")),
        name="relayout_w_in",
    )(w, w)


def _rope_apply(v, c, s):
    return v * c + pltpu.roll(v, 64, 1) * s


def _q_body(c_ref, nw_ref, wt_ref, qnw_ref, ct_ref, st_ref, o_ref):
    cn = _rms(c_ref[...], nw_ref[...]).astype(BF16)
    acc = _dot_nt(wt_ref[...], cn)
    wn = qnw_ref[:LANES, :]
    wrc = qnw_ref[LANES:2 * LANES, :] * ct_ref[...]
    wxs = qnw_ref[2 * LANES:, :] * st_ref[...]
    scale = QK_DIM ** -0.5 * LOG2_E
    for h in range(MLA_HEADS):
        qn = acc[Q_ACC_W * h:Q_ACC_W * h + LANES, :]
        qr = acc[Q_ACC_W * h + LANES:Q_ACC_W * h + 2 * LANES, :]
        qx = acc[Q_ACC_W * h + 2 * LANES:Q_ACC_W * (h + 1), :]
        ssq = jnp.sum(qn * qn + qr * qr, axis=0, keepdims=True)
        rs = lax.rsqrt(ssq / QK_DIM + EPS) * scale
        o_ref[0, HEAD_W * h:HEAD_W * h + LANES, :] = (qn * rs * wn).astype(BF16)
        o_ref[0, HEAD_W * h + LANES:HEAD_W * (h + 1), :] = ((qr * wrc + qx * wxs) * rs).astype(BF16)


def _q_proj(small, nw, wqt, qnw, ct_t, st_t, layer, tm):
    m = small.shape[0]
    n = MLA_HEADS * HEAD_W
    return pl.pallas_call(
        _q_body,
        grid=(m // tm,),
        in_specs=[pl.BlockSpec((tm, Q_LORA), lambda i: (i, 0)),
                  _lspec(layer, (1, Q_LORA)),
                  _lspec(layer, (MLA_HEADS * Q_ACC_W, Q_LORA)),
                  _lspec(layer, (Q_ACC_W, tm)),
                  pl.BlockSpec((LANES, tm), lambda i: (0, i)),
                  pl.BlockSpec((LANES, tm), lambda i: (0, i))],
        out_specs=pl.BlockSpec((1, n, tm), lambda i: (i, 0, 0)),
        out_shape=jax.ShapeDtypeStruct((m // tm, n, tm), BF16),
        compiler_params=_cparams(("parallel",)),
        name="q_proj",
    )(small, nw, wqt, qnw, ct_t, st_t)


def _kv_body(c_ref, kr_ref, nw_ref, wk_ref, wvt_ref, kwn_ref, kwr_ref, ct_ref, st_ref, k_ref, vt_ref):
    cn = _rms(c_ref[...], nw_ref[...]).astype(BF16)
    acc = _dot(cn, wk_ref[...])
    vt_ref[0] = _dot_nt(wvt_ref[...], cn).astype(BF16)
    c = ct_ref[...]
    s = st_ref[...]
    kr = kr_ref[...]
    wn = kwn_ref[...]
    wr = kwr_ref[...]
    ssr = jnp.sum(kr * kr, axis=-1, keepdims=True)
    kr_rot = _rope_apply(kr * wr, c, s)
    for h in range(MLA_HEADS):
        kn = acc[:, LANES * h:LANES * (h + 1)]
        ssq = jnp.sum(kn * kn, axis=-1, keepdims=True) + ssr
        rs = lax.rsqrt(ssq / QK_DIM + EPS)
        k_ref[:, HEAD_W * h:HEAD_W * h + LANES] = (kn * rs * wn).astype(BF16)
        k_ref[:, HEAD_W * h + LANES:HEAD_W * (h + 1)] = (kr_rot * rs).astype(BF16)


def _kv_proj(small, nw, wk, wvt, kwn, kwr, ct, st, layer, tm):
    m = small.shape[0]
    nk = MLA_HEADS * HEAD_W
    nv = MLA_HEADS * V_DIM
    kr_blk = (Q_LORA + KV_LORA) // LANES
    return pl.pallas_call(
        _kv_body,
        grid=(m // tm,),
        in_specs=[pl.BlockSpec((tm, KV_LORA), lambda i: (i, 1)),
                  pl.BlockSpec((tm, LANES), lambda i: (i, kr_blk)),
                  _lspec(layer, (1, KV_LORA)),
                  _lspec(layer, (KV_LORA, MLA_HEADS * QK_NOPE)),
                  _lspec(layer, (nv, KV_LORA)),
                  _lspec(layer, (1, LANES)),
                  _lspec(layer, (1, LANES)),
                  pl.BlockSpec((tm, LANES), lambda i: (i, 0)),
                  pl.BlockSpec((tm, LANES), lambda i: (i, 0))],
        out_specs=[pl.BlockSpec((tm, nk), lambda i: (i, 0)),
                   pl.BlockSpec((1, nv, tm), lambda i: (i, 0, 0))],
        out_shape=[jax.ShapeDtypeStruct((m, nk), BF16), jax.ShapeDtypeStruct((m // tm, nv, tm), BF16)],
        compiler_params=_cparams(("parallel",)),
        name="kv_proj",
    )(small, small, nw, wk, wvt, kwn, kwr, ct, st)


def _attn_body(q_ref, k_ref, vt_ref, o_ref, s0_scr, s1_scr, p0_scr, p1_scr, m_scr, a_scr, acc_scr, *, tq, nq, hb, qsub):
    s_scr = (s0_scr, s1_scr)
    p_scr = (p0_scr, p1_scr)
    tk = tq // 2
    key_chunk = lax.broadcasted_iota(jnp.int32, (tk, tq), 0) // CHUNK
    qry_chunk = lax.broadcasted_iota(jnp.int32, (tk, tq), 1) // CHUNK
    diag_masks = [key_chunk + h * (tk // CHUNK) <= qry_chunk for h in range(2)]
    ones_rows = jnp.ones((ACC_ROWS - V_DIM, tk), BF16)

    def scores(qs, kpair, h, mask=None):
        koff = pl.multiple_of(kpair * tq + h * tk, tk)
        for hh in range(hb):
            s = _dot(k_ref[pl.ds(koff, tk), HEAD_W * hh:HEAD_W * (hh + 1)], qs[hh])
            s_scr[h][hh] = s if mask is None else jnp.where(mask, s, -jnp.inf)

    def pv(kpair, h):
        return [_dot(jnp.concatenate([vt_ref[kpair, V_DIM * hh:V_DIM * (hh + 1), h * tk:(h + 1) * tk],
                                      ones_rows], axis=0), p_scr[h][hh])
                for hh in range(hb)]

    def softmax(h, pvs):
        for hh in range(hb):
            m = m_scr[hh]
            s = s_scr[h][hh]
            m_new = jnp.maximum(m, jnp.max(s, axis=0, keepdims=True))
            p_scr[h][hh] = jnp.exp2((s - m_new).astype(BF16))
            alpha = jnp.exp2(m - m_new)
            acc_scr[hh] = alpha * (acc_scr[hh] + a_scr[hh] * pvs[hh])
            a_scr[hh] = alpha
            m_scr[hh] = m_new

    def q_tiles(qi):
        return [jnp.concatenate([q_ref[qsub * qi + u, HEAD_W * hh:HEAD_W * (hh + 1), :] for u in range(qsub)], axis=1)
                for hh in range(hb)]

    def diag_scores(qi):
        qs = q_tiles(qi)
        for h in range(2):
            scores(qs, qi, h, diag_masks[h])

    diag_scores(0)
    a_scr[...] = jnp.zeros(a_scr.shape, F32)
    acc_scr[...] = jnp.zeros(acc_scr.shape, F32)
    for h in range(2):
        p_scr[h][...] = jnp.zeros(p_scr[h].shape, BF16)

    def q_block(qi, c):
        qoff = pl.multiple_of(qi * tq, tq)
        qs = q_tiles(qi)
        m_scr[...] = jnp.full(m_scr.shape, -jnp.inf, F32)

        def kpair_of(r):
            return jnp.where(r == 0, qi, r - 1)

        def body(r, c2):
            prev = kpair_of(jnp.maximum(r - 1, 0))
            for h in range(2):
                pvs = pv(prev, h)
                softmax(h, pvs)
                scores(qs, r, h)
            return c2

        lax.fori_loop(0, qi, body, 0)
        prev = kpair_of(jnp.maximum(qi - 1, 0))
        nxt = jnp.minimum(qi + 1, nq - 1)
        qn = q_tiles(nxt)
        for h in range(2):
            softmax(h, pv(prev, h))
            scores(qn, nxt, h, diag_masks[h])
        last = kpair_of(qi)
        pv0, pv1 = pv(last, 0), pv(last, 1)
        for hh in range(hb):
            acc = acc_scr[hh] + a_scr[hh] * pv0[hh] + pv1[hh]
            o_ref[pl.ds(qoff, tq), V_DIM * hh:V_DIM * (hh + 1)] = (
                acc[:V_DIM] / acc[V_DIM:V_DIM + 1]).T.astype(BF16)
        return c

    lax.fori_loop(0, nq, q_block, 0)


def _attention(qt, k, vt, batch, seq, tq, hb):
    m = k.shape[0]
    nq = seq // tq
    tk = tq // 2
    assert vt.shape[2] == tq
    qsub = tq // qt.shape[2]
    return pl.pallas_call(
        functools.partial(_attn_body, tq=tq, nq=nq, hb=hb, qsub=qsub),
        grid=(batch, MLA_HEADS // hb),
        in_specs=[pl.BlockSpec((nq * qsub, hb * HEAD_W, tq // qsub), lambda b, h: (b, h, 0)),
                  pl.BlockSpec((seq, hb * HEAD_W), lambda b, h: (b, h)),
                  pl.BlockSpec((nq, hb * V_DIM, tq), lambda b, h: (b, h, 0))],
        out_specs=pl.BlockSpec((seq, hb * V_DIM), lambda b, h: (b, h)),
        out_shape=jax.ShapeDtypeStruct((m, MLA_HEADS * V_DIM), BF16),
        scratch_shapes=[pltpu.VMEM((hb, tk, tq), F32), pltpu.VMEM((hb, tk, tq), F32),
                        pltpu.VMEM((hb, tk, tq), BF16), pltpu.VMEM((hb, tk, tq), BF16),
                        pltpu.VMEM((hb, 1, tq), F32), pltpu.VMEM((hb, 1, tq), F32),
                        pltpu.VMEM((hb, ACC_ROWS, tq), F32)],
        compiler_params=_cparams(("parallel", "arbitrary")),
        name="attention",
    )(qt, k, vt)


def _silu_of_half(h):
    return h + h * jnp.tanh(h)


def _silu(y):
    return _silu_of_half(0.5 * y)


def _split3(v):
    hi = v.astype(BF16)
    r = v - hi.astype(F32)
    mid = r.astype(BF16)
    lo = (r - mid.astype(F32)).astype(BF16)
    return hi, mid, lo


def _ssd_body(z_ref, x_ref, bc_ref, dt_ref, cwx_ref, cwb_ref, cbx_ref, cbb_ref, dtb_ref, alog_ref,
              dsk_ref, nw_ref, sel_ref, eye_ref, tri_ref, tri64_ref, shift_ref, y_ref, state_ref, xhalo, bchalo,
              *, cpb):
    t = CHUNK
    gn = SSM_GROUPS * SSM_STATE

    @pl.when(pl.program_id(1) == 0)
    def _():
        state_ref[...] = jnp.zeros_like(state_ref)
        xhalo[...] = jnp.zeros_like(xhalo)
        bchalo[...] = jnp.zeros_like(bchalo)

    def conv_silu(raw_ref, halo, w_ref, b_ref):
        raws = [raw_ref[c * t:(c + 1) * t, :] for c in range(cpb)]
        prevs = [halo[...]] + raws[:-1]
        halo[...] = raws[-1]
        shifted = [_dot(shift_ref[...], jnp.concatenate([prevs[c], raws[c]], axis=0)) for c in range(cpb)]
        outs = []
        for c in range(cpb):
            y = None
            for tap in range(CONV_WIDTH):
                shift = CONV_WIDTH - 1 - tap
                xt = raws[c].astype(F32) if shift == 0 else shifted[c][(shift - 1) * t:shift * t, :]
                term = xt * w_ref[tap:tap + 1, :]
                y = term if y is None else y + term
            outs.append(_silu_of_half(y + b_ref[...]))
        return outs

    xs_all = conv_silu(x_ref, xhalo, cwx_ref, cbx_ref)
    bc_all = conv_silu(bc_ref, bchalo, cwb_ref, cbb_ref)

    head_lane = lax.broadcasted_iota(jnp.int32, (t, LANES), 1) < SSM_HEADS
    neg_a = -jnp.exp(alog_ref[...])
    tri64 = tri64_ref[...]
    stacks = []
    for c in range(cpb):
        dt = jnp.where(head_lane, jax.nn.softplus(dt_ref[c * t:(c + 1) * t, :] + dtb_ref[...]), 0.0)
        h3 = _split3(dt * neg_a)
        a_cum = (_dot(tri64, h3[0]) + _dot(tri64, h3[1]) + _dot(tri64, h3[2])) * LOG2_E
        stacks += [a_cum, dt]
    s3 = [part.astype(F32) for part in _split3(jnp.concatenate(stacks, axis=0))]
    packed = jnp.concatenate([(s3[0] + pltpu.roll(s3[1], SSM_HEADS, 1)).astype(BF16), s3[2].astype(BF16)], axis=1)
    ex_all = _dot(packed, sel_ref[...])
    left = lax.broadcasted_iota(jnp.int32, (t, LANES), 1) < SSM_HEADDIM
    zero_b = jnp.zeros((t, LANES), BF16)

    pre = []
    for c in range(cpb):
        ex = ex_all[2 * t * c:2 * t * (c + 1)]
        a_col = ex[:t]
        ea = jnp.exp2(a_col)
        a_row = jnp.sum(jnp.where(eye_ref[...] > 0, a_col, 0.0), axis=0, keepdims=True)
        decay = jnp.exp2(a_col - a_row + tri_ref[...])
        xdt = xs_all[c] * ex[t:2 * t]
        pre.append((decay, ea, xdt.astype(BF16), (xdt * jnp.exp2(a_col[t - 1:t, :] - a_col)).astype(BF16)))

    for c in range(cpb):
        decay, ea, xdt_b, xw_b = pre[c]
        xs, bc = xs_all[c], bc_all[c]
        chunk_decay = ea[t - 1:t, :]
        r0 = c * t
        for g in range(SSM_GROUPS):
            lo, hi = GROUP_W * g, GROUP_W * (g + 1)
            bg = bc[:, SSM_STATE * g:SSM_STATE * (g + 1)]
            cg_b = bc[:, gn + SSM_STATE * g:gn + SSM_STATE * (g + 1)].astype(BF16)
            bg_b = bg.astype(BF16)
            cb2 = _dot_nt(cg_b, jnp.concatenate([bg_b, bg_b], axis=0))
            yd = []
            for j in range(GROUP_W // LANES):
                l0 = lo + LANES * j
                mm = (cb2 * decay[:, l0:l0 + LANES]).astype(BF16)
                xp = xdt_b[:, l0:l0 + LANES]
                rhs = jnp.concatenate([jnp.where(left, xp, zero_b), jnp.where(left, zero_b, xp)], axis=0)
                yd.append(_dot(mm, rhs))
            y_diag = jnp.concatenate(yd, axis=1)
            st = state_ref[g]
            y_off = _dot(cg_b, st.astype(BF16)) * ea[:, lo:hi]
            inc = _dot(bg.T.astype(BF16), xw_b[:, lo:hi])
            state_ref[g] = st * chunk_decay[:, lo:hi] + inc
            yg = y_diag + y_off + dsk_ref[:, lo:hi] * xs[:, lo:hi]
            yg = yg * _silu(z_ref[r0:r0 + t, lo:hi].astype(F32))
            ms = jnp.mean(yg * yg, axis=-1, keepdims=True)
            y_ref[r0:r0 + t, lo:hi] = (yg * lax.rsqrt(ms + EPS) * nw_ref[:, lo:hi]).astype(BF16)


def _ssd(big, small, cwx, cwb, cbx, cbb, dtb, alog, dsk, nw, sel, eye_t, tri_t, tri64, shift_m, layer, batch, seq,
         cpb):
    m = big.shape[0]
    t = CHUNK
    tb = cpb * t
    nb = seq // tb
    bcw = 2 * SSM_GROUPS * SSM_STATE
    bc_blk = (2 * D_INNER) // bcw
    dt_blk = (SMALL_N - LANES) // LANES
    row = lambda b, c: b * nb + c
    const = lambda shape: pl.BlockSpec(shape, lambda b, c: (0,) * len(shape))
    return pl.pallas_call(
        functools.partial(_ssd_body, cpb=cpb),
        grid=(batch, nb),
        in_specs=[pl.BlockSpec((tb, D_INNER), lambda b, c: (row(b, c), 0)),
                  pl.BlockSpec((tb, D_INNER), lambda b, c: (row(b, c), 1)),
                  pl.BlockSpec((tb, bcw), lambda b, c: (row(b, c), bc_blk)),
                  pl.BlockSpec((tb, LANES), lambda b, c: (row(b, c), dt_blk)),
                  _lspec(layer, (CONV_WIDTH, D_INNER)), _lspec(layer, (CONV_WIDTH, bcw)),
                  _lspec(layer, (1, D_INNER)), _lspec(layer, (1, bcw)),
                  _lspec(layer, (1, LANES)), _lspec(layer, (1, LANES)),
                  _lspec(layer, (1, D_INNER)), _lspec(layer, (1, D_INNER)),
                  const((2 * LANES, D_INNER)), const((t, D_INNER)), const((t, D_INNER)), const((t, t)),
                  const(((CONV_WIDTH - 1) * t, 2 * t))],
        out_specs=pl.BlockSpec((tb, D_INNER), lambda b, c: (row(b, c), 0)),
        out_shape=jax.ShapeDtypeStruct((m, D_INNER), BF16),
        scratch_shapes=[pltpu.VMEM((SSM_GROUPS, SSM_STATE, GROUP_W), F32),
                        pltpu.VMEM((t, D_INNER), BF16),
                        pltpu.VMEM((t, bcw), BF16)],
        compiler_params=_cparams(("parallel", "arbitrary")),
        name="ssd",
    )(big, big, big, small, cwx, cwb, cbx, cbb, dtb, alog, dsk, nw, sel, eye_t, tri_t, tri64, shift_m)


def _merge_body(o_ref, y_ref, wa_ref, wb_ref, g0_ref, g1_ref, out_ref):
    ya = _dot(o_ref[...], wa_ref[...])
    yb = _dot(y_ref[...], wb_ref[...])
    g0 = jax.nn.sigmoid(g0_ref[...].astype(F32))
    g1 = jax.nn.sigmoid(g1_ref[...].astype(F32))
    out_ref[...] = (g0 * ya + g1 * yb).astype(out_ref.dtype)


def _merge(o, y, wa, wb, big, layer, tm, tn):
    m = o.shape[0]
    g0_blk = (2 * D_INNER + 2 * SSM_GROUPS * SSM_STATE) // tn
    g1_blk = g0_blk + D_MODEL // tn
    return pl.pallas_call(
        _merge_body,
        grid=(m // tm, D_MODEL // tn),
        in_specs=[pl.BlockSpec((tm, o.shape[1]), lambda i, j: (i, 0)),
                  pl.BlockSpec((tm, y.shape[1]), lambda i, j: (i, 0)),
                  _lspec(layer, (wa.shape[1], tn), lambda i, j: (0, j)),
                  _lspec(layer, (wb.shape[1], tn), lambda i, j: (0, j)),
                  pl.BlockSpec((tm, tn), lambda i, j: (i, g0_blk + j)),
                  pl.BlockSpec((tm, tn), lambda i, j: (i, g1_blk + j))],
        out_specs=pl.BlockSpec((tm, tn), lambda i, j: (i, j)),
        out_shape=jax.ShapeDtypeStruct((m, D_MODEL), BF16),
        compiler_params=_cparams(("parallel", "arbitrary")),
        name="merge",
    )(o, y, wa, wb, big, big)


def _res_mm_f32w_body(x_ref, a_ref, w_ref, o_ref, wb_ref):
    @pl.when(pl.program_id(0) == 0)
    def _():
        wb_ref[...] = w_ref[...].astype(BF16)

    o_ref[...] = x_ref[...] + _dot(a_ref[...], wb_ref[...])


def _res_matmul(x, a, w, layer, tm):
    m, n = x.shape
    k = a.shape[1]
    return pl.pallas_call(
        _res_mm_f32w_body,
        grid=(m // tm,),
        in_specs=[pl.BlockSpec((tm, n), lambda i: (i, 0)),
                  pl.BlockSpec((tm, k), lambda i: (i, 0)),
                  _lspec(layer, (k, n), single=True)],
        out_specs=pl.BlockSpec((tm, n), lambda i: (i, 0)),
        out_shape=jax.ShapeDtypeStruct((m, n), F32),
        scratch_shapes=[pltpu.VMEM((k, n), BF16)],
        compiler_params=_cparams(("arbitrary",)),
        name="out_proj",
    )(x, a, w)


def _res_mm_ksplit_body(x_ref, a_ref, w_ref, o_ref):
    @pl.when(pl.program_id(2) == 0)
    def _():
        o_ref[...] = x_ref[...]

    o_ref[...] += _dot(a_ref[...], w_ref[...])


def _res_matmul_tiled(x, a, w, layer, tm, tn, tk, name):
    m, n = x.shape
    k = a.shape[1]
    return pl.pallas_call(
        _res_mm_ksplit_body,
        grid=(m // tm, n // tn, k // tk),
        in_specs=[pl.BlockSpec((tm, tn), lambda i, j, c: (i, j)),
                  pl.BlockSpec((tm, tk), lambda i, j, c: (i, c)),
                  _lspec(layer, (tk, tn), lambda i, j, c: (c, j))],
        out_specs=pl.BlockSpec((tm, tn), lambda i, j, c: (i, j)),
        out_shape=jax.ShapeDtypeStruct((m, n), F32),
        compiler_params=_cparams(("parallel", "arbitrary", "arbitrary")),
        name=name,
    )(x, a, w)


def _ple_body(x_ref, nw_ref, wg_ref, p_ref, wp_ref, o_ref, wgb_ref):
    @pl.when(pl.program_id(0) == 0)
    def _():
        wgb_ref[...] = wg_ref[...].astype(BF16)

    x = x_ref[...]
    h = _rms(x, nw_ref[...]).astype(BF16)
    gate = jax.nn.sigmoid(_dot(h, wgb_ref[...]))
    pe = _dot(p_ref[...].astype(BF16), wp_ref[...])
    o_ref[...] = x + pe * gate


def _ple(x, nw, wg, p, wp, layer, tm):
    m, d = x.shape
    return pl.pallas_call(
        _ple_body,
        grid=(m // tm,),
        in_specs=[pl.BlockSpec((tm, d), lambda i: (i, 0)),
                  _lspec(layer, (1, d)),
                  _lspec(layer, (d, d), single=True),
                  _lspec(layer, (tm, PLE_DIM), lambda i: (i, 0)),
                  _lspec(layer, (PLE_DIM, d))],
        out_specs=pl.BlockSpec((tm, d), lambda i: (i, 0)),
        out_shape=jax.ShapeDtypeStruct((m, d), F32),
        scratch_shapes=[pltpu.VMEM((d, d), BF16)],
        compiler_params=_cparams(("arbitrary",)),
        name="ple",
    )(x, nw, wg, p, wp)


def _rope_cols(w):
    half = QK_ROPE // 2
    z = jnp.zeros(w.shape[:-1] + (LANES // 2 - half,), w.dtype)
    return jnp.concatenate([w[..., :half], z, w[..., half:], z], axis=-1)


def kernel(x, p, positions, norm_mix_w, w_in, q_a_norm_w, w_uq, kv_a_norm_w, w_ukv, q_norm_w, k_norm_w,
           w_o_mla, conv_w, conv_b, dt_bias, a_log, d_skip, ssm_norm_w, w_o_ssm, w_out, norm_mlp_w, w_up,
           w_down, ple_norm_w, w_ple_gate, w_ple):
    batch, seq, d = x.shape
    m = batch * seq
    depth = w_in.shape[0]
    tm_big = min(1024, m)
    tm_mid = min(512, m)
    tm_small = min(256, m)
    tq = min(512, seq)

    inv_freq = 1.0 / (ROPE_THETA ** (jnp.arange(0, QK_ROPE, 2, dtype=F32) / QK_ROPE))
    freq = _rope_cols(jnp.concatenate([inv_freq, inv_freq]))[None, :]
    sign = _rope_cols(jnp.concatenate([-jnp.ones_like(inv_freq), jnp.ones_like(inv_freq)]))[None, :]
    ct, st = _rope_tables(positions.reshape(m, 1), freq, sign, tm_big)
    ct_t, st_t = ct.T, st.T

    lane = jnp.arange(D_INNER)
    sel_row = jnp.arange(2 * LANES)[:, None]
    sel = ((sel_row % SSM_HEADS == (lane // SSM_HEADDIM)[None, :]) & (sel_row < 3 * SSM_HEADS)).astype(BF16)
    rows = jnp.arange(CHUNK)[:, None]
    eye_t = (rows == (lane % SSM_HEADDIM)[None, :]).astype(F32)
    tri_t = jnp.where(rows >= (lane % SSM_HEADDIM)[None, :], 0.0, -jnp.inf).astype(F32)
    tri64 = (rows >= jnp.arange(CHUNK)[None, :]).astype(BF16)
    sh_rows = jnp.arange((CONV_WIDTH - 1) * CHUNK)[:, None]
    shift_m = (jnp.arange(2 * CHUNK)[None, :] == CHUNK + sh_rows % CHUNK - (sh_rows // CHUNK + 1)).astype(BF16)

    s0 = Q_LORA + KV_LORA + QK_ROPE
    s1 = s0 + D_INNER + (D_INNER + 2 * SSM_GROUPS * SSM_STATE)
    s2 = s1 + SSM_HEADS

    w_lat, w_kr, w_dt = lax.optimization_barrier(
        (w_in[..., :Q_LORA + KV_LORA], w_in[..., Q_LORA + KV_LORA:s0], w_in[..., s1:s2]))
    w_small = jnp.concatenate(
        [w_lat, _rope_cols(w_kr), w_dt, jnp.zeros((depth, d, LANES - SSM_HEADS), F32)], axis=-1).astype(BF16)
    w_big = _shift_cast(w_in, (s0 - QK_ROPE, s2 - LANES), (QK_ROPE, LANES), (s1 - s0, 2 * D_MODEL), 1024)
    wq4 = w_uq.reshape(depth, Q_LORA, MLA_HEADS, QK_DIM)
    wq_rope = _rope_cols(wq4[..., QK_NOPE:])
    wq = jnp.concatenate([wq4[..., :QK_NOPE], wq_rope, jnp.roll(wq_rope, LANES // 2, axis=-1)], axis=-1)
    wqt = jnp.swapaxes(wq.reshape(depth, Q_LORA, MLA_HEADS * Q_ACC_W), 1, 2).astype(BF16)
    wkv4 = w_ukv.reshape(depth, KV_LORA, MLA_HEADS, QK_NOPE + V_DIM)
    wk = wkv4[..., :QK_NOPE].reshape(depth, KV_LORA, -1).astype(BF16)
    wvt = jnp.swapaxes(wkv4[..., QK_NOPE:].reshape(depth, KV_LORA, -1), 1, 2).astype(BF16)
    w_o_mla_b = w_o_mla.astype(BF16)
    w_o_ssm_b = w_o_ssm.astype(BF16)
    w_down_b = w_down.astype(BF16)
    w_ple_b = w_ple.astype(BF16)

    rows3 = lambda v: v.reshape(depth, 1, -1).astype(F32)
    pad_h = jnp.zeros((depth, LANES - SSM_HEADS), F32)
    qw_rope = _rope_cols(q_norm_w[:, QK_NOPE:])
    qnw = jnp.concatenate([q_norm_w[:, :QK_NOPE], qw_rope, jnp.roll(qw_rope, LANES // 2, axis=-1)], axis=-1)
    qnw = jnp.broadcast_to(qnw[:, :, None], (depth, Q_ACC_W, tm_small)).astype(F32)
    kwn = rows3(k_norm_w[:, :QK_NOPE])
    kwr = rows3(_rope_cols(k_norm_w[:, QK_NOPE:]))
    dtb = rows3(jnp.concatenate([dt_bias, pad_h], axis=-1))
    alog = rows3(jnp.concatenate([a_log, pad_h], axis=-1))
    dsk = rows3(jnp.repeat(d_skip, SSM_HEADDIM, axis=-1))
    mix_nw, qa_nw, kva_nw = rows3(norm_mix_w), rows3(q_a_norm_w), rows3(kv_a_norm_w)
    ssm_nw, mlp_nw, ple_nw = rows3(ssm_norm_w), rows3(norm_mlp_w), rows3(ple_norm_w)
    conv_wh, conv_bh = 0.5 * conv_w, 0.5 * conv_b
    cwx, cwb = conv_wh[..., :D_INNER], conv_wh[..., D_INNER:]
    cbx, cbb = rows3(conv_bh[:, :D_INNER]), rows3(conv_bh[:, D_INNER:])
    p3 = p.reshape(depth, m, PLE_DIM)

    xf = x.reshape(m, d)
    for i in range(depth):
        small = _norm_matmul(xf, mix_nw, w_small, i, F32, tm_big, SMALL_N, "in_proj_small")
        big = _norm_matmul(xf, mix_nw, w_big, i, BF16, tm_big, 2048, "in_proj_big")
        qt = _q_proj(small, qa_nw, wqt, qnw, ct_t, st_t, i, tm_small)
        k, vt = _kv_proj(small, kva_nw, wk, wvt, kwn, kwr, ct, st, i, tq)
        o = _attention(qt, k, vt, batch, seq, tq, 2)
        y = _ssd(big, small, cwx, cwb, cbx, cbb, dtb, alog, dsk, ssm_nw, sel, eye_t, tri_t, tri64, shift_m, i,
                 batch, seq, 4)
        merged = _merge(o, y, w_o_mla_b, w_o_ssm_b, big, i, tm_big, 512)
        xf = _res_matmul(xf, merged, w_out, i, tm_mid)
        hidden = _norm_matmul(xf, mlp_nw, w_up, i, BF16, min(2048, m), 512, "mlp_up", relu2=True, single_x=True)
        xf = _res_matmul_tiled(xf, hidden, w_down_b, i, tm_big, 1024, D_FF // 2, "mlp_down")
        xf = _ple(xf, ple_nw, w_ple_gate, p3, w_ple_b, i, tm_mid)
    return xf.reshape(batch, seq, d)
```

```python
import functools

import jax
import jax.numpy as jnp
from jax import lax
from jax.experimental import pallas as pl
from jax.experimental.pallas import tpu as pltpu

F32 = jnp.float32
BF16 = jnp.bfloat16

D_MODEL = 2048
DEPTH = 4
CHUNK = 64
PLE_DIM = 256
EPS = 1e-6
MLA_HEADS = 16
Q_LORA = 512
KV_LORA = 512
QK_NOPE = 128
QK_ROPE = 64
V_DIM = 128
QK_DIM = QK_NOPE + QK_ROPE
ROPE_THETA = 10000.0
D_INNER = 2 * D_MODEL
SSM_HEADDIM = 64
SSM_HEADS = D_INNER // SSM_HEADDIM
SSM_GROUPS = 8
SSM_STATE = 128
GROUP_W = D_INNER // SSM_GROUPS
CONV_WIDTH = 4
D_FF = 4 * D_MODEL

LOG2_E = 1.4426950408889634
LANES = 128
ACC_ROWS = V_DIM + 16
Q_ACC_W = 3 * LANES
HEAD_W = 2 * LANES
SMALL_N = Q_LORA + KV_LORA + 2 * LANES
VMEM_LIMIT = 56 * 1024 * 1024


def _cparams(sem):
    return pltpu.CompilerParams(dimension_semantics=sem, vmem_limit_bytes=VMEM_LIMIT)


def _rms(xf, w):
    ms = jnp.mean(xf * xf, axis=-1, keepdims=True)
    return xf * lax.rsqrt(ms + EPS) * w


def _lspec(layer, block, imap=None, single=False):
    if imap is None:
        imap = lambda *g: (0,) * len(block)
    return pl.BlockSpec((None,) + tuple(block), lambda *g: (layer,) + tuple(imap(*g)),
                        pipeline_mode=pl.Buffered(1) if single else None)


def _dot(a, b):
    return jnp.dot(a, b, preferred_element_type=F32)


def _dot_nt(a, b):
    return lax.dot_general(a, b, (((1,), (1,)), ((), ())), preferred_element_type=F32)


def _rope_body(pos_ref, freq_ref, sign_ref, ct_ref, st_ref):
    ang = pos_ref[...].astype(F32) * freq_ref[...]
    ct_ref[...] = jnp.cos(ang)
    st_ref[...] = jnp.sin(ang) * sign_ref[...]


def _rope_tables(pos_col, freq, sign, tm):
    m = pos_col.shape[0]
    return pl.pallas_call(
        _rope_body,
        grid=(m // tm,),
        in_specs=[pl.BlockSpec((tm, 1), lambda i: (i, 0)),
                  pl.BlockSpec((1, LANES), lambda i: (0, 0)),
                  pl.BlockSpec((1, LANES), lambda i: (0, 0))],
        out_specs=[pl.BlockSpec((tm, LANES), lambda i: (i, 0)),
                   pl.BlockSpec((tm, LANES), lambda i: (i, 0))],
        out_shape=[jax.ShapeDtypeStruct((m, LANES), F32)] * 2,
        compiler_params=_cparams(("parallel",)),
        name="rope_tables",
    )(pos_col, freq, sign)


def _norm_mm_body(x_ref, nw_ref, w_ref, o_ref, h_ref, *, relu2):
    @pl.when(pl.program_id(1) == 0)
    def _():
        h_ref[...] = _rms(x_ref[...], nw_ref[...]).astype(BF16)

    y = _dot(h_ref[...], w_ref[...].astype(BF16))
    if relu2:
        y = jnp.square(jnp.maximum(y, 0.0))
    o_ref[...] = y.astype(o_ref.dtype)


def _norm_matmul(x, nw, w, layer, out_dtype, tm, tn, name, relu2=False, single_x=False):
    m, k = x.shape
    n = w.shape[2]
    return pl.pallas_call(
        functools.partial(_norm_mm_body, relu2=relu2),
        grid=(m // tm, n // tn),
        in_specs=[pl.BlockSpec((tm, k), lambda i, j: (i, 0), pipeline_mode=pl.Buffered(1) if single_x else None),
                  _lspec(layer, (1, k)),
                  _lspec(layer, (k, tn), lambda i, j: (0, j))],
        out_specs=pl.BlockSpec((tm, tn), lambda i, j: (i, j)),
        out_shape=jax.ShapeDtypeStruct((m, n), out_dtype),
        scratch_shapes=[pltpu.VMEM((tm, k), BF16)],
        compiler_params=_cparams(("parallel", "arbitrary")),
        name=name,
    )(x, nw, w)


def _shift_cast_body(a_ref, b_ref, o_ref, *, tiles0, shifts):
    v = jnp.concatenate([a_ref[...], b_ref[...]], axis=1)
    tn = o_ref.shape[1]
    first = pl.program_id(1) < tiles0

    @pl.when(first)
    def _():
        o_ref[...] = v[:, shifts[0]:shifts[0] + tn].astype(BF16)

    @pl.when(jnp.logical_not(first))
    def _():
        o_ref[...] = v[:, shifts[1]:shifts[1] + tn].astype(BF16)


def _shift_cast(w, bases, shifts, widths, tn):
    depth, k, _ = w.shape
    assert all(b % tn == 0 and n % tn == 0 and 0 < s <= LANES for b, s, n in zip(bases, shifts, widths))
    tiles0 = widths[0] // tn
    sub = tn // LANES

    def a_col(l, j):
        return jnp.where(j < tiles0, bases[0] // tn + j, bases[1] // tn + j - tiles0)

    def b_col(l, j):
        return jnp.where(j < tiles0, (bases[0] + tn) // LANES + j * sub, (bases[1] + tn) // LANES + (j - tiles0) * sub)

    return pl.pallas_call(
        functools.partial(_shift_cast_body, tiles0=tiles0, shifts=tuple(shifts)),
        grid=(depth, (widths[0] + widths[1]) // tn),
        in_specs=[pl.BlockSpec((None, k, tn), lambda l, j: (l, 0, a_col(l, j))),
                  pl.BlockSpec((None, k, LANES), lambda l, j: (l, 0, b_col(l, j)))],
        out_specs=pl.BlockSpec((None, k, tn), lambda l, j: (l, 0, j)),
        out_shape=jax.ShapeDtypeStruct((depth, k, widths[0] + widths[1]), BF16),
        compiler_params=_cparams(("parallel", "parallel")),
        name="relayout_w_in",
    )(w, w)


def _rope_apply(v, c, s):
    return v * c + pltpu.roll(v, 64, 1) * s


def _q_body(c_ref, nw_ref, wt_ref, qnw_ref, ct_ref, st_ref, o_ref):
    cn = _rms(c_ref[...], nw_ref[...]).astype(BF16)
    acc = _dot_nt(wt_ref[...], cn)
    wn = qnw_ref[:LANES, :]
    wrc = qnw_ref[LANES:2 * LANES, :] * ct_ref[...]
    wxs = qnw_ref[2 * LANES:, :] * st_ref[...]
    scale = QK_DIM ** -0.5 * LOG2_E
    for h in range(MLA_HEADS):
        qn = acc[Q_ACC_W * h:Q_ACC_W * h + LANES, :]
        qr = acc[Q_ACC_W * h + LANES:Q_ACC_W * h + 2 * LANES, :]
        qx = acc[Q_ACC_W * h + 2 * LANES:Q_ACC_W * (h + 1), :]
        ssq = jnp.sum(qn * qn + qr * qr, axis=0, keepdims=True)
        rs = lax.rsqrt(ssq / QK_DIM + EPS) * scale
        o_ref[0, HEAD_W * h:HEAD_W * h + LANES, :] = (qn * rs * wn).astype(BF16)
        o_ref[0, HEAD_W * h + LANES:HEAD_W * (h + 1), :] = ((qr * wrc + qx * wxs) * rs).astype(BF16)


def _q_proj(small, nw, wqt, qnw, ct_t, st_t, layer, tm):
    m = small.shape[0]
    n = MLA_HEADS * HEAD_W
    return pl.pallas_call(
        _q_body,
        grid=(m // tm,),
        in_specs=[pl.BlockSpec((tm, Q_LORA), lambda i: (i, 0)),
                  _lspec(layer, (1, Q_LORA)),
                  _lspec(layer, (MLA_HEADS * Q_ACC_W, Q_LORA)),
                  _lspec(layer, (Q_ACC_W, tm)),
                  pl.BlockSpec((LANES, tm), lambda i: (0, i)),
                  pl.BlockSpec((LANES, tm), lambda i: (0, i))],
        out_specs=pl.BlockSpec((1, n, tm), lambda i: (i, 0, 0)),
        out_shape=jax.ShapeDtypeStruct((m // tm, n, tm), BF16),
        compiler_params=_cparams(("parallel",)),
        name="q_proj",
    )(small, nw, wqt, qnw, ct_t, st_t)


def _kv_body(c_ref, kr_ref, nw_ref, wk_ref, wvt_ref, kwn_ref, kwr_ref, ct_ref, st_ref, k_ref, vt_ref):
    cn = _rms(c_ref[...], nw_ref[...]).astype(BF16)
    acc = _dot(cn, wk_ref[...])
    vt_ref[0] = _dot_nt(wvt_ref[...], cn).astype(BF16)
    c = ct_ref[...]
    s = st_ref[...]
    kr = kr_ref[...]
    wn = kwn_ref[...]
    wr = kwr_ref[...]
    ssr = jnp.sum(kr * kr, axis=-1, keepdims=True)
    kr_rot = _rope_apply(kr * wr, c, s)
    for h in range(MLA_HEADS):
        kn = acc[:, LANES * h:LANES * (h + 1)]
        ssq = jnp.sum(kn * kn, axis=-1, keepdims=True) + ssr
        rs = lax.rsqrt(ssq / QK_DIM + EPS)
        k_ref[:, HEAD_W * h:HEAD_W * h + LANES] = (kn * rs * wn).astype(BF16)
        k_ref[:, HEAD_W * h + LANES:HEAD_W * (h + 1)] = (kr_rot * rs).astype(BF16)


def _kv_proj(small, nw, wk, wvt, kwn, kwr, ct, st, layer, tm):
    m = small.shape[0]
    nk = MLA_HEADS * HEAD_W
    nv = MLA_HEADS * V_DIM
    kr_blk = (Q_LORA + KV_LORA) // LANES
    return pl.pallas_call(
        _kv_body,
        grid=(m // tm,),
        in_specs=[pl.BlockSpec((tm, KV_LORA), lambda i: (i, 1)),
                  pl.BlockSpec((tm, LANES), lambda i: (i, kr_blk)),
                  _lspec(layer, (1, KV_LORA)),
                  _lspec(layer, (KV_LORA, MLA_HEADS * QK_NOPE)),
                  _lspec(layer, (nv, KV_LORA)),
                  _lspec(layer, (1, LANES)),
                  _lspec(layer, (1, LANES)),
                  pl.BlockSpec((tm, LANES), lambda i: (i, 0)),
                  pl.BlockSpec((tm, LANES), lambda i: (i, 0))],
        out_specs=[pl.BlockSpec((tm, nk), lambda i: (i, 0)),
                   pl.BlockSpec((1, nv, tm), lambda i: (i, 0, 0))],
        out_shape=[jax.ShapeDtypeStruct((m, nk), BF16), jax.ShapeDtypeStruct((m // tm, nv, tm), BF16)],
        compiler_params=_cparams(("parallel",)),
        name="kv_proj",
    )(small, small, nw, wk, wvt, kwn, kwr, ct, st)


def _attn_body(q_ref, k_ref, vt_ref, o_ref, s0_scr, s1_scr, p0_scr, p1_scr, m_scr, a_scr, acc_scr, *, tq, nq, hb, qsub):
    s_scr = (s0_scr, s1_scr)
    p_scr = (p0_scr, p1_scr)
    tk = tq // 2
    key_chunk = lax.broadcasted_iota(jnp.int32, (tk, tq), 0) // CHUNK
    qry_chunk = lax.broadcasted_iota(jnp.int32, (tk, tq), 1) // CHUNK
    diag_masks = [key_chunk + h * (tk // CHUNK) <= qry_chunk for h in range(2)]
    ones_rows = jnp.ones((ACC_ROWS - V_DIM, tk), BF16)

    def scores(qs, kpair, h, mask=None):
        koff = pl.multiple_of(kpair * tq + h * tk, tk)
        for hh in range(hb):
            s = _dot(k_ref[pl.ds(koff, tk), HEAD_W * hh:HEAD_W * (hh + 1)], qs[hh])
            s_scr[h][hh] = s if mask is None else jnp.where(mask, s, -jnp.inf)

    def pv(kpair, h):
        return [_dot(jnp.concatenate([vt_ref[kpair, V_DIM * hh:V_DIM * (hh + 1), h * tk:(h + 1) * tk],
                                      ones_rows], axis=0), p_scr[h][hh])
                for hh in range(hb)]

    def softmax(h, pvs):
        for hh in range(hb):
            m = m_scr[hh]
            s = s_scr[h][hh]
            m_new = jnp.maximum(m, jnp.max(s, axis=0, keepdims=True))
            p_scr[h][hh] = jnp.exp2((s - m_new).astype(BF16))
            alpha = jnp.exp2(m - m_new)
            acc_scr[hh] = alpha * (acc_scr[hh] + a_scr[hh] * pvs[hh])
            a_scr[hh] = alpha
            m_scr[hh] = m_new

    def q_tiles(qi):
        return [jnp.concatenate([q_ref[qsub * qi + u, HEAD_W * hh:HEAD_W * (hh + 1), :] for u in range(qsub)], axis=1)
                for hh in range(hb)]

    def diag_scores(qi):
        qs = q_tiles(qi)
        for h in range(2):
            scores(qs, qi, h, diag_masks[h])

    diag_scores(0)
    a_scr[...] = jnp.zeros(a_scr.shape, F32)
    acc_scr[...] = jnp.zeros(acc_scr.shape, F32)
    for h in range(2):
        p_scr[h][...] = jnp.zeros(p_scr[h].shape, BF16)

    def q_block(qi, c):
        qoff = pl.multiple_of(qi * tq, tq)
        qs = q_tiles(qi)
        m_scr[...] = jnp.full(m_scr.shape, -jnp.inf, F32)

        def kpair_of(r):
            return jnp.where(r == 0, qi, r - 1)

        def body(r, c2):
            prev = kpair_of(jnp.maximum(r - 1, 0))
            for h in range(2):
                pvs = pv(prev, h)
                softmax(h, pvs)
                scores(qs, r, h)
            return c2

        lax.fori_loop(0, qi, body, 0)
        prev = kpair_of(jnp.maximum(qi - 1, 0))
        nxt = jnp.minimum(qi + 1, nq - 1)
        qn = q_tiles(nxt)
        for h in range(2):
            softmax(h, pv(prev, h))
            scores(qn, nxt, h, diag_masks[h])
        last = kpair_of(qi)
        pv0, pv1 = pv(last, 0), pv(last, 1)
        for hh in range(hb):
            acc = acc_scr[hh] + a_scr[hh] * pv0[hh] + pv1[hh]
            o_ref[pl.ds(qoff, tq), V_DIM * hh:V_DIM * (hh + 1)] = (
                acc[:V_DIM] / acc[V_DIM:V_DIM + 1]).T.astype(BF16)
        return c

    lax.fori_loop(0, nq, q_block, 0)


def _attention(qt, k, vt, batch, seq, tq, hb):
    m = k.shape[0]
    nq = seq // tq
    tk = tq // 2
    assert vt.shape[2] == tq
    qsub = tq // qt.shape[2]
    return pl.pallas_call(
        functools.partial(_attn_body, tq=tq, nq=nq, hb=hb, qsub=qsub),
        grid=(batch, MLA_HEADS // hb),
        in_specs=[pl.BlockSpec((nq * qsub, hb * HEAD_W, tq // qsub), lambda b, h: (b, h, 0)),
                  pl.BlockSpec((seq, hb * HEAD_W), lambda b, h: (b, h)),
                  pl.BlockSpec((nq, hb * V_DIM, tq), lambda b, h: (b, h, 0))],
        out_specs=pl.BlockSpec((seq, hb * V_DIM), lambda b, h: (b, h)),
        out_shape=jax.ShapeDtypeStruct((m, MLA_HEADS * V_DIM), BF16),
        scratch_shapes=[pltpu.VMEM((hb, tk, tq), F32), pltpu.VMEM((hb, tk, tq), F32),
                        pltpu.VMEM((hb, tk, tq), BF16), pltpu.VMEM((hb, tk, tq), BF16),
                        pltpu.VMEM((hb, 1, tq), F32), pltpu.VMEM((hb, 1, tq), F32),
                        pltpu.VMEM((hb, ACC_ROWS, tq), F32)],
        compiler_params=_cparams(("parallel", "arbitrary")),
        name="attention",
    )(qt, k, vt)


def _silu_of_half(h):
    return h + h * jnp.tanh(h)


def _silu(y):
    return _silu_of_half(0.5 * y)


def _split3(v):
    hi = v.astype(BF16)
    r = v - hi.astype(F32)
    mid = r.astype(BF16)
    lo = (r - mid.astype(F32)).astype(BF16)
    return hi, mid, lo


def _ssd_body(z_ref, x_ref, bc_ref, dt_ref, cwx_ref, cwb_ref, cbx_ref, cbb_ref, dtb_ref, alog_ref,
              dsk_ref, nw_ref, sel_ref, eye_ref, tri_ref, tri64_ref, shift_ref, y_ref, state_ref, xhalo, bchalo,
              *, cpb):
    t = CHUNK
    gn = SSM_GROUPS * SSM_STATE

    @pl.when(pl.program_id(1) == 0)
    def _():
        state_ref[...] = jnp.zeros_like(state_ref)
        xhalo[...] = jnp.zeros_like(xhalo)
        bchalo[...] = jnp.zeros_like(bchalo)

    def conv_silu(raw_ref, halo, w_ref, b_ref):
        raws = [raw_ref[c * t:(c + 1) * t, :] for c in range(cpb)]
        prevs = [halo[...]] + raws[:-1]
        halo[...] = raws[-1]
        shifted = [_dot(shift_ref[...], jnp.concatenate([prevs[c], raws[c]], axis=0)) for c in range(cpb)]
        outs = []
        for c in range(cpb):
            y = None
            for tap in range(CONV_WIDTH):
                shift = CONV_WIDTH - 1 - tap
                xt = raws[c].astype(F32) if shift == 0 else shifted[c][(shift - 1) * t:shift * t, :]
                term = xt * w_ref[tap:tap + 1, :]
                y = term if y is None else y + term
            outs.append(_silu_of_half(y + b_ref[...]))
        return outs

    xs_all = conv_silu(x_ref, xhalo, cwx_ref, cbx_ref)
    bc_all = conv_silu(bc_ref, bchalo, cwb_ref, cbb_ref)

    head_lane = lax.broadcasted_iota(jnp.int32, (t, LANES), 1) < SSM_HEADS
    neg_a = -jnp.exp(alog_ref[...])
    tri64 = tri64_ref[...]
    stacks = []
    for c in range(cpb):
        dt = jnp.where(head_lane, jax.nn.softplus(dt_ref[c * t:(c + 1) * t, :] + dtb_ref[...]), 0.0)
        h3 = _split3(dt * neg_a)
        a_cum = (_dot(tri64, h3[0]) + _dot(tri64, h3[1]) + _dot(tri64, h3[2])) * LOG2_E
        stacks += [a_cum, dt]
    s3 = [part.astype(F32) for part in _split3(jnp.concatenate(stacks, axis=0))]
    packed = jnp.concatenate([(s3[0] + pltpu.roll(s3[1], SSM_HEADS, 1)).astype(BF16), s3[2].astype(BF16)], axis=1)
    ex_all = _dot(packed, sel_ref[...])
    left = lax.broadcasted_iota(jnp.int32, (t, LANES), 1) < SSM_HEADDIM
    zero_b = jnp.zeros((t, LANES), BF16)

    pre = []
    for c in range(cpb):
        ex = ex_all[2 * t * c:2 * t * (c + 1)]
        a_col = ex[:t]
        ea = jnp.exp2(a_col)
        a_row = jnp.sum(jnp.where(eye_ref[...] > 0, a_col, 0.0), axis=0, keepdims=True)
        decay = jnp.exp2(a_col - a_row + tri_ref[...])
        xdt = xs_all[c] * ex[t:2 * t]
        pre.append((decay, ea, xdt.astype(BF16), (xdt * jnp.exp2(a_col[t - 1:t, :] - a_col)).astype(BF16)))

    for c in range(cpb):
        decay, ea, xdt_b, xw_b = pre[c]
        xs, bc = xs_all[c], bc_all[c]
        chunk_decay = ea[t - 1:t, :]
        r0 = c * t
        for g in range(SSM_GROUPS):
            lo, hi = GROUP_W * g, GROUP_W * (g + 1)
            bg = bc[:, SSM_STATE * g:SSM_STATE * (g + 1)]
            cg_b = bc[:, gn + SSM_STATE * g:gn + SSM_STATE * (g + 1)].astype(BF16)
            bg_b = bg.astype(BF16)
            cb2 = _dot_nt(cg_b, jnp.concatenate([bg_b, bg_b], axis=0))
            yd = []
            for j in range(GROUP_W // LANES):
                l0 = lo + LANES * j
                mm = (cb2 * decay[:, l0:l0 + LANES]).astype(BF16)
                xp = xdt_b[:, l0:l0 + LANES]
                rhs = jnp.concatenate([jnp.where(left, xp, zero_b), jnp.where(left, zero_b, xp)], axis=0)
                yd.append(_dot(mm, rhs))
            y_diag = jnp.concatenate(yd, axis=1)
            st = state_ref[g]
            y_off = _dot(cg_b, st.astype(BF16)) * ea[:, lo:hi]
            inc = _dot(bg.T.astype(BF16), xw_b[:, lo:hi])
            state_ref[g] = st * chunk_decay[:, lo:hi] + inc
            yg = y_diag + y_off + dsk_ref[:, lo:hi] * xs[:, lo:hi]
            yg = yg * _silu(z_ref[r0:r0 + t, lo:hi].astype(F32))
            ms = jnp.mean(yg * yg, axis=-1, keepdims=True)
            y_ref[r0:r0 + t, lo:hi] = (yg * lax.rsqrt(ms + EPS) * nw_ref[:, lo:hi]).astype(BF16)


def _ssd(big, small, cwx, cwb, cbx, cbb, dtb, alog, dsk, nw, sel, eye_t, tri_t, tri64, shift_m, layer, batch, seq,
         cpb):
    m = big.shape[0]
    t = CHUNK
    tb = cpb * t
    nb = seq // tb
    bcw = 2 * SSM_GROUPS * SSM_STATE
    bc_blk = (2 * D_INNER) // bcw
    dt_blk = (SMALL_N - LANES) // LANES
    row = lambda b, c: b * nb + c
    const = lambda shape: pl.BlockSpec(shape, lambda b, c: (0,) * len(shape))
    return pl.pallas_call(
        functools.partial(_ssd_body, cpb=cpb),
        grid=(batch, nb),
        in_specs=[pl.BlockSpec((tb, D_INNER), lambda b, c: (row(b, c), 0)),
                  pl.BlockSpec((tb, D_INNER), lambda b, c: (row(b, c), 1)),
                  pl.BlockSpec((tb, bcw), lambda b, c: (row(b, c), bc_blk)),
                  pl.BlockSpec((tb, LANES), lambda b, c: (row(b, c), dt_blk)),
                  _lspec(layer, (CONV_WIDTH, D_INNER)), _lspec(layer, (CONV_WIDTH, bcw)),
                  _lspec(layer, (1, D_INNER)), _lspec(layer, (1, bcw)),
                  _lspec(layer, (1, LANES)), _lspec(layer, (1, LANES)),
                  _lspec(layer, (1, D_INNER)), _lspec(layer, (1, D_INNER)),
                  const((2 * LANES, D_INNER)), const((t, D_INNER)), const((t, D_INNER)), const((t, t)),
                  const(((CONV_WIDTH - 1) * t, 2 * t))],
        out_specs=pl.BlockSpec((tb, D_INNER), lambda b, c: (row(b, c), 0)),
        out_shape=jax.ShapeDtypeStruct((m, D_INNER), BF16),
        scratch_shapes=[pltpu.VMEM((SSM_GROUPS, SSM_STATE, GROUP_W), F32),
                        pltpu.VMEM((t, D_INNER), BF16),
                        pltpu.VMEM((t, bcw), BF16)],
        compiler_params=_cparams(("parallel", "arbitrary")),
        name="ssd",
    )(big, big, big, small, cwx, cwb, cbx, cbb, dtb, alog, dsk, nw, sel, eye_t, tri_t, tri64, shift_m)


def _merge_body(o_ref, y_ref, wa_ref, wb_ref, g0_ref, g1_ref, out_ref):
    ya = _dot(o_ref[...], wa_ref[...])
    yb = _dot(y_ref[...], wb_ref[...])
    g0 = jax.nn.sigmoid(g0_ref[...].astype(F32))
    g1 = jax.nn.sigmoid(g1_ref[...].astype(F32))
    out_ref[...] = (g0 * ya + g1 * yb).astype(out_ref.dtype)


def _merge(o, y, wa, wb, big, layer, tm, tn):
    m = o.shape[0]
    g0_blk = (2 * D_INNER + 2 * SSM_GROUPS * SSM_STATE) // tn
    g1_blk = g0_blk + D_MODEL // tn
    return pl.pallas_call(
        _merge_body,
        grid=(m // tm, D_MODEL // tn),
        in_specs=[pl.BlockSpec((tm, o.shape[1]), lambda i, j: (i, 0)),
                  pl.BlockSpec((tm, y.shape[1]), lambda i, j: (i, 0)),
                  _lspec(layer, (wa.shape[1], tn), lambda i, j: (0, j)),
                  _lspec(layer, (wb.shape[1], tn), lambda i, j: (0, j)),
                  pl.BlockSpec((tm, tn), lambda i, j: (i, g0_blk + j)),
                  pl.BlockSpec((tm, tn), lambda i, j: (i, g1_blk + j))],
        out_specs=pl.BlockSpec((tm, tn), lambda i, j: (i, j)),
        out_shape=jax.ShapeDtypeStruct((m, D_MODEL), BF16),
        compiler_params=_cparams(("parallel", "arbitrary")),
        name="merge",
    )(o, y, wa, wb, big, big)


def _res_mm_f32w_body(x_ref, a_ref, w_ref, o_ref, wb_ref):
    @pl.when(pl.program_id(0) == 0)
    def _():
        wb_ref[...] = w_ref[...].astype(BF16)

    o_ref[...] = x_ref[...] + _dot(a_ref[...], wb_ref[...])


def _res_matmul(x, a, w, layer, tm):
    m, n = x.shape
    k = a.shape[1]
    return pl.pallas_call(
        _res_mm_f32w_body,
        grid=(m // tm,),
        in_specs=[pl.BlockSpec((tm, n), lambda i: (i, 0)),
                  pl.BlockSpec((tm, k), lambda i: (i, 0)),
                  _lspec(layer, (k, n), single=True)],
        out_specs=pl.BlockSpec((tm, n), lambda i: (i, 0)),
        out_shape=jax.ShapeDtypeStruct((m, n), F32),
        scratch_shapes=[pltpu.VMEM((k, n), BF16)],
        compiler_params=_cparams(("arbitrary",)),
        name="out_proj",
    )(x, a, w)


def _res_mm_ksplit_body(x_ref, a_ref, w_ref, o_ref):
    @pl.when(pl.program_id(2) == 0)
    def _():
        o_ref[...] = x_ref[...]

    o_ref[...] += _dot(a_ref[...], w_ref[...])


def _res_matmul_tiled(x, a, w, layer, tm, tn, tk, name):
    m, n = x.shape
    k = a.shape[1]
    return pl.pallas_call(
        _res_mm_ksplit_body,
        grid=(m // tm, n // tn, k // tk),
        in_specs=[pl.BlockSpec((tm, tn), lambda i, j, c: (i, j)),
                  pl.BlockSpec((tm, tk), lambda i, j, c: (i, c)),
                  _lspec(layer, (tk, tn), lambda i, j, c: (c, j))],
        out_specs=pl.BlockSpec((tm, tn), lambda i, j, c: (i, j)),
        out_shape=jax.ShapeDtypeStruct((m, n), F32),
        compiler_params=_cparams(("parallel", "arbitrary", "arbitrary")),
        name=name,
    )(x, a, w)


def _ple_body(x_ref, nw_ref, wg_ref, p_ref, wp_ref, o_ref, wgb_ref):
    @pl.when(pl.program_id(0) == 0)
    def _():
        wgb_ref[...] = wg_ref[...].astype(BF16)

    x = x_ref[...]
    h = _rms(x, nw_ref[...]).astype(BF16)
    gate = jax.nn.sigmoid(_dot(h, wgb_ref[...]))
    pe = _dot(p_ref[...].astype(BF16), wp_ref[...])
    o_ref[...] = x + pe * gate


def _ple(x, nw, wg, p, wp, layer, tm):
    m, d = x.shape
    return pl.pallas_call(
        _ple_body,
        grid=(m // tm,),
        in_specs=[pl.BlockSpec((tm, d), lambda i: (i, 0)),
                  _lspec(layer, (1, d)),
                  _lspec(layer, (d, d), single=True),
                  _lspec(layer, (tm, PLE_DIM), lambda i: (i, 0)),
                  _lspec(layer, (PLE_DIM, d))],
        out_specs=pl.BlockSpec((tm, d), lambda i: (i, 0)),
        out_shape=jax.ShapeDtypeStruct((m, d), F32),
        scratch_shapes=[pltpu.VMEM((d, d), BF16)],
        compiler_params=_cparams(("arbitrary",)),
        name="ple",
    )(x, nw, wg, p, wp)


def _rope_cols(w):
    half = QK_ROPE // 2
    z = jnp.zeros(w.shape[:-1] + (LANES // 2 - half,), w.dtype)
    return jnp.concatenate([w[..., :half], z, w[..., half:], z], axis=-1)


def kernel(x, p, positions, norm_mix_w, w_in, q_a_norm_w, w_uq, kv_a_norm_w, w_ukv, q_norm_w, k_norm_w,
           w_o_mla, conv_w, conv_b, dt_bias, a_log, d_skip, ssm_norm_w, w_o_ssm, w_out, norm_mlp_w, w_up,
           w_down, ple_norm_w, w_ple_gate, w_ple):
    batch, seq, d = x.shape
    m = batch * seq
    depth = w_in.shape[0]
    tm_big = min(1024, m)
    tm_mid = min(512, m)
    tm_small = min(256, m)
    tq = min(512, seq)

    inv_freq = 1.0 / (ROPE_THETA ** (jnp.arange(0, QK_ROPE, 2, dtype=F32) / QK_ROPE))
    freq = _rope_cols(jnp.concatenate([inv_freq, inv_freq]))[None, :]
    sign = _rope_cols(jnp.concatenate([-jnp.ones_like(inv_freq), jnp.ones_like(inv_freq)]))[None, :]
    ct, st = _rope_tables(positions.reshape(m, 1), freq, sign, tm_big)
    ct_t, st_t = ct.T, st.T

    lane = jnp.arange(D_INNER)
    sel_row = jnp.arange(2 * LANES)[:, None]
    sel = ((sel_row % SSM_HEADS == (lane // SSM_HEADDIM)[None, :]) & (sel_row < 3 * SSM_HEADS)).astype(BF16)
    rows = jnp.arange(CHUNK)[:, None]
    eye_t = (rows == (lane % SSM_HEADDIM)[None, :]).astype(F32)
    tri_t = jnp.where(rows >= (lane % SSM_HEADDIM)[None, :], 0.0, -jnp.inf).astype(F32)
    tri64 = (rows >= jnp.arange(CHUNK)[None, :]).astype(BF16)
    sh_rows = jnp.arange((CONV_WIDTH - 1) * CHUNK)[:, None]
    shift_m = (jnp.arange(2 * CHUNK)[None, :] == CHUNK + sh_rows % CHUNK - (sh_rows // CHUNK + 1)).astype(BF16)

    s0 = Q_LORA + KV_LORA + QK_ROPE
    s1 = s0 + D_INNER + (D_INNER + 2 * SSM_GROUPS * SSM_STATE)
    s2 = s1 + SSM_HEADS

    w_lat, w_kr, w_dt = lax.optimization_barrier(
        (w_in[..., :Q_LORA + KV_LORA], w_in[..., Q_LORA + KV_LORA:s0], w_in[..., s1:s2]))
    w_small = jnp.concatenate(
        [w_lat, _rope_cols(w_kr), w_dt, jnp.zeros((depth, d, LANES - SSM_HEADS), F32)], axis=-1).astype(BF16)
    w_big = _shift_cast(w_in, (s0 - QK_ROPE, s2 - LANES), (QK_ROPE, LANES), (s1 - s0, 2 * D_MODEL), 1024)
    wq4 = w_uq.reshape(depth, Q_LORA, MLA_HEADS, QK_DIM)
    wq_rope = _rope_cols(wq4[..., QK_NOPE:])
    wq = jnp.concatenate([wq4[..., :QK_NOPE], wq_rope, jnp.roll(wq_rope, LANES // 2, axis=-1)], axis=-1)
    wqt = jnp.swapaxes(wq.reshape(depth, Q_LORA, MLA_HEADS * Q_ACC_W), 1, 2).astype(BF16)
    wkv4 = w_ukv.reshape(depth, KV_LORA, MLA_HEADS, QK_NOPE + V_DIM)
    wk = wkv4[..., :QK_NOPE].reshape(depth, KV_LORA, -1).astype(BF16)
    wvt = jnp.swapaxes(wkv4[..., QK_NOPE:].reshape(depth, KV_LORA, -1), 1, 2).astype(BF16)
    w_o_mla_b = w_o_mla.astype(BF16)
    w_o_ssm_b = w_o_ssm.astype(BF16)
    w_up_b = w_up.astype(BF16)
    w_down_b = w_down.astype(BF16)
    w_ple_b = w_ple.astype(BF16)

    rows3 = lambda v: v.reshape(depth, 1, -1).astype(F32)
    pad_h = jnp.zeros((depth, LANES - SSM_HEADS), F32)
    qw_rope = _rope_cols(q_norm_w[:, QK_NOPE:])
    qnw = jnp.concatenate([q_norm_w[:, :QK_NOPE], qw_rope, jnp.roll(qw_rope, LANES // 2, axis=-1)], axis=-1)
    qnw = jnp.broadcast_to(qnw[:, :, None], (depth, Q_ACC_W, tm_small)).astype(F32)
    kwn = rows3(k_norm_w[:, :QK_NOPE])
    kwr = rows3(_rope_cols(k_norm_w[:, QK_NOPE:]))
    dtb = rows3(jnp.concatenate([dt_bias, pad_h], axis=-1))
    alog = rows3(jnp.concatenate([a_log, pad_h], axis=-1))
    dsk = rows3(jnp.repeat(d_skip, SSM_HEADDIM, axis=-1))
    mix_nw, qa_nw, kva_nw = rows3(norm_mix_w), rows3(q_a_norm_w), rows3(kv_a_norm_w)
    ssm_nw, mlp_nw, ple_nw = rows3(ssm_norm_w), rows3(norm_mlp_w), rows3(ple_norm_w)
    conv_wh, conv_bh = 0.5 * conv_w, 0.5 * conv_b
    cwx, cwb = conv_wh[..., :D_INNER], conv_wh[..., D_INNER:]
    cbx, cbb = rows3(conv_bh[:, :D_INNER]), rows3(conv_bh[:, D_INNER:])
    p3 = p.reshape(depth, m, PLE_DIM)

    xf = x.reshape(m, d)
    for i in range(depth):
        small = _norm_matmul(xf, mix_nw, w_small, i, F32, tm_big, SMALL_N, "in_proj_small")
        big = _norm_matmul(xf, mix_nw, w_big, i, BF16, tm_big, 2048, "in_proj_big")
        qt = _q_proj(small, qa_nw, wqt, qnw, ct_t, st_t, i, tm_small)
        k, vt = _kv_proj(small, kva_nw, wk, wvt, kwn, kwr, ct, st, i, tq)
        o = _attention(qt, k, vt, batch, seq, tq, 2)
        y = _ssd(big, small, cwx, cwb, cbx, cbb, dtb, alog, dsk, ssm_nw, sel, eye_t, tri_t, tri64, shift_m, i,
                 batch, seq, 4)
        merged = _merge(o, y, w_o_mla_b, w_o_ssm_b, big, i, tm_big, 512)
        xf = _res_matmul(xf, merged, w_out, i, tm_mid)
        hidden = _norm_matmul(xf, mlp_nw, w_up_b, i, BF16, tm_big, 2048, "mlp_up", relu2=True)
        xf = _res_matmul_tiled(xf, hidden, w_down_b, i, tm_big, 1024, D_FF // 2, "mlp_down")
        xf = _ple(xf, ple_nw, w_ple_gate, p3, w_ple_b, i, tm_mid)
    return xf.reshape(batch, seq, d)
```

```python
import functools

import jax
import jax.numpy as jnp
from jax import lax
from jax.experimental import pallas as pl
from jax.experimental.pallas import tpu as pltpu

F32 = jnp.float32
BF16 = jnp.bfloat16

D_MODEL = 2048
DEPTH = 4
CHUNK = 64
PLE_DIM = 256
EPS = 1e-6
MLA_HEADS = 16
Q_LORA = 512
KV_LORA = 512
QK_NOPE = 128
QK_ROPE = 64
V_DIM = 128
QK_DIM = QK_NOPE + QK_ROPE
ROPE_THETA = 10000.0
D_INNER = 2 * D_MODEL
SSM_HEADDIM = 64
SSM_HEADS = D_INNER // SSM_HEADDIM
SSM_GROUPS = 8
SSM_STATE = 128
GROUP_W = D_INNER // SSM_GROUPS
CONV_WIDTH = 4
D_FF = 4 * D_MODEL

LOG2_E = 1.4426950408889634
LANES = 128
ACC_ROWS = V_DIM + 16
Q_ACC_W = 3 * LANES
HEAD_W = 2 * LANES
SMALL_N = Q_LORA + KV_LORA + 2 * LANES
VMEM_LIMIT = 56 * 1024 * 1024


def _cparams(sem):
    return pltpu.CompilerParams(dimension_semantics=sem, vmem_limit_bytes=VMEM_LIMIT)


def _rms(xf, w):
    ms = jnp.mean(xf * xf, axis=-1, keepdims=True)
    return xf * lax.rsqrt(ms + EPS) * w


def _lspec(layer, block, imap=None, single=False):
    if imap is None:
        imap = lambda *g: (0,) * len(block)
    return pl.BlockSpec((None,) + tuple(block), lambda *g: (layer,) + tuple(imap(*g)),
                        pipeline_mode=pl.Buffered(1) if single else None)


def _dot(a, b):
    return jnp.dot(a, b, preferred_element_type=F32)


def _dot_nt(a, b):
    return lax.dot_general(a, b, (((1,), (1,)), ((), ())), preferred_element_type=F32)


def _rope_body(pos_ref, freq_ref, sign_ref, ct_ref, st_ref):
    ang = pos_ref[...].astype(F32) * freq_ref[...]
    ct_ref[...] = jnp.cos(ang)
    st_ref[...] = jnp.sin(ang) * sign_ref[...]


def _rope_tables(pos_col, freq, sign, tm):
    m = pos_col.shape[0]
    return pl.pallas_call(
        _rope_body,
        grid=(m // tm,),
        in_specs=[pl.BlockSpec((tm, 1), lambda i: (i, 0)),
                  pl.BlockSpec((1, LANES), lambda i: (0, 0)),
                  pl.BlockSpec((1, LANES), lambda i: (0, 0))],
        out_specs=[pl.BlockSpec((tm, LANES), lambda i: (i, 0)),
                   pl.BlockSpec((tm, LANES), lambda i: (i, 0))],
        out_shape=[jax.ShapeDtypeStruct((m, LANES), F32)] * 2,
        compiler_params=_cparams(("parallel",)),
        name="rope_tables",
    )(pos_col, freq, sign)


def _norm_mm_body(x_ref, nw_ref, w_ref, o_ref, h_ref, *, relu2):
    @pl.when(pl.program_id(1) == 0)
    def _():
        h_ref[...] = _rms(x_ref[...], nw_ref[...]).astype(BF16)

    y = _dot(h_ref[...], w_ref[...].astype(BF16))
    if relu2:
        y = jnp.square(jnp.maximum(y, 0.0))
    o_ref[...] = y.astype(o_ref.dtype)


def _norm_matmul(x, nw, w, layer, out_dtype, tm, tn, name, relu2=False, single_x=False):
    m, k = x.shape
    n = w.shape[2]
    return pl.pallas_call(
        functools.partial(_norm_mm_body, relu2=relu2),
        grid=(m // tm, n // tn),
        in_specs=[pl.BlockSpec((tm, k), lambda i, j: (i, 0), pipeline_mode=pl.Buffered(1) if single_x else None),
                  _lspec(layer, (1, k)),
                  _lspec(layer, (k, tn), lambda i, j: (0, j))],
        out_specs=pl.BlockSpec((tm, tn), lambda i, j: (i, j)),
        out_shape=jax.ShapeDtypeStruct((m, n), out_dtype),
        scratch_shapes=[pltpu.VMEM((tm, k), BF16)],
        compiler_params=_cparams(("parallel", "arbitrary")),
        name=name,
    )(x, nw, w)


def _shift_cast_body(a_ref, b_ref, o_ref, *, tiles0, shifts):
    v = jnp.concatenate([a_ref[...], b_ref[...]], axis=1)
    tn = o_ref.shape[1]
    first = pl.program_id(1) < tiles0

    @pl.when(first)
    def _():
        o_ref[...] = v[:, shifts[0]:shifts[0] + tn].astype(BF16)

    @pl.when(jnp.logical_not(first))
    def _():
        o_ref[...] = v[:, shifts[1]:shifts[1] + tn].astype(BF16)


def _shift_cast(w, bases, shifts, widths, tn):
    depth, k, _ = w.shape
    assert all(b % tn == 0 and n % tn == 0 and 0 < s <= LANES for b, s, n in zip(bases, shifts, widths))
    tiles0 = widths[0] // tn
    sub = tn // LANES

    def a_col(l, j):
        return jnp.where(j < tiles0, bases[0] // tn + j, bases[1] // tn + j - tiles0)

    def b_col(l, j):
        return jnp.where(j < tiles0, (bases[0] + tn) // LANES + j * sub, (bases[1] + tn) // LANES + (j - tiles0) * sub)

    return pl.pallas_call(
        functools.partial(_shift_cast_body, tiles0=tiles0, shifts=tuple(shifts)),
        grid=(depth, (widths[0] + widths[1]) // tn),
        in_specs=[pl.BlockSpec((None, k, tn), lambda l, j: (l, 0, a_col(l, j))),
                  pl.BlockSpec((None, k, LANES), lambda l, j: (l, 0, b_col(l, j)))],
        out_specs=pl.BlockSpec((None, k, tn), lambda l, j: (l, 0, j)),
        out_shape=jax.ShapeDtypeStruct((depth, k, widths[0] + widths[1]), BF16),
        compiler_params=_cparams(("parallel", "parallel")),
        name="relayout_w_in",
    )(w, w)


def _rope_apply(v, c, s):
    return v * c + pltpu.roll(v, 64, 1) * s


def _q_body(c_ref, nw_ref, wt_ref, qnw_ref, ct_ref, st_ref, o_ref):
    cn = _rms(c_ref[...], nw_ref[...]).astype(BF16)
    acc = _dot_nt(wt_ref[...], cn)
    wn = qnw_ref[:LANES, :]
    wrc = qnw_ref[LANES:2 * LANES, :] * ct_ref[...]
    wxs = qnw_ref[2 * LANES:, :] * st_ref[...]
    scale = QK_DIM ** -0.5 * LOG2_E
    for h in range(MLA_HEADS):
        qn = acc[Q_ACC_W * h:Q_ACC_W * h + LANES, :]
        qr = acc[Q_ACC_W * h + LANES:Q_ACC_W * h + 2 * LANES, :]
        qx = acc[Q_ACC_W * h + 2 * LANES:Q_ACC_W * (h + 1), :]
        ssq = jnp.sum(qn * qn + qr * qr, axis=0, keepdims=True)
        rs = lax.rsqrt(ssq / QK_DIM + EPS) * scale
        o_ref[0, HEAD_W * h:HEAD_W * h + LANES, :] = (qn * rs * wn).astype(BF16)
        o_ref[0, HEAD_W * h + LANES:HEAD_W * (h + 1), :] = ((qr * wrc + qx * wxs) * rs).astype(BF16)


def _q_proj(small, nw, wqt, qnw, ct_t, st_t, layer, tm):
    m = small.shape[0]
    n = MLA_HEADS * HEAD_W
    return pl.pallas_call(
        _q_body,
        grid=(m // tm,),
        in_specs=[pl.BlockSpec((tm, Q_LORA), lambda i: (i, 0)),
                  _lspec(layer, (1, Q_LORA)),
                  _lspec(layer, (MLA_HEADS * Q_ACC_W, Q_LORA)),
                  _lspec(layer, (Q_ACC_W, tm)),
                  pl.BlockSpec((LANES, tm), lambda i: (0, i)),
                  pl.BlockSpec((LANES, tm), lambda i: (0, i))],
        out_specs=pl.BlockSpec((1, n, tm), lambda i: (i, 0, 0)),
        out_shape=jax.ShapeDtypeStruct((m // tm, n, tm), BF16),
        compiler_params=_cparams(("parallel",)),
        name="q_proj",
    )(small, nw, wqt, qnw, ct_t, st_t)


def _kv_body(c_ref, kr_ref, nw_ref, wk_ref, wvt_ref, kwn_ref, kwr_ref, ct_ref, st_ref, k_ref, vt_ref):
    cn = _rms(c_ref[...], nw_ref[...]).astype(BF16)
    acc = _dot(cn, wk_ref[...])
    vt_ref[0] = _dot_nt(wvt_ref[...], cn).astype(BF16)
    c = ct_ref[...]
    s = st_ref[...]
    kr = kr_ref[...]
    wn = kwn_ref[...]
    wr = kwr_ref[...]
    ssr = jnp.sum(kr * kr, axis=-1, keepdims=True)
    kr_rot = _rope_apply(kr * wr, c, s)
    for h in range(MLA_HEADS):
        kn = acc[:, LANES * h:LANES * (h + 1)]
        ssq = jnp.sum(kn * kn, axis=-1, keepdims=True) + ssr
        rs = lax.rsqrt(ssq / QK_DIM + EPS)
        k_ref[:, HEAD_W * h:HEAD_W * h + LANES] = (kn * rs * wn).astype(BF16)
        k_ref[:, HEAD_W * h + LANES:HEAD_W * (h + 1)] = (kr_rot * rs).astype(BF16)


def _kv_proj(small, nw, wk, wvt, kwn, kwr, ct, st, layer, tm):
    m = small.shape[0]
    nk = MLA_HEADS * HEAD_W
    nv = MLA_HEADS * V_DIM
    kr_blk = (Q_LORA + KV_LORA) // LANES
    return pl.pallas_call(
        _kv_body,
        grid=(m // tm,),
        in_specs=[pl.BlockSpec((tm, KV_LORA), lambda i: (i, 1)),
                  pl.BlockSpec((tm, LANES), lambda i: (i, kr_blk)),
                  _lspec(layer, (1, KV_LORA)),
                  _lspec(layer, (KV_LORA, MLA_HEADS * QK_NOPE)),
                  _lspec(layer, (nv, KV_LORA)),
                  _lspec(layer, (1, LANES)),
                  _lspec(layer, (1, LANES)),
                  pl.BlockSpec((tm, LANES), lambda i: (i, 0)),
                  pl.BlockSpec((tm, LANES), lambda i: (i, 0))],
        out_specs=[pl.BlockSpec((tm, nk), lambda i: (i, 0)),
                   pl.BlockSpec((1, nv, tm), lambda i: (i, 0, 0))],
        out_shape=[jax.ShapeDtypeStruct((m, nk), BF16), jax.ShapeDtypeStruct((m // tm, nv, tm), BF16)],
        compiler_params=_cparams(("parallel",)),
        name="kv_proj",
    )(small, small, nw, wk, wvt, kwn, kwr, ct, st)


def _attn_body(q_ref, k_ref, vt_ref, o_ref, s0_scr, s1_scr, p0_scr, p1_scr, m_scr, a_scr, acc_scr, *, tq, nq, hb, qsub):
    s_scr = (s0_scr, s1_scr)
    p_scr = (p0_scr, p1_scr)
    tk = tq // 2
    key_chunk = lax.broadcasted_iota(jnp.int32, (tk, tq), 0) // CHUNK
    qry_chunk = lax.broadcasted_iota(jnp.int32, (tk, tq), 1) // CHUNK
    diag_masks = [key_chunk + h * (tk // CHUNK) <= qry_chunk for h in range(2)]
    ones_rows = jnp.ones((ACC_ROWS - V_DIM, tk), BF16)

    def scores(qs, kpair, h, mask=None):
        koff = pl.multiple_of(kpair * tq + h * tk, tk)
        for hh in range(hb):
            s = _dot(k_ref[pl.ds(koff, tk), HEAD_W * hh:HEAD_W * (hh + 1)], qs[hh])
            s_scr[h][hh] = s if mask is None else jnp.where(mask, s, -jnp.inf)

    def pv(kpair, h):
        return [_dot(jnp.concatenate([vt_ref[kpair, V_DIM * hh:V_DIM * (hh + 1), h * tk:(h + 1) * tk],
                                      ones_rows], axis=0), p_scr[h][hh])
                for hh in range(hb)]

    def softmax(h, pvs):
        for hh in range(hb):
            m = m_scr[hh]
            s = s_scr[h][hh]
            m_new = jnp.maximum(m, jnp.max(s, axis=0, keepdims=True))
            p_scr[h][hh] = jnp.exp2((s - m_new).astype(BF16))
            alpha = jnp.exp2(m - m_new)
            acc_scr[hh] = alpha * (acc_scr[hh] + a_scr[hh] * pvs[hh])
            a_scr[hh] = alpha
            m_scr[hh] = m_new

    def q_tiles(qi):
        return [jnp.concatenate([q_ref[qsub * qi + u, HEAD_W * hh:HEAD_W * (hh + 1), :] for u in range(qsub)], axis=1)
                for hh in range(hb)]

    def diag_scores(qi):
        qs = q_tiles(qi)
        for h in range(2):
            scores(qs, qi, h, diag_masks[h])

    diag_scores(0)
    a_scr[...] = jnp.zeros(a_scr.shape, F32)
    acc_scr[...] = jnp.zeros(acc_scr.shape, F32)
    for h in range(2):
        p_scr[h][...] = jnp.zeros(p_scr[h].shape, BF16)

    def q_block(qi, c):
        qoff = pl.multiple_of(qi * tq, tq)
        qs = q_tiles(qi)
        m_scr[...] = jnp.full(m_scr.shape, -jnp.inf, F32)

        def kpair_of(r):
            return jnp.where(r == 0, qi, r - 1)

        def body(r, c2):
            prev = kpair_of(jnp.maximum(r - 1, 0))
            for h in range(2):
                pvs = pv(prev, h)
                softmax(h, pvs)
                scores(qs, r, h)
            return c2

        lax.fori_loop(0, qi, body, 0)
        prev = kpair_of(jnp.maximum(qi - 1, 0))
        nxt = jnp.minimum(qi + 1, nq - 1)
        qn = q_tiles(nxt)
        for h in range(2):
            softmax(h, pv(prev, h))
            scores(qn, nxt, h, diag_masks[h])
        last = kpair_of(qi)
        pv0, pv1 = pv(last, 0), pv(last, 1)
        for hh in range(hb):
            acc = acc_scr[hh] + a_scr[hh] * pv0[hh] + pv1[hh]
            o_ref[pl.ds(qoff, tq), V_DIM * hh:V_DIM * (hh + 1)] = (
                acc[:V_DIM] / acc[V_DIM:V_DIM + 1]).T.astype(BF16)
        return c

    lax.fori_loop(0, nq, q_block, 0)


def _attention(qt, k, vt, batch, seq, tq, hb):
    m = k.shape[0]
    nq = seq // tq
    tk = tq // 2
    assert vt.shape[2] == tq
    qsub = tq // qt.shape[2]
    return pl.pallas_call(
        functools.partial(_attn_body, tq=tq, nq=nq, hb=hb, qsub=qsub),
        grid=(batch, MLA_HEADS // hb),
        in_specs=[pl.BlockSpec((nq * qsub, hb * HEAD_W, tq // qsub), lambda b, h: (b, h, 0)),
                  pl.BlockSpec((seq, hb * HEAD_W), lambda b, h: (b, h)),
                  pl.BlockSpec((nq, hb * V_DIM, tq), lambda b, h: (b, h, 0))],
        out_specs=pl.BlockSpec((seq, hb * V_DIM), lambda b, h: (b, h)),
        out_shape=jax.ShapeDtypeStruct((m, MLA_HEADS * V_DIM), BF16),
        scratch_shapes=[pltpu.VMEM((hb, tk, tq), F32), pltpu.VMEM((hb, tk, tq), F32),
                        pltpu.VMEM((hb, tk, tq), BF16), pltpu.VMEM((hb, tk, tq), BF16),
                        pltpu.VMEM((hb, 1, tq), F32), pltpu.VMEM((hb, 1, tq), F32),
                        pltpu.VMEM((hb, ACC_ROWS, tq), F32)],
        compiler_params=_cparams(("parallel", "arbitrary")),
        name="attention",
    )(qt, k, vt)


def _silu_of_half(h):
    return h + h * jnp.tanh(h)


def _silu(y):
    return _silu_of_half(0.5 * y)


def _split3(v):
    hi = v.astype(BF16)
    r = v - hi.astype(F32)
    mid = r.astype(BF16)
    lo = (r - mid.astype(F32)).astype(BF16)
    return hi, mid, lo


def _ssd_body(z_ref, x_ref, bc_ref, dt_ref, cwx_ref, cwb_ref, cbx_ref, cbb_ref, dtb_ref, alog_ref,
              dsk_ref, nw_ref, sel_ref, eye_ref, tri_ref, tri64_ref, shift_ref, y_ref, state_ref, xhalo, bchalo,
              *, cpb):
    t = CHUNK
    gn = SSM_GROUPS * SSM_STATE

    @pl.when(pl.program_id(1) == 0)
    def _():
        state_ref[...] = jnp.zeros_like(state_ref)
        xhalo[...] = jnp.zeros_like(xhalo)
        bchalo[...] = jnp.zeros_like(bchalo)

    def conv_silu(raw_ref, halo, w_ref, b_ref, c0, c1):
        raws = [raw_ref[c * t:(c + 1) * t, c0:c1] for c in range(cpb)]
        prevs = [halo[:, c0:c1]] + raws[:-1]
        halo[:, c0:c1] = raws[-1]
        outs = []
        for c in range(cpb):
            shifted = _dot(shift_ref[...], jnp.concatenate([prevs[c], raws[c]], axis=0))
            y = None
            for tap in range(CONV_WIDTH):
                shift = CONV_WIDTH - 1 - tap
                xt = raws[c].astype(F32) if shift == 0 else shifted[(shift - 1) * t:shift * t, :]
                term = xt * w_ref[tap:tap + 1, c0:c1]
                y = term if y is None else y + term
            outs.append(_silu_of_half(y + b_ref[:, c0:c1]))
        return outs

    head_lane = lax.broadcasted_iota(jnp.int32, (t, LANES), 1) < SSM_HEADS
    neg_a = -jnp.exp(alog_ref[...])
    tri64 = tri64_ref[...]
    stacks = []
    for c in range(cpb):
        dt = jnp.where(head_lane, jax.nn.softplus(dt_ref[c * t:(c + 1) * t, :] + dtb_ref[...]), 0.0)
        h3 = _split3(dt * neg_a)
        a_cum = (_dot(tri64, h3[0]) + _dot(tri64, h3[1]) + _dot(tri64, h3[2])) * LOG2_E
        stacks += [a_cum, dt]
    s3 = [part.astype(F32) for part in _split3(jnp.concatenate(stacks, axis=0))]
    packed = jnp.concatenate([(s3[0] + pltpu.roll(s3[1], SSM_HEADS, 1)).astype(BF16), s3[2].astype(BF16)], axis=1)
    left = lax.broadcasted_iota(jnp.int32, (t, LANES), 1) < SSM_HEADDIM
    zero_b = jnp.zeros((t, LANES), BF16)

    for g in range(SSM_GROUPS):
        lo, hi = GROUP_W * g, GROUP_W * (g + 1)
        xs_g = conv_silu(x_ref, xhalo, cwx_ref, cbx_ref, lo, hi)
        b_g = conv_silu(bc_ref, bchalo, cwb_ref, cbb_ref, SSM_STATE * g, SSM_STATE * (g + 1))
        c_g = conv_silu(bc_ref, bchalo, cwb_ref, cbb_ref, gn + SSM_STATE * g, gn + SSM_STATE * (g + 1))
        ex_g = _dot(packed, sel_ref[:, lo:hi])
        for c in range(cpb):
            r0 = c * t
            a_col = ex_g[2 * t * c:2 * t * c + t]
            ea = jnp.exp2(a_col)
            a_row = jnp.sum(jnp.where(eye_ref[:, lo:hi] > 0, a_col, 0.0), axis=0, keepdims=True)
            decay = jnp.exp2(a_col - a_row + tri_ref[:, lo:hi])
            xdt = xs_g[c] * ex_g[2 * t * c + t:2 * t * (c + 1)]
            xdt_b = xdt.astype(BF16)
            xw_b = (xdt * jnp.exp2(a_col[t - 1:t, :] - a_col)).astype(BF16)
            bg = b_g[c]
            cg_b = c_g[c].astype(BF16)
            bg_b = bg.astype(BF16)
            cb2 = _dot_nt(cg_b, jnp.concatenate([bg_b, bg_b], axis=0))
            yd = []
            for j in range(GROUP_W // LANES):
                mm = (cb2 * decay[:, LANES * j:LANES * (j + 1)]).astype(BF16)
                xp = xdt_b[:, LANES * j:LANES * (j + 1)]
                rhs = jnp.concatenate([jnp.where(left, xp, zero_b), jnp.where(left, zero_b, xp)], axis=0)
                yd.append(_dot(mm, rhs))
            y_diag = jnp.concatenate(yd, axis=1)
            st = state_ref[g]
            y_off = _dot(cg_b, st.astype(BF16)) * ea
            inc = _dot(bg.T.astype(BF16), xw_b)
            state_ref[g] = st * ea[t - 1:t, :] + inc
            yg = y_diag + y_off + dsk_ref[:, lo:hi] * xs_g[c]
            yg = yg * _silu(z_ref[r0:r0 + t, lo:hi].astype(F32))
            ms = jnp.mean(yg * yg, axis=-1, keepdims=True)
            y_ref[r0:r0 + t, lo:hi] = (yg * lax.rsqrt(ms + EPS) * nw_ref[:, lo:hi]).astype(BF16)


def _ssd(big, small, cwx, cwb, cbx, cbb, dtb, alog, dsk, nw, sel, eye_t, tri_t, tri64, shift_m, layer, batch, seq,
         cpb):
    m = big.shape[0]
    t = CHUNK
    tb = cpb * t
    nb = seq // tb
    bcw = 2 * SSM_GROUPS * SSM_STATE
    bc_blk = (2 * D_INNER) // bcw
    dt_blk = (SMALL_N - LANES) // LANES
    row = lambda b, c: b * nb + c
    const = lambda shape: pl.BlockSpec(shape, lambda b, c: (0,) * len(shape))
    return pl.pallas_call(
        functools.partial(_ssd_body, cpb=cpb),
        grid=(batch, nb),
        in_specs=[pl.BlockSpec((tb, D_INNER), lambda b, c: (row(b, c), 0)),
                  pl.BlockSpec((tb, D_INNER), lambda b, c: (row(b, c), 1)),
                  pl.BlockSpec((tb, bcw), lambda b, c: (row(b, c), bc_blk)),
                  pl.BlockSpec((tb, LANES), lambda b, c: (row(b, c), dt_blk)),
                  _lspec(layer, (CONV_WIDTH, D_INNER)), _lspec(layer, (CONV_WIDTH, bcw)),
                  _lspec(layer, (1, D_INNER)), _lspec(layer, (1, bcw)),
                  _lspec(layer, (1, LANES)), _lspec(layer, (1, LANES)),
                  _lspec(layer, (1, D_INNER)), _lspec(layer, (1, D_INNER)),
                  const((2 * LANES, D_INNER)), const((t, D_INNER)), const((t, D_INNER)), const((t, t)),
                  const(((CONV_WIDTH - 1) * t, 2 * t))],
        out_specs=pl.BlockSpec((tb, D_INNER), lambda b, c: (row(b, c), 0)),
        out_shape=jax.ShapeDtypeStruct((m, D_INNER), BF16),
        scratch_shapes=[pltpu.VMEM((SSM_GROUPS, SSM_STATE, GROUP_W), F32),
                        pltpu.VMEM((t, D_INNER), BF16),
                        pltpu.VMEM((t, bcw), BF16)],
        compiler_params=_cparams(("parallel", "arbitrary")),
        name="ssd",
    )(big, big, big, small, cwx, cwb, cbx, cbb, dtb, alog, dsk, nw, sel, eye_t, tri_t, tri64, shift_m)


def _merge_body(o_ref, y_ref, wa_ref, wb_ref, g0_ref, g1_ref, out_ref):
    ya = _dot(o_ref[...], wa_ref[...])
    yb = _dot(y_ref[...], wb_ref[...])
    g0 = jax.nn.sigmoid(g0_ref[...].astype(F32))
    g1 = jax.nn.sigmoid(g1_ref[...].astype(F32))
    out_ref[...] = (g0 * ya + g1 * yb).astype(out_ref.dtype)


def _merge(o, y, wa, wb, big, layer, tm, tn):
    m = o.shape[0]
    g0_blk = (2 * D_INNER + 2 * SSM_GROUPS * SSM_STATE) // tn
    g1_blk = g0_blk + D_MODEL // tn
    return pl.pallas_call(
        _merge_body,
        grid=(m // tm, D_MODEL // tn),
        in_specs=[pl.BlockSpec((tm, o.shape[1]), lambda i, j: (i, 0)),
                  pl.BlockSpec((tm, y.shape[1]), lambda i, j: (i, 0)),
                  _lspec(layer, (wa.shape[1], tn), lambda i, j: (0, j)),
                  _lspec(layer, (wb.shape[1], tn), lambda i, j: (0, j)),
                  pl.BlockSpec((tm, tn), lambda i, j: (i, g0_blk + j)),
                  pl.BlockSpec((tm, tn), lambda i, j: (i, g1_blk + j))],
        out_specs=pl.BlockSpec((tm, tn), lambda i, j: (i, j)),
        out_shape=jax.ShapeDtypeStruct((m, D_MODEL), BF16),
        compiler_params=_cparams(("parallel", "arbitrary")),
        name="merge",
    )(o, y, wa, wb, big, big)


def _res_mm_f32w_body(x_ref, a_ref, w_ref, o_ref, wb_ref):
    @pl.when(pl.program_id(0) == 0)
    def _():
        wb_ref[...] = w_ref[...].astype(BF16)

    o_ref[...] = x_ref[...] + _dot(a_ref[...], wb_ref[...])


def _res_matmul(x, a, w, layer, tm):
    m, n = x.shape
    k = a.shape[1]
    return pl.pallas_call(
        _res_mm_f32w_body,
        grid=(m // tm,),
        in_specs=[pl.BlockSpec((tm, n), lambda i: (i, 0)),
                  pl.BlockSpec((tm, k), lambda i: (i, 0)),
                  _lspec(layer, (k, n), single=True)],
        out_specs=pl.BlockSpec((tm, n), lambda i: (i, 0)),
        out_shape=jax.ShapeDtypeStruct((m, n), F32),
        scratch_shapes=[pltpu.VMEM((k, n), BF16)],
        compiler_params=_cparams(("arbitrary",)),
        name="out_proj",
    )(x, a, w)


def _res_mm_ksplit_body(x_ref, a_ref, w_ref, o_ref):
    @pl.when(pl.program_id(2) == 0)
    def _():
        o_ref[...] = x_ref[...]

    o_ref[...] += _dot(a_ref[...], w_ref[...])


def _res_matmul_tiled(x, a, w, layer, tm, tn, tk, name):
    m, n = x.shape
    k = a.shape[1]
    return pl.pallas_call(
        _res_mm_ksplit_body,
        grid=(m // tm, n // tn, k // tk),
        in_specs=[pl.BlockSpec((tm, tn), lambda i, j, c: (i, j)),
                  pl.BlockSpec((tm, tk), lambda i, j, c: (i, c)),
                  _lspec(layer, (tk, tn), lambda i, j, c: (c, j))],
        out_specs=pl.BlockSpec((tm, tn), lambda i, j, c: (i, j)),
        out_shape=jax.ShapeDtypeStruct((m, n), F32),
        compiler_params=_cparams(("parallel", "arbitrary", "arbitrary")),
        name=name,
    )(x, a, w)


def _ple_body(x_ref, nw_ref, wg_ref, p_ref, wp_ref, o_ref, wgb_ref):
    @pl.when(pl.program_id(0) == 0)
    def _():
        wgb_ref[...] = wg_ref[...].astype(BF16)

    x = x_ref[...]
    h = _rms(x, nw_ref[...]).astype(BF16)
    gate = jax.nn.sigmoid(_dot(h, wgb_ref[...]))
    pe = _dot(p_ref[...].astype(BF16), wp_ref[...])
    o_ref[...] = x + pe * gate


def _ple(x, nw, wg, p, wp, layer, tm):
    m, d = x.shape
    return pl.pallas_call(
        _ple_body,
        grid=(m // tm,),
        in_specs=[pl.BlockSpec((tm, d), lambda i: (i, 0)),
                  _lspec(layer, (1, d)),
                  _lspec(layer, (d, d), single=True),
                  _lspec(layer, (tm, PLE_DIM), lambda i: (i, 0)),
                  _lspec(layer, (PLE_DIM, d))],
        out_specs=pl.BlockSpec((tm, d), lambda i: (i, 0)),
        out_shape=jax.ShapeDtypeStruct((m, d), F32),
        scratch_shapes=[pltpu.VMEM((d, d), BF16)],
        compiler_params=_cparams(("arbitrary",)),
        name="ple",
    )(x, nw, wg, p, wp)


def _rope_cols(w):
    half = QK_ROPE // 2
    z = jnp.zeros(w.shape[:-1] + (LANES // 2 - half,), w.dtype)
    return jnp.concatenate([w[..., :half], z, w[..., half:], z], axis=-1)


def kernel(x, p, positions, norm_mix_w, w_in, q_a_norm_w, w_uq, kv_a_norm_w, w_ukv, q_norm_w, k_norm_w,
           w_o_mla, conv_w, conv_b, dt_bias, a_log, d_skip, ssm_norm_w, w_o_ssm, w_out, norm_mlp_w, w_up,
           w_down, ple_norm_w, w_ple_gate, w_ple):
    batch, seq, d = x.shape
    m = batch * seq
    depth = w_in.shape[0]
    tm_big = min(1024, m)
    tm_mid = min(512, m)
    tm_small = min(256, m)
    tq = min(512, seq)

    inv_freq = 1.0 / (ROPE_THETA ** (jnp.arange(0, QK_ROPE, 2, dtype=F32) / QK_ROPE))
    freq = _rope_cols(jnp.concatenate([inv_freq, inv_freq]))[None, :]
    sign = _rope_cols(jnp.concatenate([-jnp.ones_like(inv_freq), jnp.ones_like(inv_freq)]))[None, :]
    ct, st = _rope_tables(positions.reshape(m, 1), freq, sign, tm_big)
    ct_t, st_t = ct.T, st.T

    lane = jnp.arange(D_INNER)
    sel_row = jnp.arange(2 * LANES)[:, None]
    sel = ((sel_row % SSM_HEADS == (lane // SSM_HEADDIM)[None, :]) & (sel_row < 3 * SSM_HEADS)).astype(BF16)
    rows = jnp.arange(CHUNK)[:, None]
    eye_t = (rows == (lane % SSM_HEADDIM)[None, :]).astype(F32)
    tri_t = jnp.where(rows >= (lane % SSM_HEADDIM)[None, :], 0.0, -jnp.inf).astype(F32)
    tri64 = (rows >= jnp.arange(CHUNK)[None, :]).astype(BF16)
    sh_rows = jnp.arange((CONV_WIDTH - 1) * CHUNK)[:, None]
    shift_m = (jnp.arange(2 * CHUNK)[None, :] == CHUNK + sh_rows % CHUNK - (sh_rows // CHUNK + 1)).astype(BF16)

    s0 = Q_LORA + KV_LORA + QK_ROPE
    s1 = s0 + D_INNER + (D_INNER + 2 * SSM_GROUPS * SSM_STATE)
    s2 = s1 + SSM_HEADS

    w_lat, w_kr, w_dt = lax.optimization_barrier(
        (w_in[..., :Q_LORA + KV_LORA], w_in[..., Q_LORA + KV_LORA:s0], w_in[..., s1:s2]))
    w_small = jnp.concatenate(
        [w_lat, _rope_cols(w_kr), w_dt, jnp.zeros((depth, d, LANES - SSM_HEADS), F32)], axis=-1).astype(BF16)
    w_big = _shift_cast(w_in, (s0 - QK_ROPE, s2 - LANES), (QK_ROPE, LANES), (s1 - s0, 2 * D_MODEL), 1024)
    wq4 = w_uq.reshape(depth, Q_LORA, MLA_HEADS, QK_DIM)
    wq_rope = _rope_cols(wq4[..., QK_NOPE:])
    wq = jnp.concatenate([wq4[..., :QK_NOPE], wq_rope, jnp.roll(wq_rope, LANES // 2, axis=-1)], axis=-1)
    wqt = jnp.swapaxes(wq.reshape(depth, Q_LORA, MLA_HEADS * Q_ACC_W), 1, 2).astype(BF16)
    wkv4 = w_ukv.reshape(depth, KV_LORA, MLA_HEADS, QK_NOPE + V_DIM)
    wk = wkv4[..., :QK_NOPE].reshape(depth, KV_LORA, -1).astype(BF16)
    wvt = jnp.swapaxes(wkv4[..., QK_NOPE:].reshape(depth, KV_LORA, -1), 1, 2).astype(BF16)
    w_o_mla_b = w_o_mla.astype(BF16)
    w_o_ssm_b = w_o_ssm.astype(BF16)
    w_up_b = w_up.astype(BF16)
    w_down_b = w_down.astype(BF16)
    w_ple_b = w_ple.astype(BF16)

    rows3 = lambda v: v.reshape(depth, 1, -1).astype(F32)
    pad_h = jnp.zeros((depth, LANES - SSM_HEADS), F32)
    qw_rope = _rope_cols(q_norm_w[:, QK_NOPE:])
    qnw = jnp.concatenate([q_norm_w[:, :QK_NOPE], qw_rope, jnp.roll(qw_rope, LANES // 2, axis=-1)], axis=-1)
    qnw = jnp.broadcast_to(qnw[:, :, None], (depth, Q_ACC_W, tm_small)).astype(F32)
    kwn = rows3(k_norm_w[:, :QK_NOPE])
    kwr = rows3(_rope_cols(k_norm_w[:, QK_NOPE:]))
    dtb = rows3(jnp.concatenate([dt_bias, pad_h], axis=-1))
    alog = rows3(jnp.concatenate([a_log, pad_h], axis=-1))
    dsk = rows3(jnp.repeat(d_skip, SSM_HEADDIM, axis=-1))
    mix_nw, qa_nw, kva_nw = rows3(norm_mix_w), rows3(q_a_norm_w), rows3(kv_a_norm_w)
    ssm_nw, mlp_nw, ple_nw = rows3(ssm_norm_w), rows3(norm_mlp_w), rows3(ple_norm_w)
    conv_wh, conv_bh = 0.5 * conv_w, 0.5 * conv_b
    cwx, cwb = conv_wh[..., :D_INNER], conv_wh[..., D_INNER:]
    cbx, cbb = rows3(conv_bh[:, :D_INNER]), rows3(conv_bh[:, D_INNER:])
    p3 = p.reshape(depth, m, PLE_DIM)

    xf = x.reshape(m, d)
    for i in range(depth):
        small = _norm_matmul(xf, mix_nw, w_small, i, F32, tm_big, SMALL_N, "in_proj_small")
        big = _norm_matmul(xf, mix_nw, w_big, i, BF16, tm_big, 2048, "in_proj_big")
        qt = _q_proj(small, qa_nw, wqt, qnw, ct_t, st_t, i, tm_small)
        k, vt = _kv_proj(small, kva_nw, wk, wvt, kwn, kwr, ct, st, i, tq)
        o = _attention(qt, k, vt, batch, seq, tq, 2)
        y = _ssd(big, small, cwx, cwb, cbx, cbb, dtb, alog, dsk, ssm_nw, sel, eye_t, tri_t, tri64, shift_m, i,
                 batch, seq, 4)
        merged = _merge(o, y, w_o_mla_b, w_o_ssm_b, big, i, tm_big, 512)
        xf = _res_matmul(xf, merged, w_out, i, tm_mid)
        hidden = _norm_matmul(xf, mlp_nw, w_up_b, i, BF16, tm_big, 2048, "mlp_up", relu2=True)
        xf = _res_matmul_tiled(xf, hidden, w_down_b, i, tm_big, 1024, D_FF // 2, "mlp_down")
        xf = _ple(xf, ple_nw, w_ple_gate, p3, w_ple_b, i, tm_mid)
    return xf.reshape(batch, seq, d)
```
